```python
import functools
import math
import jax
import jax.numpy as jnp
from jax import lax
import numpy as np

D_MODEL = 1024
BATCH = 32
SEQ = 256
DEPTH = 2
DEC_BATCH = 2
DEC_SEQ = 4096
PAST_LEN = 512

GRID_W = 64
N_DIR = 2
N_BRANCH = 3
BRANCH_W = 512
N_MOD = 6
EPS = 1e-6
ML_HEADS = 4
ML_DK = 128
ML_DV = 128
ML_WIDTH = ML_HEADS * ML_DV
ML_CHUNK = 64
S5_WIDTH = BRANCH_W
S5_GROUP = 16
S5_GROUPS = S5_WIDTH // S5_GROUP
S5_STATE = 64
GD_HEADS = 4
GD_DK = 128
GD_DV = 128
GD_WIDTH = GD_HEADS * GD_DV
GD_CHUNK = 64
CONV_K = 5
D_FF = -(-8 * D_MODEL // (3 * 256)) * 256
IN_SIZES = (ML_HEADS * ML_DK, ML_HEADS * ML_DK, ML_WIDTH, ML_WIDTH, N_DIR * ML_HEADS, N_DIR * ML_HEADS,
            S5_WIDTH,
            3 * GD_WIDTH, GD_WIDTH, N_DIR * GD_HEADS, N_DIR * GD_HEADS,
            N_BRANCH * D_MODEL)
D_IN = sum(IN_SIZES)

kernel_name = 'hybrid_mlstm_s5_gdn_diffusion_step'


def rmsnorm(x, w):
    xf = x.astype(jnp.float32)
    y = xf * lax.rsqrt(jnp.mean(xf * xf, axis=-1, keepdims=True) + EPS)
    return (y * w.astype(jnp.float32)).astype(x.dtype)


def l2norm(x):
    return x * lax.rsqrt(jnp.sum(x * x, axis=-1, keepdims=True) + EPS)


def rev(a):
    return jnp.flip(a, axis=1)


def to_chunks(a, size):
    b, s, h = a.shape[:3]
    a = a.reshape((b, s // size, size, h) + a.shape[3:])
    return jnp.moveaxis(a, (1, 3), (0, 2))


def from_chunks(a):
    a = jnp.moveaxis(a, (0, 2), (1, 3))
    return a.reshape((a.shape[0], a.shape[1] * a.shape[2]) + a.shape[3:])


def split_in(z):
    idx, acc = [], 0
    for size in IN_SIZES[:-1]:
        acc += size
        idx.append(acc)
    return jnp.split(z, idx, axis=-1)


def grid_pos_embed(n_tok, dtype):
    rows = n_tok // GRID_W
    t = jnp.arange(rows * GRID_W)
    quarter = D_MODEL // 4
    omega = 1.0 / (10000.0 ** (jnp.arange(quarter, dtype=jnp.float32) / quarter))

    def enc(pos):
        ang = pos.astype(jnp.float32)[:, None] * omega[None, :]
        return jnp.concatenate([jnp.sin(ang), jnp.cos(ang)], axis=-1)

    return jnp.concatenate([enc(t // GRID_W), enc(t % GRID_W)], axis=-1).astype(dtype)


def mlstm_scan(q, k, v, ig, lf, C0, n0, m0):
    L = ML_CHUNK
    tri = jnp.tril(jnp.ones((L, L), dtype=bool))

    def step(carry, inp):
        C, n, m = carry
        qi, ki, vi, ii, fi = inp
        bcum = jnp.cumsum(fi, axis=-1)
        log_d = jnp.where(tri, bcum[..., :, None] - bcum[..., None, :] + ii[..., None, :], -jnp.inf)
        log_0 = bcum + m[..., None]
        m_t = jnp.maximum(log_0, jnp.max(log_d, axis=-1))
        w_0 = jnp.exp(log_0 - m_t)
        s = jnp.einsum('bhtk,bhsk->bhts', qi, ki) * jnp.exp(log_d - m_t[..., None])
        num = jnp.einsum('bhts,bhsv->bhtv', s, vi) + w_0[..., None] * jnp.einsum('bhtk,bhkv->bhtv', qi, C)
        den = jnp.sum(s, axis=-1) + w_0 * jnp.einsum('bhtk,bhk->bht', qi, n)
        h = num / jnp.maximum(jnp.abs(den), jnp.exp(-m_t))[..., None]
        m_new = m_t[..., -1]
        w_s = jnp.exp(bcum[..., -1:] - bcum + ii - m_new[..., None])
        c_0 = jnp.exp(bcum[..., -1] + m - m_new)
        C = c_0[..., None, None] * C + jnp.einsum('bhs,bhsk,bhsv->bhkv', w_s, ki, vi)
        n = c_0[..., None] * n + jnp.einsum('bhs,bhsk->bhk', w_s, ki)
        return (C, n, m_new), h

    chunks = tuple(to_chunks(a, L) for a in (q, k, v, ig, lf))
    (C, n, m), h = lax.scan(step, (C0, n0, m0), chunks)
    return from_chunks(h), (C, n, m)


def mlstm_mixer(q, k, v, o, ig, fg, i_bias, f_bias, norm_w, C0, n0, m0):
    f32 = jnp.float32
    b, s, _ = q.shape
    q = q.astype(f32).reshape(b, s, ML_HEADS, ML_DK)
    k = k.astype(f32).reshape(b, s, ML_HEADS, ML_DK) * (ML_DK ** -0.5)
    v = v.astype(f32).reshape(b, s, ML_HEADS, ML_DV)
    ig = ig.astype(f32).reshape(b, s, N_DIR, ML_HEADS) + i_bias.astype(f32)
    lf = jax.nn.log_sigmoid(fg.astype(f32).reshape(b, s, N_DIR, ML_HEADS) + f_bias.astype(f32))
    C0, n0, m0 = C0.astype(f32), n0.astype(f32), m0.astype(f32)
    h_f, st_f = mlstm_scan(q, k, v, ig[:, :, 0], lf[:, :, 0], C0[:, 0], n0[:, 0], m0[:, 0])
    h_b, st_b = mlstm_scan(rev(q), rev(k), rev(v), rev(ig[:, :, 1]), rev(lf[:, :, 1]),
                           C0[:, 1], n0[:, 1], m0[:, 1])
    h = h_f + rev(h_b)
    h = h * lax.rsqrt(jnp.mean(h * h, axis=-1, keepdims=True) + EPS) * norm_w.astype(f32).reshape(ML_HEADS, ML_DV)
    y = jax.nn.sigmoid(o.astype(f32)) * h.reshape(b, s, ML_WIDTH)
    states = tuple(jnp.stack([a_f, a_b], axis=1) for a_f, a_b in zip(st_f, st_b))
    return y, states


def s5_scan(bu_re, bu_im, lam_re, lam_im, log_step, h0_re, h0_im):
    dt = jnp.exp(log_step)[:, None]
    mag = jnp.exp(lam_re * dt)
    ab_re = mag * jnp.cos(lam_im * dt)
    ab_im = mag * jnp.sin(lam_im * dt)
    den = lam_re * lam_re + lam_im * lam_im
    nr = ab_re - 1.0
    z_re = (nr * lam_re + ab_im * lam_im) / den
    z_im = (ab_im * lam_re - nr * lam_im) / den
    x_re = z_re * bu_re - z_im * bu_im
    x_im = z_re * bu_im + z_im * bu_re
    x_re = x_re.at[:, 0].add(ab_re * h0_re - ab_im * h0_im)
    x_im = x_im.at[:, 0].add(ab_re * h0_im + ab_im * h0_re)
    a_re = jnp.broadcast_to(ab_re, x_re.shape)
    a_im = jnp.broadcast_to(ab_im, x_im.shape)

    def combine(e1, e2):
        a1r, a1i, b1r, b1i = e1
        a2r, a2i, b2r, b2i = e2
        return (a1r * a2r - a1i * a2i, a1r * a2i + a1i * a2r,
                a2r * b1r - a2i * b1i + b2r, a2r * b1i + a2i * b1r + b2i)

    _, _, h_re, h_im = lax.associative_scan(combine, (a_re, a_im, x_re, x_im), axis=1)
    return h_re, h_im


def s5_mixer(u, lam_re, lam_im, log_step, B_re, B_im, C_re, C_im, D, glu_w, glu_b, h0_re, h0_im):
    f32 = jnp.float32
    b, s, _ = u.shape
    uf = u.astype(f32)
    ug = uf.reshape(b, s, S5_GROUPS, S5_GROUP)
    bu_re = jnp.einsum('bsgc,gpc->bsgp', ug, B_re.astype(f32))
    bu_im = jnp.einsum('bsgc,gpc->bsgp', ug, B_im.astype(f32))
    lam_re, lam_im, log_step = lam_re.astype(f32), lam_im.astype(f32), log_step.astype(f32)
    h0_re, h0_im = h0_re.astype(f32), h0_im.astype(f32)
    hr_f, hi_f = s5_scan(bu_re, bu_im, lam_re[0], lam_im[0], log_step[0], h0_re[:, 0], h0_im[:, 0])
    hr_b, hi_b = s5_scan(rev(bu_re), rev(bu_im), lam_re[1], lam_im[1], log_step[1], h0_re[:, 1], h0_im[:, 1])
    hr_b, hi_b = rev(hr_b), rev(hi_b)
    y = (jnp.einsum('bsgp,gcp->bsgc', hr_f + hr_b, C_re.astype(f32))
         - jnp.einsum('bsgp,gcp->bsgc', hi_f + hi_b, C_im.astype(f32)))
    y = y.reshape(b, s, S5_WIDTH) + D.astype(f32) * uf
    y = jax.nn.gelu(y)
    y = y * jax.nn.sigmoid(y @ glu_w.astype(f32) + glu_b.astype(f32))
    st_re = jnp.stack([hr_f[:, -1], hr_b[:, 0]], axis=1)
    st_im = jnp.stack([hi_f[:, -1], hi_b[:, 0]], axis=1)
    return y, (st_re, st_im)


def depthwise_conv(x, w):
    return lax.conv_general_dilated(x, w[:, None, :], window_strides=(1,),
                                    padding=[(CONV_K // 2, CONV_K // 2)],
                                    dimension_numbers=('NWC', 'WIO', 'NWC'),
                                    feature_group_count=x.shape[-1])


def gdn_scan(q, k, v, g, beta, S0):
    L = GD_CHUNK
    qc, kc, vc, gc, bc = (to_chunks(a, L) for a in (q, k, v, g, beta))
    gc = jnp.cumsum(gc, axis=-1)
    tri = jnp.tril(jnp.ones((L, L), dtype=bool))
    strict = jnp.tril(jnp.ones((L, L), dtype=bool), -1)
    decay = jnp.exp(jnp.where(tri, gc[..., :, None] - gc[..., None, :], -jnp.inf))
    kb = kc * bc[..., None]
    lmat = jnp.where(strict, jnp.einsum('nbhtk,nbhsk->nbhts', kb, kc) * decay, 0.0)
    amat = lmat + jnp.eye(L, dtype=lmat.dtype)
    solve = functools.partial(lax.linalg.triangular_solve, left_side=True, lower=True, unit_diagonal=True)
    u = solve(amat, vc * bc[..., None])
    w = solve(amat, kb * jnp.exp(gc)[..., None])
    attn = jnp.einsum('nbhtk,nbhsk->nbhts', qc, kc) * decay
    qg = qc * jnp.exp(gc)[..., None]
    kd = kc * jnp.exp(gc[..., -1:] - gc)[..., None]

    def step(state, inp):
        qg_i, kd_i, u_i, w_i, attn_i, glast = inp
        v_new = u_i - jnp.einsum('bhsk,bhkv->bhsv', w_i, state)
        o = jnp.einsum('bhtk,bhkv->bhtv', qg_i, state) + jnp.einsum('bhts,bhsv->bhtv', attn_i, v_new)
        state = state * jnp.exp(glast)[..., None, None] + jnp.einsum('bhsk,bhsv->bhkv', kd_i, v_new)
        return state, o

    S_fin, o = lax.scan(step, S0, (qg, kd, u, w, attn, gc[..., -1]))
    return from_chunks(o), S_fin


def gdn_mixer(qkv, z, a, bt, conv_w, A_log, dt_bias, norm_w, S0):
    f32 = jnp.float32
    b, s, _ = qkv.shape
    qkv = jax.nn.silu(depthwise_conv(qkv.astype(f32), conv_w.astype(f32)))
    q, k, v = jnp.split(qkv, 3, axis=-1)
    q = l2norm(q.reshape(b, s, GD_HEADS, GD_DK)) * (GD_DK ** -0.5)
    k = l2norm(k.reshape(b, s, GD_HEADS, GD_DK))
    v = v.reshape(b, s, GD_HEADS, GD_DV)
    g = -jnp.exp(A_log.astype(f32)) * jax.nn.softplus(a.astype(f32).reshape(b, s, N_DIR, GD_HEADS) + dt_bias.astype(f32))
    beta = jax.nn.sigmoid(bt.astype(f32).reshape(b, s, N_DIR, GD_HEADS))
    S0 = S0.astype(f32)
    o_f, S_f = gdn_scan(q, k, v, g[:, :, 0], beta[:, :, 0], S0[:, 0])
    o_b, S_b = gdn_scan(rev(q), rev(k), rev(v), rev(g[:, :, 1]), rev(beta[:, :, 1]), S0[:, 1])
    o = o_f + rev(o_b)
    o = o * lax.rsqrt(jnp.mean(o * o, axis=-1, keepdims=True) + EPS) * norm_w.astype(f32)
    y = o.reshape(b, s, GD_WIDTH) * jax.nn.silu(z.astype(f32))
    return y, jnp.stack([S_f, S_b], axis=1)


def trunk_layer(x, mod, p, st):
    b, s, _ = x.shape
    sh1, sc1, g1, sh2, sc2, g2 = jnp.split(mod[:, None, :].astype(x.dtype), N_MOD, axis=-1)
    h = rmsnorm(x, p['norm1_w']) * (1 + sc1) + sh1
    z = h @ p['w_in']
    mq, mk, mv, mo, mi, mf, su, gqkv, gz, ga, gb, gates = split_in(z)
    y_a, (mC, mn, mm) = mlstm_mixer(mq, mk, mv, mo, mi, mf, p['ml_i_bias'], p['ml_f_bias'], p['ml_norm_w'],
                                    st[0], st[1], st[2])
    y_b, (sre, sim) = s5_mixer(su, p['s5_lam_re'], p['s5_lam_im'], p['s5_log_step'], p['s5_B_re'], p['s5_B_im'],
                               p['s5_C_re'], p['s5_C_im'], p['s5_D'], p['s5_glu_w'], p['s5_glu_b'], st[3], st[4])
    y_c, gS = gdn_mixer(gqkv, gz, ga, gb, p['gd_conv_w'], p['gd_A_log'], p['gd_dt_bias'], p['gd_norm_w'], st[5])
    ys = jnp.stack([y_a, y_b, y_c], axis=2).astype(x.dtype)
    gates = jax.nn.sigmoid(gates.reshape(b, s, N_BRANCH, D_MODEL))
    merged = jnp.sum(gates * jnp.einsum('bsnc,ncd->bsnd', ys, p['w_branch']), axis=2)
    x = x + g1 * (merged @ p['w_out'])
    h2 = rmsnorm(x, p['norm2_w']) * (1 + sc2) + sh2
    ff = (jax.nn.silu(h2 @ p['w_gate']) * (h2 @ p['w_up'])) @ p['w_down']
    x = x + g2 * ff
    return x, (mC, mn, mm, sre, sim, gS)


def setup_inputs(seed: int = 0) -> dict:
    key = jax.random.key(seed)
    keys = iter(jax.random.split(key, 64))

    def nrm(shape, scale):
        return scale * jax.random.normal(next(keys), shape, jnp.float32)

    def unif(shape, lo, hi):
        return jax.random.uniform(next(keys), shape, jnp.float32, lo, hi)

    L = DEPTH
    dt = jnp.exp(unif((L, N_DIR, GD_HEADS), math.log(1e-3), math.log(1e-1)))
    return {
        'x_prompt': nrm((BATCH, SEQ, D_MODEL), 1.0),
        'x_sample': nrm((DEC_BATCH, DEC_SEQ, D_MODEL), 1.0),
        'state_mlstm_C': nrm((DEC_BATCH, L, N_DIR, ML_HEADS, ML_DK, ML_DV), 0.1),
        'state_mlstm_n': nrm((DEC_BATCH, L, N_DIR, ML_HEADS, ML_DK), 0.3),
        'state_mlstm_m': nrm((DEC_BATCH, L, N_DIR, ML_HEADS), 1.0),
        'state_s5_re': nrm((DEC_BATCH, L, N_DIR, S5_GROUPS, S5_STATE), 0.3),
        'state_s5_im': nrm((DEC_BATCH, L, N_DIR, S5_GROUPS, S5_STATE), 0.3),
        'state_gdn_S': nrm((DEC_BATCH, L, N_DIR, GD_HEADS, GD_DK, GD_DV), 0.3),
        'c': nrm((DEC_BATCH, D_MODEL), 1.0),
        'c_ctx': nrm((D_MODEL,), 1.0),
        'ada_w': nrm((L, D_MODEL, N_MOD * D_MODEL), 0.3 * D_MODEL ** -0.5),
        'ada_b': nrm((L, N_MOD * D_MODEL), 0.02),
        'norm1_w': 1.0 + nrm((L, D_MODEL), 0.01),
        'w_in': nrm((L, D_MODEL, D_IN), D_MODEL ** -0.5),
        'ml_i_bias': nrm((L, N_DIR, ML_HEADS), 0.1),
        'ml_f_bias': unif((L, N_DIR, ML_HEADS), 3.0, 6.0),
        'ml_norm_w': 1.0 + nrm((L, ML_WIDTH), 0.01),
        's5_lam_re': -0.5 + nrm((L, N_DIR, S5_GROUPS, S5_STATE), 0.01),
        's5_lam_im': math.pi * jnp.arange(S5_STATE, dtype=jnp.float32) + nrm((L, N_DIR, S5_GROUPS, S5_STATE), 0.01),
        's5_log_step': unif((L, N_DIR, S5_GROUPS), math.log(1e-3), math.log(1e-1)),
        's5_B_re': nrm((L, S5_GROUPS, S5_STATE, S5_GROUP), (2 * S5_GROUP) ** -0.5),
        's5_B_im': nrm((L, S5_GROUPS, S5_STATE, S5_GROUP), (2 * S5_GROUP) ** -0.5),
        's5_C_re': nrm((L, S5_GROUPS, S5_GROUP, S5_STATE), (2 * S5_STATE) ** -0.5),
        's5_C_im': nrm((L, S5_GROUPS, S5_GROUP, S5_STATE), (2 * S5_STATE) ** -0.5),
        's5_D': nrm((L, S5_WIDTH), 1.0),
        's5_glu_w': nrm((L, S5_WIDTH, S5_WIDTH), S5_WIDTH ** -0.5),
        's5_glu_b': nrm((L, S5_WIDTH), 0.02),
        'gd_conv_w': nrm((L, CONV_K, 3 * GD_WIDTH), CONV_K ** -0.5),
        'gd_A_log': jnp.log(unif((L, N_DIR, GD_HEADS), 1.0, 16.0)),
        'gd_dt_bias': dt + jnp.log(-jnp.expm1(-dt)),
        'gd_norm_w': 1.0 + nrm((L, GD_DV), 0.01),
        'w_branch': nrm((L, N_BRANCH, BRANCH_W, D_MODEL), BRANCH_W ** -0.5),
        'w_out': nrm((L, D_MODEL, D_MODEL), D_MODEL ** -0.5),
        'norm2_w': 1.0 + nrm((L, D_MODEL), 0.01),
        'w_gate': nrm((L, D_MODEL, D_FF), D_MODEL ** -0.5),
        'w_up': nrm((L, D_MODEL, D_FF), D_MODEL ** -0.5),
        'w_down': nrm((L, D_FF, D_MODEL), D_FF ** -0.5),
        'final_norm_w': 1.0 + nrm((D_MODEL,), 0.01),
    }


def reference(x_prompt, x_sample, state_mlstm_C, state_mlstm_n, state_mlstm_m, state_s5_re, state_s5_im,
              state_gdn_S, c, c_ctx, ada_w, ada_b, norm1_w, w_in, ml_i_bias, ml_f_bias, ml_norm_w,
              s5_lam_re, s5_lam_im, s5_log_step, s5_B_re, s5_B_im, s5_C_re, s5_C_im, s5_D, s5_glu_w, s5_glu_b,
              gd_conv_w, gd_A_log, gd_dt_bias, gd_norm_w, w_branch, w_out, norm2_w, w_gate, w_up, w_down,
              final_norm_w):
    def layer_params(l):
        return dict(norm1_w=norm1_w[l], w_in=w_in[l], ml_i_bias=ml_i_bias[l], ml_f_bias=ml_f_bias[l],
                    ml_norm_w=ml_norm_w[l], s5_lam_re=s5_lam_re[l], s5_lam_im=s5_lam_im[l],
                    s5_log_step=s5_log_step[l], s5_B_re=s5_B_re[l], s5_B_im=s5_B_im[l], s5_C_re=s5_C_re[l],
                    s5_C_im=s5_C_im[l], s5_D=s5_D[l], s5_glu_w=s5_glu_w[l], s5_glu_b=s5_glu_b[l],
                    gd_conv_w=gd_conv_w[l], gd_A_log=gd_A_log[l], gd_dt_bias=gd_dt_bias[l], gd_norm_w=gd_norm_w[l],
                    w_branch=w_branch[l], w_out=w_out[l], norm2_w=norm2_w[l], w_gate=w_gate[l], w_up=w_up[l],
                    w_down=w_down[l])

    f32 = jnp.float32
    bp = x_prompt.shape[0]
    zero_states = (jnp.zeros((bp, N_DIR, ML_HEADS, ML_DK, ML_DV), f32),
                   jnp.zeros((bp, N_DIR, ML_HEADS, ML_DK), f32),
                   jnp.zeros((bp, N_DIR, ML_HEADS), f32),
                   jnp.zeros((bp, N_DIR, S5_GROUPS, S5_STATE), f32),
                   jnp.zeros((bp, N_DIR, S5_GROUPS, S5_STATE), f32),
                   jnp.zeros((bp, N_DIR, GD_HEADS, GD_DK, GD_DV), f32))
    xp = x_prompt
    per_layer = []
    for l in range(DEPTH):
        mod_ctx = (jax.nn.silu(c_ctx) @ ada_w[l] + ada_b[l])[None]
        xp, st = trunk_layer(xp, mod_ctx, layer_params(l), zero_states)
        per_layer.append(st)
    y_prompt = rmsnorm(xp, final_norm_w)
    new_mlstm_C, new_mlstm_n, new_mlstm_m, new_s5_re, new_s5_im, new_gdn_S = [
        jnp.stack([st[i] for st in per_layer], axis=1) for i in range(6)]

    xs = x_sample + grid_pos_embed(x_sample.shape[1], x_sample.dtype)
    cache = (state_mlstm_C, state_mlstm_n, state_mlstm_m, state_s5_re, state_s5_im, state_gdn_S)
    for l in range(DEPTH):
        mod = jax.nn.silu(c) @ ada_w[l] + ada_b[l]
        xs, _ = trunk_layer(xs, mod, layer_params(l), tuple(a[:, l] for a in cache))
    y_sample = rmsnorm(xs, final_norm_w)
    return (y_prompt, y_sample, new_mlstm_C, new_mlstm_n, new_mlstm_m, new_s5_re, new_s5_im, new_gdn_S)
```

```python
import functools
import math

import jax
import jax.numpy as jnp
from jax import lax
from jax.experimental import pallas as pl
from jax.experimental.pallas import tpu as pltpu

F32 = jnp.float32
BF16 = jnp.bfloat16

D_MODEL = 1024
DEPTH = 2
N_DIR = 2
N_MOD = 6
EPS = 1e-6
HEADS = 4
DH = 128
BRANCH_W = HEADS * DH
CHUNK = 64
S5_GROUPS = 32
S5_GROUP = 16
S5_STATE = 64
S5_CB = 4
S5_CBW = S5_GROUPS * S5_STATE // S5_CB
S5_SUBLANES = 8
CONV_K = 5
D_FF = -(-8 * D_MODEL // (3 * 256)) * 256
IN_SIZES = (512, 512, 512, 512, 8, 8, 512, 1536, 512, 8, 8, 3072)
GATE_W = 128

CB_GATES, CB_Q, CB_K, CB_V, CB_O, CB_SU, CB_GQKV, CB_GZ = 0, 6, 7, 8, 9, 10, 11, 14
Z_W = 15 * BRANCH_W

VMEM_LIMIT = 56 * 1024 * 1024


def _cparams(sem):
    return pltpu.CompilerParams(dimension_semantics=sem, vmem_limit_bytes=VMEM_LIMIT)


def _dot(a, b):
    return jnp.dot(a.astype(BF16), b.astype(BF16), preferred_element_type=F32)


def _dot_nt(a, b):
    return lax.dot_general(a.astype(BF16), b.astype(BF16), (((1,), (1,)), ((), ())),
                           preferred_element_type=F32)


def _dot_tn(a, b):
    return lax.dot_general(a.astype(BF16), b.astype(BF16), (((0,), (0,)), ((), ())),
                           preferred_element_type=F32)


def _split3(x):
    hi = x.astype(BF16)
    r1 = x - hi.astype(F32)
    mid = r1.astype(BF16)
    lo = (r1 - mid.astype(F32)).astype(BF16)
    return hi, mid, lo


def _sel_dot(sel, x, nt=False):
    dims = (((1,), (1,)), ((), ())) if nt else (((1,), (0,)), ((), ()))
    hi, mid, lo = _split3(x)
    f = lambda p: lax.dot_general(sel, p, dims, preferred_element_type=F32)
    return (f(hi) + f(mid)) + f(lo)


def _dot_hp(a, b):
    ah = a.astype(BF16)
    al = (a - ah.astype(F32)).astype(BF16)
    bh = b.astype(BF16)
    bl = (b - bh.astype(F32)).astype(BF16)
    f = lambda p, q: jnp.dot(p, q, preferred_element_type=F32)
    return f(ah, bh) + (f(ah, bl) + f(al, bh))


def _sigmoid(x):
    return 1.0 / (1.0 + jnp.exp(-x))


def _silu(x):
    return x * _sigmoid(x)


def _softplus(x):
    return jnp.maximum(x, 0.0) + jnp.log(1.0 + jnp.exp(-jnp.abs(x)))


def _gelu_tanh(x):
    c = math.sqrt(2.0 / math.pi)
    return 0.5 * x * (1.0 + jnp.tanh(c * (x + 0.044715 * (x * x * x))))


def _head_rms(x, w_row):
    outs = []
    for h in range(HEADS):
        xh = x[:, h * DH:(h + 1) * DH]
        outs.append(xh * lax.rsqrt(jnp.mean(xh * xh, axis=-1, keepdims=True) + EPS))
    return jnp.concatenate(outs, axis=-1) * w_row


def _dir_chunk(c, d, n):
    return c + d * (n - 1 - 2 * c)


def _mod_kernel(c_ref, w_ref, b_ref, o_ref):
    o_ref[0] = _dot(_silu(c_ref[...]), w_ref[0]) + b_ref[0]


def _modulation(cc, ada_w, ada_b):
    tn = 1536
    nmod = N_MOD * D_MODEL
    return pl.pallas_call(
        _mod_kernel,
        grid=(DEPTH, nmod // tn),
        in_specs=[pl.BlockSpec((8, D_MODEL), lambda l, j: (0, 0)),
                  pl.BlockSpec((1, D_MODEL, tn), lambda l, j: (l, 0, j)),
                  pl.BlockSpec((1, 1, tn), lambda l, j: (l, 0, j))],
        out_specs=pl.BlockSpec((1, 8, tn), lambda l, j: (l, 0, j)),
        out_shape=jax.ShapeDtypeStruct((DEPTH, 8, nmod), F32),
        name="adaln_mod",
        compiler_params=_cparams(("arbitrary", "arbitrary")),
    )(cc, ada_w, ada_b.reshape(DEPTH, 1, nmod))


def _inproj_kernel(*refs, has_pe):
    if has_pe:
        x_ref, pe_ref, mod_ref, nw_ref, w_ref, ws_ref, z_ref, zs_ref, xs_ref, hn_scr = refs
    else:
        x_ref, mod_ref, nw_ref, w_ref, ws_ref, z_ref, zs_ref, hn_scr = refs

    @pl.when(pl.program_id(1) == 0)
    def _():
        x = x_ref[...]
        if has_pe:
            x = x + pe_ref[...]
            xs_ref[...] = x
        y = x * lax.rsqrt(jnp.mean(x * x, axis=-1, keepdims=True) + EPS)
        h = (y * nw_ref[...]) * (1.0 + mod_ref[0, 1:2, :]) + mod_ref[0, 0:1, :]
        hb = h.astype(BF16)
        hn_scr[...] = hb
        zs_ref[...] = jnp.dot(hb, ws_ref[...], preferred_element_type=F32)

    z_ref[...] = jnp.dot(hn_scr[...], w_ref[...], preferred_element_type=F32)


def _inproj(x, mod, norm_w, w_big, w_small, seq, pe=None):
    m = x.shape[0]
    tm, tn = 512, 1536
    rows_per_mod = seq if mod.shape[0] > 1 else m
    has_pe = pe is not None
    row = lambda i, j: (i, 0)
    in_specs = [pl.BlockSpec((tm, D_MODEL), row)]
    args = [x]
    if has_pe:
        in_specs.append(pl.BlockSpec((tm, D_MODEL), lambda i, j: (i % (seq // tm), 0)))
        args.append(pe)
    in_specs += [pl.BlockSpec((1, N_MOD, D_MODEL), lambda i, j: (i * tm // rows_per_mod, 0, 0)),
                 pl.BlockSpec((1, D_MODEL), lambda i, j: (0, 0)),
                 pl.BlockSpec((D_MODEL, tn), lambda i, j: (0, j)),
                 pl.BlockSpec((D_MODEL, N_DIR * GATE_W), lambda i, j: (0, 0))]
    args += [mod, norm_w.reshape(1, D_MODEL), w_big, w_small]
    out_specs = [pl.BlockSpec((tm, tn), lambda i, j: (i, j)),
                 pl.BlockSpec((tm, N_DIR * GATE_W), row)]
    out_shape = [jax.ShapeDtypeStruct((m, Z_W), F32),
                 jax.ShapeDtypeStruct((m, N_DIR * GATE_W), F32)]
    if has_pe:
        out_specs.append(pl.BlockSpec((tm, D_MODEL), row))
        out_shape.append(jax.ShapeDtypeStruct((m, D_MODEL), F32))
    return pl.pallas_call(
        functools.partial(_inproj_kernel, has_pe=has_pe),
        grid=(m // tm, Z_W // tn),
        in_specs=in_specs, out_specs=out_specs, out_shape=out_shape,
        scratch_shapes=[pltpu.VMEM((tm, D_MODEL), BF16)],
        name="inproj",
        compiler_params=_cparams(("arbitrary", "arbitrary")),
    )(*args)


def _mixer_rowblk(nt):
    return lambda b, d, j: b * nt + _dir_chunk(j, d, nt)


def _chunk_masks(d):
    r = lax.broadcasted_iota(jnp.int32, (CHUNK, CHUNK), 0)
    c = lax.broadcasted_iota(jnp.int32, (CHUNK, CHUNK), 1)
    diff = (r - c) * (1 - 2 * d)
    return diff >= 0, diff > 0


def _gate_selector():
    r = lax.broadcasted_iota(jnp.int32, (16, GATE_W), 0)
    c = lax.broadcasted_iota(jnp.int32, (16, GATE_W), 1)
    return (r == c).astype(BF16)


def _mlstm_kernel(*refs, nch, zero_init, emit_state):
    q_ref, k_ref, v_ref, g_ref, bias_ref = refs[:5]
    pos = 5
    if not zero_init:
        c0_ref, n0_ref, m0_ref = refs[pos:pos + 3]
        pos += 3
    h_ref = refs[pos]
    pos += 1
    if emit_state:
        co_ref, no_ref, mo_ref = refs[pos:pos + 3]
        pos += 3
    c_scr, n_scr, m_scr = refs[pos:pos + 3]

    d = pl.program_id(1)
    j = pl.program_id(2)

    @pl.when(j == 0)
    def _():
        if zero_init:
            c_scr[...] = jnp.zeros_like(c_scr)
            n_scr[...] = jnp.zeros_like(n_scr)
            m_scr[...] = jnp.zeros_like(m_scr)
        else:
            c_scr[...] = c0_ref[0, 0]
            n_scr[...] = n0_ref[0, 0]
            m_scr[...] = m0_ref[0, 0]

    mask, _ = _chunk_masks(d)
    tri = mask.astype(BF16)
    sel = _gate_selector()
    lane = lax.broadcasted_iota(jnp.int32, (CHUNK, GATE_W), 1)
    rowid = lax.broadcasted_iota(jnp.int32, (CHUNK, 1), 0)
    last = (CHUNK - 1) * (1 - d)
    bias = bias_ref[0]
    scale = DH ** -0.5

    def chunk(c, carry):
        r0 = pl.multiple_of(_dir_chunk(c, d, nch) * CHUNK, CHUNK)
        rows = pl.ds(r0, CHUNK)
        pre = g_ref[rows, :] + bias
        lf = -_softplus(-pre)
        cum = _sel_dot(tri, lf)
        tot = jnp.sum(lf, axis=0, keepdims=True)
        xt = _sel_dot(sel, jnp.where(lane < HEADS, pre, cum), nt=True)
        for h in range(HEADS):
            cols = slice(h * DH, (h + 1) * DH)
            q = q_ref[rows, cols]
            k = k_ref[rows, cols] * scale
            v = v_ref[rows, cols]
            b_col = cum[:, HEADS + h:HEADS + h + 1]
            i_col = pre[:, h:h + 1]
            b_row = xt[HEADS + h:HEADS + h + 1, :]
            i_row = xt[h:h + 1, :]
            m_prev = m_scr[h][:, 0:1]
            log_d = jnp.where(mask, b_col - b_row + i_row, -jnp.inf)
            log_0 = b_col + m_prev
            m_t = jnp.maximum(log_0, jnp.max(log_d, axis=-1, keepdims=True))
            w_0 = jnp.exp(log_0 - m_t)
            s = _dot_nt(q, k) * jnp.exp(log_d - m_t)
            cst = c_scr[h]
            nst = n_scr[h]
            num = _dot(s, v) + w_0 * _dot(q, cst)
            den = jnp.sum(s, axis=-1, keepdims=True) + w_0 * jnp.sum(q * nst, axis=-1, keepdims=True)
            h_ref[0, rows, cols] = num / jnp.maximum(jnp.abs(den), jnp.exp(-m_t))
            b_last = tot[:, HEADS + h:HEADS + h + 1]
            m_new = jnp.sum(jnp.where(rowid == last, m_t, 0.0), axis=0, keepdims=True)
            w_s = jnp.exp(b_last - b_col + i_col - m_new)
            c_0 = jnp.exp(b_last + m_prev - m_new)
            kw = k * w_s
            c_scr[h] = c_0 * cst + _dot_tn(kw, v)
            n_scr[h] = c_0 * nst + jnp.sum(kw, axis=0, keepdims=True)
            m_scr[h] = jnp.broadcast_to(m_new, (1, DH))
        return carry

    lax.fori_loop(0, nch, chunk, 0)

    if emit_state:
        @pl.when(j == pl.num_programs(2) - 1)
        def _():
            co_ref[0, 0] = c_scr[...]
            no_ref[0, 0] = n_scr[...]
            mo_ref[0, 0] = m_scr[...]


def _mlstm(z, zs, bias, batch, seq, tb, states):
    m = batch * seq
    nt = seq // tb
    zero_init = states is None
    emit_state = states is None
    rb = _mixer_rowblk(nt)
    zspec = lambda cb: pl.BlockSpec((tb, BRANCH_W), lambda b, d, j: (rb(b, d, j), cb))
    st5 = lambda shape: pl.BlockSpec((1, 1) + shape, lambda b, d, j: (b, d) + (0,) * len(shape))
    in_specs = [zspec(CB_Q), zspec(CB_K), zspec(CB_V),
                pl.BlockSpec((tb, GATE_W), lambda b, d, j: (rb(b, d, j), d)),
                pl.BlockSpec((1, 1, GATE_W), lambda b, d, j: (d, 0, 0))]
    args = [z, z, z, zs, bias]
    if not zero_init:
        in_specs += [st5((HEADS, DH, DH)), st5((HEADS, 1, DH)), st5((HEADS, 1, DH))]
        args += list(states)
    out_specs = [pl.BlockSpec((1, tb, BRANCH_W), lambda b, d, j: (d, rb(b, d, j), 0))]
    out_shape = [jax.ShapeDtypeStruct((N_DIR, m, BRANCH_W), F32)]
    if emit_state:
        out_specs += [st5((HEADS, DH, DH)), st5((HEADS, 1, DH)), st5((HEADS, 1, DH))]
        out_shape += [jax.ShapeDtypeStruct((batch, N_DIR, HEADS, DH, DH), F32),
                      jax.ShapeDtypeStruct((batch, N_DIR, HEADS, 1, DH), F32),
                      jax.ShapeDtypeStruct((batch, N_DIR, HEADS, 1, DH), F32)]
    return pl.pallas_call(
        functools.partial(_mlstm_kernel, nch=tb // CHUNK, zero_init=zero_init, emit_state=emit_state),
        grid=(batch, N_DIR, nt),
        in_specs=in_specs, out_specs=out_specs, out_shape=out_shape,
        scratch_shapes=[pltpu.VMEM((HEADS, DH, DH), F32), pltpu.VMEM((HEADS, 1, DH), F32),
                        pltpu.VMEM((HEADS, 1, DH), F32)],
        name="mlstm",
        compiler_params=_cparams(("arbitrary", "arbitrary", "arbitrary")),
    )(*args)


def _gdn_prep_kernel(x_ref, prev_ref, next_ref, w_ref, o_ref, xe_scr, *, tb, seq):
    i = pl.program_id(0)
    p = pl.program_id(1)
    pad = 8
    at_start = (i * tb) % seq == 0
    at_end = ((i + 1) * tb) % seq == 0
    xe_scr[0:pad, :] = jnp.where(at_start, 0.0, prev_ref[...])
    xe_scr[pad:pad + tb, :] = x_ref[...]
    xe_scr[pad + tb:pad + tb + pad, :] = jnp.where(at_end, 0.0, next_ref[...])
    acc = None
    for t in range(CONV_K):
        term = xe_scr[pl.ds(pad + t - CONV_K // 2, tb), :] * w_ref[t:t + 1, :]
        acc = term if acc is None else acc + term
    y = _silu(acc)
    outs = []
    for h in range(HEADS):
        yh = y[:, h * DH:(h + 1) * DH]
        outs.append(yh * lax.rsqrt(jnp.sum(yh * yh, axis=-1, keepdims=True) + EPS))
    yn = jnp.concatenate(outs, axis=-1)
    qscale = jnp.where(p == 0, DH ** -0.5, 1.0)
    o_ref[...] = jnp.where(p == 2, y, yn * qscale)


def _gdn_prep(z, conv_w, seq):
    m = z.shape[0]
    tb = 256
    nb8 = m // 8
    return pl.pallas_call(
        functools.partial(_gdn_prep_kernel, tb=tb, seq=seq),
        grid=(m // tb, 3),
        in_specs=[pl.BlockSpec((tb, BRANCH_W), lambda i, p: (i, CB_GQKV + p)),
                  pl.BlockSpec((8, BRANCH_W), lambda i, p: (jnp.maximum(i * (tb // 8) - 1, 0), CB_GQKV + p)),
                  pl.BlockSpec((8, BRANCH_W), lambda i, p: (jnp.minimum((i + 1) * (tb // 8), nb8 - 1), CB_GQKV + p)),
                  pl.BlockSpec((CONV_K, BRANCH_W), lambda i, p: (0, p))],
        out_specs=pl.BlockSpec((tb, BRANCH_W), lambda i, p: (i, p)),
        out_shape=jax.ShapeDtypeStruct((m, 3 * BRANCH_W), F32),
        scratch_shapes=[pltpu.VMEM((tb + 16, BRANCH_W), F32)],
        name="gdn_prep",
        compiler_params=_cparams(("arbitrary", "arbitrary")),
    )(z, z, z, conv_w)


def _unit_tri_inverse(x):
    r = lax.broadcasted_iota(jnp.int32, (CHUNK, CHUNK), 0)
    c = lax.broadcasted_iota(jnp.int32, (CHUNK, CHUNK), 1)
    eye = (r == c).astype(F32)
    same = lambda s: (r >> s) == (c >> s)
    x0 = jnp.where(same(3), x, 0.0)
    x2 = _dot_hp(x0, x0)
    p = eye + x0 + x2 + _dot_hp(x0, x2)
    x4 = _dot_hp(x2, x2)
    t = p + _dot_hp(p, x4)
    for s in (4, 5, 6):
        off = jnp.where(jnp.logical_and(same(s), jnp.logical_not(same(s - 1))), x, 0.0)
        t = t + _dot_hp(_dot_hp(t, off), t)
    return t


def _gdn_kernel(*refs, nch, zero_init, emit_state):
    q_ref, k_ref, v_ref, g_ref, bias_ref, alog_ref = refs[:6]
    pos = 6
    if not zero_init:
        s0_ref = refs[pos]
        pos += 1
    o_ref = refs[pos]
    pos += 1
    if emit_state:
        so_ref = refs[pos]
        pos += 1
    s_scr = refs[pos]

    d = pl.program_id(1)
    j = pl.program_id(2)

    @pl.when(j == 0)
    def _():
        if zero_init:
            s_scr[...] = jnp.zeros_like(s_scr)
        else:
            s_scr[...] = s0_ref[0, 0]

    mask, strict = _chunk_masks(d)
    tri = mask.astype(BF16)
    sel = _gate_selector()
    bias = bias_ref[0]
    neg_a = -jnp.exp(alog_ref[0])
    g0 = 2 * HEADS

    def chunk(c, carry):
        r0 = pl.multiple_of(_dir_chunk(c, d, nch) * CHUNK, CHUNK)
        rows = pl.ds(r0, CHUNK)
        pre = g_ref[rows, :] + bias
        g_all = neg_a * _softplus(pre)
        beta_all = _sigmoid(pre)
        gc = _sel_dot(tri, g_all)
        gtot = jnp.sum(g_all, axis=0, keepdims=True)
        gt = _sel_dot(sel, gc, nt=True)
        for h in range(HEADS):
            cols = slice(h * DH, (h + 1) * DH)
            q = q_ref[rows, cols]
            k = k_ref[rows, cols]
            v = v_ref[rows, cols]
            g_col = gc[:, g0 + h:g0 + h + 1]
            g_row = gt[g0 + h:g0 + h + 1, :]
            beta = beta_all[:, g0 + HEADS + h:g0 + HEADS + h + 1]
            g_last = gtot[:, g0 + h:g0 + h + 1]
            decay = jnp.exp(jnp.where(mask, g_col - g_row, -jnp.inf))
            eg = jnp.exp(g_col)
            kb = k * beta
            lmat = jnp.where(strict, _dot_nt(kb, k) * decay, 0.0)
            tinv = _unit_tri_inverse(-lmat)
            uw = _dot(tinv, jnp.concatenate([v * beta, kb * eg], axis=-1))
            attn = _dot_nt(q, k) * decay
            st = s_scr[h]
            v_new = uw[:, :DH] - _dot(uw[:, DH:], st)
            o_ref[0, rows, cols] = _dot(q * eg, st) + _dot(attn, v_new)
            s_scr[h] = st * jnp.exp(g_last) + _dot_tn(k * jnp.exp(g_last - g_col), v_new)
        return carry

    lax.fori_loop(0, nch, chunk, 0)

    if emit_state:
        @pl.when(j == pl.num_programs(2) - 1)
        def _():
            so_ref[0, 0] = s_scr[...]


def _gdn(qkv, zs, bias, alog, batch, seq, tb, s0):
    m = batch * seq
    nt = seq // tb
    zero_init = s0 is None
    emit_state = s0 is None
    rb = _mixer_rowblk(nt)
    spec = lambda cb: pl.BlockSpec((tb, BRANCH_W), lambda b, d, j: (rb(b, d, j), cb))
    st = pl.BlockSpec((1, 1, HEADS, DH, DH), lambda b, d, j: (b, d, 0, 0, 0))
    dirrow = pl.BlockSpec((1, 1, GATE_W), lambda b, d, j: (d, 0, 0))
    in_specs = [spec(0), spec(1), spec(2),
                pl.BlockSpec((tb, GATE_W), lambda b, d, j: (rb(b, d, j), d)), dirrow, dirrow]
    args = [qkv, qkv, qkv, zs, bias, alog]
    if not zero_init:
        in_specs.append(st)
        args.append(s0)
    out_specs = [pl.BlockSpec((1, tb, BRANCH_W), lambda b, d, j: (d, rb(b, d, j), 0))]
    out_shape = [jax.ShapeDtypeStruct((N_DIR, m, BRANCH_W), F32)]
    if emit_state:
        out_specs.append(st)
        out_shape.append(jax.ShapeDtypeStruct((batch, N_DIR, HEADS, DH, DH), F32))
    return pl.pallas_call(
        functools.partial(_gdn_kernel, nch=tb // CHUNK, zero_init=zero_init, emit_state=emit_state),
        grid=(batch, N_DIR, nt),
        in_specs=in_specs, out_specs=out_specs, out_shape=out_shape,
        scratch_shapes=[pltpu.VMEM((HEADS, DH, DH), F32)],
        name="gdn",
        compiler_params=_cparams(("arbitrary", "arbitrary", "arbitrary")),
    )(*args)


def _s5_kernel(*refs, rb, seg, chain, zero_init, emit_state):
    u_ref, b_ref, c_ref, lre_ref, lim_ref, ls_ref = refs[:6]
    pos = 6
    if not zero_init:
        h0r_ref, h0i_ref = refs[pos:pos + 2]
        pos += 2
    y_ref = refs[pos]
    pos += 1
    if emit_state:
        sr_ref, si_ref = refs[pos:pos + 2]
        pos += 2
    xr, xi, er, ei, pr, pi_, car_r, car_i = refs[pos:pos + 8]

    d = pl.program_id(1)
    j = pl.program_id(3)
    w = S5_CBW
    nsub = S5_SUBLANES

    lre = lre_ref[0, 0]
    lim = lim_ref[0, 0]
    dt = jnp.exp(ls_ref[0, 0])
    mag = jnp.exp(lre * dt)
    ar = mag * jnp.cos(lim * dt)
    ai = mag * jnp.sin(lim * dt)
    den = lre * lre + lim * lim
    nr = ar - 1.0
    zr = (nr * lre + ai * lim) / den
    zi = (ai * lre - nr * lim) / den

    rc = 256
    nlt = w // 128

    def put(ref, rows, val):
        for t in range(nlt):
            ref[t, rows, :] = val[:, t * 128:(t + 1) * 128]

    def get(ref, rows):
        return jnp.concatenate([ref[t, rows, :] for t in range(nlt)], axis=-1)

    def fill(c, carry):
        rows = pl.ds(pl.multiple_of(c * rc, rc), rc)
        bu = _dot(u_ref[rows, :], b_ref[0])
        bre = bu[:, :w]
        bim = bu[:, w:]
        put(xr, rows, zr * bre - zi * bim)
        put(xi, rows, zr * bim + zi * bre)
        return carry

    lax.fori_loop(0, rb // rc, fill, 0)

    arb = jnp.broadcast_to(ar, (nsub, w))
    aib = jnp.broadcast_to(ai, (nsub, w))

    def scan_step(tt, carry):
        hr, hi = carry
        idx = pl.ds(_dir_chunk(tt, d, seg), nsub, stride=seg)
        nhr = arb * hr - aib * hi + get(xr, idx)
        nhi = arb * hi + aib * hr + get(xi, idx)
        put(xr, idx, nhr)
        put(xi, idx, nhi)
        return nhr, nhi

    init = (jnp.zeros((nsub, w), F32), jnp.zeros((nsub, w), F32))
    hr, hi = lax.fori_loop(0, seg, scan_step, init)

    if chain:
        @pl.when(j == 0)
        def _():
            if zero_init:
                car_r[...] = jnp.zeros_like(car_r)
                car_i[...] = jnp.zeros_like(car_i)
            else:
                car_r[...] = h0r_ref[0, 0, 0]
                car_i[...] = h0i_ref[0, 0, 0]

        er[...] = hr
        ei[...] = hi
        sr, si = ar, ai
        for _ in range(int(math.log2(seg))):
            sr, si = sr * sr - si * si, 2.0 * sr * si
        fr = car_r[...]
        fi = car_i[...]
        for k in range(nsub):
            row = pl.ds(_dir_chunk(k, d, nsub), 1)
            pr[row, :] = fr
            pi_[row, :] = fi
            fr, fi = er[row, :] + (sr * fr - si * fi), ei[row, :] + (sr * fi + si * fr)
        car_r[...] = fr
        car_i[...] = fi

        def fix_step(tt, carry):
            cr, ci = carry
            idx = pl.ds(_dir_chunk(tt, d, seg), nsub, stride=seg)
            ncr = arb * cr - aib * ci
            nci = arb * ci + aib * cr
            put(xr, idx, get(xr, idx) + ncr)
            put(xi, idx, get(xi, idx) + nci)
            return ncr, nci

        lax.fori_loop(0, seg, fix_step, (pr[...], pi_[...]))

    if emit_state:
        sr_ref[0, 0] = hr
        si_ref[0, 0] = hi

    def proj(c, carry):
        rows = pl.ds(pl.multiple_of(c * rc, rc), rc)
        y_ref[0, rows, :] = _dot(get(xr, rows), c_ref[0, :w, :]) + _dot(get(xi, rows), c_ref[0, w:, :])
        return carry

    lax.fori_loop(0, rb // rc, proj, 0)


def _s5(z, bbd, cbd, lre, lim, ls, batch, seq, h0):
    m = batch * seq
    nsub = S5_SUBLANES
    chain = h0 is not None
    zero_init = h0 is None
    emit_state = h0 is None
    if chain:
        seg = 256
        rb = nsub * seg
        ng, nt = batch, seq // rb
        st_spec = pl.BlockSpec((1, 1, 1, 1, S5_CBW), lambda g, d, cb, j: (d, cb, g, 0, 0))
    else:
        seg = seq
        rb = nsub * seg
        ng, nt = batch // nsub, 1
        st_spec = pl.BlockSpec((1, 1, nsub, S5_CBW), lambda g, d, cb, j: (d, cb, g, 0))
    rowblk = lambda g, d, j: g * nt + _dir_chunk(j, d, nt)
    lam_spec = pl.BlockSpec((1, 1, 1, S5_CBW), lambda g, d, cb, j: (d, cb, 0, 0))
    in_specs = [pl.BlockSpec((rb, 128), lambda g, d, cb, j: (rowblk(g, d, j), CB_SU * 4 + cb)),
                pl.BlockSpec((1, 128, 2 * S5_CBW), lambda g, d, cb, j: (cb, 0, 0)),
                pl.BlockSpec((1, 2 * S5_CBW, 128), lambda g, d, cb, j: (cb, 0, 0)),
                lam_spec, lam_spec, lam_spec]
    args = [z, bbd, cbd, lre, lim, ls]
    if not zero_init:
        in_specs += [st_spec, st_spec]
        args += list(h0)
    out_specs = [pl.BlockSpec((1, rb, 128), lambda g, d, cb, j: (d, rowblk(g, d, j), cb))]
    out_shape = [jax.ShapeDtypeStruct((N_DIR, m, BRANCH_W), F32)]
    if emit_state:
        out_specs += [st_spec, st_spec]
        out_shape += [jax.ShapeDtypeStruct((N_DIR, S5_CB, batch, S5_CBW), F32)] * 2
    vrow = lambda n: pltpu.VMEM((n, S5_CBW), F32)
    tiles = pltpu.VMEM((S5_CBW // 128, rb, 128), F32)
    return pl.pallas_call(
        functools.partial(_s5_kernel, rb=rb, seg=seg, chain=chain, zero_init=zero_init, emit_state=emit_state),
        grid=(ng, N_DIR, S5_CB, nt),
        in_specs=in_specs, out_specs=out_specs, out_shape=out_shape,
        scratch_shapes=[tiles, tiles, vrow(nsub), vrow(nsub), vrow(nsub), vrow(nsub), vrow(1), vrow(1)],
        name="s5",
        compiler_params=_cparams(("arbitrary",) * 4),
    )(*args)


def _merge_kernel(x_ref, mod_ref, gates_ref, o_ref, u_ref, gz_ref, hm_ref, ys_ref, go_ref,
                  mlw_ref, s5d_ref, gluw_ref, glub_ref, gdw_ref, wbr_ref, wout_ref, out_ref):
    ya = _sigmoid(o_ref[...]) * _head_rms(hm_ref[0] + hm_ref[1], mlw_ref[...])
    y5 = _gelu_tanh(ys_ref[0] + ys_ref[1] + s5d_ref[...] * u_ref[...])
    yb = y5 * _sigmoid(_dot(y5, gluw_ref[...]) + glub_ref[...])
    yc = _head_rms(go_ref[0] + go_ref[1], gdw_ref[...]) * _silu(gz_ref[...])
    merged = None
    for n, y in enumerate((ya, yb, yc)):
        term = _sigmoid(gates_ref[:, n * D_MODEL:(n + 1) * D_MODEL]) * _dot(y, wbr_ref[n])
        merged = term if merged is None else merged + term
    out_ref[...] = x_ref[...] + mod_ref[0, 2:3, :] * _dot(merged, wout_ref[...])


def _merge(x, mod, z, hm, ys, go, p, seq):
    m = x.shape[0]
    tm = 256
    rows_per_mod = seq if mod.shape[0] > 1 else m
    row = lambda i: (i, 0)
    zspec = lambda cb: pl.BlockSpec((tm, BRANCH_W), lambda i: (i, cb))
    dspec = pl.BlockSpec((N_DIR, tm, BRANCH_W), lambda i: (0, i, 0))
    full = lambda a: pl.BlockSpec(a.shape, lambda i: (0,) * a.ndim)
    consts = [p['ml_norm_w'], p['s5_D'], p['s5_glu_w'], p['s5_glu_b'], p['gd_norm_w'], p['w_branch'], p['w_out']]
    return pl.pallas_call(
        _merge_kernel,
        grid=(m // tm,),
        in_specs=[pl.BlockSpec((tm, D_MODEL), row),
                  pl.BlockSpec((1, N_MOD, D_MODEL), lambda i: (i * tm // rows_per_mod, 0, 0)),
                  pl.BlockSpec((tm, 3 * D_MODEL), row),
                  zspec(CB_O), zspec(CB_SU), zspec(CB_GZ), dspec, dspec, dspec] + [full(a) for a in consts],
        out_specs=pl.BlockSpec((tm, D_MODEL), row),
        out_shape=jax.ShapeDtypeStruct((m, D_MODEL), F32),
        name="merge",
        compiler_params=_cparams(("arbitrary",)),
    )(x, mod, z, z, z, z, hm, ys, go, *consts)


def _ffn_kernel(x_ref, mod_ref, nw_ref, wg_ref, wu_ref, wd_ref, fw_ref, out_ref, h_scr, acc_scr, *, final):
    jf = pl.program_id(1)

    @pl.when(jf == 0)
    def _():
        x = x_ref[...]
        y = x * lax.rsqrt(jnp.mean(x * x, axis=-1, keepdims=True) + EPS)
        h_scr[...] = ((y * nw_ref[...]) * (1.0 + mod_ref[0, 4:5, :]) + mod_ref[0, 3:4, :]).astype(BF16)
        acc_scr[...] = jnp.zeros_like(acc_scr)

    hb = h_scr[...]
    a = _silu(jnp.dot(hb, wg_ref[...], preferred_element_type=F32))
    b = jnp.dot(hb, wu_ref[...], preferred_element_type=F32)
    acc_scr[...] += _dot(a * b, wd_ref[...])

    @pl.when(jf == pl.num_programs(1) - 1)
    def _():
        x = x_ref[...] + mod_ref[0, 5:6, :] * acc_scr[...]
        if final:
            x = x * lax.rsqrt(jnp.mean(x * x, axis=-1, keepdims=True) + EPS) * fw_ref[...]
        out_ref[...] = x


def _ffn(x, mod, p, seq, final_w, final):
    m = x.shape[0]
    tm, tf = 512, D_FF // 2
    rows_per_mod = seq if mod.shape[0] > 1 else m
    row = lambda i, jf: (i, 0)
    vec = pl.BlockSpec((1, D_MODEL), lambda i, jf: (0, 0))
    return pl.pallas_call(
        functools.partial(_ffn_kernel, final=final),
        grid=(m // tm, D_FF // tf),
        in_specs=[pl.BlockSpec((tm, D_MODEL), row),
                  pl.BlockSpec((1, N_MOD, D_MODEL), lambda i, jf: (i * tm // rows_per_mod, 0, 0)),
                  vec,
                  pl.BlockSpec((D_MODEL, tf), lambda i, jf: (0, jf)),
                  pl.BlockSpec((D_MODEL, tf), lambda i, jf: (0, jf)),
                  pl.BlockSpec((tf, D_MODEL), lambda i, jf: (jf, 0)),
                  vec],
        out_specs=pl.BlockSpec((tm, D_MODEL), row),
        out_shape=jax.ShapeDtypeStruct((m, D_MODEL), F32),
        scratch_shapes=[pltpu.VMEM((tm, D_MODEL), BF16), pltpu.VMEM((tm, D_MODEL), F32)],
        name="ffn",
        compiler_params=_cparams(("arbitrary", "arbitrary")),
    )(x, mod, p['norm2_w'], p['w_gate'], p['w_up'], p['w_down'], final_w)


def _gate_lanes(parts):
    row = jnp.concatenate([a.astype(F32) for a in parts], axis=-1)
    return jnp.pad(row, ((0, 0), (0, GATE_W - row.shape[-1]))).reshape(N_DIR, 1, GATE_W)


def _block_diag(a):
    cb, g, r, c = a.shape
    eye = jnp.eye(g, dtype=a.dtype)
    return jnp.einsum('bgrc,gh->bgrhc', a, eye).reshape(cb, g * r, g * c)


def _layer_params(l, w):
    gpb = S5_GROUPS // S5_CB
    idx, acc = [], 0
    for size in IN_SIZES[:-1]:
        acc += size
        idx.append(acc)
    mq, mk, mv, mo, mi, mf, su, gqkv, gz, ga, gb, gates = jnp.split(w['w_in'][l], idx, axis=-1)
    w_big = jnp.concatenate([gates, mq, mk, mv, mo, su, gqkv, gz], axis=-1).astype(BF16)
    smalls = []
    for d in range(N_DIR):
        sl = slice(d * HEADS, (d + 1) * HEADS)
        blk = jnp.concatenate([mi[:, sl], mf[:, sl], ga[:, sl], gb[:, sl]], axis=-1)
        smalls.append(jnp.pad(blk, ((0, 0), (0, GATE_W - 4 * HEADS))))
    w_small = jnp.concatenate(smalls, axis=-1).astype(BF16)
    zeros = jnp.zeros((N_DIR, HEADS), F32)
    gate_bias = _gate_lanes([w['ml_i_bias'][l], w['ml_f_bias'][l], w['gd_dt_bias'][l], zeros])
    gate_alog = _gate_lanes([zeros, zeros, w['gd_A_log'][l], zeros])
    b_re = jnp.swapaxes(w['s5_B_re'][l], 1, 2).reshape(S5_CB, gpb, S5_GROUP, S5_STATE)
    b_im = jnp.swapaxes(w['s5_B_im'][l], 1, 2).reshape(S5_CB, gpb, S5_GROUP, S5_STATE)
    bbd = jnp.concatenate([_block_diag(b_re), _block_diag(b_im)], axis=-1).astype(BF16)
    c_re = jnp.swapaxes(w['s5_C_re'][l], 1, 2).reshape(S5_CB, gpb, S5_STATE, S5_GROUP)
    c_im = jnp.swapaxes(w['s5_C_im'][l], 1, 2).reshape(S5_CB, gpb, S5_STATE, S5_GROUP)
    cbd = jnp.concatenate([_block_diag(c_re), -_block_diag(c_im)], axis=1).astype(BF16)
    lam_shape = (N_DIR, S5_CB, 1, S5_CBW)
    ls = jnp.broadcast_to(w['s5_log_step'][l][:, :, None], (N_DIR, S5_GROUPS, S5_STATE))
    return dict(
        norm1_w=w['norm1_w'][l], w_big=w_big, w_small=w_small, gate_bias=gate_bias, gate_alog=gate_alog,
        ml_norm_w=w['ml_norm_w'][l].reshape(1, BRANCH_W),
        bbd=bbd, cbd=cbd,
        lam_re=w['s5_lam_re'][l].reshape(lam_shape), lam_im=w['s5_lam_im'][l].reshape(lam_shape),
        log_step=ls.reshape(lam_shape),
        s5_D=w['s5_D'][l].reshape(1, BRANCH_W), s5_glu_w=w['s5_glu_w'][l].astype(BF16),
        s5_glu_b=w['s5_glu_b'][l].reshape(1, BRANCH_W),
        gd_conv_w=w['gd_conv_w'][l],
        gd_norm_w=jnp.tile(w['gd_norm_w'][l], HEADS).reshape(1, BRANCH_W),
        w_branch=w['w_branch'][l].astype(BF16), w_out=w['w_out'][l].astype(BF16),
        norm2_w=w['norm2_w'][l].reshape(1, D_MODEL),
        w_gate=w['w_gate'][l].astype(BF16), w_up=w['w_up'][l].astype(BF16), w_down=w['w_down'][l].astype(BF16),
    )


def _grid_pos_embed(n_tok):
    grid_w = 64
    t = jnp.arange(n_tok)
    quarter = D_MODEL // 4
    omega = 1.0 / (10000.0 ** (jnp.arange(quarter, dtype=F32) / quarter))

    def enc(pos):
        ang = pos.astype(F32)[:, None] * omega[None, :]
        return jnp.concatenate([jnp.sin(ang), jnp.cos(ang)], axis=-1)

    return jnp.concatenate([enc(t // grid_w), enc(t % grid_w)], axis=-1)


def _trunk_layer(x, mod, p, batch, seq, tb, states, final_w, final, pe=None):
    res = _inproj(x, mod, p['norm1_w'], p['w_big'], p['w_small'], seq, pe)
    if pe is not None:
        z, zs, x = res
    else:
        z, zs = res
    if states is None:
        ml_st = s5_st = gd_st = None
    else:
        ml_st, s5_st, gd_st = states
    ml = _mlstm(z, zs, p['gate_bias'], batch, seq, tb, ml_st)
    s5 = _s5(z, p['bbd'], p['cbd'], p['lam_re'], p['lam_im'], p['log_step'], batch, seq, s5_st)
    qkv = _gdn_prep(z, p['gd_conv_w'], seq)
    gd = _gdn(qkv, zs, p['gate_bias'], p['gate_alog'], batch, seq, tb, gd_st)
    x = _merge(x, mod, z, ml[0], s5[0], gd[0], p, seq)
    x = _ffn(x, mod, p, seq, final_w, final)
    new_states = None
    if states is None:
        new_states = (ml[1], ml[2][:, :, :, 0, :], ml[3][:, :, :, 0, 0],
                      jnp.transpose(s5[1], (2, 0, 1, 3)).reshape(batch, N_DIR, S5_GROUPS, S5_STATE),
                      jnp.transpose(s5[2], (2, 0, 1, 3)).reshape(batch, N_DIR, S5_GROUPS, S5_STATE),
                      gd[1])
    return x, new_states


def kernel(x_prompt, x_sample, state_mlstm_C, state_mlstm_n, state_mlstm_m, state_s5_re, state_s5_im,
           state_gdn_S, c, c_ctx, ada_w, ada_b, norm1_w, w_in, ml_i_bias, ml_f_bias, ml_norm_w,
           s5_lam_re, s5_lam_im, s5_log_step, s5_B_re, s5_B_im, s5_C_re, s5_C_im, s5_D, s5_glu_w, s5_glu_b,
           gd_conv_w, gd_A_log, gd_dt_bias, gd_norm_w, w_branch, w_out, norm2_w, w_gate, w_up, w_down,
           final_norm_w):
    w = dict(norm1_w=norm1_w, w_in=w_in, ml_i_bias=ml_i_bias, ml_f_bias=ml_f_bias, ml_norm_w=ml_norm_w,
             s5_lam_re=s5_lam_re, s5_lam_im=s5_lam_im, s5_log_step=s5_log_step, s5_B_re=s5_B_re,
             s5_B_im=s5_B_im, s5_C_re=s5_C_re, s5_C_im=s5_C_im, s5_D=s5_D, s5_glu_w=s5_glu_w,
             s5_glu_b=s5_glu_b, gd_conv_w=gd_conv_w, gd_A_log=gd_A_log, gd_dt_bias=gd_dt_bias,
             gd_norm_w=gd_norm_w, w_branch=w_branch, w_out=w_out, norm2_w=norm2_w, w_gate=w_gate,
             w_up=w_up, w_down=w_down)
    bp, sp, _ = x_prompt.shape
    bs, ss, _ = x_sample.shape
    params = [_layer_params(l, w) for l in range(DEPTH)]
    final_w = final_norm_w.reshape(1, D_MODEL)

    cc = jnp.concatenate([c_ctx[None, :], c, jnp.zeros((8 - 1 - bs, D_MODEL), F32)], axis=0)
    mods = _modulation(cc, ada_w, ada_b).reshape(DEPTH, 8, N_MOD, D_MODEL)

    xp = x_prompt.reshape(bp * sp, D_MODEL)
    per_layer = []
    for l in range(DEPTH):
        xp, st = _trunk_layer(xp, mods[l, 0:1], params[l], bp, sp, sp, None, final_w, l == DEPTH - 1)
        per_layer.append(st)
    y_prompt = xp.reshape(bp, sp, D_MODEL)
    new_states = [jnp.stack([st[i] for st in per_layer], axis=1) for i in range(6)]

    xs = x_sample.reshape(bs * ss, D_MODEL)
    pe = _grid_pos_embed(ss)
    for l in range(DEPTH):
        ml_st = (state_mlstm_C[:, l],
                 state_mlstm_n[:, l][:, :, :, None, :],
                 jnp.broadcast_to(state_mlstm_m[:, l][:, :, :, None, None], (bs, N_DIR, HEADS, 1, DH)))
        s5_st = tuple(jnp.transpose(a[:, l].reshape(bs, N_DIR, S5_CB, 1, S5_CBW), (1, 2, 0, 3, 4))
                      for a in (state_s5_re, state_s5_im))
        xs, _ = _trunk_layer(xs, mods[l, 1:1 + bs], params[l], bs, ss, 512, (ml_st, s5_st, state_gdn_S[:, l]),
                             final_w, l == DEPTH - 1, pe if l == 0 else None)
    y_sample = xs.reshape(bs, ss, D_MODEL)
    return (y_prompt, y_sample, *new_states)
```

```python
import functools
import math

import jax
import jax.numpy as jnp
from jax import lax
from jax.experimental import pallas as pl
from jax.experimental.pallas import tpu as pltpu

F32 = jnp.float32
BF16 = jnp.bfloat16

D_MODEL = 1024
DEPTH = 2
N_DIR = 2
N_MOD = 6
EPS = 1e-6
HEADS = 4
DH = 128
BRANCH_W = HEADS * DH
CHUNK = 64
S5_GROUPS = 32
S5_GROUP = 16
S5_STATE = 64
S5_CB = 4
S5_CBW = S5_GROUPS * S5_STATE // S5_CB
S5_SUBLANES = 8
CONV_K = 5
D_FF = -(-8 * D_MODEL // (3 * 256)) * 256
IN_SIZES = (512, 512, 512, 512, 8, 8, 512, 1536, 512, 8, 8, 3072)
GATE_W = 128

CB_GATES, CB_Q, CB_K, CB_V, CB_O, CB_SU, CB_GQKV, CB_GZ = 0, 6, 7, 8, 9, 10, 11, 14
Z_W = 15 * BRANCH_W

VMEM_LIMIT = 56 * 1024 * 1024


def _cparams(sem):
    return pltpu.CompilerParams(dimension_semantics=sem, vmem_limit_bytes=VMEM_LIMIT)


def _dot(a, b):
    return jnp.dot(a.astype(BF16), b.astype(BF16), preferred_element_type=F32)


def _dot_nt(a, b):
    return lax.dot_general(a.astype(BF16), b.astype(BF16), (((1,), (1,)), ((), ())),
                           preferred_element_type=F32)


def _dot_tn(a, b):
    return lax.dot_general(a.astype(BF16), b.astype(BF16), (((0,), (0,)), ((), ())),
                           preferred_element_type=F32)


def _split3(x):
    hi = x.astype(BF16)
    r1 = x - hi.astype(F32)
    mid = r1.astype(BF16)
    lo = (r1 - mid.astype(F32)).astype(BF16)
    return hi, mid, lo


def _sel_dot(sel, x, nt=False):
    dims = (((1,), (1,)), ((), ())) if nt else (((1,), (0,)), ((), ()))
    hi, mid, lo = _split3(x)
    f = lambda p: lax.dot_general(sel, p, dims, preferred_element_type=F32)
    return (f(hi) + f(mid)) + f(lo)


def _dot_hp(a, b):
    ah = a.astype(BF16)
    al = (a - ah.astype(F32)).astype(BF16)
    bh = b.astype(BF16)
    bl = (b - bh.astype(F32)).astype(BF16)
    f = lambda p, q: jnp.dot(p, q, preferred_element_type=F32)
    return f(ah, bh) + (f(ah, bl) + f(al, bh))


def _sigmoid(x):
    return 1.0 / (1.0 + jnp.exp(-x))


def _silu(x):
    return x * _sigmoid(x)


def _softplus(x):
    return jnp.maximum(x, 0.0) + jnp.log(1.0 + jnp.exp(-jnp.abs(x)))


def _gelu_tanh(x):
    c = math.sqrt(2.0 / math.pi)
    return 0.5 * x * (1.0 + jnp.tanh(c * (x + 0.044715 * (x * x * x))))


def _head_rms(x, w_row):
    outs = []
    for h in range(HEADS):
        xh = x[:, h * DH:(h + 1) * DH]
        outs.append(xh * lax.rsqrt(jnp.mean(xh * xh, axis=-1, keepdims=True) + EPS))
    return jnp.concatenate(outs, axis=-1) * w_row


def _dir_chunk(c, d, n):
    return c + d * (n - 1 - 2 * c)


def _mod_kernel(c_ref, w_ref, b_ref, o_ref):
    o_ref[0] = _dot(_silu(c_ref[...]), w_ref[0]) + b_ref[0]


def _modulation(cc, ada_w, ada_b):
    tn = 1536
    nmod = N_MOD * D_MODEL
    return pl.pallas_call(
        _mod_kernel,
        grid=(DEPTH, nmod // tn),
        in_specs=[pl.BlockSpec((8, D_MODEL), lambda l, j: (0, 0)),
                  pl.BlockSpec((1, D_MODEL, tn), lambda l, j: (l, 0, j)),
                  pl.BlockSpec((1, 1, tn), lambda l, j: (l, 0, j))],
        out_specs=pl.BlockSpec((1, 8, tn), lambda l, j: (l, 0, j)),
        out_shape=jax.ShapeDtypeStruct((DEPTH, 8, nmod), F32),
        name="adaln_mod",
        compiler_params=_cparams(("arbitrary", "arbitrary")),
    )(cc, ada_w, ada_b.reshape(DEPTH, 1, nmod))


def _inproj_kernel(*refs, has_pe):
    if has_pe:
        x_ref, pe_ref, mod_ref, nw_ref, w_ref, ws_ref, z_ref, zs_ref, xs_ref, hn_scr = refs
    else:
        x_ref, mod_ref, nw_ref, w_ref, ws_ref, z_ref, zs_ref, hn_scr = refs

    @pl.when(pl.program_id(1) == 0)
    def _():
        x = x_ref[...]
        if has_pe:
            x = x + pe_ref[...]
            xs_ref[...] = x
        y = x * lax.rsqrt(jnp.mean(x * x, axis=-1, keepdims=True) + EPS)
        h = (y * nw_ref[...]) * (1.0 + mod_ref[0, 1:2, :]) + mod_ref[0, 0:1, :]
        hb = h.astype(BF16)
        hn_scr[...] = hb
        zs_ref[...] = jnp.dot(hb, ws_ref[...], preferred_element_type=F32)

    z_ref[...] = jnp.dot(hn_scr[...], w_ref[...], preferred_element_type=F32)


def _inproj(x, mod, norm_w, w_big, w_small, seq, pe=None):
    m = x.shape[0]
    tm, tn = 512, 1536
    rows_per_mod = seq if mod.shape[0] > 1 else m
    has_pe = pe is not None
    row = lambda i, j: (i, 0)
    in_specs = [pl.BlockSpec((tm, D_MODEL), row)]
    args = [x]
    if has_pe:
        in_specs.append(pl.BlockSpec((tm, D_MODEL), lambda i, j: (i % (seq // tm), 0)))
        args.append(pe)
    in_specs += [pl.BlockSpec((1, N_MOD, D_MODEL), lambda i, j: (i * tm // rows_per_mod, 0, 0)),
                 pl.BlockSpec((1, D_MODEL), lambda i, j: (0, 0)),
                 pl.BlockSpec((D_MODEL, tn), lambda i, j: (0, j)),
                 pl.BlockSpec((D_MODEL, N_DIR * GATE_W), lambda i, j: (0, 0))]
    args += [mod, norm_w.reshape(1, D_MODEL), w_big, w_small]
    out_specs = [pl.BlockSpec((tm, tn), lambda i, j: (i, j)),
                 pl.BlockSpec((tm, N_DIR * GATE_W), row)]
    out_shape = [jax.ShapeDtypeStruct((m, Z_W), F32),
                 jax.ShapeDtypeStruct((m, N_DIR * GATE_W), F32)]
    if has_pe:
        out_specs.append(pl.BlockSpec((tm, D_MODEL), row))
        out_shape.append(jax.ShapeDtypeStruct((m, D_MODEL), F32))
    return pl.pallas_call(
        functools.partial(_inproj_kernel, has_pe=has_pe),
        grid=(m // tm, Z_W // tn),
        in_specs=in_specs, out_specs=out_specs, out_shape=out_shape,
        scratch_shapes=[pltpu.VMEM((tm, D_MODEL), BF16)],
        name="inproj",
        compiler_params=_cparams(("arbitrary", "arbitrary")),
    )(*args)


def _mixer_rowblk(nt):
    return lambda b, d, j: b * nt + _dir_chunk(j, d, nt)


def _chunk_masks(d):
    r = lax.broadcasted_iota(jnp.int32, (CHUNK, CHUNK), 0)
    c = lax.broadcasted_iota(jnp.int32, (CHUNK, CHUNK), 1)
    diff = (r - c) * (1 - 2 * d)
    return diff >= 0, diff > 0


def _gate_selector():
    r = lax.broadcasted_iota(jnp.int32, (16, GATE_W), 0)
    c = lax.broadcasted_iota(jnp.int32, (16, GATE_W), 1)
    return (r == c).astype(BF16)


def _mlstm_kernel(*refs, nch, zero_init, emit_state):
    q_ref, k_ref, v_ref, g_ref, bias_ref = refs[:5]
    pos = 5
    if not zero_init:
        c0_ref, n0_ref, m0_ref = refs[pos:pos + 3]
        pos += 3
    h_ref = refs[pos]
    pos += 1
    if emit_state:
        co_ref, no_ref, mo_ref = refs[pos:pos + 3]
        pos += 3
    c_scr, n_scr, m_scr = refs[pos:pos + 3]

    d = pl.program_id(1)
    j = pl.program_id(2)

    @pl.when(j == 0)
    def _():
        if zero_init:
            c_scr[...] = jnp.zeros_like(c_scr)
            n_scr[...] = jnp.zeros_like(n_scr)
            m_scr[...] = jnp.zeros_like(m_scr)
        else:
            c_scr[...] = c0_ref[0, 0]
            n_scr[...] = n0_ref[0, 0]
            m_scr[...] = m0_ref[0, 0]

    mask, _ = _chunk_masks(d)
    tri = mask.astype(BF16)
    sel = _gate_selector()
    lane = lax.broadcasted_iota(jnp.int32, (CHUNK, GATE_W), 1)
    rowid = lax.broadcasted_iota(jnp.int32, (CHUNK, 1), 0)
    last = (CHUNK - 1) * (1 - d)
    bias = bias_ref[0]
    scale = DH ** -0.5

    def chunk(c, carry):
        r0 = pl.multiple_of(_dir_chunk(c, d, nch) * CHUNK, CHUNK)
        rows = pl.ds(r0, CHUNK)
        pre = g_ref[rows, :] + bias
        lf = -_softplus(-pre)
        cum = _sel_dot(tri, lf)
        tot = jnp.sum(lf, axis=0, keepdims=True)
        xt = _sel_dot(sel, jnp.where(lane < HEADS, pre, cum), nt=True)
        for h in range(HEADS):
            cols = slice(h * DH, (h + 1) * DH)
            q = q_ref[rows, cols]
            k = k_ref[rows, cols] * scale
            v = v_ref[rows, cols]
            b_col = cum[:, HEADS + h:HEADS + h + 1]
            i_col = pre[:, h:h + 1]
            b_row = xt[HEADS + h:HEADS + h + 1, :]
            i_row = xt[h:h + 1, :]
            m_prev = m_scr[h][:, 0:1]
            log_d = jnp.where(mask, b_col - b_row + i_row, -jnp.inf)
            log_0 = b_col + m_prev
            m_t = jnp.maximum(log_0, jnp.max(log_d, axis=-1, keepdims=True))
            w_0 = jnp.exp(log_0 - m_t)
            s = _dot_nt(q, k) * jnp.exp(log_d - m_t)
            cst = c_scr[h]
            nst = n_scr[h]
            num = _dot(s, v) + w_0 * _dot(q, cst)
            den = jnp.sum(s, axis=-1, keepdims=True) + w_0 * jnp.sum(q * nst, axis=-1, keepdims=True)
            h_ref[0, rows, cols] = num / jnp.maximum(jnp.abs(den), jnp.exp(-m_t))
            b_last = tot[:, HEADS + h:HEADS + h + 1]
            m_new = jnp.sum(jnp.where(rowid == last, m_t, 0.0), axis=0, keepdims=True)
            w_s = jnp.exp(b_last - b_col + i_col - m_new)
            c_0 = jnp.exp(b_last + m_prev - m_new)
            kw = k * w_s
            c_scr[h] = c_0 * cst + _dot_tn(kw, v)
            n_scr[h] = c_0 * nst + jnp.sum(kw, axis=0, keepdims=True)
            m_scr[h] = jnp.broadcast_to(m_new, (1, DH))
        return carry

    lax.fori_loop(0, nch, chunk, 0)

    if emit_state:
        @pl.when(j == pl.num_programs(2) - 1)
        def _():
            co_ref[0, 0] = c_scr[...]
            no_ref[0, 0] = n_scr[...]
            mo_ref[0, 0] = m_scr[...]


def _mlstm(z, zs, bias, batch, seq, tb, states):
    m = batch * seq
    nt = seq // tb
    zero_init = states is None
    emit_state = states is None
    rb = _mixer_rowblk(nt)
    zspec = lambda cb: pl.BlockSpec((tb, BRANCH_W), lambda b, d, j: (rb(b, d, j), cb))
    st5 = lambda shape: pl.BlockSpec((1, 1) + shape, lambda b, d, j: (b, d) + (0,) * len(shape))
    in_specs = [zspec(CB_Q), zspec(CB_K), zspec(CB_V),
                pl.BlockSpec((tb, GATE_W), lambda b, d, j: (rb(b, d, j), d)),
                pl.BlockSpec((1, 1, GATE_W), lambda b, d, j: (d, 0, 0))]
    args = [z, z, z, zs, bias]
    if not zero_init:
        in_specs += [st5((HEADS, DH, DH)), st5((HEADS, 1, DH)), st5((HEADS, 1, DH))]
        args += list(states)
    out_specs = [pl.BlockSpec((1, tb, BRANCH_W), lambda b, d, j: (d, rb(b, d, j), 0))]
    out_shape = [jax.ShapeDtypeStruct((N_DIR, m, BRANCH_W), F32)]
    if emit_state:
        out_specs += [st5((HEADS, DH, DH)), st5((HEADS, 1, DH)), st5((HEADS, 1, DH))]
        out_shape += [jax.ShapeDtypeStruct((batch, N_DIR, HEADS, DH, DH), F32),
                      jax.ShapeDtypeStruct((batch, N_DIR, HEADS, 1, DH), F32),
                      jax.ShapeDtypeStruct((batch, N_DIR, HEADS, 1, DH), F32)]
    return pl.pallas_call(
        functools.partial(_mlstm_kernel, nch=tb // CHUNK, zero_init=zero_init, emit_state=emit_state),
        grid=(batch, N_DIR, nt),
        in_specs=in_specs, out_specs=out_specs, out_shape=out_shape,
        scratch_shapes=[pltpu.VMEM((HEADS, DH, DH), F32), pltpu.VMEM((HEADS, 1, DH), F32),
                        pltpu.VMEM((HEADS, 1, DH), F32)],
        name="mlstm",
        compiler_params=_cparams(("arbitrary", "arbitrary", "arbitrary")),
    )(*args)


def _gdn_prep_kernel(x_ref, prev_ref, next_ref, w_ref, o_ref, xe_scr, *, tb, seq):
    i = pl.program_id(0)
    p = pl.program_id(1)
    pad = 8
    at_start = (i * tb) % seq == 0
    at_end = ((i + 1) * tb) % seq == 0
    xe_scr[0:pad, :] = jnp.where(at_start, 0.0, prev_ref[...])
    xe_scr[pad:pad + tb, :] = x_ref[...]
    xe_scr[pad + tb:pad + tb + pad, :] = jnp.where(at_end, 0.0, next_ref[...])
    acc = None
    for t in range(CONV_K):
        term = xe_scr[pl.ds(pad + t - CONV_K // 2, tb), :] * w_ref[t:t + 1, :]
        acc = term if acc is None else acc + term
    y = _silu(acc)
    outs = []
    for h in range(HEADS):
        yh = y[:, h * DH:(h + 1) * DH]
        outs.append(yh * lax.rsqrt(jnp.sum(yh * yh, axis=-1, keepdims=True) + EPS))
    yn = jnp.concatenate(outs, axis=-1)
    qscale = jnp.where(p == 0, DH ** -0.5, 1.0)
    o_ref[...] = jnp.where(p == 2, y, yn * qscale)


def _gdn_prep(z, conv_w, seq):
    m = z.shape[0]
    tb = 256
    nb8 = m // 8
    return pl.pallas_call(
        functools.partial(_gdn_prep_kernel, tb=tb, seq=seq),
        grid=(m // tb, 3),
        in_specs=[pl.BlockSpec((tb, BRANCH_W), lambda i, p: (i, CB_GQKV + p)),
                  pl.BlockSpec((8, BRANCH_W), lambda i, p: (jnp.maximum(i * (tb // 8) - 1, 0), CB_GQKV + p)),
                  pl.BlockSpec((8, BRANCH_W), lambda i, p: (jnp.minimum((i + 1) * (tb // 8), nb8 - 1), CB_GQKV + p)),
                  pl.BlockSpec((CONV_K, BRANCH_W), lambda i, p: (0, p))],
        out_specs=pl.BlockSpec((tb, BRANCH_W), lambda i, p: (i, p)),
        out_shape=jax.ShapeDtypeStruct((m, 3 * BRANCH_W), F32),
        scratch_shapes=[pltpu.VMEM((tb + 16, BRANCH_W), F32)],
        name="gdn_prep",
        compiler_params=_cparams(("arbitrary", "arbitrary")),
    )(z, z, z, conv_w)


HC = HEADS * CHUNK


def _stack_heads(x):
    return jnp.concatenate([x[:, h * DH:(h + 1) * DH] for h in range(HEADS)], axis=0)


def _stack_gate(x, lane0):
    return jnp.concatenate([x[:, lane0 + h:lane0 + h + 1] for h in range(HEADS)], axis=0)


def _stacked_masks(d):
    r = lax.broadcasted_iota(jnp.int32, (HC, HC), 0)
    c = lax.broadcasted_iota(jnp.int32, (HC, HC), 1)
    lg = int(math.log2(CHUNK))
    same_head = (r >> lg) == (c >> lg)
    diff = (r - c) * (1 - 2 * d)
    return jnp.logical_and(same_head, diff >= 0), jnp.logical_and(same_head, diff > 0)


def _unit_tri_solve(x, rhs):
    n = x.shape[0]
    r = lax.broadcasted_iota(jnp.int32, (n, n), 0)
    c = lax.broadcasted_iota(jnp.int32, (n, n), 1)
    eye = (r == c).astype(F32)
    same = lambda s: (r >> s) == (c >> s)
    x0 = jnp.where(same(3), x, 0.0)
    x2 = _dot(x0, x0)
    p1 = eye + x0
    y = _dot(x2, jnp.concatenate([x2, p1], axis=1))
    p2 = p1 + y[:, n:]
    t8m = (p2 - eye) + _dot(y[:, :n], p2)
    levels = range(4, int(math.log2(CHUNK)) + 1)
    offs = [jnp.where(jnp.logical_and(same(s), jnp.logical_not(same(s - 1))), x, 0.0) for s in levels]
    cur = jnp.concatenate([rhs] + offs, axis=1)
    cur = cur + _dot(t8m, cur)
    nr = rhs.shape[1]
    for _ in levels:
        m = cur[:, nr:nr + n]
        rest = cur[:, :nr] if cur.shape[1] == nr + n else jnp.concatenate([cur[:, :nr], cur[:, nr + n:]], axis=1)
        cur = rest + _dot(m, rest)
    return cur


def _gdn_kernel(*refs, nch, zero_init, emit_state):
    q_ref, k_ref, v_ref, g_ref, bias_ref, alog_ref = refs[:6]
    pos = 6
    if not zero_init:
        s0_ref = refs[pos]
        pos += 1
    o_ref = refs[pos]
    pos += 1
    if emit_state:
        so_ref = refs[pos]
        pos += 1
    s_scr, uw_scr, at_scr, qg_scr, kd_scr, gt_scr = refs[pos:pos + 6]

    d = pl.program_id(1)
    j = pl.program_id(2)

    @pl.when(j == 0)
    def _():
        if zero_init:
            s_scr[...] = jnp.zeros_like(s_scr)
        else:
            s_scr[...] = s0_ref[0, 0]

    tri = _chunk_masks(d)[0].astype(BF16)
    mask, strict = _stacked_masks(d)
    sel = _gate_selector()
    lane = lax.broadcasted_iota(jnp.int32, (HC, GATE_W), 1)
    bias = bias_ref[0]
    neg_a = -jnp.exp(alog_ref[0])
    g0 = 2 * HEADS

    def prepare(c, carry):
        rows = pl.ds(pl.multiple_of(c * CHUNK, CHUNK), CHUNK)
        pre = g_ref[rows, :] + bias
        g_all = neg_a * _softplus(pre)
        gc = _sel_dot(tri, g_all)
        gtot = jnp.sum(g_all, axis=0, keepdims=True)
        g_col = _stack_gate(gc, g0)
        beta = _stack_gate(_sigmoid(pre), g0 + HEADS)
        g_row = _sel_dot(sel, jnp.where(lane == 0, g_col, 0.0), nt=True)[0:1, :]
        g_last = jnp.concatenate([jnp.broadcast_to(gtot[:, g0 + h:g0 + h + 1], (CHUNK, 1)) for h in range(HEADS)],
                                 axis=0)
        decay = jnp.exp(jnp.where(mask, g_col - g_row, -jnp.inf))
        eg = jnp.exp(g_col)
        q = _stack_heads(q_ref[rows, :])
        k = _stack_heads(k_ref[rows, :])
        v = _stack_heads(v_ref[rows, :])
        kb = k * beta
        x = jnp.where(strict, -(_dot_nt(kb, k) * decay), 0.0)
        uw_scr[c] = _unit_tri_solve(x, jnp.concatenate([v * beta, kb * eg], axis=-1))
        at_scr[c] = _dot_nt(q, k) * decay
        qg_scr[c] = q * eg
        kd_scr[c] = k * jnp.exp(g_last - g_col)
        gt_scr[c] = gtot
        return carry

    lax.fori_loop(0, nch, prepare, 0)

    def advance(c, carry):
        ci = _dir_chunk(c, d, nch)
        rows = pl.ds(pl.multiple_of(ci * CHUNK, CHUNK), CHUNK)
        hrows = lambda h: slice(h * CHUNK, (h + 1) * CHUNK)
        st = [s_scr[h] for h in range(HEADS)]
        v_new = [uw_scr[ci, hrows(h), 0:DH] - _dot(uw_scr[ci, hrows(h), DH:2 * DH], st[h]) for h in range(HEADS)]
        qs = jnp.concatenate([_dot(qg_scr[ci, hrows(h), :], st[h]) for h in range(HEADS)], axis=0)
        o = qs + _dot(at_scr[ci], jnp.concatenate(v_new, axis=0))
        gtot = gt_scr[ci]
        for h in range(HEADS):
            o_ref[0, rows, h * DH:(h + 1) * DH] = o[hrows(h), :]
            s_scr[h] = st[h] * jnp.exp(gtot[:, g0 + h:g0 + h + 1]) + _dot_tn(kd_scr[ci, hrows(h), :], v_new[h])
        return carry

    lax.fori_loop(0, nch, advance, 0)

    if emit_state:
        @pl.when(j == pl.num_programs(2) - 1)
        def _():
            so_ref[0, 0] = s_scr[...]


def _gdn(qkv, zs, bias, alog, batch, seq, tb, s0):
    m = batch * seq
    nt = seq // tb
    nch = tb // CHUNK
    zero_init = s0 is None
    emit_state = s0 is None
    rb = _mixer_rowblk(nt)
    spec = lambda cb: pl.BlockSpec((tb, BRANCH_W), lambda b, d, j: (rb(b, d, j), cb))
    st = pl.BlockSpec((1, 1, HEADS, DH, DH), lambda b, d, j: (b, d, 0, 0, 0))
    dirrow = pl.BlockSpec((1, 1, GATE_W), lambda b, d, j: (d, 0, 0))
    in_specs = [spec(0), spec(1), spec(2),
                pl.BlockSpec((tb, GATE_W), lambda b, d, j: (rb(b, d, j), d)), dirrow, dirrow]
    args = [qkv, qkv, qkv, zs, bias, alog]
    if not zero_init:
        in_specs.append(st)
        args.append(s0)
    out_specs = [pl.BlockSpec((1, tb, BRANCH_W), lambda b, d, j: (d, rb(b, d, j), 0))]
    out_shape = [jax.ShapeDtypeStruct((N_DIR, m, BRANCH_W), F32)]
    if emit_state:
        out_specs.append(st)
        out_shape.append(jax.ShapeDtypeStruct((batch, N_DIR, HEADS, DH, DH), F32))
    return pl.pallas_call(
        functools.partial(_gdn_kernel, nch=nch, zero_init=zero_init, emit_state=emit_state),
        grid=(batch, N_DIR, nt),
        in_specs=in_specs, out_specs=out_specs, out_shape=out_shape,
        scratch_shapes=[pltpu.VMEM((HEADS, DH, DH), F32),
                        pltpu.VMEM((nch, HC, 2 * DH), F32), pltpu.VMEM((nch, HC, HC), F32),
                        pltpu.VMEM((nch, HC, DH), F32), pltpu.VMEM((nch, HC, DH), F32),
                        pltpu.VMEM((nch, 1, GATE_W), F32)],
        name="gdn",
        compiler_params=_cparams(("arbitrary", "arbitrary", "arbitrary")),
    )(*args)


def _s5_kernel(*refs, rb, seg, chain, zero_init, emit_state):
    u_ref, b_ref, c_ref, lre_ref, lim_ref, ls_ref = refs[:6]
    pos = 6
    if not zero_init:
        h0r_ref, h0i_ref = refs[pos:pos + 2]
        pos += 2
    y_ref = refs[pos]
    pos += 1
    if emit_state:
        sr_ref, si_ref = refs[pos:pos + 2]
        pos += 2
    up, yp, xr, xi, er, ei, pr, pi_, car_r, car_i = refs[pos:pos + 10]

    d = pl.program_id(1)
    j = pl.program_id(3)
    w = S5_CBW
    nsub = S5_SUBLANES

    lre = lre_ref[0, 0]
    lim = lim_ref[0, 0]
    dt = jnp.exp(ls_ref[0, 0])
    mag = jnp.exp(lre * dt)
    ar = mag * jnp.cos(lim * dt)
    ai = mag * jnp.sin(lim * dt)
    den = lre * lre + lim * lim
    nr = ar - 1.0
    zr = (nr * lre + ai * lim) / den
    zi = (ai * lre - nr * lim) / den

    rc = 256
    ngrp = seg // nsub

    def interleave(g, carry):
        for s in range(nsub):
            src = pl.ds(pl.multiple_of(s * seg + g * nsub, nsub), nsub)
            up[pl.ds(g * nsub * nsub + s, nsub, stride=nsub), :] = u_ref[src, :]
        return carry

    lax.fori_loop(0, ngrp, interleave, 0)

    def fill(c, carry):
        rows = pl.ds(pl.multiple_of(c * rc, rc), rc)
        bu = _dot(up[rows, :], b_ref[0])
        bre = bu[:, :w]
        bim = bu[:, w:]
        xr[rows, :] = zr * bre - zi * bim
        xi[rows, :] = zr * bim + zi * bre
        return carry

    lax.fori_loop(0, rb // rc, fill, 0)

    arb = jnp.broadcast_to(ar, (nsub, w))
    aib = jnp.broadcast_to(ai, (nsub, w))

    def step_rows(tt):
        return pl.ds(pl.multiple_of(_dir_chunk(tt, d, seg) * nsub, nsub), nsub)

    def scan_step(tt, carry):
        hr, hi = carry
        idx = step_rows(tt)
        nhr = arb * hr - aib * hi + xr[idx, :]
        nhi = arb * hi + aib * hr + xi[idx, :]
        xr[idx, :] = nhr
        xi[idx, :] = nhi
        return nhr, nhi

    init = (jnp.zeros((nsub, w), F32), jnp.zeros((nsub, w), F32))
    hr, hi = lax.fori_loop(0, seg, scan_step, init, unroll=4)

    if chain:
        @pl.when(j == 0)
        def _():
            if zero_init:
                car_r[...] = jnp.zeros_like(car_r)
                car_i[...] = jnp.zeros_like(car_i)
            else:
                car_r[...] = h0r_ref[0, 0, 0]
                car_i[...] = h0i_ref[0, 0, 0]

        er[...] = hr
        ei[...] = hi
        sr, si = ar, ai
        for _ in range(int(math.log2(seg))):
            sr, si = sr * sr - si * si, 2.0 * sr * si
        fr = car_r[...]
        fi = car_i[...]
        for k in range(nsub):
            row = pl.ds(_dir_chunk(k, d, nsub), 1)
            pr[row, :] = fr
            pi_[row, :] = fi
            fr, fi = er[row, :] + (sr * fr - si * fi), ei[row, :] + (sr * fi + si * fr)
        car_r[...] = fr
        car_i[...] = fi

        def fix_step(tt, carry):
            cr, ci = carry
            idx = step_rows(tt)
            ncr = arb * cr - aib * ci
            nci = arb * ci + aib * cr
            xr[idx, :] = xr[idx, :] + ncr
            xi[idx, :] = xi[idx, :] + nci
            return ncr, nci

        lax.fori_loop(0, seg, fix_step, (pr[...], pi_[...]), unroll=4)

    if emit_state:
        sr_ref[0, 0] = hr
        si_ref[0, 0] = hi

    def proj(c, carry):
        rows = pl.ds(pl.multiple_of(c * rc, rc), rc)
        yp[rows, :] = _dot(xr[rows, :], c_ref[0, :w, :]) + _dot(xi[rows, :], c_ref[0, w:, :])
        return carry

    lax.fori_loop(0, rb // rc, proj, 0)

    def deinterleave(g, carry):
        for s in range(nsub):
            dst = pl.ds(pl.multiple_of(s * seg + g * nsub, nsub), nsub)
            y_ref[0, dst, :] = yp[pl.ds(g * nsub * nsub + s, nsub, stride=nsub), :]
        return carry

    lax.fori_loop(0, ngrp, deinterleave, 0)


def _s5(z, bbd, cbd, lre, lim, ls, batch, seq, h0):
    m = batch * seq
    nsub = S5_SUBLANES
    chain = h0 is not None
    zero_init = h0 is None
    emit_state = h0 is None
    if chain:
        seg = 256
        rb = nsub * seg
        ng, nt = batch, seq // rb
        st_spec = pl.BlockSpec((1, 1, 1, 1, S5_CBW), lambda g, d, cb, j: (d, cb, g, 0, 0))
    else:
        seg = seq
        rb = nsub * seg
        ng, nt = batch // nsub, 1
        st_spec = pl.BlockSpec((1, 1, nsub, S5_CBW), lambda g, d, cb, j: (d, cb, g, 0))
    rowblk = lambda g, d, j: g * nt + _dir_chunk(j, d, nt)
    lam_spec = pl.BlockSpec((1, 1, 1, S5_CBW), lambda g, d, cb, j: (d, cb, 0, 0))
    in_specs = [pl.BlockSpec((rb, 128), lambda g, d, cb, j: (rowblk(g, d, j), CB_SU * 4 + cb)),
                pl.BlockSpec((1, 128, 2 * S5_CBW), lambda g, d, cb, j: (cb, 0, 0)),
                pl.BlockSpec((1, 2 * S5_CBW, 128), lambda g, d, cb, j: (cb, 0, 0)),
                lam_spec, lam_spec, lam_spec]
    args = [z, bbd, cbd, lre, lim, ls]
    if not zero_init:
        in_specs += [st_spec, st_spec]
        args += list(h0)
    out_specs = [pl.BlockSpec((1, rb, 128), lambda g, d, cb, j: (d, rowblk(g, d, j), cb))]
    out_shape = [jax.ShapeDtypeStruct((N_DIR, m, BRANCH_W), F32)]
    if emit_state:
        out_specs += [st_spec, st_spec]
        out_shape += [jax.ShapeDtypeStruct((N_DIR, S5_CB, batch, S5_CBW), F32)] * 2
    vrow = lambda n: pltpu.VMEM((n, S5_CBW), F32)
    lanes = pltpu.VMEM((rb, 128), F32)
    return pl.pallas_call(
        functools.partial(_s5_kernel, rb=rb, seg=seg, chain=chain, zero_init=zero_init, emit_state=emit_state),
        grid=(ng, N_DIR, S5_CB, nt),
        in_specs=in_specs, out_specs=out_specs, out_shape=out_shape,
        scratch_shapes=[lanes, lanes, vrow(rb), vrow(rb), vrow(nsub), vrow(nsub), vrow(nsub), vrow(nsub),
                        vrow(1), vrow(1)],
        name="s5",
        compiler_params=_cparams(("arbitrary",) * 4),
    )(*args)


def _merge_kernel(x_ref, mod_ref, gates_ref, o_ref, u_ref, gz_ref, hm_ref, ys_ref, go_ref,
                  mlw_ref, s5d_ref, gluw_ref, glub_ref, gdw_ref, wbr_ref, wout_ref, out_ref):
    ya = _sigmoid(o_ref[...]) * _head_rms(hm_ref[0] + hm_ref[1], mlw_ref[...])
    y5 = _gelu_tanh(ys_ref[0] + ys_ref[1] + s5d_ref[...] * u_ref[...])
    yb = y5 * _sigmoid(_dot(y5, gluw_ref[...]) + glub_ref[...])
    yc = _head_rms(go_ref[0] + go_ref[1], gdw_ref[...]) * _silu(gz_ref[...])
    merged = None
    for n, y in enumerate((ya, yb, yc)):
        term = _sigmoid(gates_ref[:, n * D_MODEL:(n + 1) * D_MODEL]) * _dot(y, wbr_ref[n])
        merged = term if merged is None else merged + term
    out_ref[...] = x_ref[...] + mod_ref[0, 2:3, :] * _dot(merged, wout_ref[...])


def _merge(x, mod, z, hm, ys, go, p, seq):
    m = x.shape[0]
    tm = 256
    rows_per_mod = seq if mod.shape[0] > 1 else m
    row = lambda i: (i, 0)
    zspec = lambda cb: pl.BlockSpec((tm, BRANCH_W), lambda i: (i, cb))
    dspec = pl.BlockSpec((N_DIR, tm, BRANCH_W), lambda i: (0, i, 0))
    full = lambda a: pl.BlockSpec(a.shape, lambda i: (0,) * a.ndim)
    consts = [p['ml_norm_w'], p['s5_D'], p['s5_glu_w'], p['s5_glu_b'], p['gd_norm_w'], p['w_branch'], p['w_out']]
    return pl.pallas_call(
        _merge_kernel,
        grid=(m // tm,),
        in_specs=[pl.BlockSpec((tm, D_MODEL), row),
                  pl.BlockSpec((1, N_MOD, D_MODEL), lambda i: (i * tm // rows_per_mod, 0, 0)),
                  pl.BlockSpec((tm, 3 * D_MODEL), row),
                  zspec(CB_O), zspec(CB_SU), zspec(CB_GZ), dspec, dspec, dspec] + [full(a) for a in consts],
        out_specs=pl.BlockSpec((tm, D_MODEL), row),
        out_shape=jax.ShapeDtypeStruct((m, D_MODEL), F32),
        name="merge",
        compiler_params=_cparams(("arbitrary",)),
    )(x, mod, z, z, z, z, hm, ys, go, *consts)


def _ffn_kernel(x_ref, mod_ref, nw_ref, wg_ref, wu_ref, wd_ref, fw_ref, out_ref, h_scr, acc_scr, *, final):
    jf = pl.program_id(1)

    @pl.when(jf == 0)
    def _():
        x = x_ref[...]
        y = x * lax.rsqrt(jnp.mean(x * x, axis=-1, keepdims=True) + EPS)
        h_scr[...] = ((y * nw_ref[...]) * (1.0 + mod_ref[0, 4:5, :]) + mod_ref[0, 3:4, :]).astype(BF16)
        acc_scr[...] = jnp.zeros_like(acc_scr)

    hb = h_scr[...]
    a = _silu(jnp.dot(hb, wg_ref[...], preferred_element_type=F32))
    b = jnp.dot(hb, wu_ref[...], preferred_element_type=F32)
    acc_scr[...] += _dot(a * b, wd_ref[...])

    @pl.when(jf == pl.num_programs(1) - 1)
    def _():
        x = x_ref[...] + mod_ref[0, 5:6, :] * acc_scr[...]
        if final:
            x = x * lax.rsqrt(jnp.mean(x * x, axis=-1, keepdims=True) + EPS) * fw_ref[...]
        out_ref[...] = x


def _ffn(x, mod, p, seq, final_w, final):
    m = x.shape[0]
    tm, tf = 512, D_FF // 2
    rows_per_mod = seq if mod.shape[0] > 1 else m
    row = lambda i, jf: (i, 0)
    vec = pl.BlockSpec((1, D_MODEL), lambda i, jf: (0, 0))
    return pl.pallas_call(
        functools.partial(_ffn_kernel, final=final),
        grid=(m // tm, D_FF // tf),
        in_specs=[pl.BlockSpec((tm, D_MODEL), row),
                  pl.BlockSpec((1, N_MOD, D_MODEL), lambda i, jf: (i * tm // rows_per_mod, 0, 0)),
                  vec,
                  pl.BlockSpec((D_MODEL, tf), lambda i, jf: (0, jf)),
                  pl.BlockSpec((D_MODEL, tf), lambda i, jf: (0, jf)),
                  pl.BlockSpec((tf, D_MODEL), lambda i, jf: (jf, 0)),
                  vec],
        out_specs=pl.BlockSpec((tm, D_MODEL), row),
        out_shape=jax.ShapeDtypeStruct((m, D_MODEL), F32),
        scratch_shapes=[pltpu.VMEM((tm, D_MODEL), BF16), pltpu.VMEM((tm, D_MODEL), F32)],
        name="ffn",
        compiler_params=_cparams(("arbitrary", "arbitrary")),
    )(x, mod, p['norm2_w'], p['w_gate'], p['w_up'], p['w_down'], final_w)


def _gate_lanes(parts):
    row = jnp.concatenate([a.astype(F32) for a in parts], axis=-1)
    return jnp.pad(row, ((0, 0), (0, GATE_W - row.shape[-1]))).reshape(N_DIR, 1, GATE_W)


def _block_diag(a):
    cb, g, r, c = a.shape
    eye = jnp.eye(g, dtype=a.dtype)
    return jnp.einsum('bgrc,gh->bgrhc', a, eye).reshape(cb, g * r, g * c)


def _layer_params(l, w):
    gpb = S5_GROUPS // S5_CB
    idx, acc = [], 0
    for size in IN_SIZES[:-1]:
        acc += size
        idx.append(acc)
    mq, mk, mv, mo, mi, mf, su, gqkv, gz, ga, gb, gates = jnp.split(w['w_in'][l], idx, axis=-1)
    w_big = jnp.concatenate([gates, mq, mk, mv, mo, su, gqkv, gz], axis=-1).astype(BF16)
    smalls = []
    for d in range(N_DIR):
        sl = slice(d * HEADS, (d + 1) * HEADS)
        blk = jnp.concatenate([mi[:, sl], mf[:, sl], ga[:, sl], gb[:, sl]], axis=-1)
        smalls.append(jnp.pad(blk, ((0, 0), (0, GATE_W - 4 * HEADS))))
    w_small = jnp.concatenate(smalls, axis=-1).astype(BF16)
    zeros = jnp.zeros((N_DIR, HEADS), F32)
    gate_bias = _gate_lanes([w['ml_i_bias'][l], w['ml_f_bias'][l], w['gd_dt_bias'][l], zeros])
    gate_alog = _gate_lanes([zeros, zeros, w['gd_A_log'][l], zeros])
    b_re = jnp.swapaxes(w['s5_B_re'][l], 1, 2).reshape(S5_CB, gpb, S5_GROUP, S5_STATE)
    b_im = jnp.swapaxes(w['s5_B_im'][l], 1, 2).reshape(S5_CB, gpb, S5_GROUP, S5_STATE)
    bbd = jnp.concatenate([_block_diag(b_re), _block_diag(b_im)], axis=-1).astype(BF16)
    c_re = jnp.swapaxes(w['s5_C_re'][l], 1, 2).reshape(S5_CB, gpb, S5_STATE, S5_GROUP)
    c_im = jnp.swapaxes(w['s5_C_im'][l], 1, 2).reshape(S5_CB, gpb, S5_STATE, S5_GROUP)
    cbd = jnp.concatenate([_block_diag(c_re), -_block_diag(c_im)], axis=1).astype(BF16)
    lam_shape = (N_DIR, S5_CB, 1, S5_CBW)
    ls = jnp.broadcast_to(w['s5_log_step'][l][:, :, None], (N_DIR, S5_GROUPS, S5_STATE))
    return dict(
        norm1_w=w['norm1_w'][l], w_big=w_big, w_small=w_small, gate_bias=gate_bias, gate_alog=gate_alog,
        ml_norm_w=w['ml_norm_w'][l].reshape(1, BRANCH_W),
        bbd=bbd, cbd=cbd,
        lam_re=w['s5_lam_re'][l].reshape(lam_shape), lam_im=w['s5_lam_im'][l].reshape(lam_shape),
        log_step=ls.reshape(lam_shape),
        s5_D=w['s5_D'][l].reshape(1, BRANCH_W), s5_glu_w=w['s5_glu_w'][l].astype(BF16),
        s5_glu_b=w['s5_glu_b'][l].reshape(1, BRANCH_W),
        gd_conv_w=w['gd_conv_w'][l],
        gd_norm_w=jnp.tile(w['gd_norm_w'][l], HEADS).reshape(1, BRANCH_W),
        w_branch=w['w_branch'][l].astype(BF16), w_out=w['w_out'][l].astype(BF16),
        norm2_w=w['norm2_w'][l].reshape(1, D_MODEL),
        w_gate=w['w_gate'][l].astype(BF16), w_up=w['w_up'][l].astype(BF16), w_down=w['w_down'][l].astype(BF16),
    )


def _grid_pos_embed(n_tok):
    grid_w = 64
    t = jnp.arange(n_tok)
    quarter = D_MODEL // 4
    omega = 1.0 / (10000.0 ** (jnp.arange(quarter, dtype=F32) / quarter))

    def enc(pos):
        ang = pos.astype(F32)[:, None] * omega[None, :]
        return jnp.concatenate([jnp.sin(ang), jnp.cos(ang)], axis=-1)

    return jnp.concatenate([enc(t // grid_w), enc(t % grid_w)], axis=-1)


def _trunk_layer(x, mod, p, batch, seq, tb, states, final_w, final, pe=None):
    res = _inproj(x, mod, p['norm1_w'], p['w_big'], p['w_small'], seq, pe)
    if pe is not None:
        z, zs, x = res
    else:
        z, zs = res
    if states is None:
        ml_st = s5_st = gd_st = None
    else:
        ml_st, s5_st, gd_st = states
    ml = _mlstm(z, zs, p['gate_bias'], batch, seq, tb, ml_st)
    s5 = _s5(z, p['bbd'], p['cbd'], p['lam_re'], p['lam_im'], p['log_step'], batch, seq, s5_st)
    qkv = _gdn_prep(z, p['gd_conv_w'], seq)
    gd = _gdn(qkv, zs, p['gate_bias'], p['gate_alog'], batch, seq, tb, gd_st)
    x = _merge(x, mod, z, ml[0], s5[0], gd[0], p, seq)
    x = _ffn(x, mod, p, seq, final_w, final)
    new_states = None
    if states is None:
        new_states = (ml[1], ml[2][:, :, :, 0, :], ml[3][:, :, :, 0, 0],
                      jnp.transpose(s5[1], (2, 0, 1, 3)).reshape(batch, N_DIR, S5_GROUPS, S5_STATE),
                      jnp.transpose(s5[2], (2, 0, 1, 3)).reshape(batch, N_DIR, S5_GROUPS, S5_STATE),
                      gd[1])
    return x, new_states


def kernel(x_prompt, x_sample, state_mlstm_C, state_mlstm_n, state_mlstm_m, state_s5_re, state_s5_im,
           state_gdn_S, c, c_ctx, ada_w, ada_b, norm1_w, w_in, ml_i_bias, ml_f_bias, ml_norm_w,
           s5_lam_re, s5_lam_im, s5_log_step, s5_B_re, s5_B_im, s5_C_re, s5_C_im, s5_D, s5_glu_w, s5_glu_b,
           gd_conv_w, gd_A_log, gd_dt_bias, gd_norm_w, w_branch, w_out, norm2_w, w_gate, w_up, w_down,
           final_norm_w):
    w = dict(norm1_w=norm1_w, w_in=w_in, ml_i_bias=ml_i_bias, ml_f_bias=ml_f_bias, ml_norm_w=ml_norm_w,
             s5_lam_re=s5_lam_re, s5_lam_im=s5_lam_im, s5_log_step=s5_log_step, s5_B_re=s5_B_re,
             s5_B_im=s5_B_im, s5_C_re=s5_C_re, s5_C_im=s5_C_im, s5_D=s5_D, s5_glu_w=s5_glu_w,
             s5_glu_b=s5_glu_b, gd_conv_w=gd_conv_w, gd_A_log=gd_A_log, gd_dt_bias=gd_dt_bias,
             gd_norm_w=gd_norm_w, w_branch=w_branch, w_out=w_out, norm2_w=norm2_w, w_gate=w_gate,
             w_up=w_up, w_down=w_down)
    bp, sp, _ = x_prompt.shape
    bs, ss, _ = x_sample.shape
    params = [_layer_params(l, w) for l in range(DEPTH)]
    final_w = final_norm_w.reshape(1, D_MODEL)

    cc = jnp.concatenate([c_ctx[None, :], c, jnp.zeros((8 - 1 - bs, D_MODEL), F32)], axis=0)
    mods = _modulation(cc, ada_w, ada_b).reshape(DEPTH, 8, N_MOD, D_MODEL)

    xp = x_prompt.reshape(bp * sp, D_MODEL)
    per_layer = []
    for l in range(DEPTH):
        xp, st = _trunk_layer(xp, mods[l, 0:1], params[l], bp, sp, sp, None, final_w, l == DEPTH - 1)
        per_layer.append(st)
    y_prompt = xp.reshape(bp, sp, D_MODEL)
    new_states = [jnp.stack([st[i] for st in per_layer], axis=1) for i in range(6)]

    xs = x_sample.reshape(bs * ss, D_MODEL)
    pe = _grid_pos_embed(ss)
    for l in range(DEPTH):
        ml_st = (state_mlstm_C[:, l],
                 state_mlstm_n[:, l][:, :, :, None, :],
                 jnp.broadcast_to(state_mlstm_m[:, l][:, :, :, None, None], (bs, N_DIR, HEADS, 1, DH)))
        s5_st = tuple(jnp.transpose(a[:, l].reshape(bs, N_DIR, S5_CB, 1, S5_CBW), (1, 2, 0, 3, 4))
                      for a in (state_s5_re, state_s5_im))
        xs, _ = _trunk_layer(xs, mods[l, 1:1 + bs], params[l], bs, ss, 512, (ml_st, s5_st, state_gdn_S[:, l]),
                             final_w, l == DEPTH - 1, pe if l == 0 else None)
    y_sample = xs.reshape(bs, ss, D_MODEL)
    return (y_prompt, y_sample, *new_states)
```

```python
import functools
import math

import jax
import jax.numpy as jnp
from jax import lax
from jax.experimental import pallas as pl
from jax.experimental.pallas import tpu as pltpu

F32 = jnp.float32
BF16 = jnp.bfloat16

D_MODEL = 1024
DEPTH = 2
N_DIR = 2
N_MOD = 6
EPS = 1e-6
HEADS = 4
DH = 128
BRANCH_W = HEADS * DH
CHUNK = 64
S5_GROUPS = 32
S5_GROUP = 16
S5_STATE = 64
S5_CB = 4
S5_CBW = S5_GROUPS * S5_STATE // S5_CB
S5_SUBLANES = 8
CONV_K = 5
D_FF = -(-8 * D_MODEL // (3 * 256)) * 256
IN_SIZES = (512, 512, 512, 512, 8, 8, 512, 1536, 512, 8, 8, 3072)
GATE_W = 128

CB_GATES, CB_Q, CB_K, CB_V, CB_O, CB_SU, CB_GQKV, CB_GZ = 0, 6, 7, 8, 9, 10, 11, 14
Z_W = 15 * BRANCH_W

VMEM_LIMIT = 56 * 1024 * 1024


def _cparams(sem):
    return pltpu.CompilerParams(dimension_semantics=sem, vmem_limit_bytes=VMEM_LIMIT)


def _dot(a, b):
    return jnp.dot(a.astype(BF16), b.astype(BF16), preferred_element_type=F32)


def _dot_nt(a, b):
    return lax.dot_general(a.astype(BF16), b.astype(BF16), (((1,), (1,)), ((), ())),
                           preferred_element_type=F32)


def _dot_tn(a, b):
    return lax.dot_general(a.astype(BF16), b.astype(BF16), (((0,), (0,)), ((), ())),
                           preferred_element_type=F32)


def _split3(x):
    hi = x.astype(BF16)
    r1 = x - hi.astype(F32)
    mid = r1.astype(BF16)
    lo = (r1 - mid.astype(F32)).astype(BF16)
    return hi, mid, lo


def _sel_dot(sel, x, nt=False):
    dims = (((1,), (1,)), ((), ())) if nt else (((1,), (0,)), ((), ()))
    hi, mid, lo = _split3(x)
    f = lambda p: lax.dot_general(sel, p, dims, preferred_element_type=F32)
    return (f(hi) + f(mid)) + f(lo)


def _dot_hp(a, b):
    ah = a.astype(BF16)
    al = (a - ah.astype(F32)).astype(BF16)
    bh = b.astype(BF16)
    bl = (b - bh.astype(F32)).astype(BF16)
    f = lambda p, q: jnp.dot(p, q, preferred_element_type=F32)
    return f(ah, bh) + (f(ah, bl) + f(al, bh))


def _sigmoid(x):
    return 1.0 / (1.0 + jnp.exp(-x))


def _silu(x):
    return x * _sigmoid(x)


def _softplus(x):
    return jnp.maximum(x, 0.0) + jnp.log(1.0 + jnp.exp(-jnp.abs(x)))


def _gelu_tanh(x):
    c = math.sqrt(2.0 / math.pi)
    return 0.5 * x * (1.0 + jnp.tanh(c * (x + 0.044715 * (x * x * x))))


def _head_rms(x, w_row):
    outs = []
    for h in range(HEADS):
        xh = x[:, h * DH:(h + 1) * DH]
        outs.append(xh * lax.rsqrt(jnp.mean(xh * xh, axis=-1, keepdims=True) + EPS))
    return jnp.concatenate(outs, axis=-1) * w_row


def _dir_chunk(c, d, n):
    return c + d * (n - 1 - 2 * c)


def _mod_kernel(c_ref, w_ref, b_ref, o_ref):
    o_ref[0] = _dot(_silu(c_ref[...]), w_ref[0]) + b_ref[0]


def _modulation(cc, ada_w, ada_b):
    tn = 1536
    nmod = N_MOD * D_MODEL
    return pl.pallas_call(
        _mod_kernel,
        grid=(DEPTH, nmod // tn),
        in_specs=[pl.BlockSpec((8, D_MODEL), lambda l, j: (0, 0)),
                  pl.BlockSpec((1, D_MODEL, tn), lambda l, j: (l, 0, j)),
                  pl.BlockSpec((1, 1, tn), lambda l, j: (l, 0, j))],
        out_specs=pl.BlockSpec((1, 8, tn), lambda l, j: (l, 0, j)),
        out_shape=jax.ShapeDtypeStruct((DEPTH, 8, nmod), F32),
        name="adaln_mod",
        compiler_params=_cparams(("arbitrary", "arbitrary")),
    )(cc, ada_w, ada_b.reshape(DEPTH, 1, nmod))


def _inproj_kernel(*refs, has_pe):
    if has_pe:
        x_ref, pe_ref, mod_ref, nw_ref, w_ref, ws_ref, z_ref, zs_ref, xs_ref, hn_scr = refs
    else:
        x_ref, mod_ref, nw_ref, w_ref, ws_ref, z_ref, zs_ref, hn_scr = refs

    @pl.when(pl.program_id(1) == 0)
    def _():
        x = x_ref[...]
        if has_pe:
            x = x + pe_ref[...]
            xs_ref[...] = x
        y = x * lax.rsqrt(jnp.mean(x * x, axis=-1, keepdims=True) + EPS)
        h = (y * nw_ref[...]) * (1.0 + mod_ref[0, 1:2, :]) + mod_ref[0, 0:1, :]
        hb = h.astype(BF16)
        hn_scr[...] = hb
        zs_ref[...] = jnp.dot(hb, ws_ref[...], preferred_element_type=F32)

    z_ref[...] = jnp.dot(hn_scr[...], w_ref[...], preferred_element_type=F32)


def _inproj(x, mod, norm_w, w_big, w_small, seq, pe=None):
    m = x.shape[0]
    tm, tn = 512, 1536
    rows_per_mod = seq if mod.shape[0] > 1 else m
    has_pe = pe is not None
    row = lambda i, j: (i, 0)
    in_specs = [pl.BlockSpec((tm, D_MODEL), row)]
    args = [x]
    if has_pe:
        in_specs.append(pl.BlockSpec((tm, D_MODEL), lambda i, j: (i % (seq // tm), 0)))
        args.append(pe)
    in_specs += [pl.BlockSpec((1, N_MOD, D_MODEL), lambda i, j: (i * tm // rows_per_mod, 0, 0)),
                 pl.BlockSpec((1, D_MODEL), lambda i, j: (0, 0)),
                 pl.BlockSpec((D_MODEL, tn), lambda i, j: (0, j)),
                 pl.BlockSpec((D_MODEL, N_DIR * GATE_W), lambda i, j: (0, 0))]
    args += [mod, norm_w.reshape(1, D_MODEL), w_big, w_small]
    out_specs = [pl.BlockSpec((tm, tn), lambda i, j: (i, j)),
                 pl.BlockSpec((tm, N_DIR * GATE_W), row)]
    out_shape = [jax.ShapeDtypeStruct((m, Z_W), F32),
                 jax.ShapeDtypeStruct((m, N_DIR * GATE_W), F32)]
    if has_pe:
        out_specs.append(pl.BlockSpec((tm, D_MODEL), row))
        out_shape.append(jax.ShapeDtypeStruct((m, D_MODEL), F32))
    return pl.pallas_call(
        functools.partial(_inproj_kernel, has_pe=has_pe),
        grid=(m // tm, Z_W // tn),
        in_specs=in_specs, out_specs=out_specs, out_shape=out_shape,
        scratch_shapes=[pltpu.VMEM((tm, D_MODEL), BF16)],
        name="inproj",
        compiler_params=_cparams(("arbitrary", "arbitrary")),
    )(*args)


def _mixer_rowblk(nt):
    return lambda b, d, j: b * nt + _dir_chunk(j, d, nt)


def _gate_selector():
    r = lax.broadcasted_iota(jnp.int32, (16, GATE_W), 0)
    c = lax.broadcasted_iota(jnp.int32, (16, GATE_W), 1)
    return (r == c).astype(BF16)


def _block_gate_sums(vals, other, n_other, cls_scr, cum_scr):
    tri = (cls_scr[...] > 0).astype(BF16)
    for g in range(vals.shape[0] // HC):
        grp = slice(g * HC, (g + 1) * HC)
        cum_scr[grp, :] = _sel_dot(tri, vals[grp, :])
    lane = lax.broadcasted_iota(jnp.int32, vals.shape, 1)
    return _sel_dot(_gate_selector(), jnp.where(lane < n_other, other, cum_scr[...]), nt=True)


def _mlstm_kernel(*refs, nch, zero_init, emit_state):
    q_ref, k_ref, v_ref, g_ref, bias_ref = refs[:5]
    pos = 5
    if not zero_init:
        c0_ref, n0_ref, m0_ref = refs[pos:pos + 3]
        pos += 3
    h_ref = refs[pos]
    pos += 1
    if emit_state:
        co_ref, no_ref, mo_ref = refs[pos:pos + 3]
        pos += 3
    c_scr, n_scr, m_scr, cls_scr, cum_scr, row_scr = refs[pos:pos + 6]

    d = pl.program_id(1)
    j = pl.program_id(2)

    @pl.when(j == 0)
    def _():
        if zero_init:
            c_scr[...] = jnp.zeros_like(c_scr)
            n_scr[...] = jnp.zeros_like(n_scr)
            m_scr[...] = jnp.zeros_like(m_scr)
        else:
            c_scr[...] = c0_ref[0, 0]
            n_scr[...] = n0_ref[0, 0]
            m_scr[...] = m0_ref[0, 0]

    cls_scr[...] = _pair_classes(d)
    rowid = lax.broadcasted_iota(jnp.int32, (CHUNK, 1), 0)
    last = (CHUNK - 1) * (1 - d)
    bias = bias_ref[0]
    scale = DH ** -0.5
    hrows = lambda h: slice(h * CHUNK, (h + 1) * CHUNK)
    per_head = lambda vals: jnp.concatenate([jnp.broadcast_to(a, (CHUNK, a.shape[1])) for a in vals], axis=0)

    pre_all = g_ref[...] + bias
    xt = _block_gate_sums(-_softplus(-pre_all), pre_all, HEADS, cls_scr, cum_scr)
    for c in range(nch):
        cs = slice(c * CHUNK, (c + 1) * CHUNK)
        row_scr[c] = jnp.concatenate([xt[h:h + 1, cs] - xt[HEADS + h:HEADS + h + 1, cs] for h in range(HEADS)],
                                     axis=1)

    def chunk(c, carry):
        ci = _dir_chunk(c, d, nch)
        r0 = pl.multiple_of(ci * CHUNK, CHUNK)
        rows = pl.ds(r0, CHUNK)
        pre = g_ref[rows, :] + bias
        cum = cum_scr[rows, :]
        tot = cum_scr[pl.ds(r0 + last, 1), :]
        b_col = _stack_gate(cum, HEADS)
        i_col = _stack_gate(pre, 0)
        r_row = row_scr[ci]
        m_prev = [m_scr[h][:, 0:1] for h in range(HEADS)]
        b_last = [tot[:, HEADS + h:HEADS + h + 1] for h in range(HEADS)]
        log_d = jnp.where(cls_scr[...] > 0, b_col + r_row, -jnp.inf)
        log_0 = b_col + per_head(m_prev)
        m_t = jnp.maximum(log_0, jnp.max(log_d, axis=-1, keepdims=True))
        w_0 = jnp.exp(log_0 - m_t)
        q = _stack_heads(q_ref[rows, :])
        k = _stack_heads(k_ref[rows, :]) * scale
        v = _stack_heads(v_ref[rows, :])
        s = _dot_nt(q, k) * jnp.exp(log_d - m_t)
        cst = [c_scr[h] for h in range(HEADS)]
        nst = [n_scr[h] for h in range(HEADS)]
        qc = jnp.concatenate([_dot(q[hrows(h), :], cst[h]) for h in range(HEADS)], axis=0)
        num = _dot(s, v) + w_0 * qc
        den = jnp.sum(s, axis=-1, keepdims=True) + w_0 * jnp.sum(q * per_head(nst), axis=-1, keepdims=True)
        hv = num / jnp.maximum(jnp.abs(den), jnp.exp(-m_t))
        m_new = [jnp.sum(jnp.where(rowid == last, m_t[hrows(h), :], 0.0), axis=0, keepdims=True)
                 for h in range(HEADS)]
        w_s = jnp.exp(per_head(b_last) - b_col + i_col - per_head(m_new))
        kw = k * w_s
        for h in range(HEADS):
            h_ref[0, rows, h * DH:(h + 1) * DH] = hv[hrows(h), :]
            c_0 = jnp.exp(b_last[h] + m_prev[h] - m_new[h])
            c_scr[h] = c_0 * cst[h] + _dot_tn(kw[hrows(h), :], v[hrows(h), :])
            n_scr[h] = c_0 * nst[h] + jnp.sum(kw[hrows(h), :], axis=0, keepdims=True)
            m_scr[h] = jnp.broadcast_to(m_new[h], (1, DH))
        return carry

    lax.fori_loop(0, nch, chunk, 0)

    if emit_state:
        @pl.when(j == pl.num_programs(2) - 1)
        def _():
            co_ref[0, 0] = c_scr[...]
            no_ref[0, 0] = n_scr[...]
            mo_ref[0, 0] = m_scr[...]


def _mlstm(z, zs, bias, batch, seq, tb, states):
    m = batch * seq
    nt = seq // tb
    zero_init = states is None
    emit_state = states is None
    rb = _mixer_rowblk(nt)
    zspec = lambda cb: pl.BlockSpec((tb, BRANCH_W), lambda b, d, j: (rb(b, d, j), cb))
    st5 = lambda shape: pl.BlockSpec((1, 1) + shape, lambda b, d, j: (b, d) + (0,) * len(shape))
    in_specs = [zspec(CB_Q), zspec(CB_K), zspec(CB_V),
                pl.BlockSpec((tb, GATE_W), lambda b, d, j: (rb(b, d, j), d)),
                pl.BlockSpec((1, 1, GATE_W), lambda b, d, j: (d, 0, 0))]
    args = [z, z, z, zs, bias]
    if not zero_init:
        in_specs += [st5((HEADS, DH, DH)), st5((HEADS, 1, DH)), st5((HEADS, 1, DH))]
        args += list(states)
    out_specs = [pl.BlockSpec((1, tb, BRANCH_W), lambda b, d, j: (d, rb(b, d, j), 0))]
    out_shape = [jax.ShapeDtypeStruct((N_DIR, m, BRANCH_W), F32)]
    if emit_state:
        out_specs += [st5((HEADS, DH, DH)), st5((HEADS, 1, DH)), st5((HEADS, 1, DH))]
        out_shape += [jax.ShapeDtypeStruct((batch, N_DIR, HEADS, DH, DH), F32),
                      jax.ShapeDtypeStruct((batch, N_DIR, HEADS, 1, DH), F32),
                      jax.ShapeDtypeStruct((batch, N_DIR, HEADS, 1, DH), F32)]
    return pl.pallas_call(
        functools.partial(_mlstm_kernel, nch=tb // CHUNK, zero_init=zero_init, emit_state=emit_state),
        grid=(batch, N_DIR, nt),
        in_specs=in_specs, out_specs=out_specs, out_shape=out_shape,
        scratch_shapes=[pltpu.VMEM((HEADS, DH, DH), F32), pltpu.VMEM((HEADS, 1, DH), F32),
                        pltpu.VMEM((HEADS, 1, DH), F32), pltpu.VMEM((HC, HC), jnp.int32),
                        pltpu.VMEM((tb, GATE_W), F32), pltpu.VMEM((tb // CHUNK, 1, HC), F32)],
        name="mlstm",
        compiler_params=_cparams(("arbitrary", "arbitrary", "arbitrary")),
    )(*args)


def _gdn_prep_kernel(x_ref, prev_ref, next_ref, w_ref, o_ref, xe_scr, *, tb, seq):
    i = pl.program_id(0)
    p = pl.program_id(1)
    pad = 8
    at_start = (i * tb) % seq == 0
    at_end = ((i + 1) * tb) % seq == 0
    xe_scr[0:pad, :] = jnp.where(at_start, 0.0, prev_ref[...])
    xe_scr[pad:pad + tb, :] = x_ref[...]
    xe_scr[pad + tb:pad + tb + pad, :] = jnp.where(at_end, 0.0, next_ref[...])
    acc = None
    for t in range(CONV_K):
        term = xe_scr[pl.ds(pad + t - CONV_K // 2, tb), :] * w_ref[t:t + 1, :]
        acc = term if acc is None else acc + term
    y = _silu(acc)
    outs = []
    for h in range(HEADS):
        yh = y[:, h * DH:(h + 1) * DH]
        outs.append(yh * lax.rsqrt(jnp.sum(yh * yh, axis=-1, keepdims=True) + EPS))
    yn = jnp.concatenate(outs, axis=-1)
    qscale = jnp.where(p == 0, DH ** -0.5, 1.0)
    o_ref[...] = jnp.where(p == 2, y, yn * qscale)


def _gdn_prep(z, conv_w, seq):
    m = z.shape[0]
    tb = 256
    nb8 = m // 8
    return pl.pallas_call(
        functools.partial(_gdn_prep_kernel, tb=tb, seq=seq),
        grid=(m // tb, 3),
        in_specs=[pl.BlockSpec((tb, BRANCH_W), lambda i, p: (i, CB_GQKV + p)),
                  pl.BlockSpec((8, BRANCH_W), lambda i, p: (jnp.maximum(i * (tb // 8) - 1, 0), CB_GQKV + p)),
                  pl.BlockSpec((8, BRANCH_W), lambda i, p: (jnp.minimum((i + 1) * (tb // 8), nb8 - 1), CB_GQKV + p)),
                  pl.BlockSpec((CONV_K, BRANCH_W), lambda i, p: (0, p))],
        out_specs=pl.BlockSpec((tb, BRANCH_W), lambda i, p: (i, p)),
        out_shape=jax.ShapeDtypeStruct((m, 3 * BRANCH_W), F32),
        scratch_shapes=[pltpu.VMEM((tb + 16, BRANCH_W), F32)],
        name="gdn_prep",
        compiler_params=_cparams(("arbitrary", "arbitrary")),
    )(z, z, z, conv_w)


HC = HEADS * CHUNK


def _stack_heads(x):
    return jnp.concatenate([x[:, h * DH:(h + 1) * DH] for h in range(HEADS)], axis=0)


def _stack_gate(x, lane0):
    return jnp.concatenate([x[:, lane0 + h:lane0 + h + 1] for h in range(HEADS)], axis=0)


BASE_LG = 3
CHUNK_LG = int(math.log2(CHUNK))


def _pair_classes(d):
    r = lax.broadcasted_iota(jnp.int32, (HC, HC), 0)
    c = lax.broadcasted_iota(jnp.int32, (HC, HC), 1)
    cls = jnp.full((HC, HC), CHUNK_LG, jnp.int32)
    for s in range(CHUNK_LG - 1, BASE_LG - 1, -1):
        cls = jnp.where((r >> s) == (c >> s), s, cls)
    diff = (r - c) * (1 - 2 * d)
    cls = jnp.where(diff == 0, 1, cls)
    return jnp.where(jnp.logical_and((r >> CHUNK_LG) == (c >> CHUNK_LG), diff >= 0), cls, 0)


def _unit_tri_solve(x, cls, rhs):
    n = x.shape[0]
    x0 = jnp.where(cls == BASE_LG, x, 0.0)
    x2 = _dot(x0, x0)
    y = _dot(x2, jnp.concatenate([x2, x0], axis=1))
    x4 = y[:, :n]
    q2 = x0 + x2 + y[:, n:]
    t8m = q2 + x4 + _dot(x4, q2)
    levels = range(BASE_LG + 1, CHUNK_LG + 1)
    cur = jnp.concatenate([rhs] + [jnp.where(cls == s, x, 0.0) for s in levels], axis=1)
    cur = cur + _dot(t8m, cur)
    nr = rhs.shape[1]
    for _ in levels:
        m = cur[:, nr:nr + n]
        rest = cur[:, :nr] if cur.shape[1] == nr + n else jnp.concatenate([cur[:, :nr], cur[:, nr + n:]], axis=1)
        cur = rest + _dot(m, rest)
    return cur


def _gdn_kernel(*refs, nch, zero_init, emit_state):
    q_ref, k_ref, v_ref, g_ref, bias_ref, alog_ref = refs[:6]
    pos = 6
    if not zero_init:
        s0_ref = refs[pos]
        pos += 1
    o_ref = refs[pos]
    pos += 1
    if emit_state:
        so_ref = refs[pos]
        pos += 1
    s_scr, cls_scr, cum_scr, row_scr, uw_scr, at_scr, qg_scr, kd_scr, gt_scr = refs[pos:pos + 9]

    d = pl.program_id(1)
    j = pl.program_id(2)

    @pl.when(j == 0)
    def _():
        if zero_init:
            s_scr[...] = jnp.zeros_like(s_scr)
        else:
            s_scr[...] = s0_ref[0, 0]

    cls_scr[...] = _pair_classes(d)
    last = (CHUNK - 1) * (1 - d)
    bias = bias_ref[0]
    neg_a = -jnp.exp(alog_ref[0])
    g0 = 2 * HEADS

    g_all = neg_a * _softplus(g_ref[...] + bias)
    xt = _block_gate_sums(g_all, g_all, 0, cls_scr, cum_scr)
    for c in range(nch):
        cs = slice(c * CHUNK, (c + 1) * CHUNK)
        row_scr[c] = jnp.concatenate([xt[g0 + h:g0 + h + 1, cs] for h in range(HEADS)], axis=1)

    def prepare(c, carry):
        r0 = pl.multiple_of(c * CHUNK, CHUNK)
        rows = pl.ds(r0, CHUNK)
        gtot = cum_scr[pl.ds(r0 + last, 1), :]
        g_col = _stack_gate(cum_scr[rows, :], g0)
        beta = _stack_gate(_sigmoid(g_ref[rows, :] + bias), g0 + HEADS)
        g_row = row_scr[c]
        g_last = jnp.concatenate([jnp.broadcast_to(gtot[:, g0 + h:g0 + h + 1], (CHUNK, 1)) for h in range(HEADS)],
                                 axis=0)
        cls = cls_scr[...]
        decay = jnp.exp(jnp.where(cls > 0, g_col - g_row, -jnp.inf))
        eg = jnp.exp(g_col)
        q = _stack_heads(q_ref[rows, :])
        k = _stack_heads(k_ref[rows, :])
        v = _stack_heads(v_ref[rows, :])
        kb = k * beta
        x = -(_dot_nt(kb, k) * decay)
        uw_scr[c] = _unit_tri_solve(x, cls, jnp.concatenate([v * beta, kb * eg], axis=-1))
        at_scr[c] = _dot_nt(q, k) * decay
        qg_scr[c] = q * eg
        kd_scr[c] = k * jnp.exp(g_last - g_col)
        gt_scr[c] = gtot
        return carry

    lax.fori_loop(0, nch, prepare, 0, unroll=2)

    def advance(c, carry):
        ci = _dir_chunk(c, d, nch)
        rows = pl.ds(pl.multiple_of(ci * CHUNK, CHUNK), CHUNK)
        hrows = lambda h: slice(h * CHUNK, (h + 1) * CHUNK)
        st = [s_scr[h] for h in range(HEADS)]
        v_new = [uw_scr[ci, hrows(h), 0:DH] - _dot(uw_scr[ci, hrows(h), DH:2 * DH], st[h]) for h in range(HEADS)]
        qs = jnp.concatenate([_dot(qg_scr[ci, hrows(h), :], st[h]) for h in range(HEADS)], axis=0)
        o = qs + _dot(at_scr[ci], jnp.concatenate(v_new, axis=0))
        gtot = gt_scr[ci]
        for h in range(HEADS):
            o_ref[0, rows, h * DH:(h + 1) * DH] = o[hrows(h), :]
            s_scr[h] = st[h] * jnp.exp(gtot[:, g0 + h:g0 + h + 1]) + _dot_tn(kd_scr[ci, hrows(h), :], v_new[h])
        return carry

    lax.fori_loop(0, nch, advance, 0)

    if emit_state:
        @pl.when(j == pl.num_programs(2) - 1)
        def _():
            so_ref[0, 0] = s_scr[...]


def _gdn(qkv, zs, bias, alog, batch, seq, tb, s0):
    m = batch * seq
    nt = seq // tb
    nch = tb // CHUNK
    zero_init = s0 is None
    emit_state = s0 is None
    rb = _mixer_rowblk(nt)
    spec = lambda cb: pl.BlockSpec((tb, BRANCH_W), lambda b, d, j: (rb(b, d, j), cb))
    st = pl.BlockSpec((1, 1, HEADS, DH, DH), lambda b, d, j: (b, d, 0, 0, 0))
    dirrow = pl.BlockSpec((1, 1, GATE_W), lambda b, d, j: (d, 0, 0))
    in_specs = [spec(0), spec(1), spec(2),
                pl.BlockSpec((tb, GATE_W), lambda b, d, j: (rb(b, d, j), d)), dirrow, dirrow]
    args = [qkv, qkv, qkv, zs, bias, alog]
    if not zero_init:
        in_specs.append(st)
        args.append(s0)
    out_specs = [pl.BlockSpec((1, tb, BRANCH_W), lambda b, d, j: (d, rb(b, d, j), 0))]
    out_shape = [jax.ShapeDtypeStruct((N_DIR, m, BRANCH_W), F32)]
    if emit_state:
        out_specs.append(st)
        out_shape.append(jax.ShapeDtypeStruct((batch, N_DIR, HEADS, DH, DH), F32))
    return pl.pallas_call(
        functools.partial(_gdn_kernel, nch=nch, zero_init=zero_init, emit_state=emit_state),
        grid=(batch, N_DIR, nt),
        in_specs=in_specs, out_specs=out_specs, out_shape=out_shape,
        scratch_shapes=[pltpu.VMEM((HEADS, DH, DH), F32), pltpu.VMEM((HC, HC), jnp.int32),
                        pltpu.VMEM((tb, GATE_W), F32), pltpu.VMEM((nch, 1, HC), F32),
                        pltpu.VMEM((nch, HC, 2 * DH), F32), pltpu.VMEM((nch, HC, HC), F32),
                        pltpu.VMEM((nch, HC, DH), F32), pltpu.VMEM((nch, HC, DH), F32),
                        pltpu.VMEM((nch, 1, GATE_W), F32)],
        name="gdn",
        compiler_params=_cparams(("arbitrary", "arbitrary", "arbitrary")),
    )(*args)


def _s5_kernel(*refs, rb, seg, chain, zero_init, emit_state):
    u_ref, b_ref, c_ref, lre_ref, lim_ref, ls_ref = refs[:6]
    pos = 6
    if not zero_init:
        h0r_ref, h0i_ref = refs[pos:pos + 2]
        pos += 2
    y_ref = refs[pos]
    pos += 1
    if emit_state:
        sr_ref, si_ref = refs[pos:pos + 2]
        pos += 2
    up, yp, xr, xi, er, ei, pr, pi_, car_r, car_i = refs[pos:pos + 10]

    d = pl.program_id(1)
    j = pl.program_id(3)
    w = S5_CBW
    nsub = S5_SUBLANES

    lre = lre_ref[0, 0]
    lim = lim_ref[0, 0]
    dt = jnp.exp(ls_ref[0, 0])
    mag = jnp.exp(lre * dt)
    ar = mag * jnp.cos(lim * dt)
    ai = mag * jnp.sin(lim * dt)
    den = lre * lre + lim * lim
    nr = ar - 1.0
    zr = (nr * lre + ai * lim) / den
    zi = (ai * lre - nr * lim) / den

    rc = 256
    ngrp = seg // nsub

    def interleave(g, carry):
        for s in range(nsub):
            src = pl.ds(pl.multiple_of(s * seg + g * nsub, nsub), nsub)
            up[pl.ds(g * nsub * nsub + s, nsub, stride=nsub), :] = u_ref[src, :]
        return carry

    lax.fori_loop(0, ngrp, interleave, 0)

    def fill(c, carry):
        rows = pl.ds(pl.multiple_of(c * rc, rc), rc)
        bu = _dot(up[rows, :], b_ref[0])
        bre = bu[:, :w]
        bim = bu[:, w:]
        xr[rows, :] = zr * bre - zi * bim
        xi[rows, :] = zr * bim + zi * bre
        return carry

    lax.fori_loop(0, rb // rc, fill, 0, unroll=2)

    arb = jnp.broadcast_to(ar, (nsub, w))
    aib = jnp.broadcast_to(ai, (nsub, w))

    def step_rows(tt):
        return pl.ds(pl.multiple_of(_dir_chunk(tt, d, seg) * nsub, nsub), nsub)

    def scan_step(tt, carry):
        hr, hi = carry
        idx = step_rows(tt)
        nhr = arb * hr - aib * hi + xr[idx, :]
        nhi = arb * hi + aib * hr + xi[idx, :]
        xr[idx, :] = nhr
        xi[idx, :] = nhi
        return nhr, nhi

    init = (jnp.zeros((nsub, w), F32), jnp.zeros((nsub, w), F32))
    hr, hi = lax.fori_loop(0, seg, scan_step, init, unroll=4)

    if chain:
        @pl.when(j == 0)
        def _():
            if zero_init:
                car_r[...] = jnp.zeros_like(car_r)
                car_i[...] = jnp.zeros_like(car_i)
            else:
                car_r[...] = h0r_ref[0, 0, 0]
                car_i[...] = h0i_ref[0, 0, 0]

        er[...] = hr
        ei[...] = hi
        sr, si = ar, ai
        for _ in range(int(math.log2(seg))):
            sr, si = sr * sr - si * si, 2.0 * sr * si
        fr = car_r[...]
        fi = car_i[...]
        for k in range(nsub):
            row = pl.ds(_dir_chunk(k, d, nsub), 1)
            pr[row, :] = fr
            pi_[row, :] = fi
            fr, fi = er[row, :] + (sr * fr - si * fi), ei[row, :] + (sr * fi + si * fr)
        car_r[...] = fr
        car_i[...] = fi

        def fix_step(tt, carry):
            cr, ci = carry
            idx = step_rows(tt)
            ncr = arb * cr - aib * ci
            nci = arb * ci + aib * cr
            xr[idx, :] = xr[idx, :] + ncr
            xi[idx, :] = xi[idx, :] + nci
            return ncr, nci

        lax.fori_loop(0, seg, fix_step, (pr[...], pi_[...]), unroll=4)

    if emit_state:
        sr_ref[0, 0] = hr
        si_ref[0, 0] = hi

    def proj(c, carry):
        rows = pl.ds(pl.multiple_of(c * rc, rc), rc)
        yp[rows, :] = _dot(xr[rows, :], c_ref[0, :w, :]) + _dot(xi[rows, :], c_ref[0, w:, :])
        return carry

    lax.fori_loop(0, rb // rc, proj, 0, unroll=4)

    def deinterleave(g, carry):
        for s in range(nsub):
            dst = pl.ds(pl.multiple_of(s * seg + g * nsub, nsub), nsub)
            y_ref[0, dst, :] = yp[pl.ds(g * nsub * nsub + s, nsub, stride=nsub), :]
        return carry

    lax.fori_loop(0, ngrp, deinterleave, 0)


def _s5(z, bbd, cbd, lre, lim, ls, batch, seq, h0):
    m = batch * seq
    nsub = S5_SUBLANES
    chain = h0 is not None
    zero_init = h0 is None
    emit_state = h0 is None
    if chain:
        seg = 256
        rb = nsub * seg
        ng, nt = batch, seq // rb
        st_spec = pl.BlockSpec((1, 1, 1, 1, S5_CBW), lambda g, d, cb, j: (d, cb, g, 0, 0))
    else:
        seg = seq
        rb = nsub * seg
        ng, nt = batch // nsub, 1
        st_spec = pl.BlockSpec((1, 1, nsub, S5_CBW), lambda g, d, cb, j: (d, cb, g, 0))
    rowblk = lambda g, d, j: g * nt + _dir_chunk(j, d, nt)
    lam_spec = pl.BlockSpec((1, 1, 1, S5_CBW), lambda g, d, cb, j: (d, cb, 0, 0))
    in_specs = [pl.BlockSpec((rb, 128), lambda g, d, cb, j: (rowblk(g, d, j), CB_SU * 4 + cb)),
                pl.BlockSpec((1, 128, 2 * S5_CBW), lambda g, d, cb, j: (cb, 0, 0)),
                pl.BlockSpec((1, 2 * S5_CBW, 128), lambda g, d, cb, j: (cb, 0, 0)),
                lam_spec, lam_spec, lam_spec]
    args = [z, bbd, cbd, lre, lim, ls]
    if not zero_init:
        in_specs += [st_spec, st_spec]
        args += list(h0)
    out_specs = [pl.BlockSpec((1, rb, 128), lambda g, d, cb, j: (d, rowblk(g, d, j), cb))]
    out_shape = [jax.ShapeDtypeStruct((N_DIR, m, BRANCH_W), F32)]
    if emit_state:
        out_specs += [st_spec, st_spec]
        out_shape += [jax.ShapeDtypeStruct((N_DIR, S5_CB, batch, S5_CBW), F32)] * 2
    vrow = lambda n: pltpu.VMEM((n, S5_CBW), F32)
    lanes = pltpu.VMEM((rb, 128), F32)
    return pl.pallas_call(
        functools.partial(_s5_kernel, rb=rb, seg=seg, chain=chain, zero_init=zero_init, emit_state=emit_state),
        grid=(ng, N_DIR, S5_CB, nt),
        in_specs=in_specs, out_specs=out_specs, out_shape=out_shape,
        scratch_shapes=[lanes, lanes, vrow(rb), vrow(rb), vrow(nsub), vrow(nsub), vrow(nsub), vrow(nsub),
                        vrow(1), vrow(1)],
        name="s5",
        compiler_params=_cparams(("arbitrary",) * 4),
    )(*args)


def _merge_kernel(x_ref, mod_ref, gates_ref, o_ref, u_ref, gz_ref, hm_ref, ys_ref, go_ref,
                  mlw_ref, s5d_ref, gluw_ref, glub_ref, gdw_ref, wbr_ref, wout_ref, out_ref):
    ya = _sigmoid(o_ref[...]) * _head_rms(hm_ref[0] + hm_ref[1], mlw_ref[...])
    y5 = _gelu_tanh(ys_ref[0] + ys_ref[1] + s5d_ref[...] * u_ref[...])
    yb = y5 * _sigmoid(_dot(y5, gluw_ref[...]) + glub_ref[...])
    yc = _head_rms(go_ref[0] + go_ref[1], gdw_ref[...]) * _silu(gz_ref[...])
    merged = None
    for n, y in enumerate((ya, yb, yc)):
        term = _sigmoid(gates_ref[:, n * D_MODEL:(n + 1) * D_MODEL]) * _dot(y, wbr_ref[n])
        merged = term if merged is None else merged + term
    out_ref[...] = x_ref[...] + mod_ref[0, 2:3, :] * _dot(merged, wout_ref[...])


def _merge(x, mod, z, hm, ys, go, p, seq):
    m = x.shape[0]
    tm = 256
    rows_per_mod = seq if mod.shape[0] > 1 else m
    row = lambda i: (i, 0)
    zspec = lambda cb: pl.BlockSpec((tm, BRANCH_W), lambda i: (i, cb))
    dspec = pl.BlockSpec((N_DIR, tm, BRANCH_W), lambda i: (0, i, 0))
    full = lambda a: pl.BlockSpec(a.shape, lambda i: (0,) * a.ndim)
    consts = [p['ml_norm_w'], p['s5_D'], p['s5_glu_w'], p['s5_glu_b'], p['gd_norm_w'], p['w_branch'], p['w_out']]
    return pl.pallas_call(
        _merge_kernel,
        grid=(m // tm,),
        in_specs=[pl.BlockSpec((tm, D_MODEL), row),
                  pl.BlockSpec((1, N_MOD, D_MODEL), lambda i: (i * tm // rows_per_mod, 0, 0)),
                  pl.BlockSpec((tm, 3 * D_MODEL), row),
                  zspec(CB_O), zspec(CB_SU), zspec(CB_GZ), dspec, dspec, dspec] + [full(a) for a in consts],
        out_specs=pl.BlockSpec((tm, D_MODEL), row),
        out_shape=jax.ShapeDtypeStruct((m, D_MODEL), F32),
        name="merge",
        compiler_params=_cparams(("arbitrary",)),
    )(x, mod, z, z, z, z, hm, ys, go, *consts)


def _ffn_kernel(x_ref, mod_ref, nw_ref, wg_ref, wu_ref, wd_ref, fw_ref, out_ref, h_scr, acc_scr, *, final):
    jf = pl.program_id(1)

    @pl.when(jf == 0)
    def _():
        x = x_ref[...]
        y = x * lax.rsqrt(jnp.mean(x * x, axis=-1, keepdims=True) + EPS)
        h_scr[...] = ((y * nw_ref[...]) * (1.0 + mod_ref[0, 4:5, :]) + mod_ref[0, 3:4, :]).astype(BF16)
        acc_scr[...] = jnp.zeros_like(acc_scr)

    hb = h_scr[...]
    a = _silu(jnp.dot(hb, wg_ref[...], preferred_element_type=F32))
    b = jnp.dot(hb, wu_ref[...], preferred_element_type=F32)
    acc_scr[...] += _dot(a * b, wd_ref[...])

    @pl.when(jf == pl.num_programs(1) - 1)
    def _():
        x = x_ref[...] + mod_ref[0, 5:6, :] * acc_scr[...]
        if final:
            x = x * lax.rsqrt(jnp.mean(x * x, axis=-1, keepdims=True) + EPS) * fw_ref[...]
        out_ref[...] = x


def _ffn(x, mod, p, seq, final_w, final):
    m = x.shape[0]
    tm, tf = 512, D_FF // 2
    rows_per_mod = seq if mod.shape[0] > 1 else m
    row = lambda i, jf: (i, 0)
    vec = pl.BlockSpec((1, D_MODEL), lambda i, jf: (0, 0))
    return pl.pallas_call(
        functools.partial(_ffn_kernel, final=final),
        grid=(m // tm, D_FF // tf),
        in_specs=[pl.BlockSpec((tm, D_MODEL), row),
                  pl.BlockSpec((1, N_MOD, D_MODEL), lambda i, jf: (i * tm // rows_per_mod, 0, 0)),
                  vec,
                  pl.BlockSpec((D_MODEL, tf), lambda i, jf: (0, jf)),
                  pl.BlockSpec((D_MODEL, tf), lambda i, jf: (0, jf)),
                  pl.BlockSpec((tf, D_MODEL), lambda i, jf: (jf, 0)),
                  vec],
        out_specs=pl.BlockSpec((tm, D_MODEL), row),
        out_shape=jax.ShapeDtypeStruct((m, D_MODEL), F32),
        scratch_shapes=[pltpu.VMEM((tm, D_MODEL), BF16), pltpu.VMEM((tm, D_MODEL), F32)],
        name="ffn",
        compiler_params=_cparams(("arbitrary", "arbitrary")),
    )(x, mod, p['norm2_w'], p['w_gate'], p['w_up'], p['w_down'], final_w)


def _gate_lanes(parts):
    row = jnp.concatenate([a.astype(F32) for a in parts], axis=-1)
    return jnp.pad(row, ((0, 0), (0, GATE_W - row.shape[-1]))).reshape(N_DIR, 1, GATE_W)


def _block_diag(a):
    cb, g, r, c = a.shape
    eye = jnp.eye(g, dtype=a.dtype)
    return jnp.einsum('bgrc,gh->bgrhc', a, eye).reshape(cb, g * r, g * c)


def _layer_params(l, w):
    gpb = S5_GROUPS // S5_CB
    idx, acc = [], 0
    for size in IN_SIZES[:-1]:
        acc += size
        idx.append(acc)
    mq, mk, mv, mo, mi, mf, su, gqkv, gz, ga, gb, gates = jnp.split(w['w_in'][l], idx, axis=-1)
    w_big = jnp.concatenate([gates, mq, mk, mv, mo, su, gqkv, gz], axis=-1).astype(BF16)
    smalls = []
    for d in range(N_DIR):
        sl = slice(d * HEADS, (d + 1) * HEADS)
        blk = jnp.concatenate([mi[:, sl], mf[:, sl], ga[:, sl], gb[:, sl]], axis=-1)
        smalls.append(jnp.pad(blk, ((0, 0), (0, GATE_W - 4 * HEADS))))
    w_small = jnp.concatenate(smalls, axis=-1).astype(BF16)
    zeros = jnp.zeros((N_DIR, HEADS), F32)
    gate_bias = _gate_lanes([w['ml_i_bias'][l], w['ml_f_bias'][l], w['gd_dt_bias'][l], zeros])
    gate_alog = _gate_lanes([zeros, zeros, w['gd_A_log'][l], zeros])
    b_re = jnp.swapaxes(w['s5_B_re'][l], 1, 2).reshape(S5_CB, gpb, S5_GROUP, S5_STATE)
    b_im = jnp.swapaxes(w['s5_B_im'][l], 1, 2).reshape(S5_CB, gpb, S5_GROUP, S5_STATE)
    bbd = jnp.concatenate([_block_diag(b_re), _block_diag(b_im)], axis=-1).astype(BF16)
    c_re = jnp.swapaxes(w['s5_C_re'][l], 1, 2).reshape(S5_CB, gpb, S5_STATE, S5_GROUP)
    c_im = jnp.swapaxes(w['s5_C_im'][l], 1, 2).reshape(S5_CB, gpb, S5_STATE, S5_GROUP)
    cbd = jnp.concatenate([_block_diag(c_re), -_block_diag(c_im)], axis=1).astype(BF16)
    lam_shape = (N_DIR, S5_CB, 1, S5_CBW)
    ls = jnp.broadcast_to(w['s5_log_step'][l][:, :, None], (N_DIR, S5_GROUPS, S5_STATE))
    return dict(
        norm1_w=w['norm1_w'][l], w_big=w_big, w_small=w_small, gate_bias=gate_bias, gate_alog=gate_alog,
        ml_norm_w=w['ml_norm_w'][l].reshape(1, BRANCH_W),
        bbd=bbd, cbd=cbd,
        lam_re=w['s5_lam_re'][l].reshape(lam_shape), lam_im=w['s5_lam_im'][l].reshape(lam_shape),
        log_step=ls.reshape(lam_shape),
        s5_D=w['s5_D'][l].reshape(1, BRANCH_W), s5_glu_w=w['s5_glu_w'][l].astype(BF16),
        s5_glu_b=w['s5_glu_b'][l].reshape(1, BRANCH_W),
        gd_conv_w=w['gd_conv_w'][l],
        gd_norm_w=jnp.tile(w['gd_norm_w'][l], HEADS).reshape(1, BRANCH_W),
        w_branch=w['w_branch'][l].astype(BF16), w_out=w['w_out'][l].astype(BF16),
        norm2_w=w['norm2_w'][l].reshape(1, D_MODEL),
        w_gate=w['w_gate'][l].astype(BF16), w_up=w['w_up'][l].astype(BF16), w_down=w['w_down'][l].astype(BF16),
    )


def _grid_pos_embed(n_tok):
    grid_w = 64
    t = jnp.arange(n_tok)
    quarter = D_MODEL // 4
    omega = 1.0 / (10000.0 ** (jnp.arange(quarter, dtype=F32) / quarter))

    def enc(pos):
        ang = pos.astype(F32)[:, None] * omega[None, :]
        return jnp.concatenate([jnp.sin(ang), jnp.cos(ang)], axis=-1)

    return jnp.concatenate([enc(t // grid_w), enc(t % grid_w)], axis=-1)


def _trunk_layer(x, mod, p, batch, seq, tb, states, final_w, final, pe=None):
    res = _inproj(x, mod, p['norm1_w'], p['w_big'], p['w_small'], seq, pe)
    if pe is not None:
        z, zs, x = res
    else:
        z, zs = res
    if states is None:
        ml_st = s5_st = gd_st = None
    else:
        ml_st, s5_st, gd_st = states
    ml = _mlstm(z, zs, p['gate_bias'], batch, seq, tb, ml_st)
    s5 = _s5(z, p['bbd'], p['cbd'], p['lam_re'], p['lam_im'], p['log_step'], batch, seq, s5_st)
    qkv = _gdn_prep(z, p['gd_conv_w'], seq)
    gd = _gdn(qkv, zs, p['gate_bias'], p['gate_alog'], batch, seq, tb, gd_st)
    x = _merge(x, mod, z, ml[0], s5[0], gd[0], p, seq)
    x = _ffn(x, mod, p, seq, final_w, final)
    new_states = None
    if states is None:
        new_states = (ml[1], ml[2][:, :, :, 0, :], ml[3][:, :, :, 0, 0],
                      jnp.transpose(s5[1], (2, 0, 1, 3)).reshape(batch, N_DIR, S5_GROUPS, S5_STATE),
                      jnp.transpose(s5[2], (2, 0, 1, 3)).reshape(batch, N_DIR, S5_GROUPS, S5_STATE),
                      gd[1])
    return x, new_states


def kernel(x_prompt, x_sample, state_mlstm_C, state_mlstm_n, state_mlstm_m, state_s5_re, state_s5_im,
           state_gdn_S, c, c_ctx, ada_w, ada_b, norm1_w, w_in, ml_i_bias, ml_f_bias, ml_norm_w,
           s5_lam_re, s5_lam_im, s5_log_step, s5_B_re, s5_B_im, s5_C_re, s5_C_im, s5_D, s5_glu_w, s5_glu_b,
           gd_conv_w, gd_A_log, gd_dt_bias, gd_norm_w, w_branch, w_out, norm2_w, w_gate, w_up, w_down,
           final_norm_w):
    w = dict(norm1_w=norm1_w, w_in=w_in, ml_i_bias=ml_i_bias, ml_f_bias=ml_f_bias, ml_norm_w=ml_norm_w,
             s5_lam_re=s5_lam_re, s5_lam_im=s5_lam_im, s5_log_step=s5_log_step, s5_B_re=s5_B_re,
             s5_B_im=s5_B_im, s5_C_re=s5_C_re, s5_C_im=s5_C_im, s5_D=s5_D, s5_glu_w=s5_glu_w,
             s5_glu_b=s5_glu_b, gd_conv_w=gd_conv_w, gd_A_log=gd_A_log, gd_dt_bias=gd_dt_bias,
             gd_norm_w=gd_norm_w, w_branch=w_branch, w_out=w_out, norm2_w=norm2_w, w_gate=w_gate,
             w_up=w_up, w_down=w_down)
    bp, sp, _ = x_prompt.shape
    bs, ss, _ = x_sample.shape
    params = [_layer_params(l, w) for l in range(DEPTH)]
    final_w = final_norm_w.reshape(1, D_MODEL)

    cc = jnp.concatenate([c_ctx[None, :], c, jnp.zeros((8 - 1 - bs, D_MODEL), F32)], axis=0)
    mods = _modulation(cc, ada_w, ada_b).reshape(DEPTH, 8, N_MOD, D_MODEL)

    xp = x_prompt.reshape(bp * sp, D_MODEL)
    per_layer = []
    for l in range(DEPTH):
        xp, st = _trunk_layer(xp, mods[l, 0:1], params[l], bp, sp, sp, None, final_w, l == DEPTH - 1)
        per_layer.append(st)
    y_prompt = xp.reshape(bp, sp, D_MODEL)
    new_states = [jnp.stack([st[i] for st in per_layer], axis=1) for i in range(6)]

    xs = x_sample.reshape(bs * ss, D_MODEL)
    pe = _grid_pos_embed(ss)
    for l in range(DEPTH):
        ml_st = (state_mlstm_C[:, l],
                 state_mlstm_n[:, l][:, :, :, None, :],
                 jnp.broadcast_to(state_mlstm_m[:, l][:, :, :, None, None], (bs, N_DIR, HEADS, 1, DH)))
        s5_st = tuple(jnp.transpose(a[:, l].reshape(bs, N_DIR, S5_CB, 1, S5_CBW), (1, 2, 0, 3, 4))
                      for a in (state_s5_re, state_s5_im))
        xs, _ = _trunk_layer(xs, mods[l, 1:1 + bs], params[l], bs, ss, 512, (ml_st, s5_st, state_gdn_S[:, l]),
                             final_w, l == DEPTH - 1, pe if l == 0 else None)
    y_sample = xs.reshape(bs, ss, D_MODEL)
    return (y_prompt, y_sample, *new_states)
```

```python
import functools
import math

import jax
import jax.numpy as jnp
from jax import lax
from jax.experimental import pallas as pl
from jax.experimental.pallas import tpu as pltpu

F32 = jnp.float32
BF16 = jnp.bfloat16

D_MODEL = 1024
DEPTH = 2
N_DIR = 2
N_MOD = 6
EPS = 1e-6
HEADS = 4
DH = 128
BRANCH_W = HEADS * DH
CHUNK = 64
S5_GROUPS = 32
S5_GROUP = 16
S5_STATE = 64
S5_CB = 4
S5_CBW = S5_GROUPS * S5_STATE // S5_CB
S5_SUBLANES = 8
CONV_K = 5
D_FF = -(-8 * D_MODEL // (3 * 256)) * 256
IN_SIZES = (512, 512, 512, 512, 8, 8, 512, 1536, 512, 8, 8, 3072)
GATE_W = 128

CB_GATES, CB_Q, CB_K, CB_V, CB_O, CB_SU, CB_GQKV, CB_GZ = 0, 6, 7, 8, 9, 10, 11, 14
Z_W = 15 * BRANCH_W

VMEM_LIMIT = 56 * 1024 * 1024


def _cparams(sem):
    return pltpu.CompilerParams(dimension_semantics=sem, vmem_limit_bytes=VMEM_LIMIT)


def _dot(a, b):
    return jnp.dot(a.astype(BF16), b.astype(BF16), preferred_element_type=F32)


def _dot_nt(a, b):
    return lax.dot_general(a.astype(BF16), b.astype(BF16), (((1,), (1,)), ((), ())),
                           preferred_element_type=F32)


def _dot_tn(a, b):
    return lax.dot_general(a.astype(BF16), b.astype(BF16), (((0,), (0,)), ((), ())),
                           preferred_element_type=F32)


def _split3(x):
    hi = x.astype(BF16)
    r1 = x - hi.astype(F32)
    mid = r1.astype(BF16)
    lo = (r1 - mid.astype(F32)).astype(BF16)
    return hi, mid, lo


def _sel_dot(sel, x, nt=False):
    dims = (((1,), (1,)), ((), ())) if nt else (((1,), (0,)), ((), ()))
    hi, mid, lo = _split3(x)
    f = lambda p: lax.dot_general(sel, p, dims, preferred_element_type=F32)
    return (f(hi) + f(mid)) + f(lo)


def _dot_hp(a, b):
    ah = a.astype(BF16)
    al = (a - ah.astype(F32)).astype(BF16)
    bh = b.astype(BF16)
    bl = (b - bh.astype(F32)).astype(BF16)
    f = lambda p, q: jnp.dot(p, q, preferred_element_type=F32)
    return f(ah, bh) + (f(ah, bl) + f(al, bh))


def _sigmoid(x):
    return 1.0 / (1.0 + jnp.exp(-x))


def _silu(x):
    return x * _sigmoid(x)


def _softplus(x):
    return jnp.maximum(x, 0.0) + jnp.log(1.0 + jnp.exp(-jnp.abs(x)))


def _gelu_tanh(x):
    c = math.sqrt(2.0 / math.pi)
    return 0.5 * x * (1.0 + jnp.tanh(c * (x + 0.044715 * (x * x * x))))


def _head_rms(x, w_row):
    outs = []
    for h in range(HEADS):
        xh = x[:, h * DH:(h + 1) * DH]
        outs.append(xh * lax.rsqrt(jnp.mean(xh * xh, axis=-1, keepdims=True) + EPS))
    return jnp.concatenate(outs, axis=-1) * w_row


def _dir_chunk(c, d, n):
    return c + d * (n - 1 - 2 * c)


def _mod_kernel(c_ref, w_ref, b_ref, o_ref):
    o_ref[0] = _dot(_silu(c_ref[...]), w_ref[0]) + b_ref[0]


def _modulation(cc, ada_w, ada_b):
    tn = 1536
    nmod = N_MOD * D_MODEL
    return pl.pallas_call(
        _mod_kernel,
        grid=(DEPTH, nmod // tn),
        in_specs=[pl.BlockSpec((8, D_MODEL), lambda l, j: (0, 0)),
                  pl.BlockSpec((1, D_MODEL, tn), lambda l, j: (l, 0, j)),
                  pl.BlockSpec((1, 1, tn), lambda l, j: (l, 0, j))],
        out_specs=pl.BlockSpec((1, 8, tn), lambda l, j: (l, 0, j)),
        out_shape=jax.ShapeDtypeStruct((DEPTH, 8, nmod), F32),
        name="adaln_mod",
        compiler_params=_cparams(("arbitrary", "arbitrary")),
    )(cc, ada_w, ada_b.reshape(DEPTH, 1, nmod))


def _inproj_kernel(*refs, has_pe):
    if has_pe:
        x_ref, pe_ref, mod_ref, nw_ref, w_ref, ws_ref, z_ref, zs_ref, xs_ref, hn_scr = refs
    else:
        x_ref, mod_ref, nw_ref, w_ref, ws_ref, z_ref, zs_ref, hn_scr = refs

    @pl.when(pl.program_id(1) == 0)
    def _():
        x = x_ref[...]
        if has_pe:
            x = x + pe_ref[...]
            xs_ref[...] = x
        y = x * lax.rsqrt(jnp.mean(x * x, axis=-1, keepdims=True) + EPS)
        h = (y * nw_ref[...]) * (1.0 + mod_ref[0, 1:2, :]) + mod_ref[0, 0:1, :]
        hb = h.astype(BF16)
        hn_scr[...] = hb
        zs_ref[...] = jnp.dot(hb, ws_ref[...], preferred_element_type=F32)

    z_ref[...] = jnp.dot(hn_scr[...], w_ref[...], preferred_element_type=F32)


def _inproj(x, mod, norm_w, w_big, w_small, seq, pe=None):
    m = x.shape[0]
    tm, tn = 1024, 1536
    rows_per_mod = seq if mod.shape[0] > 1 else m
    has_pe = pe is not None
    row = lambda i, j: (i, 0)
    in_specs = [pl.BlockSpec((tm, D_MODEL), row)]
    args = [x]
    if has_pe:
        in_specs.append(pl.BlockSpec((tm, D_MODEL), lambda i, j: (i % (seq // tm), 0)))
        args.append(pe)
    in_specs += [pl.BlockSpec((1, N_MOD, D_MODEL), lambda i, j: (i * tm // rows_per_mod, 0, 0)),
                 pl.BlockSpec((1, D_MODEL), lambda i, j: (0, 0)),
                 pl.BlockSpec((D_MODEL, tn), lambda i, j: (0, j)),
                 pl.BlockSpec((D_MODEL, N_DIR * GATE_W), lambda i, j: (0, 0))]
    args += [mod, norm_w.reshape(1, D_MODEL), w_big, w_small]
    out_specs = [pl.BlockSpec((tm, tn), lambda i, j: (i, j)),
                 pl.BlockSpec((tm, N_DIR * GATE_W), row)]
    out_shape = [jax.ShapeDtypeStruct((m, Z_W), F32),
                 jax.ShapeDtypeStruct((m, N_DIR * GATE_W), F32)]
    if has_pe:
        out_specs.append(pl.BlockSpec((tm, D_MODEL), row))
        out_shape.append(jax.ShapeDtypeStruct((m, D_MODEL), F32))
    return pl.pallas_call(
        functools.partial(_inproj_kernel, has_pe=has_pe),
        grid=(m // tm, Z_W // tn),
        in_specs=in_specs, out_specs=out_specs, out_shape=out_shape,
        scratch_shapes=[pltpu.VMEM((tm, D_MODEL), BF16)],
        name="inproj",
        compiler_params=_cparams(("arbitrary", "arbitrary")),
    )(*args)


def _mixer_rowblk(nt):
    return lambda b, d, j: b * nt + _dir_chunk(j, d, nt)


def _gate_selector():
    r = lax.broadcasted_iota(jnp.int32, (16, GATE_W), 0)
    c = lax.broadcasted_iota(jnp.int32, (16, GATE_W), 1)
    return (r == c).astype(BF16)


def _block_gate_sums(vals, other, n_other, cls_scr, cum_scr):
    tri = (cls_scr[...] > 0).astype(BF16)
    for g in range(vals.shape[0] // HC):
        grp = slice(g * HC, (g + 1) * HC)
        cum_scr[grp, :] = _sel_dot(tri, vals[grp, :])
    lane = lax.broadcasted_iota(jnp.int32, vals.shape, 1)
    mixed = jnp.where(lane < n_other, other, cum_scr[...])
    return _sel_dot(_gate_selector(), mixed, nt=True), mixed


def _lane_dense(x, lanes):
    r = lax.broadcasted_iota(jnp.int32, (GATE_W, len(lanes) * DH), 0)
    c = lax.broadcasted_iota(jnp.int32, (GATE_W, len(lanes) * DH), 1)
    pick = functools.reduce(jnp.logical_or, [jnp.logical_and(c // DH == i, r == ln) for i, ln in enumerate(lanes)])
    e = pick.astype(BF16)
    hi, mid, lo = _split3(x)
    f = lambda p: jnp.dot(p, e, preferred_element_type=F32)
    return (f(hi) + f(mid)) + f(lo)


def _mlstm_kernel(*refs, nch, zero_init, emit_state):
    q_ref, k_ref, v_ref, g_ref, bias_ref = refs[:5]
    pos = 5
    if not zero_init:
        c0_ref, m0_ref = refs[pos:pos + 2]
        pos += 2
    h_ref = refs[pos]
    pos += 1
    if emit_state:
        co_ref, mo_ref = refs[pos:pos + 2]
        pos += 2
    c_scr, m_scr, cls_scr, cum_scr, row_scr, dense_scr, av_scr, ml_scr = refs[pos:pos + 8]

    d = pl.program_id(1)
    j = pl.program_id(2)

    @pl.when(j == 0)
    def _():
        if zero_init:
            c_scr[...] = jnp.zeros_like(c_scr)
            m_scr[...] = jnp.zeros_like(m_scr)
        else:
            c_scr[...] = c0_ref[0, 0]
            m_scr[...] = m0_ref[0, 0]

    cls_scr[...] = _pair_classes(d)
    rowid = lax.broadcasted_iota(jnp.int32, (CHUNK, DH), 0)
    last = (CHUNK - 1) * (1 - d)
    bias = bias_ref[0]
    scale = DH ** -0.5
    hrows = lambda h: slice(h * CHUNK, (h + 1) * CHUNK)
    per_head = lambda vals: jnp.concatenate([jnp.broadcast_to(a, (CHUNK, a.shape[1])) for a in vals], axis=0)
    twice = lambda a: jnp.concatenate([a, a], axis=1)
    ones = jnp.ones((HC, DH), F32)

    pre_all = g_ref[...] + bias
    xt, mixed = _block_gate_sums(-_softplus(-pre_all), pre_all, HEADS, cls_scr, cum_scr)
    for c in range(nch):
        cs = slice(c * CHUNK, (c + 1) * CHUNK)
        row_scr[c] = jnp.concatenate([xt[h:h + 1, cs] - xt[HEADS + h:HEADS + h + 1, cs] for h in range(HEADS)],
                                     axis=1)
    for h in range(HEADS):
        dense_scr[h] = _lane_dense(mixed, (HEADS + h, h))

    stacked = lambda rows, lo: jnp.concatenate([dense_scr[h, rows, lo:lo + DH] for h in range(HEADS)], axis=0)
    npar = MLSTM_LOCKSTEP

    def local(cp, carry):
        cs = [cp * npar + i for i in range(npar)]
        rows = [pl.ds(pl.multiple_of(c * CHUNK, CHUNK), CHUNK) for c in cs]
        cls = cls_scr[...]
        log_d = [jnp.where(cls > 0, twice(stacked(r, 0)) + row_scr[c], -jnp.inf) for r, c in zip(rows, cs)]
        ml = [jnp.max(a, axis=-1, keepdims=True) for a in log_d]
        q = [_stack_heads(q_ref[r, :]) for r in rows]
        k = [_stack_heads(k_ref[r, :]) * scale for r in rows]
        v = [jnp.concatenate([_stack_heads(v_ref[r, :]), ones], axis=1) for r in rows]
        s = [_dot_nt(a, b) * jnp.exp(ld - m) for a, b, ld, m in zip(q, k, log_d, ml)]
        av = [_dot(a, b) for a, b in zip(s, v)]
        for i, c in enumerate(cs):
            av_scr[c] = av[i]
            ml_scr[c] = jnp.broadcast_to(ml[i], (HC, DH))
        return carry

    lax.fori_loop(0, nch // npar, local, 0)

    def advance(c, carry):
        ci = _dir_chunk(c, d, nch)
        r0 = pl.multiple_of(ci * CHUNK, CHUNK)
        rows = pl.ds(r0, CHUNK)
        b_tok = stacked(rows, 0)
        i_tok = stacked(rows, DH)
        at_last = lambda a: [jnp.sum(jnp.where(rowid == last, a[hrows(h), :], 0.0), axis=0, keepdims=True)
                             for h in range(HEADS)]
        b_last = at_last(b_tok)
        m_prev = [m_scr[h] for h in range(HEADS)]
        ml = ml_scr[ci]
        log_0 = b_tok + per_head(m_prev)
        m_t = jnp.maximum(log_0, ml)
        w_0 = jnp.exp(log_0 - m_t)
        f = jnp.exp(ml - m_t)
        q = _stack_heads(q_ref[rows, :])
        k = _stack_heads(k_ref[rows, :]) * scale
        v = jnp.concatenate([_stack_heads(v_ref[rows, :]), ones], axis=1)
        cst = [c_scr[h] for h in range(HEADS)]
        qc = jnp.concatenate([_dot(q[hrows(h), :], cst[h]) for h in range(HEADS)], axis=0)
        num = f * av_scr[ci, :, 0:DH] + w_0 * qc[:, 0:DH]
        den = f * av_scr[ci, :, DH:2 * DH] + w_0 * qc[:, DH:2 * DH]
        hv = num / jnp.maximum(jnp.abs(den), jnp.exp(-m_t))
        m_new = at_last(m_t)
        kw = k * jnp.exp(per_head(b_last) - b_tok + i_tok - per_head(m_new))
        for h in range(HEADS):
            h_ref[0, rows, h * DH:(h + 1) * DH] = hv[hrows(h), :]
            c_0 = jnp.exp(b_last[h] + m_prev[h] - m_new[h])
            c_scr[h] = twice(c_0) * cst[h] + _dot_tn(kw[hrows(h), :], v[hrows(h), :])
            m_scr[h] = m_new[h]
        return carry

    lax.fori_loop(0, nch, advance, 0)

    if emit_state:
        @pl.when(j == pl.num_programs(2) - 1)
        def _():
            co_ref[0, 0] = c_scr[...]
            mo_ref[0, 0] = m_scr[...]


def _mlstm(z, zs, bias, batch, seq, tb, states):
    m = batch * seq
    nt = seq // tb
    nch = tb // CHUNK
    zero_init = states is None
    emit_state = states is None
    rb = _mixer_rowblk(nt)
    zspec = lambda cb: pl.BlockSpec((tb, BRANCH_W), lambda b, d, j: (rb(b, d, j), cb))
    st5 = lambda shape: pl.BlockSpec((1, 1) + shape, lambda b, d, j: (b, d) + (0,) * len(shape))
    in_specs = [zspec(CB_Q), zspec(CB_K), zspec(CB_V),
                pl.BlockSpec((tb, GATE_W), lambda b, d, j: (rb(b, d, j), d)),
                pl.BlockSpec((1, 1, GATE_W), lambda b, d, j: (d, 0, 0))]
    args = [z, z, z, zs, bias]
    if not zero_init:
        in_specs += [st5((HEADS, DH, 2 * DH)), st5((HEADS, 1, DH))]
        args += list(states)
    out_specs = [pl.BlockSpec((1, tb, BRANCH_W), lambda b, d, j: (d, rb(b, d, j), 0))]
    out_shape = [jax.ShapeDtypeStruct((N_DIR, m, BRANCH_W), F32)]
    if emit_state:
        out_specs += [st5((HEADS, DH, 2 * DH)), st5((HEADS, 1, DH))]
        out_shape += [jax.ShapeDtypeStruct((batch, N_DIR, HEADS, DH, 2 * DH), F32),
                      jax.ShapeDtypeStruct((batch, N_DIR, HEADS, 1, DH), F32)]
    return pl.pallas_call(
        functools.partial(_mlstm_kernel, nch=nch, zero_init=zero_init, emit_state=emit_state),
        grid=(batch, N_DIR, nt),
        in_specs=in_specs, out_specs=out_specs, out_shape=out_shape,
        scratch_shapes=[pltpu.VMEM((HEADS, DH, 2 * DH), F32), pltpu.VMEM((HEADS, 1, DH), F32),
                        pltpu.VMEM((HC, HC), jnp.int32),
                        pltpu.VMEM((tb, GATE_W), F32), pltpu.VMEM((nch, 1, HC), F32),
                        pltpu.VMEM((HEADS, tb, 2 * DH), F32),
                        pltpu.VMEM((nch, HC, 2 * DH), F32), pltpu.VMEM((nch, HC, DH), F32)],
        name="mlstm",
        compiler_params=_cparams(("arbitrary", "arbitrary", "arbitrary")),
    )(*args)


def _gdn_prep_kernel(x_ref, prev_ref, next_ref, w_ref, o_ref, xe_scr, *, tb, seq):
    i = pl.program_id(0)
    p = pl.program_id(1)
    pad = 8
    at_start = (i * tb) % seq == 0
    at_end = ((i + 1) * tb) % seq == 0
    xe_scr[0:pad, :] = jnp.where(at_start, 0.0, prev_ref[...])
    xe_scr[pad:pad + tb, :] = x_ref[...]
    xe_scr[pad + tb:pad + tb + pad, :] = jnp.where(at_end, 0.0, next_ref[...])
    acc = None
    for t in range(CONV_K):
        term = xe_scr[pl.ds(pad + t - CONV_K // 2, tb), :] * w_ref[t:t + 1, :]
        acc = term if acc is None else acc + term
    y = _silu(acc)
    outs = []
    for h in range(HEADS):
        yh = y[:, h * DH:(h + 1) * DH]
        outs.append(yh * lax.rsqrt(jnp.sum(yh * yh, axis=-1, keepdims=True) + EPS))
    yn = jnp.concatenate(outs, axis=-1)
    qscale = jnp.where(p == 0, DH ** -0.5, 1.0)
    o_ref[...] = jnp.where(p == 2, y, yn * qscale)


def _gdn_prep(z, conv_w, seq):
    m = z.shape[0]
    tb = 256
    nb8 = m // 8
    return pl.pallas_call(
        functools.partial(_gdn_prep_kernel, tb=tb, seq=seq),
        grid=(m // tb, 3),
        in_specs=[pl.BlockSpec((tb, BRANCH_W), lambda i, p: (i, CB_GQKV + p)),
                  pl.BlockSpec((8, BRANCH_W), lambda i, p: (jnp.maximum(i * (tb // 8) - 1, 0), CB_GQKV + p)),
                  pl.BlockSpec((8, BRANCH_W), lambda i, p: (jnp.minimum((i + 1) * (tb // 8), nb8 - 1), CB_GQKV + p)),
                  pl.BlockSpec((CONV_K, BRANCH_W), lambda i, p: (0, p))],
        out_specs=pl.BlockSpec((tb, BRANCH_W), lambda i, p: (i, p)),
        out_shape=jax.ShapeDtypeStruct((m, 3 * BRANCH_W), F32),
        scratch_shapes=[pltpu.VMEM((tb + 16, BRANCH_W), F32)],
        name="gdn_prep",
        compiler_params=_cparams(("arbitrary", "arbitrary")),
    )(z, z, z, conv_w)


HC = HEADS * CHUNK


def _stack_heads(x):
    return jnp.concatenate([x[:, h * DH:(h + 1) * DH] for h in range(HEADS)], axis=0)


def _stack_gate(x, lane0):
    return jnp.concatenate([x[:, lane0 + h:lane0 + h + 1] for h in range(HEADS)], axis=0)


GDN_LOCKSTEP = 4
MLSTM_LOCKSTEP = 4
BASE_LG = 3
CHUNK_LG = int(math.log2(CHUNK))


def _pair_classes(d):
    r = lax.broadcasted_iota(jnp.int32, (HC, HC), 0)
    c = lax.broadcasted_iota(jnp.int32, (HC, HC), 1)
    cls = jnp.full((HC, HC), CHUNK_LG, jnp.int32)
    for s in range(CHUNK_LG - 1, BASE_LG - 1, -1):
        cls = jnp.where((r >> s) == (c >> s), s, cls)
    diff = (r - c) * (1 - 2 * d)
    cls = jnp.where(diff == 0, 1, cls)
    return jnp.where(jnp.logical_and((r >> CHUNK_LG) == (c >> CHUNK_LG), diff >= 0), cls, 0)


def _unit_tri_solve(xs, cls, rhss):
    n = xs[0].shape[0]
    nr = rhss[0].shape[1]
    x0 = [jnp.where(cls == BASE_LG, x, 0.0) for x in xs]
    x2 = [_dot(a, a) for a in x0]
    y = [_dot(b, jnp.concatenate([b, a], axis=1)) for a, b in zip(x0, x2)]
    x4 = [t[:, :n] for t in y]
    q2 = [a + b + t[:, n:] for a, b, t in zip(x0, x2, y)]
    t8m = [a + b + _dot(b, a) for a, b in zip(q2, x4)]
    levels = range(BASE_LG + 1, CHUNK_LG + 1)
    cur = [jnp.concatenate([r] + [jnp.where(cls == s, x, 0.0) for s in levels], axis=1) for x, r in zip(xs, rhss)]
    cur = [a + _dot(t, a) for a, t in zip(cur, t8m)]
    for _ in levels:
        last = cur[0].shape[1] == nr + n
        rest = [a[:, :nr] if last else jnp.concatenate([a[:, :nr], a[:, nr + n:]], axis=1) for a in cur]
        cur = [r + _dot(a[:, nr:nr + n], r) for a, r in zip(cur, rest)]
    return cur


def _gdn_kernel(*refs, nch, zero_init, emit_state):
    q_ref, k_ref, v_ref, g_ref, bias_ref, alog_ref = refs[:6]
    pos = 6
    if not zero_init:
        s0_ref = refs[pos]
        pos += 1
    o_ref = refs[pos]
    pos += 1
    if emit_state:
        so_ref = refs[pos]
        pos += 1
    s_scr, cls_scr, cum_scr, row_scr, uw_scr, at_scr, qg_scr, kd_scr, gt_scr = refs[pos:pos + 9]

    d = pl.program_id(1)
    j = pl.program_id(2)

    @pl.when(j == 0)
    def _():
        if zero_init:
            s_scr[...] = jnp.zeros_like(s_scr)
        else:
            s_scr[...] = s0_ref[0, 0]

    cls_scr[...] = _pair_classes(d)
    last = (CHUNK - 1) * (1 - d)
    bias = bias_ref[0]
    neg_a = -jnp.exp(alog_ref[0])
    g0 = 2 * HEADS

    g_all = neg_a * _softplus(g_ref[...] + bias)
    xt, _ = _block_gate_sums(g_all, g_all, 0, cls_scr, cum_scr)
    for c in range(nch):
        cs = slice(c * CHUNK, (c + 1) * CHUNK)
        row_scr[c] = jnp.concatenate([xt[g0 + h:g0 + h + 1, cs] for h in range(HEADS)], axis=1)

    npar = GDN_LOCKSTEP

    def prepare(cp, carry):
        cs = [cp * npar + i for i in range(npar)]
        r0 = [pl.multiple_of(c * CHUNK, CHUNK) for c in cs]
        rows = [pl.ds(r, CHUNK) for r in r0]
        gtot = [cum_scr[pl.ds(r + last, 1), :] for r in r0]
        g_col = [_stack_gate(cum_scr[r, :], g0) for r in rows]
        beta = [_stack_gate(_sigmoid(g_ref[r, :] + bias), g0 + HEADS) for r in rows]
        g_last = [jnp.concatenate([jnp.broadcast_to(t[:, g0 + h:g0 + h + 1], (CHUNK, 1)) for h in range(HEADS)],
                                  axis=0) for t in gtot]
        cls = cls_scr[...]
        decay = [jnp.exp(jnp.where(cls > 0, gc - row_scr[c], -jnp.inf)) for gc, c in zip(g_col, cs)]
        eg = [jnp.exp(gc) for gc in g_col]
        q = [_stack_heads(q_ref[r, :]) for r in rows]
        k = [_stack_heads(k_ref[r, :]) for r in rows]
        v = [_stack_heads(v_ref[r, :]) for r in rows]
        kb = [a * b for a, b in zip(k, beta)]
        x = [-(_dot_nt(a, b) * dc) for a, b, dc in zip(kb, k, decay)]
        rhs = [jnp.concatenate([a * b, kbi * e], axis=-1) for a, b, kbi, e in zip(v, beta, kb, eg)]
        uw = _unit_tri_solve(x, cls, rhs)
        attn = [_dot_nt(a, b) * dc for a, b, dc in zip(q, k, decay)]
        for i, c in enumerate(cs):
            uw_scr[c] = uw[i]
            at_scr[c] = attn[i]
            qg_scr[c] = q[i] * eg[i]
            kd_scr[c] = k[i] * jnp.exp(g_last[i] - g_col[i])
            gt_scr[c] = gtot[i]
        return carry

    lax.fori_loop(0, nch // npar, prepare, 0)

    def advance(c, carry):
        ci = _dir_chunk(c, d, nch)
        rows = pl.ds(pl.multiple_of(ci * CHUNK, CHUNK), CHUNK)
        hrows = lambda h: slice(h * CHUNK, (h + 1) * CHUNK)
        st = [s_scr[h] for h in range(HEADS)]
        v_new = [uw_scr[ci, hrows(h), 0:DH] - _dot(uw_scr[ci, hrows(h), DH:2 * DH], st[h]) for h in range(HEADS)]
        qs = jnp.concatenate([_dot(qg_scr[ci, hrows(h), :], st[h]) for h in range(HEADS)], axis=0)
        o = qs + _dot(at_scr[ci], jnp.concatenate(v_new, axis=0))
        gtot = gt_scr[ci]
        for h in range(HEADS):
            o_ref[0, rows, h * DH:(h + 1) * DH] = o[hrows(h), :]
            s_scr[h] = st[h] * jnp.exp(gtot[:, g0 + h:g0 + h + 1]) + _dot_tn(kd_scr[ci, hrows(h), :], v_new[h])
        return carry

    lax.fori_loop(0, nch, advance, 0)

    if emit_state:
        @pl.when(j == pl.num_programs(2) - 1)
        def _():
            so_ref[0, 0] = s_scr[...]


def _gdn(qkv, zs, bias, alog, batch, seq, tb, s0):
    m = batch * seq
    nt = seq // tb
    nch = tb // CHUNK
    zero_init = s0 is None
    emit_state = s0 is None
    rb = _mixer_rowblk(nt)
    spec = lambda cb: pl.BlockSpec((tb, BRANCH_W), lambda b, d, j: (rb(b, d, j), cb))
    st = pl.BlockSpec((1, 1, HEADS, DH, DH), lambda b, d, j: (b, d, 0, 0, 0))
    dirrow = pl.BlockSpec((1, 1, GATE_W), lambda b, d, j: (d, 0, 0))
    in_specs = [spec(0), spec(1), spec(2),
                pl.BlockSpec((tb, GATE_W), lambda b, d, j: (rb(b, d, j), d)), dirrow, dirrow]
    args = [qkv, qkv, qkv, zs, bias, alog]
    if not zero_init:
        in_specs.append(st)
        args.append(s0)
    out_specs = [pl.BlockSpec((1, tb, BRANCH_W), lambda b, d, j: (d, rb(b, d, j), 0))]
    out_shape = [jax.ShapeDtypeStruct((N_DIR, m, BRANCH_W), F32)]
    if emit_state:
        out_specs.append(st)
        out_shape.append(jax.ShapeDtypeStruct((batch, N_DIR, HEADS, DH, DH), F32))
    return pl.pallas_call(
        functools.partial(_gdn_kernel, nch=nch, zero_init=zero_init, emit_state=emit_state),
        grid=(batch, N_DIR, nt),
        in_specs=in_specs, out_specs=out_specs, out_shape=out_shape,
        scratch_shapes=[pltpu.VMEM((HEADS, DH, DH), F32), pltpu.VMEM((HC, HC), jnp.int32),
                        pltpu.VMEM((tb, GATE_W), F32), pltpu.VMEM((nch, 1, HC), F32),
                        pltpu.VMEM((nch, HC, 2 * DH), F32), pltpu.VMEM((nch, HC, HC), F32),
                        pltpu.VMEM((nch, HC, DH), F32), pltpu.VMEM((nch, HC, DH), F32),
                        pltpu.VMEM((nch, 1, GATE_W), F32)],
        name="gdn",
        compiler_params=_cparams(("arbitrary", "arbitrary", "arbitrary")),
    )(*args)


def _s5_kernel(*refs, rb, seg, chain, zero_init, emit_state):
    u_ref, b_ref, c_ref, lre_ref, lim_ref, ls_ref = refs[:6]
    pos = 6
    if not zero_init:
        h0r_ref, h0i_ref = refs[pos:pos + 2]
        pos += 2
    y_ref = refs[pos]
    pos += 1
    if emit_state:
        sr_ref, si_ref = refs[pos:pos + 2]
        pos += 2
    up, yp, xr, xi, er, ei, pr, pi_, car_r, car_i = refs[pos:pos + 10]

    d = pl.program_id(1)
    j = pl.program_id(3)
    w = S5_CBW
    nsub = S5_SUBLANES

    lre = lre_ref[0, 0]
    lim = lim_ref[0, 0]
    dt = jnp.exp(ls_ref[0, 0])
    mag = jnp.exp(lre * dt)
    ar = mag * jnp.cos(lim * dt)
    ai = mag * jnp.sin(lim * dt)
    den = lre * lre + lim * lim
    nr = ar - 1.0
    zr = (nr * lre + ai * lim) / den
    zi = (ai * lre - nr * lim) / den

    rc = 256
    ngrp = seg // nsub

    def interleave(g, carry):
        for s in range(nsub):
            src = pl.ds(pl.multiple_of(s * seg + g * nsub, nsub), nsub)
            up[pl.ds(g * nsub * nsub + s, nsub, stride=nsub), :] = u_ref[src, :]
        return carry

    lax.fori_loop(0, ngrp, interleave, 0)

    def fill(c, carry):
        rows = pl.ds(pl.multiple_of(c * rc, rc), rc)
        bu = _dot(up[rows, :], b_ref[0])
        bre = bu[:, :w]
        bim = bu[:, w:]
        xr[rows, :] = zr * bre - zi * bim
        xi[rows, :] = zr * bim + zi * bre
        return carry

    lax.fori_loop(0, rb // rc, fill, 0, unroll=2)

    arb = jnp.broadcast_to(ar, (nsub, w))
    aib = jnp.broadcast_to(ai, (nsub, w))

    def step_rows(tt):
        return pl.ds(pl.multiple_of(_dir_chunk(tt, d, seg) * nsub, nsub), nsub)

    def scan_step(tt, carry):
        hr, hi = carry
        idx = step_rows(tt)
        nhr = arb * hr - aib * hi + xr[idx, :]
        nhi = arb * hi + aib * hr + xi[idx, :]
        xr[idx, :] = nhr
        xi[idx, :] = nhi
        return nhr, nhi

    init = (jnp.zeros((nsub, w), F32), jnp.zeros((nsub, w), F32))
    hr, hi = lax.fori_loop(0, seg, scan_step, init, unroll=4)

    if chain:
        @pl.when(j == 0)
        def _():
            if zero_init:
                car_r[...] = jnp.zeros_like(car_r)
                car_i[...] = jnp.zeros_like(car_i)
            else:
                car_r[...] = h0r_ref[0, 0, 0]
                car_i[...] = h0i_ref[0, 0, 0]

        er[...] = hr
        ei[...] = hi
        sr, si = ar, ai
        for _ in range(int(math.log2(seg))):
            sr, si = sr * sr - si * si, 2.0 * sr * si
        fr = car_r[...]
        fi = car_i[...]
        for k in range(nsub):
            row = pl.ds(_dir_chunk(k, d, nsub), 1)
            pr[row, :] = fr
            pi_[row, :] = fi
            fr, fi = er[row, :] + (sr * fr - si * fi), ei[row, :] + (sr * fi + si * fr)
        car_r[...] = fr
        car_i[...] = fi

        def fix_step(tt, carry):
            cr, ci = carry
            idx = step_rows(tt)
            ncr = arb * cr - aib * ci
            nci = arb * ci + aib * cr
            xr[idx, :] = xr[idx, :] + ncr
            xi[idx, :] = xi[idx, :] + nci
            return ncr, nci

        lax.fori_loop(0, seg, fix_step, (pr[...], pi_[...]), unroll=4)

    if emit_state:
        sr_ref[0, 0] = hr
        si_ref[0, 0] = hi

    def proj(c, carry):
        rows = pl.ds(pl.multiple_of(c * rc, rc), rc)
        yp[rows, :] = _dot(xr[rows, :], c_ref[0, :w, :]) + _dot(xi[rows, :], c_ref[0, w:, :])
        return carry

    lax.fori_loop(0, rb // rc, proj, 0, unroll=4)

    def deinterleave(g, carry):
        for s in range(nsub):
            dst = pl.ds(pl.multiple_of(s * seg + g * nsub, nsub), nsub)
            y_ref[0, dst, :] = yp[pl.ds(g * nsub * nsub + s, nsub, stride=nsub), :]
        return carry

    lax.fori_loop(0, ngrp, deinterleave, 0)


def _s5(z, bbd, cbd, lre, lim, ls, batch, seq, h0):
    m = batch * seq
    nsub = S5_SUBLANES
    chain = h0 is not None
    zero_init = h0 is None
    emit_state = h0 is None
    if chain:
        seg = 256
        rb = nsub * seg
        ng, nt = batch, seq // rb
        st_spec = pl.BlockSpec((1, 1, 1, 1, S5_CBW), lambda g, d, cb, j: (d, cb, g, 0, 0))
    else:
        seg = seq
        rb = nsub * seg
        ng, nt = batch // nsub, 1
        st_spec = pl.BlockSpec((1, 1, nsub, S5_CBW), lambda g, d, cb, j: (d, cb, g, 0))
    rowblk = lambda g, d, j: g * nt + _dir_chunk(j, d, nt)
    lam_spec = pl.BlockSpec((1, 1, 1, S5_CBW), lambda g, d, cb, j: (d, cb, 0, 0))
    in_specs = [pl.BlockSpec((rb, 128), lambda g, d, cb, j: (rowblk(g, d, j), CB_SU * 4 + cb)),
                pl.BlockSpec((1, 128, 2 * S5_CBW), lambda g, d, cb, j: (cb, 0, 0)),
                pl.BlockSpec((1, 2 * S5_CBW, 128), lambda g, d, cb, j: (cb, 0, 0)),
                lam_spec, lam_spec, lam_spec]
    args = [z, bbd, cbd, lre, lim, ls]
    if not zero_init:
        in_specs += [st_spec, st_spec]
        args += list(h0)
    out_specs = [pl.BlockSpec((1, rb, 128), lambda g, d, cb, j: (d, rowblk(g, d, j), cb))]
    out_shape = [jax.ShapeDtypeStruct((N_DIR, m, BRANCH_W), F32)]
    if emit_state:
        out_specs += [st_spec, st_spec]
        out_shape += [jax.ShapeDtypeStruct((N_DIR, S5_CB, batch, S5_CBW), F32)] * 2
    vrow = lambda n: pltpu.VMEM((n, S5_CBW), F32)
    lanes = pltpu.VMEM((rb, 128), F32)
    return pl.pallas_call(
        functools.partial(_s5_kernel, rb=rb, seg=seg, chain=chain, zero_init=zero_init, emit_state=emit_state),
        grid=(ng, N_DIR, S5_CB, nt),
        in_specs=in_specs, out_specs=out_specs, out_shape=out_shape,
        scratch_shapes=[lanes, lanes, vrow(rb), vrow(rb), vrow(nsub), vrow(nsub), vrow(nsub), vrow(nsub),
                        vrow(1), vrow(1)],
        name="s5",
        compiler_params=_cparams(("arbitrary",) * 4),
    )(*args)


def _merge_kernel(x_ref, mod_ref, gates_ref, o_ref, u_ref, gz_ref, hm_ref, ys_ref, go_ref,
                  mlw_ref, s5d_ref, gluw_ref, glub_ref, gdw_ref, wbr_ref, wout_ref, out_ref):
    ya = _sigmoid(o_ref[...]) * _head_rms(hm_ref[0] + hm_ref[1], mlw_ref[...])
    y5 = _gelu_tanh(ys_ref[0] + ys_ref[1] + s5d_ref[...] * u_ref[...])
    yb = y5 * _sigmoid(_dot(y5, gluw_ref[...]) + glub_ref[...])
    yc = _head_rms(go_ref[0] + go_ref[1], gdw_ref[...]) * _silu(gz_ref[...])
    merged = None
    for n, y in enumerate((ya, yb, yc)):
        term = _sigmoid(gates_ref[:, n * D_MODEL:(n + 1) * D_MODEL]) * _dot(y, wbr_ref[n])
        merged = term if merged is None else merged + term
    out_ref[...] = x_ref[...] + mod_ref[0, 2:3, :] * _dot(merged, wout_ref[...])


def _merge(x, mod, z, hm, ys, go, p, seq):
    m = x.shape[0]
    tm = 256
    rows_per_mod = seq if mod.shape[0] > 1 else m
    row = lambda i: (i, 0)
    zspec = lambda cb: pl.BlockSpec((tm, BRANCH_W), lambda i: (i, cb))
    dspec = pl.BlockSpec((N_DIR, tm, BRANCH_W), lambda i: (0, i, 0))
    full = lambda a: pl.BlockSpec(a.shape, lambda i: (0,) * a.ndim)
    consts = [p['ml_norm_w'], p['s5_D'], p['s5_glu_w'], p['s5_glu_b'], p['gd_norm_w'], p['w_branch'], p['w_out']]
    return pl.pallas_call(
        _merge_kernel,
        grid=(m // tm,),
        in_specs=[pl.BlockSpec((tm, D_MODEL), row),
                  pl.BlockSpec((1, N_MOD, D_MODEL), lambda i: (i * tm // rows_per_mod, 0, 0)),
                  pl.BlockSpec((tm, 3 * D_MODEL), row),
                  zspec(CB_O), zspec(CB_SU), zspec(CB_GZ), dspec, dspec, dspec] + [full(a) for a in consts],
        out_specs=pl.BlockSpec((tm, D_MODEL), row),
        out_shape=jax.ShapeDtypeStruct((m, D_MODEL), F32),
        name="merge",
        compiler_params=_cparams(("arbitrary",)),
    )(x, mod, z, z, z, z, hm, ys, go, *consts)


def _ffn_kernel(x_ref, mod_ref, nw_ref, wg_ref, wu_ref, wd_ref, fw_ref, out_ref, h_scr, acc_scr, *, final):
    jf = pl.program_id(1)

    @pl.when(jf == 0)
    def _():
        x = x_ref[...]
        y = x * lax.rsqrt(jnp.mean(x * x, axis=-1, keepdims=True) + EPS)
        h_scr[...] = ((y * nw_ref[...]) * (1.0 + mod_ref[0, 4:5, :]) + mod_ref[0, 3:4, :]).astype(BF16)
        acc_scr[...] = jnp.zeros_like(acc_scr)

    hb = h_scr[...]
    a = _silu(jnp.dot(hb, wg_ref[...], preferred_element_type=F32))
    b = jnp.dot(hb, wu_ref[...], preferred_element_type=F32)
    acc_scr[...] += _dot(a * b, wd_ref[...])

    @pl.when(jf == pl.num_programs(1) - 1)
    def _():
        x = x_ref[...] + mod_ref[0, 5:6, :] * acc_scr[...]
        if final:
            x = x * lax.rsqrt(jnp.mean(x * x, axis=-1, keepdims=True) + EPS) * fw_ref[...]
        out_ref[...] = x


def _ffn(x, mod, p, seq, final_w, final):
    m = x.shape[0]
    tm, tf = 1024, D_FF // 2
    rows_per_mod = seq if mod.shape[0] > 1 else m
    row = lambda i, jf: (i, 0)
    vec = pl.BlockSpec((1, D_MODEL), lambda i, jf: (0, 0))
    return pl.pallas_call(
        functools.partial(_ffn_kernel, final=final),
        grid=(m // tm, D_FF // tf),
        in_specs=[pl.BlockSpec((tm, D_MODEL), row),
                  pl.BlockSpec((1, N_MOD, D_MODEL), lambda i, jf: (i * tm // rows_per_mod, 0, 0)),
                  vec,
                  pl.BlockSpec((D_MODEL, tf), lambda i, jf: (0, jf)),
                  pl.BlockSpec((D_MODEL, tf), lambda i, jf: (0, jf)),
                  pl.BlockSpec((tf, D_MODEL), lambda i, jf: (jf, 0)),
                  vec],
        out_specs=pl.BlockSpec((tm, D_MODEL), row),
        out_shape=jax.ShapeDtypeStruct((m, D_MODEL), F32),
        scratch_shapes=[pltpu.VMEM((tm, D_MODEL), BF16), pltpu.VMEM((tm, D_MODEL), F32)],
        name="ffn",
        compiler_params=_cparams(("arbitrary", "arbitrary")),
    )(x, mod, p['norm2_w'], p['w_gate'], p['w_up'], p['w_down'], final_w)


def _gate_lanes(parts):
    row = jnp.concatenate([a.astype(F32) for a in parts], axis=-1)
    return jnp.pad(row, ((0, 0), (0, GATE_W - row.shape[-1]))).reshape(N_DIR, 1, GATE_W)


def _block_diag(a):
    cb, g, r, c = a.shape
    eye = jnp.eye(g, dtype=a.dtype)
    return jnp.einsum('bgrc,gh->bgrhc', a, eye).reshape(cb, g * r, g * c)


def _layer_params(l, w):
    gpb = S5_GROUPS // S5_CB
    idx, acc = [], 0
    for size in IN_SIZES[:-1]:
        acc += size
        idx.append(acc)
    mq, mk, mv, mo, mi, mf, su, gqkv, gz, ga, gb, gates = jnp.split(w['w_in'][l], idx, axis=-1)
    w_big = jnp.concatenate([gates, mq, mk, mv, mo, su, gqkv, gz], axis=-1).astype(BF16)
    smalls = []
    for d in range(N_DIR):
        sl = slice(d * HEADS, (d + 1) * HEADS)
        blk = jnp.concatenate([mi[:, sl], mf[:, sl], ga[:, sl], gb[:, sl]], axis=-1)
        smalls.append(jnp.pad(blk, ((0, 0), (0, GATE_W - 4 * HEADS))))
    w_small = jnp.concatenate(smalls, axis=-1).astype(BF16)
    zeros = jnp.zeros((N_DIR, HEADS), F32)
    gate_bias = _gate_lanes([w['ml_i_bias'][l], w['ml_f_bias'][l], w['gd_dt_bias'][l], zeros])
    gate_alog = _gate_lanes([zeros, zeros, w['gd_A_log'][l], zeros])
    b_re = jnp.swapaxes(w['s5_B_re'][l], 1, 2).reshape(S5_CB, gpb, S5_GROUP, S5_STATE)
    b_im = jnp.swapaxes(w['s5_B_im'][l], 1, 2).reshape(S5_CB, gpb, S5_GROUP, S5_STATE)
    bbd = jnp.concatenate([_block_diag(b_re), _block_diag(b_im)], axis=-1).astype(BF16)
    c_re = jnp.swapaxes(w['s5_C_re'][l], 1, 2).reshape(S5_CB, gpb, S5_STATE, S5_GROUP)
    c_im = jnp.swapaxes(w['s5_C_im'][l], 1, 2).reshape(S5_CB, gpb, S5_STATE, S5_GROUP)
    cbd = jnp.concatenate([_block_diag(c_re), -_block_diag(c_im)], axis=1).astype(BF16)
    lam_shape = (N_DIR, S5_CB, 1, S5_CBW)
    ls = jnp.broadcast_to(w['s5_log_step'][l][:, :, None], (N_DIR, S5_GROUPS, S5_STATE))
    return dict(
        norm1_w=w['norm1_w'][l], w_big=w_big, w_small=w_small, gate_bias=gate_bias, gate_alog=gate_alog,
        ml_norm_w=w['ml_norm_w'][l].reshape(1, BRANCH_W),
        bbd=bbd, cbd=cbd,
        lam_re=w['s5_lam_re'][l].reshape(lam_shape), lam_im=w['s5_lam_im'][l].reshape(lam_shape),
        log_step=ls.reshape(lam_shape),
        s5_D=w['s5_D'][l].reshape(1, BRANCH_W), s5_glu_w=w['s5_glu_w'][l].astype(BF16),
        s5_glu_b=w['s5_glu_b'][l].reshape(1, BRANCH_W),
        gd_conv_w=w['gd_conv_w'][l],
        gd_norm_w=jnp.tile(w['gd_norm_w'][l], HEADS).reshape(1, BRANCH_W),
        w_branch=w['w_branch'][l].astype(BF16), w_out=w['w_out'][l].astype(BF16),
        norm2_w=w['norm2_w'][l].reshape(1, D_MODEL),
        w_gate=w['w_gate'][l].astype(BF16), w_up=w['w_up'][l].astype(BF16), w_down=w['w_down'][l].astype(BF16),
    )


def _grid_pos_embed(n_tok):
    grid_w = 64
    t = jnp.arange(n_tok)
    quarter = D_MODEL // 4
    omega = 1.0 / (10000.0 ** (jnp.arange(quarter, dtype=F32) / quarter))

    def enc(pos):
        ang = pos.astype(F32)[:, None] * omega[None, :]
        return jnp.concatenate([jnp.sin(ang), jnp.cos(ang)], axis=-1)

    return jnp.concatenate([enc(t // grid_w), enc(t % grid_w)], axis=-1)


def _trunk_layer(x, mod, p, batch, seq, tb, states, final_w, final, pe=None):
    res = _inproj(x, mod, p['norm1_w'], p['w_big'], p['w_small'], seq, pe)
    if pe is not None:
        z, zs, x = res
    else:
        z, zs = res
    if states is None:
        ml_st = s5_st = gd_st = None
    else:
        ml_st, s5_st, gd_st = states
    ml = _mlstm(z, zs, p['gate_bias'], batch, seq, tb, ml_st)
    s5 = _s5(z, p['bbd'], p['cbd'], p['lam_re'], p['lam_im'], p['log_step'], batch, seq, s5_st)
    qkv = _gdn_prep(z, p['gd_conv_w'], seq)
    gd = _gdn(qkv, zs, p['gate_bias'], p['gate_alog'], batch, seq, tb, gd_st)
    x = _merge(x, mod, z, ml[0], s5[0], gd[0], p, seq)
    x = _ffn(x, mod, p, seq, final_w, final)
    new_states = None
    if states is None:
        new_states = (ml[1][..., :DH], ml[1][..., DH], ml[2][:, :, :, 0, 0],
                      jnp.transpose(s5[1], (2, 0, 1, 3)).reshape(batch, N_DIR, S5_GROUPS, S5_STATE),
                      jnp.transpose(s5[2], (2, 0, 1, 3)).reshape(batch, N_DIR, S5_GROUPS, S5_STATE),
                      gd[1])
    return x, new_states


def kernel(x_prompt, x_sample, state_mlstm_C, state_mlstm_n, state_mlstm_m, state_s5_re, state_s5_im,
           state_gdn_S, c, c_ctx, ada_w, ada_b, norm1_w, w_in, ml_i_bias, ml_f_bias, ml_norm_w,
           s5_lam_re, s5_lam_im, s5_log_step, s5_B_re, s5_B_im, s5_C_re, s5_C_im, s5_D, s5_glu_w, s5_glu_b,
           gd_conv_w, gd_A_log, gd_dt_bias, gd_norm_w, w_branch, w_out, norm2_w, w_gate, w_up, w_down,
           final_norm_w):
    w = dict(norm1_w=norm1_w, w_in=w_in, ml_i_bias=ml_i_bias, ml_f_bias=ml_f_bias, ml_norm_w=ml_norm_w,
             s5_lam_re=s5_lam_re, s5_lam_im=s5_lam_im, s5_log_step=s5_log_step, s5_B_re=s5_B_re,
             s5_B_im=s5_B_im, s5_C_re=s5_C_re, s5_C_im=s5_C_im, s5_D=s5_D, s5_glu_w=s5_glu_w,
             s5_glu_b=s5_glu_b, gd_conv_w=gd_conv_w, gd_A_log=gd_A_log, gd_dt_bias=gd_dt_bias,
             gd_norm_w=gd_norm_w, w_branch=w_branch, w_out=w_out, norm2_w=norm2_w, w_gate=w_gate,
             w_up=w_up, w_down=w_down)
    bp, sp, _ = x_prompt.shape
    bs, ss, _ = x_sample.shape
    params = [_layer_params(l, w) for l in range(DEPTH)]
    final_w = final_norm_w.reshape(1, D_MODEL)

    cc = jnp.concatenate([c_ctx[None, :], c, jnp.zeros((8 - 1 - bs, D_MODEL), F32)], axis=0)
    mods = _modulation(cc, ada_w, ada_b).reshape(DEPTH, 8, N_MOD, D_MODEL)

    xp = x_prompt.reshape(bp * sp, D_MODEL)
    per_layer = []
    for l in range(DEPTH):
        xp, st = _trunk_layer(xp, mods[l, 0:1], params[l], bp, sp, sp, None, final_w, l == DEPTH - 1)
        per_layer.append(st)
    y_prompt = xp.reshape(bp, sp, D_MODEL)
    new_states = [jnp.stack([st[i] for st in per_layer], axis=1) for i in range(6)]

    xs = x_sample.reshape(bs * ss, D_MODEL)
    pe = _grid_pos_embed(ss)
    for l in range(DEPTH):
        n_cols = jnp.broadcast_to(state_mlstm_n[:, l][..., None], (bs, N_DIR, HEADS, DH, DH))
        ml_st = (jnp.concatenate([state_mlstm_C[:, l], n_cols], axis=-1),
                 jnp.broadcast_to(state_mlstm_m[:, l][:, :, :, None, None], (bs, N_DIR, HEADS, 1, DH)))
        s5_st = tuple(jnp.transpose(a[:, l].reshape(bs, N_DIR, S5_CB, 1, S5_CBW), (1, 2, 0, 3, 4))
                      for a in (state_s5_re, state_s5_im))
        xs, _ = _trunk_layer(xs, mods[l, 1:1 + bs], params[l], bs, ss, 512, (ml_st, s5_st, state_gdn_S[:, l]),
                             final_w, l == DEPTH - 1, pe if l == 0 else None)
    y_sample = xs.reshape(bs, ss, D_MODEL)
    return (y_prompt, y_sample, *new_states)
```

```python
import functools
import math

import jax
import jax.numpy as jnp
import numpy as np
from jax import lax
from jax.experimental import pallas as pl
from jax.experimental.pallas import tpu as pltpu

F32 = jnp.float32
BF16 = jnp.bfloat16

D_MODEL = 1024
DEPTH = 2
N_DIR = 2
N_MOD = 6
EPS = 1e-6
HEADS = 4
DH = 128
BRANCH_W = HEADS * DH
CHUNK = 64
S5_GROUPS = 32
S5_GROUP = 16
S5_STATE = 64
S5_CB = 4
S5_CBW = S5_GROUPS * S5_STATE // S5_CB
S5_SUBLANES = 8
CONV_K = 5
D_FF = -(-8 * D_MODEL // (3 * 256)) * 256
IN_SIZES = (512, 512, 512, 512, 8, 8, 512, 1536, 512, 8, 8, 3072)
GATE_W = 128

CB_GATES, CB_Q, CB_K, CB_V, CB_O, CB_SU, CB_GQKV, CB_GZ = 0, 6, 7, 8, 9, 10, 11, 14
Z_W = 15 * BRANCH_W

VMEM_LIMIT = 56 * 1024 * 1024


def _cparams(sem):
    return pltpu.CompilerParams(dimension_semantics=sem, vmem_limit_bytes=VMEM_LIMIT)


def _dot(a, b):
    return jnp.dot(a.astype(BF16), b.astype(BF16), preferred_element_type=F32)


def _dot_nt(a, b):
    return lax.dot_general(a.astype(BF16), b.astype(BF16), (((1,), (1,)), ((), ())),
                           preferred_element_type=F32)


def _dot_tn(a, b):
    return lax.dot_general(a.astype(BF16), b.astype(BF16), (((0,), (0,)), ((), ())),
                           preferred_element_type=F32)


def _split3(x):
    hi = x.astype(BF16)
    r1 = x - hi.astype(F32)
    mid = r1.astype(BF16)
    lo = (r1 - mid.astype(F32)).astype(BF16)
    return hi, mid, lo


def _sel_dot(sel, x, nt=False):
    dims = (((1,), (1,)), ((), ())) if nt else (((1,), (0,)), ((), ()))
    hi, mid, lo = _split3(x)
    f = lambda p: lax.dot_general(sel, p, dims, preferred_element_type=F32)
    return (f(hi) + f(mid)) + f(lo)


def _dot_hp(a, b):
    ah = a.astype(BF16)
    al = (a - ah.astype(F32)).astype(BF16)
    bh = b.astype(BF16)
    bl = (b - bh.astype(F32)).astype(BF16)
    f = lambda p, q: jnp.dot(p, q, preferred_element_type=F32)
    return f(ah, bh) + (f(ah, bl) + f(al, bh))


def _sigmoid(x):
    return 1.0 / (1.0 + jnp.exp(-x))


def _silu(x):
    return x * _sigmoid(x)


def _softplus(x):
    return jnp.maximum(x, 0.0) + jnp.log(1.0 + jnp.exp(-jnp.abs(x)))


def _gelu_tanh(x):
    c = math.sqrt(2.0 / math.pi)
    return 0.5 * x * (1.0 + jnp.tanh(c * (x + 0.044715 * (x * x * x))))


def _head_rms(x, w_row):
    outs = []
    for h in range(HEADS):
        xh = x[:, h * DH:(h + 1) * DH]
        outs.append(xh * lax.rsqrt(jnp.mean(xh * xh, axis=-1, keepdims=True) + EPS))
    return jnp.concatenate(outs, axis=-1) * w_row


def _dir_chunk(c, d, n):
    return c + d * (n - 1 - 2 * c)


def _mod_kernel(c_ref, w_ref, b_ref, o_ref):
    o_ref[0] = _dot(_silu(c_ref[...]), w_ref[0]) + b_ref[0]


def _modulation(cc, ada_w, ada_b):
    tn = 1536
    nmod = N_MOD * D_MODEL
    return pl.pallas_call(
        _mod_kernel,
        grid=(DEPTH, nmod // tn),
        in_specs=[pl.BlockSpec((8, D_MODEL), lambda l, j: (0, 0)),
                  pl.BlockSpec((1, D_MODEL, tn), lambda l, j: (l, 0, j)),
                  pl.BlockSpec((1, 1, tn), lambda l, j: (l, 0, j))],
        out_specs=pl.BlockSpec((1, 8, tn), lambda l, j: (l, 0, j)),
        out_shape=jax.ShapeDtypeStruct((DEPTH, 8, nmod), F32),
        name="adaln_mod",
        compiler_params=_cparams(("arbitrary", "arbitrary")),
    )(cc, ada_w, ada_b.reshape(DEPTH, 1, nmod))


def _inproj_kernel(*refs, has_pe):
    if has_pe:
        x_ref, pe_ref, mod_ref, nw_ref, w_ref, ws_ref, z_ref, zs_ref, xs_ref, hn_scr = refs
    else:
        x_ref, mod_ref, nw_ref, w_ref, ws_ref, z_ref, zs_ref, hn_scr = refs

    @pl.when(pl.program_id(1) == 0)
    def _():
        x = x_ref[...]
        if has_pe:
            x = x + pe_ref[...]
            xs_ref[...] = x
        y = x * lax.rsqrt(jnp.mean(x * x, axis=-1, keepdims=True) + EPS)
        h = (y * nw_ref[...]) * (1.0 + mod_ref[0, 1:2, :]) + mod_ref[0, 0:1, :]
        hb = h.astype(BF16)
        hn_scr[...] = hb
        zs_ref[...] = jnp.dot(hb, ws_ref[...], preferred_element_type=F32)

    z_ref[...] = jnp.dot(hn_scr[...], w_ref[...], preferred_element_type=F32)


def _inproj(x, mod, norm_w, w_big, w_small, seq, pe=None):
    m = x.shape[0]
    tm, tn = 1024, 1536
    rows_per_mod = seq if mod.shape[0] > 1 else m
    has_pe = pe is not None
    row = lambda i, j: (i, 0)
    in_specs = [pl.BlockSpec((tm, D_MODEL), row)]
    args = [x]
    if has_pe:
        in_specs.append(pl.BlockSpec((tm, D_MODEL), lambda i, j: (i % (seq // tm), 0)))
        args.append(pe)
    in_specs += [pl.BlockSpec((1, N_MOD, D_MODEL), lambda i, j: (i * tm // rows_per_mod, 0, 0)),
                 pl.BlockSpec((1, D_MODEL), lambda i, j: (0, 0)),
                 pl.BlockSpec((D_MODEL, tn), lambda i, j: (0, j)),
                 pl.BlockSpec((D_MODEL, N_DIR * GATE_W), lambda i, j: (0, 0))]
    args += [mod, norm_w.reshape(1, D_MODEL), w_big, w_small]
    out_specs = [pl.BlockSpec((tm, tn), lambda i, j: (i, j)),
                 pl.BlockSpec((tm, N_DIR * GATE_W), row)]
    out_shape = [jax.ShapeDtypeStruct((m, Z_W), F32),
                 jax.ShapeDtypeStruct((m, N_DIR * GATE_W), F32)]
    if has_pe:
        out_specs.append(pl.BlockSpec((tm, D_MODEL), row))
        out_shape.append(jax.ShapeDtypeStruct((m, D_MODEL), F32))
    return pl.pallas_call(
        functools.partial(_inproj_kernel, has_pe=has_pe),
        grid=(m // tm, Z_W // tn),
        in_specs=in_specs, out_specs=out_specs, out_shape=out_shape,
        scratch_shapes=[pltpu.VMEM((tm, D_MODEL), BF16)],
        name="inproj",
        compiler_params=_cparams(("arbitrary", "arbitrary")),
    )(*args)


def _mixer_rowblk(nt):
    return lambda b, d, j: b * nt + _dir_chunk(j, d, nt)


def _gate_selector():
    r = lax.broadcasted_iota(jnp.int32, (16, GATE_W), 0)
    c = lax.broadcasted_iota(jnp.int32, (16, GATE_W), 1)
    return (r == c).astype(BF16)


def _block_gate_sums(vals, other, n_other, cls_scr, cum_scr):
    tri = (cls_scr[...] > 0).astype(BF16)
    for g in range(vals.shape[0] // HC):
        grp = slice(g * HC, (g + 1) * HC)
        cum_scr[grp, :] = _sel_dot(tri, vals[grp, :])
    lane = lax.broadcasted_iota(jnp.int32, vals.shape, 1)
    mixed = jnp.where(lane < n_other, other, cum_scr[...])
    return _sel_dot(_gate_selector(), mixed, nt=True), mixed


def _lane_dense(x, lanes):
    r = lax.broadcasted_iota(jnp.int32, (GATE_W, len(lanes) * DH), 0)
    c = lax.broadcasted_iota(jnp.int32, (GATE_W, len(lanes) * DH), 1)
    pick = functools.reduce(jnp.logical_or, [jnp.logical_and(c // DH == i, r == ln) for i, ln in enumerate(lanes)])
    e = pick.astype(BF16)
    hi, mid, lo = _split3(x)
    f = lambda p: jnp.dot(p, e, preferred_element_type=F32)
    return (f(hi) + f(mid)) + f(lo)


def _mlstm_kernel(*refs, nch, zero_init, emit_state):
    q_ref, k_ref, v_ref, g_ref, bias_ref = refs[:5]
    pos = 5
    if not zero_init:
        c0_ref, m0_ref = refs[pos:pos + 2]
        pos += 2
    h_ref = refs[pos]
    pos += 1
    if emit_state:
        co_ref, mo_ref = refs[pos:pos + 2]
        pos += 2
    c_scr, m_scr, cls_all, cum_scr, row_scr, dense_scr, av_scr, ml_scr = refs[pos:pos + 8]

    d = pl.program_id(1)
    j = pl.program_id(2)

    @pl.when(j == 0)
    def _():
        if zero_init:
            c_scr[...] = jnp.zeros_like(c_scr)
            m_scr[...] = jnp.zeros_like(m_scr)
        else:
            c_scr[...] = c0_ref[0, 0]
            m_scr[...] = m0_ref[0, 0]

    _fill_pair_classes(cls_all)
    cls_scr = cls_all.at[d]
    rowid = lax.broadcasted_iota(jnp.int32, (CHUNK, DH), 0)
    last = (CHUNK - 1) * (1 - d)
    bias = bias_ref[0]
    scale = DH ** -0.5
    hrows = lambda h: slice(h * CHUNK, (h + 1) * CHUNK)
    per_head = lambda vals: jnp.concatenate([jnp.broadcast_to(a, (CHUNK, a.shape[1])) for a in vals], axis=0)
    twice = lambda a: jnp.concatenate([a, a], axis=1)
    ones = jnp.ones((HC, DH), F32)

    pre_all = g_ref[...] + bias
    xt, mixed = _block_gate_sums(-_softplus(-pre_all), pre_all, HEADS, cls_scr, cum_scr)
    for c in range(nch):
        cs = slice(c * CHUNK, (c + 1) * CHUNK)
        row_scr[c] = jnp.concatenate([xt[h:h + 1, cs] - xt[HEADS + h:HEADS + h + 1, cs] for h in range(HEADS)],
                                     axis=1)
    for h in range(HEADS):
        dense_scr[h] = _lane_dense(mixed, (HEADS + h, h))

    stacked = lambda rows, lo: jnp.concatenate([dense_scr[h, rows, lo:lo + DH] for h in range(HEADS)], axis=0)
    npar = MLSTM_LOCKSTEP

    def local(cp, carry):
        cs = [cp * npar + i for i in range(npar)]
        rows = [pl.ds(pl.multiple_of(c * CHUNK, CHUNK), CHUNK) for c in cs]
        cls = cls_scr[...]
        log_d = [jnp.where(cls > 0, twice(stacked(r, 0)) + row_scr[c], -jnp.inf) for r, c in zip(rows, cs)]
        ml = [jnp.max(a, axis=-1, keepdims=True) for a in log_d]
        q = [_stack_heads(q_ref[r, :]) for r in rows]
        k = [_stack_heads(k_ref[r, :]) * scale for r in rows]
        v = [jnp.concatenate([_stack_heads(v_ref[r, :]), ones], axis=1) for r in rows]
        s = [_dot_nt(a, b) * jnp.exp(ld - m) for a, b, ld, m in zip(q, k, log_d, ml)]
        av = [_dot(a, b) for a, b in zip(s, v)]
        for i, c in enumerate(cs):
            av_scr[c] = av[i]
            ml_scr[c] = jnp.broadcast_to(ml[i], (HC, DH))
        return carry

    lax.fori_loop(0, nch // npar, local, 0)

    def advance(c, carry):
        ci = _dir_chunk(c, d, nch)
        r0 = pl.multiple_of(ci * CHUNK, CHUNK)
        rows = pl.ds(r0, CHUNK)
        b_tok = stacked(rows, 0)
        i_tok = stacked(rows, DH)
        at_last = lambda a: [jnp.sum(jnp.where(rowid == last, a[hrows(h), :], 0.0), axis=0, keepdims=True)
                             for h in range(HEADS)]
        b_last = at_last(b_tok)
        m_prev = [m_scr[h] for h in range(HEADS)]
        ml = ml_scr[ci]
        log_0 = b_tok + per_head(m_prev)
        m_t = jnp.maximum(log_0, ml)
        w_0 = jnp.exp(log_0 - m_t)
        f = jnp.exp(ml - m_t)
        q = _stack_heads(q_ref[rows, :])
        k = _stack_heads(k_ref[rows, :]) * scale
        v = jnp.concatenate([_stack_heads(v_ref[rows, :]), ones], axis=1)
        cst = [c_scr[h] for h in range(HEADS)]
        qc = jnp.concatenate([_dot(q[hrows(h), :], cst[h]) for h in range(HEADS)], axis=0)
        num = f * av_scr[ci, :, 0:DH] + w_0 * qc[:, 0:DH]
        den = f * av_scr[ci, :, DH:2 * DH] + w_0 * qc[:, DH:2 * DH]
        hv = num / jnp.maximum(jnp.abs(den), jnp.exp(-m_t))
        m_new = at_last(m_t)
        kw = k * jnp.exp(per_head(b_last) - b_tok + i_tok - per_head(m_new))
        for h in range(HEADS):
            h_ref[0, rows, h * DH:(h + 1) * DH] = hv[hrows(h), :]
            c_0 = jnp.exp(b_last[h] + m_prev[h] - m_new[h])
            c_scr[h] = twice(c_0) * cst[h] + _dot_tn(kw[hrows(h), :], v[hrows(h), :])
            m_scr[h] = m_new[h]
        return carry

    lax.fori_loop(0, nch, advance, 0)

    if emit_state:
        @pl.when(j == pl.num_programs(2) - 1)
        def _():
            co_ref[0, 0] = c_scr[...]
            mo_ref[0, 0] = m_scr[...]


def _mlstm(z, zs, bias, batch, seq, tb, states):
    m = batch * seq
    nt = seq // tb
    nch = tb // CHUNK
    zero_init = states is None
    emit_state = states is None
    rb = _mixer_rowblk(nt)
    zspec = lambda cb: pl.BlockSpec((tb, BRANCH_W), lambda b, d, j: (rb(b, d, j), cb))
    st5 = lambda shape: pl.BlockSpec((1, 1) + shape, lambda b, d, j: (b, d) + (0,) * len(shape))
    in_specs = [zspec(CB_Q), zspec(CB_K), zspec(CB_V),
                pl.BlockSpec((tb, GATE_W), lambda b, d, j: (rb(b, d, j), d)),
                pl.BlockSpec((1, 1, GATE_W), lambda b, d, j: (d, 0, 0))]
    args = [z, z, z, zs, bias]
    if not zero_init:
        in_specs += [st5((HEADS, DH, 2 * DH)), st5((HEADS, 1, DH))]
        args += list(states)
    out_specs = [pl.BlockSpec((1, tb, BRANCH_W), lambda b, d, j: (d, rb(b, d, j), 0))]
    out_shape = [jax.ShapeDtypeStruct((N_DIR, m, BRANCH_W), F32)]
    if emit_state:
        out_specs += [st5((HEADS, DH, 2 * DH)), st5((HEADS, 1, DH))]
        out_shape += [jax.ShapeDtypeStruct((batch, N_DIR, HEADS, DH, 2 * DH), F32),
                      jax.ShapeDtypeStruct((batch, N_DIR, HEADS, 1, DH), F32)]
    return pl.pallas_call(
        functools.partial(_mlstm_kernel, nch=nch, zero_init=zero_init, emit_state=emit_state),
        grid=(batch, N_DIR, nt),
        in_specs=in_specs, out_specs=out_specs, out_shape=out_shape,
        scratch_shapes=[pltpu.VMEM((HEADS, DH, 2 * DH), F32), pltpu.VMEM((HEADS, 1, DH), F32),
                        pltpu.VMEM((N_DIR, HC, HC), jnp.int32),
                        pltpu.VMEM((tb, GATE_W), F32), pltpu.VMEM((nch, 1, HC), F32),
                        pltpu.VMEM((HEADS, tb, 2 * DH), F32),
                        pltpu.VMEM((nch, HC, 2 * DH), F32), pltpu.VMEM((nch, HC, DH), F32)],
        name="mlstm",
        compiler_params=_cparams(("arbitrary", "arbitrary", "arbitrary")),
    )(*args)


def _gdn_prep_kernel(x_ref, prev_ref, next_ref, w_ref, o_ref, xe_scr, *, tb, seq):
    i = pl.program_id(0)
    p = pl.program_id(1)
    pad = 8
    at_start = (i * tb) % seq == 0
    at_end = ((i + 1) * tb) % seq == 0
    xe_scr[0:pad, :] = jnp.where(at_start, 0.0, prev_ref[...])
    xe_scr[pad:pad + tb, :] = x_ref[...]
    xe_scr[pad + tb:pad + tb + pad, :] = jnp.where(at_end, 0.0, next_ref[...])
    acc = None
    for t in range(CONV_K):
        term = xe_scr[pl.ds(pad + t - CONV_K // 2, tb), :] * w_ref[t:t + 1, :]
        acc = term if acc is None else acc + term
    y = _silu(acc)
    outs = []
    for h in range(HEADS):
        yh = y[:, h * DH:(h + 1) * DH]
        outs.append(yh * lax.rsqrt(jnp.sum(yh * yh, axis=-1, keepdims=True) + EPS))
    yn = jnp.concatenate(outs, axis=-1)
    qscale = jnp.where(p == 0, DH ** -0.5, 1.0)
    o_ref[...] = jnp.where(p == 2, y, yn * qscale)


def _gdn_prep(z, conv_w, seq):
    m = z.shape[0]
    tb = 256
    nb8 = m // 8
    return pl.pallas_call(
        functools.partial(_gdn_prep_kernel, tb=tb, seq=seq),
        grid=(m // tb, 3),
        in_specs=[pl.BlockSpec((tb, BRANCH_W), lambda i, p: (i, CB_GQKV + p)),
                  pl.BlockSpec((8, BRANCH_W), lambda i, p: (jnp.maximum(i * (tb // 8) - 1, 0), CB_GQKV + p)),
                  pl.BlockSpec((8, BRANCH_W), lambda i, p: (jnp.minimum((i + 1) * (tb // 8), nb8 - 1), CB_GQKV + p)),
                  pl.BlockSpec((CONV_K, BRANCH_W), lambda i, p: (0, p))],
        out_specs=pl.BlockSpec((tb, BRANCH_W), lambda i, p: (i, p)),
        out_shape=jax.ShapeDtypeStruct((m, 3 * BRANCH_W), F32),
        scratch_shapes=[pltpu.VMEM((tb + 16, BRANCH_W), F32)],
        name="gdn_prep",
        compiler_params=_cparams(("arbitrary", "arbitrary")),
    )(z, z, z, conv_w)


HC = HEADS * CHUNK


def _stack_heads(x):
    return jnp.concatenate([x[:, h * DH:(h + 1) * DH] for h in range(HEADS)], axis=0)


def _stack_gate(x, lane0):
    return jnp.concatenate([x[:, lane0 + h:lane0 + h + 1] for h in range(HEADS)], axis=0)


GDN_LOCKSTEP = 4
MLSTM_LOCKSTEP = 4
BASE_LG = 3
CHUNK_LG = int(math.log2(CHUNK))


def _pair_classes(d):
    r = lax.broadcasted_iota(jnp.int32, (HC, HC), 0)
    c = lax.broadcasted_iota(jnp.int32, (HC, HC), 1)
    cls = jnp.full((HC, HC), CHUNK_LG, jnp.int32)
    for s in range(CHUNK_LG - 1, BASE_LG - 1, -1):
        cls = jnp.where((r >> s) == (c >> s), s, cls)
    diff = (r - c) * (1 - 2 * d)
    cls = jnp.where(diff == 0, 1, cls)
    return jnp.where(jnp.logical_and((r >> CHUNK_LG) == (c >> CHUNK_LG), diff >= 0), cls, 0)


def _fill_pair_classes(cls_all):
    first = functools.reduce(jnp.logical_and, [pl.program_id(i) == 0 for i in range(3)])

    @pl.when(first)
    def _():
        for d in range(N_DIR):
            cls_all[d] = _pair_classes(d)


def _unit_tri_solve(xs, cls, rhss):
    n = xs[0].shape[0]
    nr = rhss[0].shape[1]
    x0 = [jnp.where(cls == BASE_LG, x, 0.0) for x in xs]
    x2 = [_dot(a, a) for a in x0]
    y = [_dot(b, jnp.concatenate([b, a], axis=1)) for a, b in zip(x0, x2)]
    x4 = [t[:, :n] for t in y]
    q2 = [a + b + t[:, n:] for a, b, t in zip(x0, x2, y)]
    t8m = [a + b + _dot(b, a) for a, b in zip(q2, x4)]
    levels = range(BASE_LG + 1, CHUNK_LG + 1)
    cur = [jnp.concatenate([r] + [jnp.where(cls == s, x, 0.0) for s in levels], axis=1) for x, r in zip(xs, rhss)]
    cur = [a + _dot(t, a) for a, t in zip(cur, t8m)]
    for _ in levels:
        last = cur[0].shape[1] == nr + n
        rest = [a[:, :nr] if last else jnp.concatenate([a[:, :nr], a[:, nr + n:]], axis=1) for a in cur]
        cur = [r + _dot(a[:, nr:nr + n], r) for a, r in zip(cur, rest)]
    return cur


def _gdn_kernel(*refs, nch, zero_init, emit_state):
    q_ref, k_ref, v_ref, g_ref, bias_ref, alog_ref = refs[:6]
    pos = 6
    if not zero_init:
        s0_ref = refs[pos]
        pos += 1
    o_ref = refs[pos]
    pos += 1
    if emit_state:
        so_ref = refs[pos]
        pos += 1
    s_scr, cls_all, cum_scr, row_scr, uw_scr, at_scr, qg_scr, kd_scr, gt_scr = refs[pos:pos + 9]

    d = pl.program_id(1)
    j = pl.program_id(2)

    @pl.when(j == 0)
    def _():
        if zero_init:
            s_scr[...] = jnp.zeros_like(s_scr)
        else:
            s_scr[...] = s0_ref[0, 0]

    _fill_pair_classes(cls_all)
    cls_scr = cls_all.at[d]
    last = (CHUNK - 1) * (1 - d)
    bias = bias_ref[0]
    neg_a = -jnp.exp(alog_ref[0])
    g0 = 2 * HEADS

    g_all = neg_a * _softplus(g_ref[...] + bias)
    xt, _ = _block_gate_sums(g_all, g_all, 0, cls_scr, cum_scr)
    for c in range(nch):
        cs = slice(c * CHUNK, (c + 1) * CHUNK)
        row_scr[c] = jnp.concatenate([xt[g0 + h:g0 + h + 1, cs] for h in range(HEADS)], axis=1)

    npar = GDN_LOCKSTEP

    def prepare(cp, carry):
        cs = [cp * npar + i for i in range(npar)]
        r0 = [pl.multiple_of(c * CHUNK, CHUNK) for c in cs]
        rows = [pl.ds(r, CHUNK) for r in r0]
        gtot = [cum_scr[pl.ds(r + last, 1), :] for r in r0]
        g_col = [_stack_gate(cum_scr[r, :], g0) for r in rows]
        beta = [_stack_gate(_sigmoid(g_ref[r, :] + bias), g0 + HEADS) for r in rows]
        g_last = [jnp.concatenate([jnp.broadcast_to(t[:, g0 + h:g0 + h + 1], (CHUNK, 1)) for h in range(HEADS)],
                                  axis=0) for t in gtot]
        cls = cls_scr[...]
        decay = [jnp.exp(jnp.where(cls > 0, gc - row_scr[c], -jnp.inf)) for gc, c in zip(g_col, cs)]
        eg = [jnp.exp(gc) for gc in g_col]
        q = [_stack_heads(q_ref[r, :]) for r in rows]
        k = [_stack_heads(k_ref[r, :]) for r in rows]
        v = [_stack_heads(v_ref[r, :]) for r in rows]
        kb = [a * b for a, b in zip(k, beta)]
        x = [-(_dot_nt(a, b) * dc) for a, b, dc in zip(kb, k, decay)]
        rhs = [jnp.concatenate([a * b, kbi * e], axis=-1) for a, b, kbi, e in zip(v, beta, kb, eg)]
        uw = _unit_tri_solve(x, cls, rhs)
        attn = [_dot_nt(a, b) * dc for a, b, dc in zip(q, k, decay)]
        for i, c in enumerate(cs):
            uw_scr[c] = uw[i]
            at_scr[c] = attn[i]
            qg_scr[c] = q[i] * eg[i]
            kd_scr[c] = k[i] * jnp.exp(g_last[i] - g_col[i])
            gt_scr[c] = gtot[i]
        return carry

    lax.fori_loop(0, nch // npar, prepare, 0)

    def advance(c, carry):
        ci = _dir_chunk(c, d, nch)
        rows = pl.ds(pl.multiple_of(ci * CHUNK, CHUNK), CHUNK)
        hrows = lambda h: slice(h * CHUNK, (h + 1) * CHUNK)
        st = [s_scr[h] for h in range(HEADS)]
        v_new = [uw_scr[ci, hrows(h), 0:DH] - _dot(uw_scr[ci, hrows(h), DH:2 * DH], st[h]) for h in range(HEADS)]
        qs = jnp.concatenate([_dot(qg_scr[ci, hrows(h), :], st[h]) for h in range(HEADS)], axis=0)
        o = qs + _dot(at_scr[ci], jnp.concatenate(v_new, axis=0))
        gtot = gt_scr[ci]
        for h in range(HEADS):
            o_ref[0, rows, h * DH:(h + 1) * DH] = o[hrows(h), :]
            s_scr[h] = st[h] * jnp.exp(gtot[:, g0 + h:g0 + h + 1]) + _dot_tn(kd_scr[ci, hrows(h), :], v_new[h])
        return carry

    lax.fori_loop(0, nch, advance, 0)

    if emit_state:
        @pl.when(j == pl.num_programs(2) - 1)
        def _():
            so_ref[0, 0] = s_scr[...]


def _gdn(qkv, zs, bias, alog, batch, seq, tb, s0):
    m = batch * seq
    nt = seq // tb
    nch = tb // CHUNK
    zero_init = s0 is None
    emit_state = s0 is None
    rb = _mixer_rowblk(nt)
    spec = lambda cb: pl.BlockSpec((tb, BRANCH_W), lambda b, d, j: (rb(b, d, j), cb))
    st = pl.BlockSpec((1, 1, HEADS, DH, DH), lambda b, d, j: (b, d, 0, 0, 0))
    dirrow = pl.BlockSpec((1, 1, GATE_W), lambda b, d, j: (d, 0, 0))
    in_specs = [spec(0), spec(1), spec(2),
                pl.BlockSpec((tb, GATE_W), lambda b, d, j: (rb(b, d, j), d)), dirrow, dirrow]
    args = [qkv, qkv, qkv, zs, bias, alog]
    if not zero_init:
        in_specs.append(st)
        args.append(s0)
    out_specs = [pl.BlockSpec((1, tb, BRANCH_W), lambda b, d, j: (d, rb(b, d, j), 0))]
    out_shape = [jax.ShapeDtypeStruct((N_DIR, m, BRANCH_W), F32)]
    if emit_state:
        out_specs.append(st)
        out_shape.append(jax.ShapeDtypeStruct((batch, N_DIR, HEADS, DH, DH), F32))
    return pl.pallas_call(
        functools.partial(_gdn_kernel, nch=nch, zero_init=zero_init, emit_state=emit_state),
        grid=(batch, N_DIR, nt),
        in_specs=in_specs, out_specs=out_specs, out_shape=out_shape,
        scratch_shapes=[pltpu.VMEM((HEADS, DH, DH), F32), pltpu.VMEM((N_DIR, HC, HC), jnp.int32),
                        pltpu.VMEM((tb, GATE_W), F32), pltpu.VMEM((nch, 1, HC), F32),
                        pltpu.VMEM((nch, HC, 2 * DH), F32), pltpu.VMEM((nch, HC, HC), F32),
                        pltpu.VMEM((nch, HC, DH), F32), pltpu.VMEM((nch, HC, DH), F32),
                        pltpu.VMEM((nch, 1, GATE_W), F32)],
        name="gdn",
        compiler_params=_cparams(("arbitrary", "arbitrary", "arbitrary")),
    )(*args)


def _s5_discretise(lre, lim, ls):
    dt = jnp.exp(ls)
    mag = jnp.exp(lre * dt)
    ar = mag * jnp.cos(lim * dt)
    ai = mag * jnp.sin(lim * dt)
    den = lre * lre + lim * lim
    nr = ar - 1.0
    return ar, ai, (nr * lre + ai * lim) / den, (ai * lre - nr * lim) / den


def _s5_kernel(*refs, rb, seg, chain):
    u_ref, b_ref, c_ref, lre_ref, lim_ref, ls_ref = refs[:6]
    pos = 6
    if chain:
        h0r_ref, h0i_ref = refs[pos:pos + 2]
        pos += 2
    y_ref = refs[pos]
    pos += 1
    if not chain:
        sr_ref, si_ref = refs[pos:pos + 2]
        pos += 2
    up, yp, xr, xi, er, ei, pr, pi_ = refs[pos:pos + 8]

    w = S5_CBW
    nsub = S5_SUBLANES
    rc = 256
    ngrp = seg // nsub
    par = [_s5_discretise(lre_ref[d, 0], lim_ref[d, 0], ls_ref[d, 0]) for d in range(N_DIR)]

    def interleave(g, carry):
        for s in range(nsub):
            src = pl.ds(pl.multiple_of(s * seg + g * nsub, nsub), nsub)
            up[pl.ds(g * nsub * nsub + s, nsub, stride=nsub), :] = u_ref[src, :]
        return carry

    lax.fori_loop(0, ngrp, interleave, 0)

    def fill(c, carry):
        rows = pl.ds(pl.multiple_of(c * rc, rc), rc)
        bu = _dot(up[rows, :], b_ref[0])
        bre = bu[:, :w]
        bim = bu[:, w:]
        for d in range(N_DIR):
            _, _, zr, zi = par[d]
            xr[d, rows, :] = zr * bre - zi * bim
            xi[d, rows, :] = zr * bim + zi * bre
        return carry

    lax.fori_loop(0, rb // rc, fill, 0, unroll=2)

    ab = [(jnp.broadcast_to(p[0], (nsub, w)), jnp.broadcast_to(p[1], (nsub, w))) for p in par]

    def step_rows(tt, d):
        t = tt if d == 0 else seg - 1 - tt
        return pl.ds(pl.multiple_of(t * nsub, nsub), nsub)

    def scan_step(tt, carry):
        out = []
        for d in range(N_DIR):
            hr, hi = carry[d]
            arb, aib = ab[d]
            idx = step_rows(tt, d)
            nhr = arb * hr - aib * hi + xr[d, idx, :]
            nhi = arb * hi + aib * hr + xi[d, idx, :]
            xr[d, idx, :] = nhr
            xi[d, idx, :] = nhi
            out.append((nhr, nhi))
        return tuple(out)

    zero = jnp.zeros((nsub, w), F32)
    ends = lax.fori_loop(0, seg, scan_step, ((zero, zero), (zero, zero)), unroll=2)

    if chain:
        for d in range(N_DIR):
            ar, ai = par[d][0], par[d][1]
            er[d] = ends[d][0]
            ei[d] = ends[d][1]
            sr, si = ar, ai
            for _ in range(int(math.log2(seg))):
                sr, si = sr * sr - si * si, 2.0 * sr * si
            fr = h0r_ref[d, 0, 0]
            fi = h0i_ref[d, 0, 0]
            for k in range(nsub):
                row = k if d == 0 else nsub - 1 - k
                pr[d, row:row + 1, :] = fr
                pi_[d, row:row + 1, :] = fi
                fr, fi = (er[d, row:row + 1, :] + (sr * fr - si * fi),
                          ei[d, row:row + 1, :] + (sr * fi + si * fr))

        def fix_step(tt, carry):
            out = []
            for d in range(N_DIR):
                cr, ci = carry[d]
                arb, aib = ab[d]
                idx = step_rows(tt, d)
                ncr = arb * cr - aib * ci
                nci = arb * ci + aib * cr
                xr[d, idx, :] = xr[d, idx, :] + ncr
                xi[d, idx, :] = xi[d, idx, :] + nci
                out.append((ncr, nci))
            return tuple(out)

        lax.fori_loop(0, seg, fix_step, tuple((pr[d], pi_[d]) for d in range(N_DIR)), unroll=2)
    else:
        for d in range(N_DIR):
            sr_ref[d, 0] = ends[d][0]
            si_ref[d, 0] = ends[d][1]

    def proj(c, carry):
        rows = pl.ds(pl.multiple_of(c * rc, rc), rc)
        yp[rows, :] = (_dot(xr[0, rows, :] + xr[1, rows, :], c_ref[0, :w, :])
                       + _dot(xi[0, rows, :] + xi[1, rows, :], c_ref[0, w:, :]))
        return carry

    lax.fori_loop(0, rb // rc, proj, 0, unroll=4)

    def deinterleave(g, carry):
        for s in range(nsub):
            dst = pl.ds(pl.multiple_of(s * seg + g * nsub, nsub), nsub)
            y_ref[dst, :] = yp[pl.ds(g * nsub * nsub + s, nsub, stride=nsub), :]
        return carry

    lax.fori_loop(0, ngrp, deinterleave, 0)


def _s5(z, bbd, cbd, lre, lim, ls, batch, seq, h0):
    m = batch * seq
    nsub = S5_SUBLANES
    chain = h0 is not None
    if chain:
        seg = seq // nsub
        ng = batch
        st_spec = pl.BlockSpec((N_DIR, 1, 1, 1, S5_CBW), lambda g, cb: (0, cb, g, 0, 0))
    else:
        seg = seq
        ng = batch // nsub
        st_spec = pl.BlockSpec((N_DIR, 1, nsub, S5_CBW), lambda g, cb: (0, cb, g, 0))
    rb = nsub * seg
    lam_spec = pl.BlockSpec((N_DIR, 1, 1, S5_CBW), lambda g, cb: (0, cb, 0, 0))
    in_specs = [pl.BlockSpec((rb, 128), lambda g, cb: (g, CB_SU * 4 + cb)),
                pl.BlockSpec((1, 128, 2 * S5_CBW), lambda g, cb: (cb, 0, 0)),
                pl.BlockSpec((1, 2 * S5_CBW, 128), lambda g, cb: (cb, 0, 0)),
                lam_spec, lam_spec, lam_spec]
    args = [z, bbd, cbd, lre, lim, ls]
    if chain:
        in_specs += [st_spec, st_spec]
        args += list(h0)
    out_specs = [pl.BlockSpec((rb, 128), lambda g, cb: (g, cb))]
    out_shape = [jax.ShapeDtypeStruct((m, BRANCH_W), F32)]
    if not chain:
        out_specs += [st_spec, st_spec]
        out_shape += [jax.ShapeDtypeStruct((N_DIR, S5_CB, batch, S5_CBW), F32)] * 2
    vdir = lambda n: pltpu.VMEM((N_DIR, n, S5_CBW), F32)
    lanes = pltpu.VMEM((rb, 128), F32)
    return pl.pallas_call(
        functools.partial(_s5_kernel, rb=rb, seg=seg, chain=chain),
        grid=(ng, S5_CB),
        in_specs=in_specs, out_specs=out_specs, out_shape=out_shape,
        scratch_shapes=[lanes, lanes, vdir(rb), vdir(rb), vdir(nsub), vdir(nsub), vdir(nsub), vdir(nsub)],
        name="s5",
        compiler_params=_cparams(("arbitrary", "arbitrary")),
    )(*args)


def _merge_kernel(x_ref, mod_ref, gates_ref, o_ref, u_ref, gz_ref, hm_ref, ys_ref, go_ref,
                  mlw_ref, s5d_ref, gluw_ref, glub_ref, gdw_ref, wbr_ref, wout_ref, out_ref):
    ya = _sigmoid(o_ref[...]) * _head_rms(hm_ref[0] + hm_ref[1], mlw_ref[...])
    y5 = _gelu_tanh(ys_ref[...] + s5d_ref[...] * u_ref[...])
    yb = y5 * _sigmoid(_dot(y5, gluw_ref[...]) + glub_ref[...])
    yc = _head_rms(go_ref[0] + go_ref[1], gdw_ref[...]) * _silu(gz_ref[...])
    merged = None
    for n, y in enumerate((ya, yb, yc)):
        term = _sigmoid(gates_ref[:, n * D_MODEL:(n + 1) * D_MODEL]) * _dot(y, wbr_ref[n])
        merged = term if merged is None else merged + term
    out_ref[...] = x_ref[...] + mod_ref[0, 2:3, :] * _dot(merged, wout_ref[...])


def _merge(x, mod, z, hm, ys, go, p, seq):
    m = x.shape[0]
    tm = 256
    rows_per_mod = seq if mod.shape[0] > 1 else m
    row = lambda i: (i, 0)
    zspec = lambda cb: pl.BlockSpec((tm, BRANCH_W), lambda i: (i, cb))
    dspec = pl.BlockSpec((N_DIR, tm, BRANCH_W), lambda i: (0, i, 0))
    full = lambda a: pl.BlockSpec(a.shape, lambda i: (0,) * a.ndim)
    consts = [p['ml_norm_w'], p['s5_D'], p['s5_glu_w'], p['s5_glu_b'], p['gd_norm_w'], p['w_branch'], p['w_out']]
    return pl.pallas_call(
        _merge_kernel,
        grid=(m // tm,),
        in_specs=[pl.BlockSpec((tm, D_MODEL), row),
                  pl.BlockSpec((1, N_MOD, D_MODEL), lambda i: (i * tm // rows_per_mod, 0, 0)),
                  pl.BlockSpec((tm, 3 * D_MODEL), row),
                  zspec(CB_O), zspec(CB_SU), zspec(CB_GZ), dspec, pl.BlockSpec((tm, BRANCH_W), row), dspec]
                 + [full(a) for a in consts],
        out_specs=pl.BlockSpec((tm, D_MODEL), row),
        out_shape=jax.ShapeDtypeStruct((m, D_MODEL), F32),
        name="merge",
        compiler_params=_cparams(("arbitrary",)),
    )(x, mod, z, z, z, z, hm, ys, go, *consts)


def _ffn_kernel(x_ref, mod_ref, nw_ref, wg_ref, wu_ref, wd_ref, fw_ref, out_ref, h_scr, acc_scr, *, final):
    jf = pl.program_id(1)

    @pl.when(jf == 0)
    def _():
        x = x_ref[...]
        y = x * lax.rsqrt(jnp.mean(x * x, axis=-1, keepdims=True) + EPS)
        h_scr[...] = ((y * nw_ref[...]) * (1.0 + mod_ref[0, 4:5, :]) + mod_ref[0, 3:4, :]).astype(BF16)
        acc_scr[...] = jnp.zeros_like(acc_scr)

    hb = h_scr[...]
    a = _silu(jnp.dot(hb, wg_ref[...], preferred_element_type=F32))
    b = jnp.dot(hb, wu_ref[...], preferred_element_type=F32)
    acc_scr[...] += _dot(a * b, wd_ref[...])

    @pl.when(jf == pl.num_programs(1) - 1)
    def _():
        x = x_ref[...] + mod_ref[0, 5:6, :] * acc_scr[...]
        if final:
            x = x * lax.rsqrt(jnp.mean(x * x, axis=-1, keepdims=True) + EPS) * fw_ref[...]
        out_ref[...] = x


def _ffn(x, mod, p, seq, final_w, final):
    m = x.shape[0]
    tm, tf = 1024, D_FF // 2
    rows_per_mod = seq if mod.shape[0] > 1 else m
    row = lambda i, jf: (i, 0)
    vec = pl.BlockSpec((1, D_MODEL), lambda i, jf: (0, 0))
    return pl.pallas_call(
        functools.partial(_ffn_kernel, final=final),
        grid=(m // tm, D_FF // tf),
        in_specs=[pl.BlockSpec((tm, D_MODEL), row),
                  pl.BlockSpec((1, N_MOD, D_MODEL), lambda i, jf: (i * tm // rows_per_mod, 0, 0)),
                  vec,
                  pl.BlockSpec((D_MODEL, tf), lambda i, jf: (0, jf)),
                  pl.BlockSpec((D_MODEL, tf), lambda i, jf: (0, jf)),
                  pl.BlockSpec((tf, D_MODEL), lambda i, jf: (jf, 0)),
                  vec],
        out_specs=pl.BlockSpec((tm, D_MODEL), row),
        out_shape=jax.ShapeDtypeStruct((m, D_MODEL), F32),
        scratch_shapes=[pltpu.VMEM((tm, D_MODEL), BF16), pltpu.VMEM((tm, D_MODEL), F32)],
        name="ffn",
        compiler_params=_cparams(("arbitrary", "arbitrary")),
    )(x, mod, p['norm2_w'], p['w_gate'], p['w_up'], p['w_down'], final_w)


def _gate_lanes(parts):
    row = jnp.concatenate([a.astype(F32) for a in parts], axis=-1)
    return jnp.pad(row, ((0, 0), (0, GATE_W - row.shape[-1]))).reshape(N_DIR, 1, GATE_W)


def _block_diag(a):
    cb, g, r, c = a.shape
    eye = jnp.eye(g, dtype=a.dtype)
    return jnp.einsum('bgrc,gh->bgrhc', a, eye).reshape(cb, g * r, g * c)


def _layer_params(l, w):
    gpb = S5_GROUPS // S5_CB
    idx, acc = [], 0
    for size in IN_SIZES[:-1]:
        acc += size
        idx.append(acc)
    mq, mk, mv, mo, mi, mf, su, gqkv, gz, ga, gb, gates = jnp.split(w['w_in'][l], idx, axis=-1)
    w_big = jnp.concatenate([gates, mq, mk, mv, mo, su, gqkv, gz], axis=-1).astype(BF16)
    smalls = []
    for d in range(N_DIR):
        sl = slice(d * HEADS, (d + 1) * HEADS)
        blk = jnp.concatenate([mi[:, sl], mf[:, sl], ga[:, sl], gb[:, sl]], axis=-1)
        smalls.append(jnp.pad(blk, ((0, 0), (0, GATE_W - 4 * HEADS))))
    w_small = jnp.concatenate(smalls, axis=-1).astype(BF16)
    zeros = jnp.zeros((N_DIR, HEADS), F32)
    gate_bias = _gate_lanes([w['ml_i_bias'][l], w['ml_f_bias'][l], w['gd_dt_bias'][l], zeros])
    gate_alog = _gate_lanes([zeros, zeros, w['gd_A_log'][l], zeros])
    b_re = jnp.swapaxes(w['s5_B_re'][l], 1, 2).reshape(S5_CB, gpb, S5_GROUP, S5_STATE)
    b_im = jnp.swapaxes(w['s5_B_im'][l], 1, 2).reshape(S5_CB, gpb, S5_GROUP, S5_STATE)
    bbd = jnp.concatenate([_block_diag(b_re), _block_diag(b_im)], axis=-1).astype(BF16)
    c_re = jnp.swapaxes(w['s5_C_re'][l], 1, 2).reshape(S5_CB, gpb, S5_STATE, S5_GROUP)
    c_im = jnp.swapaxes(w['s5_C_im'][l], 1, 2).reshape(S5_CB, gpb, S5_STATE, S5_GROUP)
    cbd = jnp.concatenate([_block_diag(c_re), -_block_diag(c_im)], axis=1).astype(BF16)
    lam_shape = (N_DIR, S5_CB, 1, S5_CBW)
    ls = jnp.broadcast_to(w['s5_log_step'][l][:, :, None], (N_DIR, S5_GROUPS, S5_STATE))
    return dict(
        norm1_w=w['norm1_w'][l], w_big=w_big, w_small=w_small, gate_bias=gate_bias, gate_alog=gate_alog,
        ml_norm_w=w['ml_norm_w'][l].reshape(1, BRANCH_W),
        bbd=bbd, cbd=cbd,
        lam_re=w['s5_lam_re'][l].reshape(lam_shape), lam_im=w['s5_lam_im'][l].reshape(lam_shape),
        log_step=ls.reshape(lam_shape),
        s5_D=w['s5_D'][l].reshape(1, BRANCH_W), s5_glu_w=w['s5_glu_w'][l].astype(BF16),
        s5_glu_b=w['s5_glu_b'][l].reshape(1, BRANCH_W),
        gd_conv_w=w['gd_conv_w'][l],
        gd_norm_w=jnp.tile(w['gd_norm_w'][l], HEADS).reshape(1, BRANCH_W),
        w_branch=w['w_branch'][l].astype(BF16), w_out=w['w_out'][l].astype(BF16),
        norm2_w=w['norm2_w'][l].reshape(1, D_MODEL),
        w_gate=w['w_gate'][l].astype(BF16), w_up=w['w_up'][l].astype(BF16), w_down=w['w_down'][l].astype(BF16),
    )


def _grid_pos_embed(n_tok):
    grid_w = 64
    t = np.arange(n_tok)
    quarter = D_MODEL // 4
    omega = (1.0 / (10000.0 ** (np.arange(quarter, dtype=np.float32) / quarter))).astype(np.float32)

    def enc(pos):
        ang = pos.astype(np.float32)[:, None] * omega[None, :]
        return np.concatenate([np.sin(ang), np.cos(ang)], axis=-1)

    return jnp.asarray(np.concatenate([enc(t // grid_w), enc(t % grid_w)], axis=-1).astype(np.float32))


def _trunk_layer(x, mod, p, batch, seq, tb, states, final_w, final, pe=None):
    res = _inproj(x, mod, p['norm1_w'], p['w_big'], p['w_small'], seq, pe)
    if pe is not None:
        z, zs, x = res
    else:
        z, zs = res
    if states is None:
        ml_st = s5_st = gd_st = None
    else:
        ml_st, s5_st, gd_st = states
    ml = _mlstm(z, zs, p['gate_bias'], batch, seq, tb, ml_st)
    s5 = _s5(z, p['bbd'], p['cbd'], p['lam_re'], p['lam_im'], p['log_step'], batch, seq, s5_st)
    qkv = _gdn_prep(z, p['gd_conv_w'], seq)
    gd = _gdn(qkv, zs, p['gate_bias'], p['gate_alog'], batch, seq, tb, gd_st)
    x = _merge(x, mod, z, ml[0], s5[0], gd[0], p, seq)
    x = _ffn(x, mod, p, seq, final_w, final)
    new_states = None
    if states is None:
        new_states = (ml[1][..., :DH], ml[1][..., DH], ml[2][:, :, :, 0, 0],
                      jnp.transpose(s5[1], (2, 0, 1, 3)).reshape(batch, N_DIR, S5_GROUPS, S5_STATE),
                      jnp.transpose(s5[2], (2, 0, 1, 3)).reshape(batch, N_DIR, S5_GROUPS, S5_STATE),
                      gd[1])
    return x, new_states


def kernel(x_prompt, x_sample, state_mlstm_C, state_mlstm_n, state_mlstm_m, state_s5_re, state_s5_im,
           state_gdn_S, c, c_ctx, ada_w, ada_b, norm1_w, w_in, ml_i_bias, ml_f_bias, ml_norm_w,
           s5_lam_re, s5_lam_im, s5_log_step, s5_B_re, s5_B_im, s5_C_re, s5_C_im, s5_D, s5_glu_w, s5_glu_b,
           gd_conv_w, gd_A_log, gd_dt_bias, gd_norm_w, w_branch, w_out, norm2_w, w_gate, w_up, w_down,
           final_norm_w):
    w = dict(norm1_w=norm1_w, w_in=w_in, ml_i_bias=ml_i_bias, ml_f_bias=ml_f_bias, ml_norm_w=ml_norm_w,
             s5_lam_re=s5_lam_re, s5_lam_im=s5_lam_im, s5_log_step=s5_log_step, s5_B_re=s5_B_re,
             s5_B_im=s5_B_im, s5_C_re=s5_C_re, s5_C_im=s5_C_im, s5_D=s5_D, s5_glu_w=s5_glu_w,
             s5_glu_b=s5_glu_b, gd_conv_w=gd_conv_w, gd_A_log=gd_A_log, gd_dt_bias=gd_dt_bias,
             gd_norm_w=gd_norm_w, w_branch=w_branch, w_out=w_out, norm2_w=norm2_w, w_gate=w_gate,
             w_up=w_up, w_down=w_down)
    bp, sp, _ = x_prompt.shape
    bs, ss, _ = x_sample.shape
    params = [_layer_params(l, w) for l in range(DEPTH)]
    final_w = final_norm_w.reshape(1, D_MODEL)

    cc = jnp.concatenate([c_ctx[None, :], c, jnp.zeros((8 - 1 - bs, D_MODEL), F32)], axis=0)
    mods = _modulation(cc, ada_w, ada_b).reshape(DEPTH, 8, N_MOD, D_MODEL)

    xp = x_prompt.reshape(bp * sp, D_MODEL)
    per_layer = []
    for l in range(DEPTH):
        xp, st = _trunk_layer(xp, mods[l, 0:1], params[l], bp, sp, sp, None, final_w, l == DEPTH - 1)
        per_layer.append(st)
    y_prompt = xp.reshape(bp, sp, D_MODEL)
    new_states = [jnp.stack([st[i] for st in per_layer], axis=1) for i in range(6)]

    xs = x_sample.reshape(bs * ss, D_MODEL)
    pe = _grid_pos_embed(ss)
    for l in range(DEPTH):
        n_cols = jnp.broadcast_to(state_mlstm_n[:, l][..., None], (bs, N_DIR, HEADS, DH, DH))
        ml_st = (jnp.concatenate([state_mlstm_C[:, l], n_cols], axis=-1),
                 jnp.broadcast_to(state_mlstm_m[:, l][:, :, :, None, None], (bs, N_DIR, HEADS, 1, DH)))
        s5_st = tuple(jnp.transpose(a[:, l].reshape(bs, N_DIR, S5_CB, 1, S5_CBW), (1, 2, 0, 3, 4))
                      for a in (state_s5_re, state_s5_im))
        xs, _ = _trunk_layer(xs, mods[l, 1:1 + bs], params[l], bs, ss, 512, (ml_st, s5_st, state_gdn_S[:, l]),
                             final_w, l == DEPTH - 1, pe if l == 0 else None)
    y_sample = xs.reshape(bs, ss, D_MODEL)
    return (y_prompt, y_sample, *new_states)
```

```python
import functools
import math

import jax
import jax.numpy as jnp
import numpy as np
from jax import lax
from jax.experimental import pallas as pl
from jax.experimental.pallas import tpu as pltpu

F32 = jnp.float32
BF16 = jnp.bfloat16

D_MODEL = 1024
DEPTH = 2
N_DIR = 2
N_MOD = 6
EPS = 1e-6
HEADS = 4
DH = 128
BRANCH_W = HEADS * DH
CHUNK = 64
S5_GROUPS = 32
S5_GROUP = 16
S5_STATE = 64
S5_CB = 4
S5_CBW = S5_GROUPS * S5_STATE // S5_CB
S5_SUBLANES = 8
CONV_K = 5
D_FF = -(-8 * D_MODEL // (3 * 256)) * 256
IN_SIZES = (512, 512, 512, 512, 8, 8, 512, 1536, 512, 8, 8, 3072)
GATE_W = 128

CB_GATES, CB_Q, CB_K, CB_V, CB_O, CB_SU, CB_GQKV, CB_GZ = 0, 6, 7, 8, 9, 10, 11, 14
Z_W = 15 * BRANCH_W

VMEM_LIMIT = 56 * 1024 * 1024


def _cparams(sem):
    return pltpu.CompilerParams(dimension_semantics=sem, vmem_limit_bytes=VMEM_LIMIT)


def _dot(a, b):
    return jnp.dot(a.astype(BF16), b.astype(BF16), preferred_element_type=F32)


def _dot_nt(a, b):
    return lax.dot_general(a.astype(BF16), b.astype(BF16), (((1,), (1,)), ((), ())),
                           preferred_element_type=F32)


def _dot_tn(a, b):
    return lax.dot_general(a.astype(BF16), b.astype(BF16), (((0,), (0,)), ((), ())),
                           preferred_element_type=F32)


def _split3(x):
    hi = x.astype(BF16)
    r1 = x - hi.astype(F32)
    mid = r1.astype(BF16)
    lo = (r1 - mid.astype(F32)).astype(BF16)
    return hi, mid, lo


def _sel_dot(sel, x, nt=False):
    dims = (((1,), (1,)), ((), ())) if nt else (((1,), (0,)), ((), ()))
    hi, mid, lo = _split3(x)
    f = lambda p: lax.dot_general(sel, p, dims, preferred_element_type=F32)
    return (f(hi) + f(mid)) + f(lo)


def _dot_hp(a, b):
    ah = a.astype(BF16)
    al = (a - ah.astype(F32)).astype(BF16)
    bh = b.astype(BF16)
    bl = (b - bh.astype(F32)).astype(BF16)
    f = lambda p, q: jnp.dot(p, q, preferred_element_type=F32)
    return f(ah, bh) + (f(ah, bl) + f(al, bh))


def _sigmoid(x):
    return 1.0 / (1.0 + jnp.exp(-x))


def _silu(x):
    return x * _sigmoid(x)


def _softplus(x):
    return jnp.maximum(x, 0.0) + jnp.log(1.0 + jnp.exp(-jnp.abs(x)))


def _gelu_tanh(x):
    c = math.sqrt(2.0 / math.pi)
    return 0.5 * x * (1.0 + jnp.tanh(c * (x + 0.044715 * (x * x * x))))


def _head_rms(x, w_row):
    outs = []
    for h in range(HEADS):
        xh = x[:, h * DH:(h + 1) * DH]
        outs.append(xh * lax.rsqrt(jnp.mean(xh * xh, axis=-1, keepdims=True) + EPS))
    return jnp.concatenate(outs, axis=-1) * w_row


def _dir_chunk(c, d, n):
    return c + d * (n - 1 - 2 * c)


def _mod_kernel(c_ref, w_ref, b_ref, o_ref):
    o_ref[0] = _dot(_silu(c_ref[...]), w_ref[0]) + b_ref[0]


def _modulation(cc, ada_w, ada_b):
    tn = 1536
    nmod = N_MOD * D_MODEL
    return pl.pallas_call(
        _mod_kernel,
        grid=(DEPTH, nmod // tn),
        in_specs=[pl.BlockSpec((8, D_MODEL), lambda l, j: (0, 0)),
                  pl.BlockSpec((1, D_MODEL, tn), lambda l, j: (l, 0, j)),
                  pl.BlockSpec((1, 1, tn), lambda l, j: (l, 0, j))],
        out_specs=pl.BlockSpec((1, 8, tn), lambda l, j: (l, 0, j)),
        out_shape=jax.ShapeDtypeStruct((DEPTH, 8, nmod), F32),
        name="adaln_mod",
        compiler_params=_cparams(("arbitrary", "arbitrary")),
    )(cc, ada_w, ada_b.reshape(DEPTH, 1, nmod))


def _inproj_kernel(*refs, has_pe):
    if has_pe:
        x_ref, pe_ref, mod_ref, nw_ref, w_ref, ws_ref, z_ref, zs_ref, xs_ref, hn_scr = refs
    else:
        x_ref, mod_ref, nw_ref, w_ref, ws_ref, z_ref, zs_ref, hn_scr = refs

    @pl.when(pl.program_id(1) == 0)
    def _():
        x = x_ref[...]
        if has_pe:
            x = x + pe_ref[...]
            xs_ref[...] = x
        y = x * lax.rsqrt(jnp.mean(x * x, axis=-1, keepdims=True) + EPS)
        h = (y * nw_ref[0]) * (1.0 + mod_ref[0, 1:2, :]) + mod_ref[0, 0:1, :]
        hb = h.astype(BF16)
        hn_scr[...] = hb
        zs_ref[...] = jnp.dot(hb, ws_ref[0], preferred_element_type=F32)

    z_ref[...] = jnp.dot(hn_scr[...], w_ref[0], preferred_element_type=F32)


def _inproj(x, mod, p, l, seq, pe=None):
    m = x.shape[0]
    tm, tn = 1024, 1536
    rows_per_mod = seq if mod.shape[0] > 1 else m
    has_pe = pe is not None
    row = lambda i, j: (i, 0)
    in_specs = [pl.BlockSpec((tm, D_MODEL), row)]
    args = [x]
    if has_pe:
        in_specs.append(pl.BlockSpec((tm, D_MODEL), lambda i, j: (i % (seq // tm), 0)))
        args.append(pe)
    in_specs += [pl.BlockSpec((1, N_MOD, D_MODEL), lambda i, j: (i * tm // rows_per_mod, 0, 0)),
                 pl.BlockSpec((1, 1, D_MODEL), lambda i, j: (l, 0, 0)),
                 pl.BlockSpec((1, D_MODEL, tn), lambda i, j: (l, 0, j)),
                 pl.BlockSpec((1, D_MODEL, N_DIR * GATE_W), lambda i, j: (l, 0, 0))]
    args += [mod, p['norm1_w'], p['w_big'], p['w_small']]
    out_specs = [pl.BlockSpec((tm, tn), lambda i, j: (i, j)),
                 pl.BlockSpec((tm, N_DIR * GATE_W), row)]
    out_shape = [jax.ShapeDtypeStruct((m, Z_W), F32),
                 jax.ShapeDtypeStruct((m, N_DIR * GATE_W), F32)]
    if has_pe:
        out_specs.append(pl.BlockSpec((tm, D_MODEL), row))
        out_shape.append(jax.ShapeDtypeStruct((m, D_MODEL), F32))
    return pl.pallas_call(
        functools.partial(_inproj_kernel, has_pe=has_pe),
        grid=(m // tm, Z_W // tn),
        in_specs=in_specs, out_specs=out_specs, out_shape=out_shape,
        scratch_shapes=[pltpu.VMEM((tm, D_MODEL), BF16)],
        name="inproj",
        compiler_params=_cparams(("arbitrary", "arbitrary")),
    )(*args)


def _mixer_rowblk(nt):
    return lambda b, d, j: b * nt + _dir_chunk(j, d, nt)


def _gate_selector():
    r = lax.broadcasted_iota(jnp.int32, (16, GATE_W), 0)
    c = lax.broadcasted_iota(jnp.int32, (16, GATE_W), 1)
    return (r == c).astype(BF16)


def _block_gate_sums(vals, other, n_other, cls_scr, cum_scr):
    tri = (cls_scr[...] > 0).astype(BF16)
    for g in range(vals.shape[0] // HC):
        grp = slice(g * HC, (g + 1) * HC)
        cum_scr[grp, :] = _sel_dot(tri, vals[grp, :])
    lane = lax.broadcasted_iota(jnp.int32, vals.shape, 1)
    mixed = jnp.where(lane < n_other, other, cum_scr[...])
    return _sel_dot(_gate_selector(), mixed, nt=True), mixed


def _mlstm_kernel(*refs, nch, zero_init, emit_state):
    q_ref, k_ref, v_ref, g_ref, bias_ref = refs[:5]
    pos = 5
    if not zero_init:
        c0_ref, m0_ref = refs[pos:pos + 2]
        pos += 2
    h_ref = refs[pos]
    pos += 1
    if emit_state:
        co_ref, mo_ref = refs[pos:pos + 2]
        pos += 2
    c_scr, m_scr, cls_all, cum_scr, row_scr, dense_scr, av_scr, ml_scr = refs[pos:pos + 8]

    d = pl.program_id(1)
    j = pl.program_id(2)

    @pl.when(j == 0)
    def _():
        if zero_init:
            c_scr[...] = jnp.zeros_like(c_scr)
            m_scr[...] = jnp.zeros_like(m_scr)
        else:
            c_scr[...] = c0_ref[0, 0]
            m_scr[...] = m0_ref[0, 0]

    _fill_pair_classes(cls_all)
    cls_scr = cls_all.at[d]
    rowid = lax.broadcasted_iota(jnp.int32, (CHUNK, DH), 0)
    last = (CHUNK - 1) * (1 - d)
    bias = bias_ref[0, 0]
    scale = DH ** -0.5
    hrows = lambda h: slice(h * CHUNK, (h + 1) * CHUNK)
    per_head = lambda vals: jnp.concatenate([jnp.broadcast_to(a, (CHUNK, a.shape[1])) for a in vals], axis=0)
    twice = lambda a: jnp.concatenate([a, a], axis=1)
    ones = jnp.ones((HC, DH), F32)

    pre_all = g_ref[...] + bias
    xt, mixed = _block_gate_sums(-_softplus(-pre_all), pre_all, HEADS, cls_scr, cum_scr)
    for c in range(nch):
        cs = slice(c * CHUNK, (c + 1) * CHUNK)
        row_scr[c] = jnp.concatenate([xt[h:h + 1, cs] - xt[HEADS + h:HEADS + h + 1, cs] for h in range(HEADS)],
                                     axis=1)
    for h in range(HEADS):
        dense_scr[h, :, 0:DH] = jnp.broadcast_to(mixed[:, HEADS + h:HEADS + h + 1], (nch * CHUNK, DH))
        dense_scr[h, :, DH:2 * DH] = jnp.broadcast_to(mixed[:, h:h + 1], (nch * CHUNK, DH))

    stacked = lambda rows, lo: jnp.concatenate([dense_scr[h, rows, lo:lo + DH] for h in range(HEADS)], axis=0)
    npar = MLSTM_LOCKSTEP

    def local(cp, carry):
        cs = [cp * npar + i for i in range(npar)]
        rows = [pl.ds(pl.multiple_of(c * CHUNK, CHUNK), CHUNK) for c in cs]
        cls = cls_scr[...]
        log_d = [jnp.where(cls > 0, twice(stacked(r, 0)) + row_scr[c], -jnp.inf) for r, c in zip(rows, cs)]
        ml = [jnp.max(a, axis=-1, keepdims=True) for a in log_d]
        q = [_stack_heads(q_ref[r, :]) for r in rows]
        k = [_stack_heads(k_ref[r, :]) * scale for r in rows]
        v = [jnp.concatenate([_stack_heads(v_ref[r, :]), ones], axis=1) for r in rows]
        s = [_dot_nt(a, b) * jnp.exp(ld - m) for a, b, ld, m in zip(q, k, log_d, ml)]
        av = [_dot(a, b) for a, b in zip(s, v)]
        for i, c in enumerate(cs):
            av_scr[c] = av[i]
            ml_scr[c] = jnp.broadcast_to(ml[i], (HC, DH))
        return carry

    lax.fori_loop(0, nch // npar, local, 0)

    def advance(c, carry):
        ci = _dir_chunk(c, d, nch)
        r0 = pl.multiple_of(ci * CHUNK, CHUNK)
        rows = pl.ds(r0, CHUNK)
        b_tok = stacked(rows, 0)
        i_tok = stacked(rows, DH)
        at_last = lambda a: [jnp.sum(jnp.where(rowid == last, a[hrows(h), :], 0.0), axis=0, keepdims=True)
                             for h in range(HEADS)]
        b_last = at_last(b_tok)
        m_prev = [m_scr[h] for h in range(HEADS)]
        ml = ml_scr[ci]
        log_0 = b_tok + per_head(m_prev)
        m_t = jnp.maximum(log_0, ml)
        w_0 = jnp.exp(log_0 - m_t)
        f = jnp.exp(ml - m_t)
        q = _stack_heads(q_ref[rows, :])
        k = _stack_heads(k_ref[rows, :]) * scale
        v = jnp.concatenate([_stack_heads(v_ref[rows, :]), ones], axis=1)
        cst = [c_scr[h] for h in range(HEADS)]
        qc = jnp.concatenate([_dot(q[hrows(h), :], cst[h]) for h in range(HEADS)], axis=0)
        num = f * av_scr[ci, :, 0:DH] + w_0 * qc[:, 0:DH]
        den = f * av_scr[ci, :, DH:2 * DH] + w_0 * qc[:, DH:2 * DH]
        hv = num / jnp.maximum(jnp.abs(den), jnp.exp(-m_t))
        m_new = at_last(m_t)
        kw = k * jnp.exp(per_head(b_last) - b_tok + i_tok - per_head(m_new))
        for h in range(HEADS):
            h_ref[0, rows, h * DH:(h + 1) * DH] = hv[hrows(h), :]
            c_0 = jnp.exp(b_last[h] + m_prev[h] - m_new[h])
            c_scr[h] = twice(c_0) * cst[h] + _dot_tn(kw[hrows(h), :], v[hrows(h), :])
            m_scr[h] = m_new[h]
        return carry

    lax.fori_loop(0, nch, advance, 0)

    if emit_state:
        @pl.when(j == pl.num_programs(2) - 1)
        def _():
            co_ref[0, 0] = c_scr[...]
            mo_ref[0, 0] = m_scr[...]


def _mlstm(z, zs, p, l, batch, seq, tb, states):
    m = batch * seq
    nt = seq // tb
    nch = tb // CHUNK
    zero_init = states is None
    emit_state = states is None
    rb = _mixer_rowblk(nt)
    zspec = lambda cb: pl.BlockSpec((tb, BRANCH_W), lambda b, d, j: (rb(b, d, j), cb))
    st5 = lambda shape: pl.BlockSpec((1, 1) + shape, lambda b, d, j: (b, d) + (0,) * len(shape))
    in_specs = [zspec(CB_Q), zspec(CB_K), zspec(CB_V),
                pl.BlockSpec((tb, GATE_W), lambda b, d, j: (rb(b, d, j), d)),
                pl.BlockSpec((1, 1, 1, GATE_W), lambda b, d, j: (l, d, 0, 0))]
    args = [z, z, z, zs, p['gate_bias']]
    if not zero_init:
        in_specs += [st5((HEADS, DH, 2 * DH)), st5((HEADS, 1, DH))]
        args += list(states)
    out_specs = [pl.BlockSpec((1, tb, BRANCH_W), lambda b, d, j: (d, rb(b, d, j), 0))]
    out_shape = [jax.ShapeDtypeStruct((N_DIR, m, BRANCH_W), F32)]
    if emit_state:
        out_specs += [st5((HEADS, DH, 2 * DH)), st5((HEADS, 1, DH))]
        out_shape += [jax.ShapeDtypeStruct((batch, N_DIR, HEADS, DH, 2 * DH), F32),
                      jax.ShapeDtypeStruct((batch, N_DIR, HEADS, 1, DH), F32)]
    return pl.pallas_call(
        functools.partial(_mlstm_kernel, nch=nch, zero_init=zero_init, emit_state=emit_state),
        grid=(batch, N_DIR, nt),
        in_specs=in_specs, out_specs=out_specs, out_shape=out_shape,
        scratch_shapes=[pltpu.VMEM((HEADS, DH, 2 * DH), F32), pltpu.VMEM((HEADS, 1, DH), F32),
                        pltpu.VMEM((N_DIR, HC, HC), jnp.int32),
                        pltpu.VMEM((tb, GATE_W), F32), pltpu.VMEM((nch, 1, HC), F32),
                        pltpu.VMEM((HEADS, tb, 2 * DH), F32),
                        pltpu.VMEM((nch, HC, 2 * DH), F32), pltpu.VMEM((nch, HC, DH), F32)],
        name="mlstm",
        compiler_params=_cparams(("arbitrary", "arbitrary", "arbitrary")),
    )(*args)


def _gdn_prep_kernel(x_ref, prev_ref, next_ref, w_ref, o_ref, xe_scr, *, tb, seq):
    i = pl.program_id(0)
    p = pl.program_id(1)
    pad = 8
    at_start = (i * tb) % seq == 0
    at_end = ((i + 1) * tb) % seq == 0
    xe_scr[0:pad, :] = jnp.where(at_start, 0.0, prev_ref[...])
    xe_scr[pad:pad + tb, :] = x_ref[...]
    xe_scr[pad + tb:pad + tb + pad, :] = jnp.where(at_end, 0.0, next_ref[...])
    acc = None
    for t in range(CONV_K):
        term = xe_scr[pl.ds(pad + t - CONV_K // 2, tb), :] * w_ref[0, t:t + 1, :]
        acc = term if acc is None else acc + term
    y = _silu(acc)
    outs = []
    for h in range(HEADS):
        yh = y[:, h * DH:(h + 1) * DH]
        outs.append(yh * lax.rsqrt(jnp.sum(yh * yh, axis=-1, keepdims=True) + EPS))
    yn = jnp.concatenate(outs, axis=-1)
    qscale = jnp.where(p == 0, DH ** -0.5, 1.0)
    o_ref[...] = jnp.where(p == 2, y, yn * qscale)


def _gdn_prep(z, p, l, seq):
    m = z.shape[0]
    tb = 256
    nb8 = m // 8
    return pl.pallas_call(
        functools.partial(_gdn_prep_kernel, tb=tb, seq=seq),
        grid=(m // tb, 3),
        in_specs=[pl.BlockSpec((tb, BRANCH_W), lambda i, part: (i, CB_GQKV + part)),
                  pl.BlockSpec((8, BRANCH_W), lambda i, part: (jnp.maximum(i * (tb // 8) - 1, 0), CB_GQKV + part)),
                  pl.BlockSpec((8, BRANCH_W), lambda i, part: (jnp.minimum((i + 1) * (tb // 8), nb8 - 1), CB_GQKV + part)),
                  pl.BlockSpec((1, CONV_K, BRANCH_W), lambda i, part: (l, 0, part))],
        out_specs=pl.BlockSpec((tb, BRANCH_W), lambda i, part: (i, part)),
        out_shape=jax.ShapeDtypeStruct((m, 3 * BRANCH_W), F32),
        scratch_shapes=[pltpu.VMEM((tb + 16, BRANCH_W), F32)],
        name="gdn_prep",
        compiler_params=_cparams(("arbitrary", "arbitrary")),
    )(z, z, z, p['gd_conv_w'])


HC = HEADS * CHUNK


def _stack_heads(x):
    return jnp.concatenate([x[:, h * DH:(h + 1) * DH] for h in range(HEADS)], axis=0)


def _stack_gate(x, lane0):
    return jnp.concatenate([x[:, lane0 + h:lane0 + h + 1] for h in range(HEADS)], axis=0)


GDN_LOCKSTEP = 4
MLSTM_LOCKSTEP = 4
BASE_LG = 3
CHUNK_LG = int(math.log2(CHUNK))


def _pair_classes(d):
    r = lax.broadcasted_iota(jnp.int32, (HC, HC), 0)
    c = lax.broadcasted_iota(jnp.int32, (HC, HC), 1)
    cls = jnp.full((HC, HC), CHUNK_LG, jnp.int32)
    for s in range(CHUNK_LG - 1, BASE_LG - 1, -1):
        cls = jnp.where((r >> s) == (c >> s), s, cls)
    diff = (r - c) * (1 - 2 * d)
    cls = jnp.where(diff == 0, 1, cls)
    return jnp.where(jnp.logical_and((r >> CHUNK_LG) == (c >> CHUNK_LG), diff >= 0), cls, 0)


def _fill_pair_classes(cls_all):
    first = functools.reduce(jnp.logical_and, [pl.program_id(i) == 0 for i in range(3)])

    @pl.when(first)
    def _():
        for d in range(N_DIR):
            cls_all[d] = _pair_classes(d)


def _unit_tri_solve(xs, cls, rhss):
    n = xs[0].shape[0]
    nr = rhss[0].shape[1]
    x0 = [jnp.where(cls == BASE_LG, x, 0.0) for x in xs]
    x2 = [_dot(a, a) for a in x0]
    y = [_dot(b, jnp.concatenate([b, a], axis=1)) for a, b in zip(x0, x2)]
    x4 = [t[:, :n] for t in y]
    q2 = [a + b + t[:, n:] for a, b, t in zip(x0, x2, y)]
    t8m = [a + b + _dot(b, a) for a, b in zip(q2, x4)]
    levels = range(BASE_LG + 1, CHUNK_LG + 1)
    cur = [jnp.concatenate([r] + [jnp.where(cls == s, x, 0.0) for s in levels], axis=1) for x, r in zip(xs, rhss)]
    cur = [a + _dot(t, a) for a, t in zip(cur, t8m)]
    for _ in levels:
        last = cur[0].shape[1] == nr + n
        rest = [a[:, :nr] if last else jnp.concatenate([a[:, :nr], a[:, nr + n:]], axis=1) for a in cur]
        cur = [r + _dot(a[:, nr:nr + n], r) for a, r in zip(cur, rest)]
    return cur


def _gdn_kernel(*refs, nch, zero_init, emit_state):
    q_ref, k_ref, v_ref, g_ref, bias_ref, alog_ref = refs[:6]
    pos = 6
    if not zero_init:
        s0_ref = refs[pos]
        pos += 1
    o_ref = refs[pos]
    pos += 1
    if emit_state:
        so_ref = refs[pos]
        pos += 1
    s_scr, cls_all, cum_scr, row_scr, uw_scr, at_scr, qg_scr, kd_scr, gt_scr = refs[pos:pos + 9]

    d = pl.program_id(1)
    j = pl.program_id(2)

    @pl.when(j == 0)
    def _():
        if zero_init:
            s_scr[...] = jnp.zeros_like(s_scr)
        else:
            s_scr[...] = s0_ref[0, 0]

    _fill_pair_classes(cls_all)
    cls_scr = cls_all.at[d]
    last = (CHUNK - 1) * (1 - d)
    bias = bias_ref[0, 0]
    neg_a = -jnp.exp(alog_ref[0, 0])
    g0 = 2 * HEADS

    g_all = neg_a * _softplus(g_ref[...] + bias)
    xt, _ = _block_gate_sums(g_all, g_all, 0, cls_scr, cum_scr)
    for c in range(nch):
        cs = slice(c * CHUNK, (c + 1) * CHUNK)
        row_scr[c] = jnp.concatenate([xt[g0 + h:g0 + h + 1, cs] for h in range(HEADS)], axis=1)

    npar = GDN_LOCKSTEP

    def prepare(cp, carry):
        cs = [cp * npar + i for i in range(npar)]
        r0 = [pl.multiple_of(c * CHUNK, CHUNK) for c in cs]
        rows = [pl.ds(r, CHUNK) for r in r0]
        gtot = [cum_scr[pl.ds(r + last, 1), :] for r in r0]
        g_col = [_stack_gate(cum_scr[r, :], g0) for r in rows]
        beta = [_stack_gate(_sigmoid(g_ref[r, :] + bias), g0 + HEADS) for r in rows]
        g_last = [jnp.concatenate([jnp.broadcast_to(t[:, g0 + h:g0 + h + 1], (CHUNK, 1)) for h in range(HEADS)],
                                  axis=0) for t in gtot]
        cls = cls_scr[...]
        decay = [jnp.exp(jnp.where(cls > 0, gc - row_scr[c], -jnp.inf)) for gc, c in zip(g_col, cs)]
        eg = [jnp.exp(gc) for gc in g_col]
        q = [_stack_heads(q_ref[r, :]) for r in rows]
        k = [_stack_heads(k_ref[r, :]) for r in rows]
        v = [_stack_heads(v_ref[r, :]) for r in rows]
        kb = [a * b for a, b in zip(k, beta)]
        x = [-(_dot_nt(a, b) * dc) for a, b, dc in zip(kb, k, decay)]
        rhs = [jnp.concatenate([a * b, kbi * e], axis=-1) for a, b, kbi, e in zip(v, beta, kb, eg)]
        uw = _unit_tri_solve(x, cls, rhs)
        attn = [_dot_nt(a, b) * dc for a, b, dc in zip(q, k, decay)]
        for i, c in enumerate(cs):
            uw_scr[c] = uw[i]
            at_scr[c] = attn[i]
            qg_scr[c] = q[i] * eg[i]
            kd_scr[c] = k[i] * jnp.exp(g_last[i] - g_col[i])
            gt_scr[c] = gtot[i]
        return carry

    lax.fori_loop(0, nch // npar, prepare, 0)

    def advance(c, carry):
        ci = _dir_chunk(c, d, nch)
        rows = pl.ds(pl.multiple_of(ci * CHUNK, CHUNK), CHUNK)
        hrows = lambda h: slice(h * CHUNK, (h + 1) * CHUNK)
        st = [s_scr[h] for h in range(HEADS)]
        v_new = [uw_scr[ci, hrows(h), 0:DH] - _dot(uw_scr[ci, hrows(h), DH:2 * DH], st[h]) for h in range(HEADS)]
        qs = jnp.concatenate([_dot(qg_scr[ci, hrows(h), :], st[h]) for h in range(HEADS)], axis=0)
        o = qs + _dot(at_scr[ci], jnp.concatenate(v_new, axis=0))
        gtot = gt_scr[ci]
        for h in range(HEADS):
            o_ref[0, rows, h * DH:(h + 1) * DH] = o[hrows(h), :]
            s_scr[h] = st[h] * jnp.exp(gtot[:, g0 + h:g0 + h + 1]) + _dot_tn(kd_scr[ci, hrows(h), :], v_new[h])
        return carry

    lax.fori_loop(0, nch, advance, 0)

    if emit_state:
        @pl.when(j == pl.num_programs(2) - 1)
        def _():
            so_ref[0, 0] = s_scr[...]


def _gdn(qkv, zs, p, l, batch, seq, tb, s0):
    m = batch * seq
    nt = seq // tb
    nch = tb // CHUNK
    zero_init = s0 is None
    emit_state = s0 is None
    rb = _mixer_rowblk(nt)
    spec = lambda cb: pl.BlockSpec((tb, BRANCH_W), lambda b, d, j: (rb(b, d, j), cb))
    st = pl.BlockSpec((1, 1, HEADS, DH, DH), lambda b, d, j: (b, d, 0, 0, 0))
    dirrow = pl.BlockSpec((1, 1, 1, GATE_W), lambda b, d, j: (l, d, 0, 0))
    in_specs = [spec(0), spec(1), spec(2),
                pl.BlockSpec((tb, GATE_W), lambda b, d, j: (rb(b, d, j), d)), dirrow, dirrow]
    args = [qkv, qkv, qkv, zs, p['gate_bias'], p['gate_alog']]
    if not zero_init:
        in_specs.append(st)
        args.append(s0)
    out_specs = [pl.BlockSpec((1, tb, BRANCH_W), lambda b, d, j: (d, rb(b, d, j), 0))]
    out_shape = [jax.ShapeDtypeStruct((N_DIR, m, BRANCH_W), F32)]
    if emit_state:
        out_specs.append(st)
        out_shape.append(jax.ShapeDtypeStruct((batch, N_DIR, HEADS, DH, DH), F32))
    return pl.pallas_call(
        functools.partial(_gdn_kernel, nch=nch, zero_init=zero_init, emit_state=emit_state),
        grid=(batch, N_DIR, nt),
        in_specs=in_specs, out_specs=out_specs, out_shape=out_shape,
        scratch_shapes=[pltpu.VMEM((HEADS, DH, DH), F32), pltpu.VMEM((N_DIR, HC, HC), jnp.int32),
                        pltpu.VMEM((tb, GATE_W), F32), pltpu.VMEM((nch, 1, HC), F32),
                        pltpu.VMEM((nch, HC, 2 * DH), F32), pltpu.VMEM((nch, HC, HC), F32),
                        pltpu.VMEM((nch, HC, DH), F32), pltpu.VMEM((nch, HC, DH), F32),
                        pltpu.VMEM((nch, 1, GATE_W), F32)],
        name="gdn",
        compiler_params=_cparams(("arbitrary", "arbitrary", "arbitrary")),
    )(*args)


def _s5_discretise(lre, lim, ls):
    dt = jnp.exp(ls)
    mag = jnp.exp(lre * dt)
    ar = mag * jnp.cos(lim * dt)
    ai = mag * jnp.sin(lim * dt)
    den = lre * lre + lim * lim
    nr = ar - 1.0
    return ar, ai, (nr * lre + ai * lim) / den, (ai * lre - nr * lim) / den


def _s5_kernel(*refs, rb, seg, chain):
    u_ref, b_ref, c_ref, lre_ref, lim_ref, ls_ref = refs[:6]
    pos = 6
    if chain:
        h0r_ref, h0i_ref = refs[pos:pos + 2]
        pos += 2
    y_ref = refs[pos]
    pos += 1
    if not chain:
        sr_ref, si_ref = refs[pos:pos + 2]
        pos += 2
    up, yp, xr, xi, er, ei, pr, pi_ = refs[pos:pos + 8]

    w = S5_CBW
    nsub = S5_SUBLANES
    rc = 256
    ngrp = seg // nsub
    par = [_s5_discretise(lre_ref[0, d, 0], lim_ref[0, d, 0], ls_ref[0, d, 0]) for d in range(N_DIR)]

    def interleave(g, carry):
        for s in range(nsub):
            src = pl.ds(pl.multiple_of(s * seg + g * nsub, nsub), nsub)
            up[pl.ds(g * nsub * nsub + s, nsub, stride=nsub), :] = u_ref[src, :]
        return carry

    lax.fori_loop(0, ngrp, interleave, 0)

    def fill(c, carry):
        rows = pl.ds(pl.multiple_of(c * rc, rc), rc)
        bu = _dot(up[rows, :], b_ref[0, 0])
        bre = bu[:, :w]
        bim = bu[:, w:]
        for d in range(N_DIR):
            _, _, zr, zi = par[d]
            xr[d, rows, :] = zr * bre - zi * bim
            xi[d, rows, :] = zr * bim + zi * bre
        return carry

    lax.fori_loop(0, rb // rc, fill, 0, unroll=2)

    ab = [(jnp.broadcast_to(p[0], (nsub, w)), jnp.broadcast_to(p[1], (nsub, w))) for p in par]

    def step_rows(tt, d):
        t = tt if d == 0 else seg - 1 - tt
        return pl.ds(pl.multiple_of(t * nsub, nsub), nsub)

    def scan_step(tt, carry):
        out = []
        for d in range(N_DIR):
            hr, hi = carry[d]
            arb, aib = ab[d]
            idx = step_rows(tt, d)
            nhr = arb * hr - aib * hi + xr[d, idx, :]
            nhi = arb * hi + aib * hr + xi[d, idx, :]
            xr[d, idx, :] = nhr
            xi[d, idx, :] = nhi
            out.append((nhr, nhi))
        return tuple(out)

    zero = jnp.zeros((nsub, w), F32)
    ends = lax.fori_loop(0, seg, scan_step, ((zero, zero), (zero, zero)), unroll=2)

    if chain:
        for d in range(N_DIR):
            ar, ai = par[d][0], par[d][1]
            er[d] = ends[d][0]
            ei[d] = ends[d][1]
            sr, si = ar, ai
            for _ in range(int(math.log2(seg))):
                sr, si = sr * sr - si * si, 2.0 * sr * si
            fr = h0r_ref[d, 0, 0]
            fi = h0i_ref[d, 0, 0]
            for k in range(nsub):
                row = k if d == 0 else nsub - 1 - k
                pr[d, row:row + 1, :] = fr
                pi_[d, row:row + 1, :] = fi
                fr, fi = (er[d, row:row + 1, :] + (sr * fr - si * fi),
                          ei[d, row:row + 1, :] + (sr * fi + si * fr))

        def fix_step(tt, carry):
            out = []
            for d in range(N_DIR):
                cr, ci = carry[d]
                arb, aib = ab[d]
                idx = step_rows(tt, d)
                ncr = arb * cr - aib * ci
                nci = arb * ci + aib * cr
                xr[d, idx, :] = xr[d, idx, :] + ncr
                xi[d, idx, :] = xi[d, idx, :] + nci
                out.append((ncr, nci))
            return tuple(out)

        lax.fori_loop(0, seg, fix_step, tuple((pr[d], pi_[d]) for d in range(N_DIR)), unroll=2)
    else:
        for d in range(N_DIR):
            sr_ref[d, 0] = ends[d][0]
            si_ref[d, 0] = ends[d][1]

    def proj(c, carry):
        rows = pl.ds(pl.multiple_of(c * rc, rc), rc)
        yp[rows, :] = (_dot(xr[0, rows, :] + xr[1, rows, :], c_ref[0, 0, :w, :])
                       + _dot(xi[0, rows, :] + xi[1, rows, :], c_ref[0, 0, w:, :]))
        return carry

    lax.fori_loop(0, rb // rc, proj, 0, unroll=4)

    def deinterleave(g, carry):
        for s in range(nsub):
            dst = pl.ds(pl.multiple_of(s * seg + g * nsub, nsub), nsub)
            y_ref[dst, :] = yp[pl.ds(g * nsub * nsub + s, nsub, stride=nsub), :]
        return carry

    lax.fori_loop(0, ngrp, deinterleave, 0)


def _s5(z, p, l, batch, seq, h0):
    m = batch * seq
    nsub = S5_SUBLANES
    chain = h0 is not None
    if chain:
        seg = seq // nsub
        ng = batch
        st_spec = pl.BlockSpec((N_DIR, 1, 1, 1, S5_CBW), lambda g, cb: (0, cb, g, 0, 0))
    else:
        seg = seq
        ng = batch // nsub
        st_spec = pl.BlockSpec((N_DIR, 1, nsub, S5_CBW), lambda g, cb: (0, cb, g, 0))
    rb = nsub * seg
    lam_spec = pl.BlockSpec((1, N_DIR, 1, 1, S5_CBW), lambda g, cb: (l, 0, cb, 0, 0))
    in_specs = [pl.BlockSpec((rb, 128), lambda g, cb: (g, CB_SU * 4 + cb)),
                pl.BlockSpec((1, 1, 128, 2 * S5_CBW), lambda g, cb: (l, cb, 0, 0)),
                pl.BlockSpec((1, 1, 2 * S5_CBW, 128), lambda g, cb: (l, cb, 0, 0)),
                lam_spec, lam_spec, lam_spec]
    args = [z, p['bbd'], p['cbd'], p['lam_re'], p['lam_im'], p['log_step']]
    if chain:
        in_specs += [st_spec, st_spec]
        args += list(h0)
    out_specs = [pl.BlockSpec((rb, 128), lambda g, cb: (g, cb))]
    out_shape = [jax.ShapeDtypeStruct((m, BRANCH_W), F32)]
    if not chain:
        out_specs += [st_spec, st_spec]
        out_shape += [jax.ShapeDtypeStruct((N_DIR, S5_CB, batch, S5_CBW), F32)] * 2
    vdir = lambda n: pltpu.VMEM((N_DIR, n, S5_CBW), F32)
    lanes = pltpu.VMEM((rb, 128), F32)
    return pl.pallas_call(
        functools.partial(_s5_kernel, rb=rb, seg=seg, chain=chain),
        grid=(ng, S5_CB),
        in_specs=in_specs, out_specs=out_specs, out_shape=out_shape,
        scratch_shapes=[lanes, lanes, vdir(rb), vdir(rb), vdir(nsub), vdir(nsub), vdir(nsub), vdir(nsub)],
        name="s5",
        compiler_params=_cparams(("arbitrary", "arbitrary")),
    )(*args)


def _merge_kernel(x_ref, mod_ref, gates_ref, o_ref, u_ref, gz_ref, hm_ref, ys_ref, go_ref,
                  mlw_ref, s5d_ref, gluw_ref, glub_ref, gdw_ref, wbr_ref, wout_ref, out_ref):
    ya = _sigmoid(o_ref[...]) * _head_rms(hm_ref[0] + hm_ref[1], mlw_ref[0])
    y5 = _gelu_tanh(ys_ref[...] + s5d_ref[0] * u_ref[...])
    yb = y5 * _sigmoid(_dot(y5, gluw_ref[0]) + glub_ref[0])
    yc = _head_rms(go_ref[0] + go_ref[1], gdw_ref[0]) * _silu(gz_ref[...])
    merged = None
    for n, y in enumerate((ya, yb, yc)):
        term = _sigmoid(gates_ref[:, n * D_MODEL:(n + 1) * D_MODEL]) * _dot(y, wbr_ref[0, n])
        merged = term if merged is None else merged + term
    out_ref[...] = x_ref[...] + mod_ref[0, 2:3, :] * _dot(merged, wout_ref[0])


def _merge(x, mod, z, hm, ys, go, p, l, seq):
    m = x.shape[0]
    tm = 256
    rows_per_mod = seq if mod.shape[0] > 1 else m
    row = lambda i: (i, 0)
    zspec = lambda cb: pl.BlockSpec((tm, BRANCH_W), lambda i: (i, cb))
    dspec = pl.BlockSpec((N_DIR, tm, BRANCH_W), lambda i: (0, i, 0))
    full = lambda a: pl.BlockSpec((1,) + a.shape[1:], lambda i: (l,) + (0,) * (a.ndim - 1))
    consts = [p['ml_norm_w'], p['s5_D'], p['s5_glu_w'], p['s5_glu_b'], p['gd_norm_w'], p['w_branch'], p['w_out']]
    return pl.pallas_call(
        _merge_kernel,
        grid=(m // tm,),
        in_specs=[pl.BlockSpec((tm, D_MODEL), row),
                  pl.BlockSpec((1, N_MOD, D_MODEL), lambda i: (i * tm // rows_per_mod, 0, 0)),
                  pl.BlockSpec((tm, 3 * D_MODEL), row),
                  zspec(CB_O), zspec(CB_SU), zspec(CB_GZ), dspec, pl.BlockSpec((tm, BRANCH_W), row), dspec]
                 + [full(a) for a in consts],
        out_specs=pl.BlockSpec((tm, D_MODEL), row),
        out_shape=jax.ShapeDtypeStruct((m, D_MODEL), F32),
        name="merge",
        compiler_params=_cparams(("arbitrary",)),
    )(x, mod, z, z, z, z, hm, ys, go, *consts)


def _ffn_kernel(x_ref, mod_ref, nw_ref, wg_ref, wu_ref, wd_ref, fw_ref, out_ref, h_scr, acc_scr, *, final):
    jf = pl.program_id(1)

    @pl.when(jf == 0)
    def _():
        x = x_ref[...]
        y = x * lax.rsqrt(jnp.mean(x * x, axis=-1, keepdims=True) + EPS)
        h_scr[...] = ((y * nw_ref[0]) * (1.0 + mod_ref[0, 4:5, :]) + mod_ref[0, 3:4, :]).astype(BF16)
        acc_scr[...] = jnp.zeros_like(acc_scr)

    hb = h_scr[...]
    a = _silu(jnp.dot(hb, wg_ref[0], preferred_element_type=F32))
    b = jnp.dot(hb, wu_ref[0], preferred_element_type=F32)
    acc_scr[...] += _dot(a * b, wd_ref[0])

    @pl.when(jf == pl.num_programs(1) - 1)
    def _():
        x = x_ref[...] + mod_ref[0, 5:6, :] * acc_scr[...]
        if final:
            x = x * lax.rsqrt(jnp.mean(x * x, axis=-1, keepdims=True) + EPS) * fw_ref[...]
        out_ref[...] = x


def _ffn(x, mod, p, l, seq, final_w, final):
    m = x.shape[0]
    tm, tf = 1024, D_FF // 2
    rows_per_mod = seq if mod.shape[0] > 1 else m
    row = lambda i, jf: (i, 0)
    vec = pl.BlockSpec((1, D_MODEL), lambda i, jf: (0, 0))
    return pl.pallas_call(
        functools.partial(_ffn_kernel, final=final),
        grid=(m // tm, D_FF // tf),
        in_specs=[pl.BlockSpec((tm, D_MODEL), row),
                  pl.BlockSpec((1, N_MOD, D_MODEL), lambda i, jf: (i * tm // rows_per_mod, 0, 0)),
                  pl.BlockSpec((1, 1, D_MODEL), lambda i, jf: (l, 0, 0)),
                  pl.BlockSpec((1, D_MODEL, tf), lambda i, jf: (l, 0, jf)),
                  pl.BlockSpec((1, D_MODEL, tf), lambda i, jf: (l, 0, jf)),
                  pl.BlockSpec((1, tf, D_MODEL), lambda i, jf: (l, jf, 0)),
                  vec],
        out_specs=pl.BlockSpec((tm, D_MODEL), row),
        out_shape=jax.ShapeDtypeStruct((m, D_MODEL), F32),
        scratch_shapes=[pltpu.VMEM((tm, D_MODEL), BF16), pltpu.VMEM((tm, D_MODEL), F32)],
        name="ffn",
        compiler_params=_cparams(("arbitrary", "arbitrary")),
    )(x, mod, p['norm2_w'], p['w_gate'], p['w_up'], p['w_down'], final_w)


def _gate_lanes(parts):
    row = jnp.concatenate([a.astype(F32) for a in parts], axis=-1)
    return jnp.pad(row, ((0, 0), (0, 0), (0, GATE_W - row.shape[-1]))).reshape(DEPTH, N_DIR, 1, GATE_W)


def _block_diag(a):
    dp, cb, g, r, c = a.shape
    eye = jnp.eye(g, dtype=a.dtype)
    return jnp.einsum('lbgrc,gh->lbgrhc', a, eye).reshape(dp, cb, g * r, g * c)


def _prep_params(w):
    gpb = S5_GROUPS // S5_CB
    idx, acc = [], 0
    for size in IN_SIZES[:-1]:
        acc += size
        idx.append(acc)
    mq, mk, mv, mo, mi, mf, su, gqkv, gz, ga, gb, gates = jnp.split(w['w_in'], idx, axis=-1)
    w_big = jnp.concatenate([gates, mq, mk, mv, mo, su, gqkv, gz], axis=-1).astype(BF16)
    smalls = []
    for d in range(N_DIR):
        sl = slice(d * HEADS, (d + 1) * HEADS)
        blk = jnp.concatenate([mi[..., sl], mf[..., sl], ga[..., sl], gb[..., sl]], axis=-1)
        smalls.append(jnp.pad(blk, ((0, 0), (0, 0), (0, GATE_W - 4 * HEADS))))
    w_small = jnp.concatenate(smalls, axis=-1).astype(BF16)
    zeros = jnp.zeros((DEPTH, N_DIR, HEADS), F32)
    gate_bias = _gate_lanes([w['ml_i_bias'], w['ml_f_bias'], w['gd_dt_bias'], zeros])
    gate_alog = _gate_lanes([zeros, zeros, w['gd_A_log'], zeros])
    b_shape = (DEPTH, S5_CB, gpb, S5_GROUP, S5_STATE)
    c_shape = (DEPTH, S5_CB, gpb, S5_STATE, S5_GROUP)
    b_re = jnp.swapaxes(w['s5_B_re'], 2, 3).reshape(b_shape)
    b_im = jnp.swapaxes(w['s5_B_im'], 2, 3).reshape(b_shape)
    bbd = jnp.concatenate([_block_diag(b_re), _block_diag(b_im)], axis=-1).astype(BF16)
    c_re = jnp.swapaxes(w['s5_C_re'], 2, 3).reshape(c_shape)
    c_im = jnp.swapaxes(w['s5_C_im'], 2, 3).reshape(c_shape)
    cbd = jnp.concatenate([_block_diag(c_re), -_block_diag(c_im)], axis=2).astype(BF16)
    lam_shape = (DEPTH, N_DIR, S5_CB, 1, S5_CBW)
    ls = jnp.broadcast_to(w['s5_log_step'][..., None], (DEPTH, N_DIR, S5_GROUPS, S5_STATE))
    vec = lambda a: a.reshape(DEPTH, 1, a.shape[-1])
    return dict(
        norm1_w=vec(w['norm1_w']), w_big=w_big, w_small=w_small, gate_bias=gate_bias, gate_alog=gate_alog,
        ml_norm_w=vec(w['ml_norm_w']), bbd=bbd, cbd=cbd,
        lam_re=w['s5_lam_re'].reshape(lam_shape), lam_im=w['s5_lam_im'].reshape(lam_shape),
        log_step=ls.reshape(lam_shape),
        s5_D=vec(w['s5_D']), s5_glu_w=w['s5_glu_w'].astype(BF16), s5_glu_b=vec(w['s5_glu_b']),
        gd_conv_w=w['gd_conv_w'], gd_norm_w=vec(jnp.tile(w['gd_norm_w'], (1, HEADS))),
        w_branch=w['w_branch'].astype(BF16), w_out=w['w_out'].astype(BF16), norm2_w=vec(w['norm2_w']),
        w_gate=w['w_gate'].astype(BF16), w_up=w['w_up'].astype(BF16), w_down=w['w_down'].astype(BF16),
    )


def _grid_pos_embed(n_tok):
    grid_w = 64
    t = np.arange(n_tok)
    quarter = D_MODEL // 4
    omega = (1.0 / (10000.0 ** (np.arange(quarter, dtype=np.float32) / quarter))).astype(np.float32)

    def enc(pos):
        ang = pos.astype(np.float32)[:, None] * omega[None, :]
        return np.concatenate([np.sin(ang), np.cos(ang)], axis=-1)

    return jnp.asarray(np.concatenate([enc(t // grid_w), enc(t % grid_w)], axis=-1).astype(np.float32))


def _trunk_layer(x, mod, p, l, batch, seq, tb, states, final_w, pe=None):
    res = _inproj(x, mod, p, l, seq, pe)
    if pe is not None:
        z, zs, x = res
    else:
        z, zs = res
    if states is None:
        ml_st = s5_st = gd_st = None
    else:
        ml_st, s5_st, gd_st = states
    ml = _mlstm(z, zs, p, l, batch, seq, tb, ml_st)
    s5 = _s5(z, p, l, batch, seq, s5_st)
    qkv = _gdn_prep(z, p, l, seq)
    gd = _gdn(qkv, zs, p, l, batch, seq, tb, gd_st)
    x = _merge(x, mod, z, ml[0], s5[0], gd[0], p, l, seq)
    x = _ffn(x, mod, p, l, seq, final_w, l == DEPTH - 1)
    new_states = None
    if states is None:
        new_states = (ml[1][..., :DH], ml[1][..., DH], ml[2][:, :, :, 0, 0],
                      jnp.transpose(s5[1], (2, 0, 1, 3)).reshape(batch, N_DIR, S5_GROUPS, S5_STATE),
                      jnp.transpose(s5[2], (2, 0, 1, 3)).reshape(batch, N_DIR, S5_GROUPS, S5_STATE),
                      gd[1])
    return x, new_states


def kernel(x_prompt, x_sample, state_mlstm_C, state_mlstm_n, state_mlstm_m, state_s5_re, state_s5_im,
           state_gdn_S, c, c_ctx, ada_w, ada_b, norm1_w, w_in, ml_i_bias, ml_f_bias, ml_norm_w,
           s5_lam_re, s5_lam_im, s5_log_step, s5_B_re, s5_B_im, s5_C_re, s5_C_im, s5_D, s5_glu_w, s5_glu_b,
           gd_conv_w, gd_A_log, gd_dt_bias, gd_norm_w, w_branch, w_out, norm2_w, w_gate, w_up, w_down,
           final_norm_w):
    w = dict(norm1_w=norm1_w, w_in=w_in, ml_i_bias=ml_i_bias, ml_f_bias=ml_f_bias, ml_norm_w=ml_norm_w,
             s5_lam_re=s5_lam_re, s5_lam_im=s5_lam_im, s5_log_step=s5_log_step, s5_B_re=s5_B_re,
             s5_B_im=s5_B_im, s5_C_re=s5_C_re, s5_C_im=s5_C_im, s5_D=s5_D, s5_glu_w=s5_glu_w,
             s5_glu_b=s5_glu_b, gd_conv_w=gd_conv_w, gd_A_log=gd_A_log, gd_dt_bias=gd_dt_bias,
             gd_norm_w=gd_norm_w, w_branch=w_branch, w_out=w_out, norm2_w=norm2_w, w_gate=w_gate,
             w_up=w_up, w_down=w_down)
    bp, sp, _ = x_prompt.shape
    bs, ss, _ = x_sample.shape
    params = _prep_params(w)
    final_w = final_norm_w.reshape(1, D_MODEL)

    cc = jnp.concatenate([c_ctx[None, :], c, jnp.zeros((8 - 1 - bs, D_MODEL), F32)], axis=0)
    mods = _modulation(cc, ada_w, ada_b).reshape(DEPTH, 8, N_MOD, D_MODEL)

    xp = x_prompt.reshape(bp * sp, D_MODEL)
    per_layer = []
    for l in range(DEPTH):
        xp, st = _trunk_layer(xp, mods[l, 0:1], params, l, bp, sp, sp, None, final_w)
        per_layer.append(st)
    y_prompt = xp.reshape(bp, sp, D_MODEL)
    new_states = [jnp.stack([st[i] for st in per_layer], axis=1) for i in range(6)]

    xs = x_sample.reshape(bs * ss, D_MODEL)
    pe = _grid_pos_embed(ss)
    for l in range(DEPTH):
        n_cols = jnp.broadcast_to(state_mlstm_n[:, l][..., None], (bs, N_DIR, HEADS, DH, DH))
        ml_st = (jnp.concatenate([state_mlstm_C[:, l], n_cols], axis=-1),
                 jnp.broadcast_to(state_mlstm_m[:, l][:, :, :, None, None], (bs, N_DIR, HEADS, 1, DH)))
        s5_st = tuple(jnp.transpose(a[:, l].reshape(bs, N_DIR, S5_CB, 1, S5_CBW), (1, 2, 0, 3, 4))
                      for a in (state_s5_re, state_s5_im))
        xs, _ = _trunk_layer(xs, mods[l, 1:1 + bs], params, l, bs, ss, 512, (ml_st, s5_st, state_gdn_S[:, l]),
                             final_w, pe if l == 0 else None)
    y_sample = xs.reshape(bs, ss, D_MODEL)
    return (y_prompt, y_sample, *new_states)
```

```python
import functools
import math

import jax
import jax.numpy as jnp
import numpy as np
from jax import lax
from jax.experimental import pallas as pl
from jax.experimental.pallas import tpu as pltpu

F32 = jnp.float32
BF16 = jnp.bfloat16

D_MODEL = 1024
DEPTH = 2
N_DIR = 2
N_MOD = 6
EPS = 1e-6
HEADS = 4
DH = 128
BRANCH_W = HEADS * DH
CHUNK = 64
S5_GROUPS = 32
S5_GROUP = 16
S5_STATE = 64
S5_CB = 4
S5_CBW = S5_GROUPS * S5_STATE // S5_CB
S5_SUBLANES = 8
CONV_K = 5
D_FF = -(-8 * D_MODEL // (3 * 256)) * 256
IN_SIZES = (512, 512, 512, 512, 8, 8, 512, 1536, 512, 8, 8, 3072)
GATE_W = 128

GATES_W = 3 * D_MODEL
CB_Q, CB_K, CB_V, CB_O, CB_SU, CB_GQKV, CB_GZ = 0, 1, 2, 3, 4, 5, 8
Z_W = 9 * BRANCH_W

VMEM_LIMIT = 56 * 1024 * 1024


def _cparams(sem):
    return pltpu.CompilerParams(dimension_semantics=sem, vmem_limit_bytes=VMEM_LIMIT)


def _dot(a, b):
    return jnp.dot(a.astype(BF16), b.astype(BF16), preferred_element_type=F32)


def _dot_nt(a, b):
    return lax.dot_general(a.astype(BF16), b.astype(BF16), (((1,), (1,)), ((), ())),
                           preferred_element_type=F32)


def _dot_tn(a, b):
    return lax.dot_general(a.astype(BF16), b.astype(BF16), (((0,), (0,)), ((), ())),
                           preferred_element_type=F32)


def _split3(x):
    hi = x.astype(BF16)
    r1 = x - hi.astype(F32)
    mid = r1.astype(BF16)
    lo = (r1 - mid.astype(F32)).astype(BF16)
    return hi, mid, lo


def _sel_dot(sel, x, nt=False):
    dims = (((1,), (1,)), ((), ())) if nt else (((1,), (0,)), ((), ()))
    hi, mid, lo = _split3(x)
    f = lambda p: lax.dot_general(sel, p, dims, preferred_element_type=F32)
    return (f(hi) + f(mid)) + f(lo)


def _dot_hp(a, b):
    ah = a.astype(BF16)
    al = (a - ah.astype(F32)).astype(BF16)
    bh = b.astype(BF16)
    bl = (b - bh.astype(F32)).astype(BF16)
    f = lambda p, q: jnp.dot(p, q, preferred_element_type=F32)
    return f(ah, bh) + (f(ah, bl) + f(al, bh))


def _sigmoid(x):
    return 1.0 / (1.0 + jnp.exp(-x))


def _silu(x):
    return x * _sigmoid(x)


def _softplus(x):
    return jnp.maximum(x, 0.0) + jnp.log(1.0 + jnp.exp(-jnp.abs(x)))


def _gelu_tanh(x):
    c = math.sqrt(2.0 / math.pi)
    return 0.5 * x * (1.0 + jnp.tanh(c * (x + 0.044715 * (x * x * x))))


def _head_rms(x, w_row):
    outs = []
    for h in range(HEADS):
        xh = x[:, h * DH:(h + 1) * DH]
        outs.append(xh * lax.rsqrt(jnp.mean(xh * xh, axis=-1, keepdims=True) + EPS))
    return jnp.concatenate(outs, axis=-1) * w_row


def _dir_chunk(c, d, n):
    return c + d * (n - 1 - 2 * c)


def _mod_kernel(c_ref, w_ref, b_ref, o_ref):
    o_ref[0] = _dot(_silu(c_ref[...]), w_ref[0]) + b_ref[0]


def _modulation(cc, ada_w, ada_b):
    tn = 1536
    nmod = N_MOD * D_MODEL
    return pl.pallas_call(
        _mod_kernel,
        grid=(DEPTH, nmod // tn),
        in_specs=[pl.BlockSpec((8, D_MODEL), lambda l, j: (0, 0)),
                  pl.BlockSpec((1, D_MODEL, tn), lambda l, j: (l, 0, j)),
                  pl.BlockSpec((1, 1, tn), lambda l, j: (l, 0, j))],
        out_specs=pl.BlockSpec((1, 8, tn), lambda l, j: (l, 0, j)),
        out_shape=jax.ShapeDtypeStruct((DEPTH, 8, nmod), F32),
        name="adaln_mod",
        compiler_params=_cparams(("arbitrary", "arbitrary")),
    )(cc, ada_w, ada_b.reshape(DEPTH, 1, nmod))


def _inproj_kernel(*refs, has_pe, gate_tiles):
    if has_pe:
        x_ref, pe_ref, mod_ref, nw_ref, w_ref, ws_ref, gates_ref, z_ref, zs_ref, xs_ref, hn_scr = refs
    else:
        x_ref, mod_ref, nw_ref, w_ref, ws_ref, gates_ref, z_ref, zs_ref, hn_scr = refs
    j = pl.program_id(1)

    @pl.when(j == 0)
    def _():
        x = x_ref[...]
        if has_pe:
            x = x + pe_ref[...]
            xs_ref[...] = x
        y = x * lax.rsqrt(jnp.mean(x * x, axis=-1, keepdims=True) + EPS)
        h = (y * nw_ref[0]) * (1.0 + mod_ref[0, 1:2, :]) + mod_ref[0, 0:1, :]
        hb = h.astype(BF16)
        hn_scr[...] = hb
        zs_ref[...] = jnp.dot(hb, ws_ref[0], preferred_element_type=F32)

    acc = jnp.dot(hn_scr[...], w_ref[0], preferred_element_type=F32)

    @pl.when(j < gate_tiles)
    def _():
        gates_ref[...] = acc.astype(BF16)

    @pl.when(j >= gate_tiles)
    def _():
        z_ref[...] = acc


def _inproj(x, mod, p, l, seq, pe=None):
    m = x.shape[0]
    tm, tn = 1024, 768
    gate_tiles = GATES_W // tn
    rows_per_mod = seq if mod.shape[0] > 1 else m
    has_pe = pe is not None
    row = lambda i, j: (i, 0)
    in_specs = [pl.BlockSpec((tm, D_MODEL), row)]
    args = [x]
    if has_pe:
        in_specs.append(pl.BlockSpec((tm, D_MODEL), lambda i, j: (i % (seq // tm), 0)))
        args.append(pe)
    in_specs += [pl.BlockSpec((1, N_MOD, D_MODEL), lambda i, j: (i * tm // rows_per_mod, 0, 0)),
                 pl.BlockSpec((1, 1, D_MODEL), lambda i, j: (l, 0, 0)),
                 pl.BlockSpec((1, D_MODEL, tn), lambda i, j: (l, 0, j)),
                 pl.BlockSpec((1, D_MODEL, N_DIR * GATE_W), lambda i, j: (l, 0, 0))]
    args += [mod, p['norm1_w'], p['w_big'], p['w_small']]
    out_specs = [pl.BlockSpec((tm, tn), lambda i, j: (i, jnp.minimum(j, gate_tiles - 1))),
                 pl.BlockSpec((tm, tn), lambda i, j: (i, jnp.maximum(j - gate_tiles, 0))),
                 pl.BlockSpec((tm, N_DIR * GATE_W), row)]
    out_shape = [jax.ShapeDtypeStruct((m, GATES_W), BF16),
                 jax.ShapeDtypeStruct((m, Z_W), F32),
                 jax.ShapeDtypeStruct((m, N_DIR * GATE_W), F32)]
    if has_pe:
        out_specs.append(pl.BlockSpec((tm, D_MODEL), row))
        out_shape.append(jax.ShapeDtypeStruct((m, D_MODEL), F32))
    return pl.pallas_call(
        functools.partial(_inproj_kernel, has_pe=has_pe, gate_tiles=gate_tiles),
        grid=(m // tm, (GATES_W + Z_W) // tn),
        in_specs=in_specs, out_specs=out_specs, out_shape=out_shape,
        scratch_shapes=[pltpu.VMEM((tm, D_MODEL), BF16)],
        name="inproj",
        compiler_params=_cparams(("arbitrary", "arbitrary")),
    )(*args)


def _mixer_rowblk(nt):
    return lambda b, d, j: b * nt + _dir_chunk(j, d, nt)


def _gate_selector():
    r = lax.broadcasted_iota(jnp.int32, (16, GATE_W), 0)
    c = lax.broadcasted_iota(jnp.int32, (16, GATE_W), 1)
    return (r == c).astype(BF16)


def _block_gate_sums(vals, other, n_other, cls_scr, cum_scr):
    tri = (cls_scr[...] > 0).astype(BF16)
    for g in range(vals.shape[0] // HC):
        grp = slice(g * HC, (g + 1) * HC)
        cum_scr[grp, :] = _sel_dot(tri, vals[grp, :])
    lane = lax.broadcasted_iota(jnp.int32, vals.shape, 1)
    mixed = jnp.where(lane < n_other, other, cum_scr[...])
    return _sel_dot(_gate_selector(), mixed, nt=True), mixed


def _mlstm_kernel(*refs, nch, zero_init, emit_state):
    q_ref, k_ref, v_ref, g_ref, bias_ref = refs[:5]
    pos = 5
    if not zero_init:
        c0_ref, m0_ref = refs[pos:pos + 2]
        pos += 2
    h_ref = refs[pos]
    pos += 1
    if emit_state:
        co_ref, mo_ref = refs[pos:pos + 2]
        pos += 2
    c_scr, m_scr, cls_all, cum_scr, row_scr, dense_scr, av_scr, ml_scr = refs[pos:pos + 8]

    d = pl.program_id(1)
    j = pl.program_id(2)

    @pl.when(j == 0)
    def _():
        if zero_init:
            c_scr[...] = jnp.zeros_like(c_scr)
            m_scr[...] = jnp.zeros_like(m_scr)
        else:
            c_scr[...] = c0_ref[0, 0]
            m_scr[...] = m0_ref[0, 0]

    _fill_pair_classes(cls_all)
    cls_scr = cls_all.at[d]
    rowid = lax.broadcasted_iota(jnp.int32, (CHUNK, DH), 0)
    last = (CHUNK - 1) * (1 - d)
    bias = bias_ref[0, 0]
    scale = DH ** -0.5
    hrows = lambda h: slice(h * CHUNK, (h + 1) * CHUNK)
    per_head = lambda vals: jnp.concatenate([jnp.broadcast_to(a, (CHUNK, a.shape[1])) for a in vals], axis=0)
    twice = lambda a: jnp.concatenate([a, a], axis=1)
    ones = jnp.ones((HC, DH), F32)

    pre_all = g_ref[...] + bias
    xt, mixed = _block_gate_sums(-_softplus(-pre_all), pre_all, HEADS, cls_scr, cum_scr)
    for c in range(nch):
        cs = slice(c * CHUNK, (c + 1) * CHUNK)
        row_scr[c] = jnp.concatenate([xt[h:h + 1, cs] - xt[HEADS + h:HEADS + h + 1, cs] for h in range(HEADS)],
                                     axis=1)
    for h in range(HEADS):
        dense_scr[h, :, 0:DH] = jnp.broadcast_to(mixed[:, HEADS + h:HEADS + h + 1], (nch * CHUNK, DH))
        dense_scr[h, :, DH:2 * DH] = jnp.broadcast_to(mixed[:, h:h + 1], (nch * CHUNK, DH))

    stacked = lambda rows, lo: jnp.concatenate([dense_scr[h, rows, lo:lo + DH] for h in range(HEADS)], axis=0)
    npar = MLSTM_LOCKSTEP

    def local(cp, carry):
        cs = [cp * npar + i for i in range(npar)]
        rows = [pl.ds(pl.multiple_of(c * CHUNK, CHUNK), CHUNK) for c in cs]
        cls = cls_scr[...]
        log_d = [jnp.where(cls > 0, twice(stacked(r, 0)) + row_scr[c], -jnp.inf) for r, c in zip(rows, cs)]
        ml = [jnp.max(a, axis=-1, keepdims=True) for a in log_d]
        q = [_stack_heads(q_ref[r, :]) for r in rows]
        k = [_stack_heads(k_ref[r, :]) * scale for r in rows]
        v = [jnp.concatenate([_stack_heads(v_ref[r, :]), ones], axis=1) for r in rows]
        s = [_dot_nt(a, b) * jnp.exp(ld - m) for a, b, ld, m in zip(q, k, log_d, ml)]
        av = [_dot(a, b) for a, b in zip(s, v)]
        for i, c in enumerate(cs):
            av_scr[c] = av[i]
            ml_scr[c] = jnp.broadcast_to(ml[i], (HC, DH))
        return carry

    lax.fori_loop(0, nch // npar, local, 0)

    def advance(c, carry):
        ci = _dir_chunk(c, d, nch)
        r0 = pl.multiple_of(ci * CHUNK, CHUNK)
        rows = pl.ds(r0, CHUNK)
        b_tok = stacked(rows, 0)
        i_tok = stacked(rows, DH)
        at_last = lambda a: [jnp.sum(jnp.where(rowid == last, a[hrows(h), :], 0.0), axis=0, keepdims=True)
                             for h in range(HEADS)]
        b_last = at_last(b_tok)
        m_prev = [m_scr[h] for h in range(HEADS)]
        ml = ml_scr[ci]
        log_0 = b_tok + per_head(m_prev)
        m_t = jnp.maximum(log_0, ml)
        w_0 = jnp.exp(log_0 - m_t)
        f = jnp.exp(ml - m_t)
        q = _stack_heads(q_ref[rows, :])
        k = _stack_heads(k_ref[rows, :]) * scale
        v = jnp.concatenate([_stack_heads(v_ref[rows, :]), ones], axis=1)
        cst = [c_scr[h] for h in range(HEADS)]
        qc = jnp.concatenate([_dot(q[hrows(h), :], cst[h]) for h in range(HEADS)], axis=0)
        num = f * av_scr[ci, :, 0:DH] + w_0 * qc[:, 0:DH]
        den = f * av_scr[ci, :, DH:2 * DH] + w_0 * qc[:, DH:2 * DH]
        hv = num / jnp.maximum(jnp.abs(den), jnp.exp(-m_t))
        m_new = at_last(m_t)
        kw = k * jnp.exp(per_head(b_last) - b_tok + i_tok - per_head(m_new))
        for h in range(HEADS):
            h_ref[0, rows, h * DH:(h + 1) * DH] = hv[hrows(h), :].astype(BF16)
            c_0 = jnp.exp(b_last[h] + m_prev[h] - m_new[h])
            c_scr[h] = twice(c_0) * cst[h] + _dot_tn(kw[hrows(h), :], v[hrows(h), :])
            m_scr[h] = m_new[h]
        return carry

    lax.fori_loop(0, nch, advance, 0)

    if emit_state:
        @pl.when(j == pl.num_programs(2) - 1)
        def _():
            co_ref[0, 0] = c_scr[...]
            mo_ref[0, 0] = m_scr[...]


def _mlstm(z, zs, p, l, batch, seq, tb, states):
    m = batch * seq
    nt = seq // tb
    nch = tb // CHUNK
    zero_init = states is None
    emit_state = states is None
    rb = _mixer_rowblk(nt)
    zspec = lambda cb: pl.BlockSpec((tb, BRANCH_W), lambda b, d, j: (rb(b, d, j), cb))
    st5 = lambda shape: pl.BlockSpec((1, 1) + shape, lambda b, d, j: (b, d) + (0,) * len(shape))
    in_specs = [zspec(CB_Q), zspec(CB_K), zspec(CB_V),
                pl.BlockSpec((tb, GATE_W), lambda b, d, j: (rb(b, d, j), d)),
                pl.BlockSpec((1, 1, 1, GATE_W), lambda b, d, j: (l, d, 0, 0))]
    args = [z, z, z, zs, p['gate_bias']]
    if not zero_init:
        in_specs += [st5((HEADS, DH, 2 * DH)), st5((HEADS, 1, DH))]
        args += list(states)
    out_specs = [pl.BlockSpec((1, tb, BRANCH_W), lambda b, d, j: (d, rb(b, d, j), 0))]
    out_shape = [jax.ShapeDtypeStruct((N_DIR, m, BRANCH_W), BF16)]
    if emit_state:
        out_specs += [st5((HEADS, DH, 2 * DH)), st5((HEADS, 1, DH))]
        out_shape += [jax.ShapeDtypeStruct((batch, N_DIR, HEADS, DH, 2 * DH), F32),
                      jax.ShapeDtypeStruct((batch, N_DIR, HEADS, 1, DH), F32)]
    return pl.pallas_call(
        functools.partial(_mlstm_kernel, nch=nch, zero_init=zero_init, emit_state=emit_state),
        grid=(batch, N_DIR, nt),
        in_specs=in_specs, out_specs=out_specs, out_shape=out_shape,
        scratch_shapes=[pltpu.VMEM((HEADS, DH, 2 * DH), F32), pltpu.VMEM((HEADS, 1, DH), F32),
                        pltpu.VMEM((N_DIR, HC, HC), jnp.int32),
                        pltpu.VMEM((tb, GATE_W), F32), pltpu.VMEM((nch, 1, HC), F32),
                        pltpu.VMEM((HEADS, tb, 2 * DH), F32),
                        pltpu.VMEM((nch, HC, 2 * DH), F32), pltpu.VMEM((nch, HC, DH), F32)],
        name="mlstm",
        compiler_params=_cparams(("arbitrary", "arbitrary", "arbitrary")),
    )(*args)


def _gdn_prep_kernel(x_ref, prev_ref, next_ref, w_ref, o_ref, xe_scr, *, tb, seq):
    i = pl.program_id(0)
    p = pl.program_id(1)
    pad = 8
    at_start = (i * tb) % seq == 0
    at_end = ((i + 1) * tb) % seq == 0
    xe_scr[0:pad, :] = jnp.where(at_start, 0.0, prev_ref[...])
    xe_scr[pad:pad + tb, :] = x_ref[...]
    xe_scr[pad + tb:pad + tb + pad, :] = jnp.where(at_end, 0.0, next_ref[...])
    acc = None
    for t in range(CONV_K):
        term = xe_scr[pl.ds(pad + t - CONV_K // 2, tb), :] * w_ref[0, t:t + 1, :]
        acc = term if acc is None else acc + term
    y = _silu(acc)
    outs = []
    for h in range(HEADS):
        yh = y[:, h * DH:(h + 1) * DH]
        outs.append(yh * lax.rsqrt(jnp.sum(yh * yh, axis=-1, keepdims=True) + EPS))
    yn = jnp.concatenate(outs, axis=-1)
    qscale = jnp.where(p == 0, DH ** -0.5, 1.0)
    o_ref[...] = jnp.where(p == 2, y, yn * qscale)


def _gdn_prep(z, p, l, seq):
    m = z.shape[0]
    tb = 256
    nb8 = m // 8
    return pl.pallas_call(
        functools.partial(_gdn_prep_kernel, tb=tb, seq=seq),
        grid=(m // tb, 3),
        in_specs=[pl.BlockSpec((tb, BRANCH_W), lambda i, part: (i, CB_GQKV + part)),
                  pl.BlockSpec((8, BRANCH_W), lambda i, part: (jnp.maximum(i * (tb // 8) - 1, 0), CB_GQKV + part)),
                  pl.BlockSpec((8, BRANCH_W), lambda i, part: (jnp.minimum((i + 1) * (tb // 8), nb8 - 1), CB_GQKV + part)),
                  pl.BlockSpec((1, CONV_K, BRANCH_W), lambda i, part: (l, 0, part))],
        out_specs=pl.BlockSpec((tb, BRANCH_W), lambda i, part: (i, part)),
        out_shape=jax.ShapeDtypeStruct((m, 3 * BRANCH_W), F32),
        scratch_shapes=[pltpu.VMEM((tb + 16, BRANCH_W), F32)],
        name="gdn_prep",
        compiler_params=_cparams(("arbitrary", "arbitrary")),
    )(z, z, z, p['gd_conv_w'])


HC = HEADS * CHUNK


def _stack_heads(x):
    return jnp.concatenate([x[:, h * DH:(h + 1) * DH] for h in range(HEADS)], axis=0)


def _stack_gate(x, lane0):
    return jnp.concatenate([x[:, lane0 + h:lane0 + h + 1] for h in range(HEADS)], axis=0)


GDN_LOCKSTEP = 4
MLSTM_LOCKSTEP = 4
BASE_LG = 3
CHUNK_LG = int(math.log2(CHUNK))


def _pair_classes(d):
    r = lax.broadcasted_iota(jnp.int32, (HC, HC), 0)
    c = lax.broadcasted_iota(jnp.int32, (HC, HC), 1)
    cls = jnp.full((HC, HC), CHUNK_LG, jnp.int32)
    for s in range(CHUNK_LG - 1, BASE_LG - 1, -1):
        cls = jnp.where((r >> s) == (c >> s), s, cls)
    diff = (r - c) * (1 - 2 * d)
    cls = jnp.where(diff == 0, 1, cls)
    return jnp.where(jnp.logical_and((r >> CHUNK_LG) == (c >> CHUNK_LG), diff >= 0), cls, 0)


def _fill_pair_classes(cls_all):
    first = functools.reduce(jnp.logical_and, [pl.program_id(i) == 0 for i in range(3)])

    @pl.when(first)
    def _():
        for d in range(N_DIR):
            cls_all[d] = _pair_classes(d)


def _unit_tri_solve(xs, cls, rhss):
    n = xs[0].shape[0]
    nr = rhss[0].shape[1]
    x0 = [jnp.where(cls == BASE_LG, x, 0.0) for x in xs]
    x2 = [_dot(a, a) for a in x0]
    y = [_dot(b, jnp.concatenate([b, a], axis=1)) for a, b in zip(x0, x2)]
    x4 = [t[:, :n] for t in y]
    q2 = [a + b + t[:, n:] for a, b, t in zip(x0, x2, y)]
    t8m = [a + b + _dot(b, a) for a, b in zip(q2, x4)]
    levels = range(BASE_LG + 1, CHUNK_LG + 1)
    cur = [jnp.concatenate([r] + [jnp.where(cls == s, x, 0.0) for s in levels], axis=1) for x, r in zip(xs, rhss)]
    cur = [a + _dot(t, a) for a, t in zip(cur, t8m)]
    for _ in levels:
        last = cur[0].shape[1] == nr + n
        rest = [a[:, :nr] if last else jnp.concatenate([a[:, :nr], a[:, nr + n:]], axis=1) for a in cur]
        cur = [r + _dot(a[:, nr:nr + n], r) for a, r in zip(cur, rest)]
    return cur


def _gdn_kernel(*refs, nch, zero_init, emit_state):
    q_ref, k_ref, v_ref, g_ref, bias_ref, alog_ref = refs[:6]
    pos = 6
    if not zero_init:
        s0_ref = refs[pos]
        pos += 1
    o_ref = refs[pos]
    pos += 1
    if emit_state:
        so_ref = refs[pos]
        pos += 1
    s_scr, cls_all, cum_scr, row_scr, uw_scr, at_scr, qg_scr, kd_scr, gt_scr = refs[pos:pos + 9]

    d = pl.program_id(1)
    j = pl.program_id(2)

    @pl.when(j == 0)
    def _():
        if zero_init:
            s_scr[...] = jnp.zeros_like(s_scr)
        else:
            s_scr[...] = s0_ref[0, 0]

    _fill_pair_classes(cls_all)
    cls_scr = cls_all.at[d]
    last = (CHUNK - 1) * (1 - d)
    bias = bias_ref[0, 0]
    neg_a = -jnp.exp(alog_ref[0, 0])
    g0 = 2 * HEADS

    g_all = neg_a * _softplus(g_ref[...] + bias)
    xt, _ = _block_gate_sums(g_all, g_all, 0, cls_scr, cum_scr)
    for c in range(nch):
        cs = slice(c * CHUNK, (c + 1) * CHUNK)
        row_scr[c] = jnp.concatenate([xt[g0 + h:g0 + h + 1, cs] for h in range(HEADS)], axis=1)

    npar = GDN_LOCKSTEP

    def prepare(cp, carry):
        cs = [cp * npar + i for i in range(npar)]
        r0 = [pl.multiple_of(c * CHUNK, CHUNK) for c in cs]
        rows = [pl.ds(r, CHUNK) for r in r0]
        gtot = [cum_scr[pl.ds(r + last, 1), :] for r in r0]
        g_col = [_stack_gate(cum_scr[r, :], g0) for r in rows]
        beta = [_stack_gate(_sigmoid(g_ref[r, :] + bias), g0 + HEADS) for r in rows]
        g_last = [jnp.concatenate([jnp.broadcast_to(t[:, g0 + h:g0 + h + 1], (CHUNK, 1)) for h in range(HEADS)],
                                  axis=0) for t in gtot]
        cls = cls_scr[...]
        decay = [jnp.exp(jnp.where(cls > 0, gc - row_scr[c], -jnp.inf)) for gc, c in zip(g_col, cs)]
        eg = [jnp.exp(gc) for gc in g_col]
        q = [_stack_heads(q_ref[r, :]) for r in rows]
        k = [_stack_heads(k_ref[r, :]) for r in rows]
        v = [_stack_heads(v_ref[r, :]) for r in rows]
        kb = [a * b for a, b in zip(k, beta)]
        x = [-(_dot_nt(a, b) * dc) for a, b, dc in zip(kb, k, decay)]
        rhs = [jnp.concatenate([a * b, kbi * e], axis=-1) for a, b, kbi, e in zip(v, beta, kb, eg)]
        uw = _unit_tri_solve(x, cls, rhs)
        attn = [_dot_nt(a, b) * dc for a, b, dc in zip(q, k, decay)]
        for i, c in enumerate(cs):
            uw_scr[c] = uw[i]
            at_scr[c] = attn[i]
            qg_scr[c] = q[i] * eg[i]
            kd_scr[c] = k[i] * jnp.exp(g_last[i] - g_col[i])
            gt_scr[c] = gtot[i]
        return carry

    lax.fori_loop(0, nch // npar, prepare, 0)

    def advance(c, carry):
        ci = _dir_chunk(c, d, nch)
        rows = pl.ds(pl.multiple_of(ci * CHUNK, CHUNK), CHUNK)
        hrows = lambda h: slice(h * CHUNK, (h + 1) * CHUNK)
        st = [s_scr[h] for h in range(HEADS)]
        v_new = [uw_scr[ci, hrows(h), 0:DH] - _dot(uw_scr[ci, hrows(h), DH:2 * DH], st[h]) for h in range(HEADS)]
        qs = jnp.concatenate([_dot(qg_scr[ci, hrows(h), :], st[h]) for h in range(HEADS)], axis=0)
        o = qs + _dot(at_scr[ci], jnp.concatenate(v_new, axis=0))
        gtot = gt_scr[ci]
        for h in range(HEADS):
            o_ref[0, rows, h * DH:(h + 1) * DH] = o[hrows(h), :].astype(BF16)
            s_scr[h] = st[h] * jnp.exp(gtot[:, g0 + h:g0 + h + 1]) + _dot_tn(kd_scr[ci, hrows(h), :], v_new[h])
        return carry

    lax.fori_loop(0, nch, advance, 0)

    if emit_state:
        @pl.when(j == pl.num_programs(2) - 1)
        def _():
            so_ref[0, 0] = s_scr[...]


def _gdn(qkv, zs, p, l, batch, seq, tb, s0):
    m = batch * seq
    nt = seq // tb
    nch = tb // CHUNK
    zero_init = s0 is None
    emit_state = s0 is None
    rb = _mixer_rowblk(nt)
    spec = lambda cb: pl.BlockSpec((tb, BRANCH_W), lambda b, d, j: (rb(b, d, j), cb))
    st = pl.BlockSpec((1, 1, HEADS, DH, DH), lambda b, d, j: (b, d, 0, 0, 0))
    dirrow = pl.BlockSpec((1, 1, 1, GATE_W), lambda b, d, j: (l, d, 0, 0))
    in_specs = [spec(0), spec(1), spec(2),
                pl.BlockSpec((tb, GATE_W), lambda b, d, j: (rb(b, d, j), d)), dirrow, dirrow]
    args = [qkv, qkv, qkv, zs, p['gate_bias'], p['gate_alog']]
    if not zero_init:
        in_specs.append(st)
        args.append(s0)
    out_specs = [pl.BlockSpec((1, tb, BRANCH_W), lambda b, d, j: (d, rb(b, d, j), 0))]
    out_shape = [jax.ShapeDtypeStruct((N_DIR, m, BRANCH_W), BF16)]
    if emit_state:
        out_specs.append(st)
        out_shape.append(jax.ShapeDtypeStruct((batch, N_DIR, HEADS, DH, DH), F32))
    return pl.pallas_call(
        functools.partial(_gdn_kernel, nch=nch, zero_init=zero_init, emit_state=emit_state),
        grid=(batch, N_DIR, nt),
        in_specs=in_specs, out_specs=out_specs, out_shape=out_shape,
        scratch_shapes=[pltpu.VMEM((HEADS, DH, DH), F32), pltpu.VMEM((N_DIR, HC, HC), jnp.int32),
                        pltpu.VMEM((tb, GATE_W), F32), pltpu.VMEM((nch, 1, HC), F32),
                        pltpu.VMEM((nch, HC, 2 * DH), F32), pltpu.VMEM((nch, HC, HC), F32),
                        pltpu.VMEM((nch, HC, DH), F32), pltpu.VMEM((nch, HC, DH), F32),
                        pltpu.VMEM((nch, 1, GATE_W), F32)],
        name="gdn",
        compiler_params=_cparams(("arbitrary", "arbitrary", "arbitrary")),
    )(*args)


def _s5_discretise(lre, lim, ls):
    dt = jnp.exp(ls)
    mag = jnp.exp(lre * dt)
    ar = mag * jnp.cos(lim * dt)
    ai = mag * jnp.sin(lim * dt)
    den = lre * lre + lim * lim
    nr = ar - 1.0
    return ar, ai, (nr * lre + ai * lim) / den, (ai * lre - nr * lim) / den


def _s5_kernel(*refs, rb, seg, chain):
    u_ref, b_ref, c_ref, lre_ref, lim_ref, ls_ref = refs[:6]
    pos = 6
    if chain:
        h0r_ref, h0i_ref = refs[pos:pos + 2]
        pos += 2
    y_ref = refs[pos]
    pos += 1
    if not chain:
        sr_ref, si_ref = refs[pos:pos + 2]
        pos += 2
    up, yp, xr, xi, er, ei, pr, pi_ = refs[pos:pos + 8]

    w = S5_CBW
    nsub = S5_SUBLANES
    rc = 256
    ngrp = seg // nsub
    par = [_s5_discretise(lre_ref[0, d, 0], lim_ref[0, d, 0], ls_ref[0, d, 0]) for d in range(N_DIR)]

    def interleave(g, carry):
        for s in range(nsub):
            src = pl.ds(pl.multiple_of(s * seg + g * nsub, nsub), nsub)
            up[pl.ds(g * nsub * nsub + s, nsub, stride=nsub), :] = u_ref[src, :]
        return carry

    lax.fori_loop(0, ngrp, interleave, 0)

    def fill(c, carry):
        rows = pl.ds(pl.multiple_of(c * rc, rc), rc)
        bu = _dot(up[rows, :], b_ref[0, 0])
        bre = bu[:, :w]
        bim = bu[:, w:]
        for d in range(N_DIR):
            _, _, zr, zi = par[d]
            xr[d, rows, :] = zr * bre - zi * bim
            xi[d, rows, :] = zr * bim + zi * bre
        return carry

    lax.fori_loop(0, rb // rc, fill, 0, unroll=2)

    ab = [(jnp.broadcast_to(p[0], (nsub, w)), jnp.broadcast_to(p[1], (nsub, w))) for p in par]

    def step_rows(tt, d):
        t = tt if d == 0 else seg - 1 - tt
        return pl.ds(pl.multiple_of(t * nsub, nsub), nsub)

    def scan_step(tt, carry):
        out = []
        for d in range(N_DIR):
            hr, hi = carry[d]
            arb, aib = ab[d]
            idx = step_rows(tt, d)
            nhr = arb * hr - aib * hi + xr[d, idx, :]
            nhi = arb * hi + aib * hr + xi[d, idx, :]
            xr[d, idx, :] = nhr
            xi[d, idx, :] = nhi
            out.append((nhr, nhi))
        return tuple(out)

    zero = jnp.zeros((nsub, w), F32)
    ends = lax.fori_loop(0, seg, scan_step, ((zero, zero), (zero, zero)), unroll=2)

    if chain:
        for d in range(N_DIR):
            ar, ai = par[d][0], par[d][1]
            er[d] = ends[d][0]
            ei[d] = ends[d][1]
            sr, si = ar, ai
            for _ in range(int(math.log2(seg))):
                sr, si = sr * sr - si * si, 2.0 * sr * si
            fr = h0r_ref[d, 0, 0]
            fi = h0i_ref[d, 0, 0]
            for k in range(nsub):
                row = k if d == 0 else nsub - 1 - k
                pr[d, row:row + 1, :] = fr
                pi_[d, row:row + 1, :] = fi
                fr, fi = (er[d, row:row + 1, :] + (sr * fr - si * fi),
                          ei[d, row:row + 1, :] + (sr * fi + si * fr))

        def fix_step(tt, carry):
            out = []
            for d in range(N_DIR):
                cr, ci = carry[d]
                arb, aib = ab[d]
                idx = step_rows(tt, d)
                ncr = arb * cr - aib * ci
                nci = arb * ci + aib * cr
                xr[d, idx, :] = xr[d, idx, :] + ncr
                xi[d, idx, :] = xi[d, idx, :] + nci
                out.append((ncr, nci))
            return tuple(out)

        lax.fori_loop(0, seg, fix_step, tuple((pr[d], pi_[d]) for d in range(N_DIR)), unroll=2)
    else:
        for d in range(N_DIR):
            sr_ref[d, 0] = ends[d][0]
            si_ref[d, 0] = ends[d][1]

    def proj(c, carry):
        rows = pl.ds(pl.multiple_of(c * rc, rc), rc)
        yp[rows, :] = (_dot(xr[0, rows, :] + xr[1, rows, :], c_ref[0, 0, :w, :])
                       + _dot(xi[0, rows, :] + xi[1, rows, :], c_ref[0, 0, w:, :]))
        return carry

    lax.fori_loop(0, rb // rc, proj, 0, unroll=4)

    def deinterleave(g, carry):
        for s in range(nsub):
            dst = pl.ds(pl.multiple_of(s * seg + g * nsub, nsub), nsub)
            y_ref[dst, :] = yp[pl.ds(g * nsub * nsub + s, nsub, stride=nsub), :]
        return carry

    lax.fori_loop(0, ngrp, deinterleave, 0)


def _s5(z, p, l, batch, seq, h0):
    m = batch * seq
    nsub = S5_SUBLANES
    chain = h0 is not None
    if chain:
        seg = seq // nsub
        ng = batch
        st_spec = pl.BlockSpec((N_DIR, 1, 1, 1, S5_CBW), lambda g, cb: (0, cb, g, 0, 0))
    else:
        seg = seq
        ng = batch // nsub
        st_spec = pl.BlockSpec((N_DIR, 1, nsub, S5_CBW), lambda g, cb: (0, cb, g, 0))
    rb = nsub * seg
    lam_spec = pl.BlockSpec((1, N_DIR, 1, 1, S5_CBW), lambda g, cb: (l, 0, cb, 0, 0))
    in_specs = [pl.BlockSpec((rb, 128), lambda g, cb: (g, CB_SU * 4 + cb)),
                pl.BlockSpec((1, 1, 128, 2 * S5_CBW), lambda g, cb: (l, cb, 0, 0)),
                pl.BlockSpec((1, 1, 2 * S5_CBW, 128), lambda g, cb: (l, cb, 0, 0)),
                lam_spec, lam_spec, lam_spec]
    args = [z, p['bbd'], p['cbd'], p['lam_re'], p['lam_im'], p['log_step']]
    if chain:
        in_specs += [st_spec, st_spec]
        args += list(h0)
    out_specs = [pl.BlockSpec((rb, 128), lambda g, cb: (g, cb))]
    out_shape = [jax.ShapeDtypeStruct((m, BRANCH_W), F32)]
    if not chain:
        out_specs += [st_spec, st_spec]
        out_shape += [jax.ShapeDtypeStruct((N_DIR, S5_CB, batch, S5_CBW), F32)] * 2
    vdir = lambda n: pltpu.VMEM((N_DIR, n, S5_CBW), F32)
    lanes = pltpu.VMEM((rb, 128), F32)
    return pl.pallas_call(
        functools.partial(_s5_kernel, rb=rb, seg=seg, chain=chain),
        grid=(ng, S5_CB),
        in_specs=in_specs, out_specs=out_specs, out_shape=out_shape,
        scratch_shapes=[lanes, lanes, vdir(rb), vdir(rb), vdir(nsub), vdir(nsub), vdir(nsub), vdir(nsub)],
        name="s5",
        compiler_params=_cparams(("arbitrary", "arbitrary")),
    )(*args)


def _merge_kernel(x_ref, mod_ref, gates_ref, o_ref, u_ref, gz_ref, hm_ref, ys_ref, go_ref,
                  mlw_ref, s5d_ref, gluw_ref, glub_ref, gdw_ref, wbr_ref, wout_ref, out_ref):
    both = lambda ref: ref[0].astype(F32) + ref[1].astype(F32)
    ya = _sigmoid(o_ref[...]) * _head_rms(both(hm_ref), mlw_ref[0])
    y5 = _gelu_tanh(ys_ref[...] + s5d_ref[0] * u_ref[...])
    yb = y5 * _sigmoid(_dot(y5, gluw_ref[0]) + glub_ref[0])
    yc = _head_rms(both(go_ref), gdw_ref[0]) * _silu(gz_ref[...])
    merged = None
    for n, y in enumerate((ya, yb, yc)):
        gate = _sigmoid(gates_ref[:, n * D_MODEL:(n + 1) * D_MODEL].astype(F32))
        term = gate * _dot(y, wbr_ref[0, n])
        merged = term if merged is None else merged + term
    out_ref[...] = x_ref[...] + mod_ref[0, 2:3, :] * _dot(merged, wout_ref[0])


def _merge(x, mod, gates, z, hm, ys, go, p, l, seq):
    m = x.shape[0]
    tm = 256
    rows_per_mod = seq if mod.shape[0] > 1 else m
    row = lambda i: (i, 0)
    zspec = lambda cb: pl.BlockSpec((tm, BRANCH_W), lambda i: (i, cb))
    dspec = pl.BlockSpec((N_DIR, tm, BRANCH_W), lambda i: (0, i, 0))
    full = lambda a: pl.BlockSpec((1,) + a.shape[1:], lambda i: (l,) + (0,) * (a.ndim - 1))
    consts = [p['ml_norm_w'], p['s5_D'], p['s5_glu_w'], p['s5_glu_b'], p['gd_norm_w'], p['w_branch'], p['w_out']]
    return pl.pallas_call(
        _merge_kernel,
        grid=(m // tm,),
        in_specs=[pl.BlockSpec((tm, D_MODEL), row),
                  pl.BlockSpec((1, N_MOD, D_MODEL), lambda i: (i * tm // rows_per_mod, 0, 0)),
                  pl.BlockSpec((tm, GATES_W), row),
                  zspec(CB_O), zspec(CB_SU), zspec(CB_GZ), dspec, pl.BlockSpec((tm, BRANCH_W), row), dspec]
                 + [full(a) for a in consts],
        out_specs=pl.BlockSpec((tm, D_MODEL), row),
        out_shape=jax.ShapeDtypeStruct((m, D_MODEL), F32),
        name="merge",
        compiler_params=_cparams(("arbitrary",)),
    )(x, mod, gates, z, z, z, hm, ys, go, *consts)


def _ffn_kernel(x_ref, mod_ref, nw_ref, wg_ref, wu_ref, wd_ref, fw_ref, out_ref, h_scr, acc_scr, *, final):
    jf = pl.program_id(1)

    @pl.when(jf == 0)
    def _():
        x = x_ref[...]
        y = x * lax.rsqrt(jnp.mean(x * x, axis=-1, keepdims=True) + EPS)
        h_scr[...] = ((y * nw_ref[0]) * (1.0 + mod_ref[0, 4:5, :]) + mod_ref[0, 3:4, :]).astype(BF16)
        acc_scr[...] = jnp.zeros_like(acc_scr)

    hb = h_scr[...]
    a = _silu(jnp.dot(hb, wg_ref[0], preferred_element_type=F32))
    b = jnp.dot(hb, wu_ref[0], preferred_element_type=F32)
    acc_scr[...] += _dot(a * b, wd_ref[0])

    @pl.when(jf == pl.num_programs(1) - 1)
    def _():
        x = x_ref[...] + mod_ref[0, 5:6, :] * acc_scr[...]
        if final:
            x = x * lax.rsqrt(jnp.mean(x * x, axis=-1, keepdims=True) + EPS) * fw_ref[...]
        out_ref[...] = x


def _ffn(x, mod, p, l, seq, final_w, final):
    m = x.shape[0]
    tm, tf = 1024, D_FF // 2
    rows_per_mod = seq if mod.shape[0] > 1 else m
    row = lambda i, jf: (i, 0)
    vec = pl.BlockSpec((1, D_MODEL), lambda i, jf: (0, 0))
    return pl.pallas_call(
        functools.partial(_ffn_kernel, final=final),
        grid=(m // tm, D_FF // tf),
        in_specs=[pl.BlockSpec((tm, D_MODEL), row),
                  pl.BlockSpec((1, N_MOD, D_MODEL), lambda i, jf: (i * tm // rows_per_mod, 0, 0)),
                  pl.BlockSpec((1, 1, D_MODEL), lambda i, jf: (l, 0, 0)),
                  pl.BlockSpec((1, D_MODEL, tf), lambda i, jf: (l, 0, jf)),
                  pl.BlockSpec((1, D_MODEL, tf), lambda i, jf: (l, 0, jf)),
                  pl.BlockSpec((1, tf, D_MODEL), lambda i, jf: (l, jf, 0)),
                  vec],
        out_specs=pl.BlockSpec((tm, D_MODEL), row),
        out_shape=jax.ShapeDtypeStruct((m, D_MODEL), F32),
        scratch_shapes=[pltpu.VMEM((tm, D_MODEL), BF16), pltpu.VMEM((tm, D_MODEL), F32)],
        name="ffn",
        compiler_params=_cparams(("arbitrary", "arbitrary")),
    )(x, mod, p['norm2_w'], p['w_gate'], p['w_up'], p['w_down'], final_w)


def _gate_lanes(parts):
    row = jnp.concatenate([a.astype(F32) for a in parts], axis=-1)
    return jnp.pad(row, ((0, 0), (0, 0), (0, GATE_W - row.shape[-1]))).reshape(DEPTH, N_DIR, 1, GATE_W)


def _block_diag(a):
    dp, cb, g, r, c = a.shape
    eye = jnp.eye(g, dtype=a.dtype)
    return jnp.einsum('lbgrc,gh->lbgrhc', a, eye).reshape(dp, cb, g * r, g * c)


def _prep_params(w):
    gpb = S5_GROUPS // S5_CB
    idx, acc = [], 0
    for size in IN_SIZES[:-1]:
        acc += size
        idx.append(acc)
    mq, mk, mv, mo, mi, mf, su, gqkv, gz, ga, gb, gates = jnp.split(w['w_in'], idx, axis=-1)
    w_big = jnp.concatenate([gates, mq, mk, mv, mo, su, gqkv, gz], axis=-1).astype(BF16)
    smalls = []
    for d in range(N_DIR):
        sl = slice(d * HEADS, (d + 1) * HEADS)
        blk = jnp.concatenate([mi[..., sl], mf[..., sl], ga[..., sl], gb[..., sl]], axis=-1)
        smalls.append(jnp.pad(blk, ((0, 0), (0, 0), (0, GATE_W - 4 * HEADS))))
    w_small = jnp.concatenate(smalls, axis=-1).astype(BF16)
    zeros = jnp.zeros((DEPTH, N_DIR, HEADS), F32)
    gate_bias = _gate_lanes([w['ml_i_bias'], w['ml_f_bias'], w['gd_dt_bias'], zeros])
    gate_alog = _gate_lanes([zeros, zeros, w['gd_A_log'], zeros])
    b_shape = (DEPTH, S5_CB, gpb, S5_GROUP, S5_STATE)
    c_shape = (DEPTH, S5_CB, gpb, S5_STATE, S5_GROUP)
    b_re = jnp.swapaxes(w['s5_B_re'], 2, 3).reshape(b_shape)
    b_im = jnp.swapaxes(w['s5_B_im'], 2, 3).reshape(b_shape)
    bbd = jnp.concatenate([_block_diag(b_re), _block_diag(b_im)], axis=-1).astype(BF16)
    c_re = jnp.swapaxes(w['s5_C_re'], 2, 3).reshape(c_shape)
    c_im = jnp.swapaxes(w['s5_C_im'], 2, 3).reshape(c_shape)
    cbd = jnp.concatenate([_block_diag(c_re), -_block_diag(c_im)], axis=2).astype(BF16)
    lam_shape = (DEPTH, N_DIR, S5_CB, 1, S5_CBW)
    ls = jnp.broadcast_to(w['s5_log_step'][..., None], (DEPTH, N_DIR, S5_GROUPS, S5_STATE))
    vec = lambda a: a.reshape(DEPTH, 1, a.shape[-1])
    return dict(
        norm1_w=vec(w['norm1_w']), w_big=w_big, w_small=w_small, gate_bias=gate_bias, gate_alog=gate_alog,
        ml_norm_w=vec(w['ml_norm_w']), bbd=bbd, cbd=cbd,
        lam_re=w['s5_lam_re'].reshape(lam_shape), lam_im=w['s5_lam_im'].reshape(lam_shape),
        log_step=ls.reshape(lam_shape),
        s5_D=vec(w['s5_D']), s5_glu_w=w['s5_glu_w'].astype(BF16), s5_glu_b=vec(w['s5_glu_b']),
        gd_conv_w=w['gd_conv_w'], gd_norm_w=vec(jnp.tile(w['gd_norm_w'], (1, HEADS))),
        w_branch=w['w_branch'].astype(BF16), w_out=w['w_out'].astype(BF16), norm2_w=vec(w['norm2_w']),
        w_gate=w['w_gate'].astype(BF16), w_up=w['w_up'].astype(BF16), w_down=w['w_down'].astype(BF16),
    )


def _grid_pos_embed(n_tok):
    grid_w = 64
    t = np.arange(n_tok)
    quarter = D_MODEL // 4
    omega = (1.0 / (10000.0 ** (np.arange(quarter, dtype=np.float32) / quarter))).astype(np.float32)

    def enc(pos):
        ang = pos.astype(np.float32)[:, None] * omega[None, :]
        return np.concatenate([np.sin(ang), np.cos(ang)], axis=-1)

    return jnp.asarray(np.concatenate([enc(t // grid_w), enc(t % grid_w)], axis=-1).astype(np.float32))


def _trunk_layer(x, mod, p, l, batch, seq, tb, states, final_w, pe=None):
    res = _inproj(x, mod, p, l, seq, pe)
    if pe is not None:
        gates, z, zs, x = res
    else:
        gates, z, zs = res
    if states is None:
        ml_st = s5_st = gd_st = None
    else:
        ml_st, s5_st, gd_st = states
    ml = _mlstm(z, zs, p, l, batch, seq, tb, ml_st)
    s5 = _s5(z, p, l, batch, seq, s5_st)
    qkv = _gdn_prep(z, p, l, seq)
    gd = _gdn(qkv, zs, p, l, batch, seq, tb, gd_st)
    x = _merge(x, mod, gates, z, ml[0], s5[0], gd[0], p, l, seq)
    x = _ffn(x, mod, p, l, seq, final_w, l == DEPTH - 1)
    new_states = None
    if states is None:
        new_states = (ml[1][..., :DH], ml[1][..., DH], ml[2][:, :, :, 0, 0],
                      jnp.transpose(s5[1], (2, 0, 1, 3)).reshape(batch, N_DIR, S5_GROUPS, S5_STATE),
                      jnp.transpose(s5[2], (2, 0, 1, 3)).reshape(batch, N_DIR, S5_GROUPS, S5_STATE),
                      gd[1])
    return x, new_states


def kernel(x_prompt, x_sample, state_mlstm_C, state_mlstm_n, state_mlstm_m, state_s5_re, state_s5_im,
           state_gdn_S, c, c_ctx, ada_w, ada_b, norm1_w, w_in, ml_i_bias, ml_f_bias, ml_norm_w,
           s5_lam_re, s5_lam_im, s5_log_step, s5_B_re, s5_B_im, s5_C_re, s5_C_im, s5_D, s5_glu_w, s5_glu_b,
           gd_conv_w, gd_A_log, gd_dt_bias, gd_norm_w, w_branch, w_out, norm2_w, w_gate, w_up, w_down,
           final_norm_w):
    w = dict(norm1_w=norm1_w, w_in=w_in, ml_i_bias=ml_i_bias, ml_f_bias=ml_f_bias, ml_norm_w=ml_norm_w,
             s5_lam_re=s5_lam_re, s5_lam_im=s5_lam_im, s5_log_step=s5_log_step, s5_B_re=s5_B_re,
             s5_B_im=s5_B_im, s5_C_re=s5_C_re, s5_C_im=s5_C_im, s5_D=s5_D, s5_glu_w=s5_glu_w,
             s5_glu_b=s5_glu_b, gd_conv_w=gd_conv_w, gd_A_log=gd_A_log, gd_dt_bias=gd_dt_bias,
             gd_norm_w=gd_norm_w, w_branch=w_branch, w_out=w_out, norm2_w=norm2_w, w_gate=w_gate,
             w_up=w_up, w_down=w_down)
    bp, sp, _ = x_prompt.shape
    bs, ss, _ = x_sample.shape
    params = _prep_params(w)
    final_w = final_norm_w.reshape(1, D_MODEL)

    cc = jnp.concatenate([c_ctx[None, :], c, jnp.zeros((8 - 1 - bs, D_MODEL), F32)], axis=0)
    mods = _modulation(cc, ada_w, ada_b).reshape(DEPTH, 8, N_MOD, D_MODEL)

    xp = x_prompt.reshape(bp * sp, D_MODEL)
    per_layer = []
    for l in range(DEPTH):
        xp, st = _trunk_layer(xp, mods[l, 0:1], params, l, bp, sp, sp, None, final_w)
        per_layer.append(st)
    y_prompt = xp.reshape(bp, sp, D_MODEL)
    new_states = [jnp.stack([st[i] for st in per_layer], axis=1) for i in range(6)]

    xs = x_sample.reshape(bs * ss, D_MODEL)
    pe = _grid_pos_embed(ss)
    for l in range(DEPTH):
        n_cols = jnp.broadcast_to(state_mlstm_n[:, l][..., None], (bs, N_DIR, HEADS, DH, DH))
        ml_st = (jnp.concatenate([state_mlstm_C[:, l], n_cols], axis=-1),
                 jnp.broadcast_to(state_mlstm_m[:, l][:, :, :, None, None], (bs, N_DIR, HEADS, 1, DH)))
        s5_st = tuple(jnp.transpose(a[:, l].reshape(bs, N_DIR, S5_CB, 1, S5_CBW), (1, 2, 0, 3, 4))
                      for a in (state_s5_re, state_s5_im))
        xs, _ = _trunk_layer(xs, mods[l, 1:1 + bs], params, l, bs, ss, 512, (ml_st, s5_st, state_gdn_S[:, l]),
                             final_w, pe if l == 0 else None)
    y_sample = xs.reshape(bs, ss, D_MODEL)
    return (y_prompt, y_sample, *new_states)
```

```python
import functools
import math

import jax
import jax.numpy as jnp
import numpy as np
from jax import lax
from jax.experimental import pallas as pl
from jax.experimental.pallas import tpu as pltpu

F32 = jnp.float32
BF16 = jnp.bfloat16

D_MODEL = 1024
DEPTH = 2
N_DIR = 2
N_MOD = 6
EPS = 1e-6
HEADS = 4
DH = 128
BRANCH_W = HEADS * DH
CHUNK = 64
S5_GROUPS = 32
S5_GROUP = 16
S5_STATE = 64
S5_CB = 4
S5_CBW = S5_GROUPS * S5_STATE // S5_CB
S5_SUBLANES = 8
CONV_K = 5
D_FF = -(-8 * D_MODEL // (3 * 256)) * 256
IN_SIZES = (512, 512, 512, 512, 8, 8, 512, 1536, 512, 8, 8, 3072)
GATE_W = 128

GATES_W = 3 * D_MODEL
CB_Q, CB_K, CB_V, CB_O, CB_SU, CB_GQKV, CB_GZ = 0, 1, 2, 3, 4, 5, 8
Z_W = 9 * BRANCH_W

VMEM_LIMIT = 56 * 1024 * 1024


def _cparams(sem):
    return pltpu.CompilerParams(dimension_semantics=sem, vmem_limit_bytes=VMEM_LIMIT)


def _dot(a, b):
    return jnp.dot(a.astype(BF16), b.astype(BF16), preferred_element_type=F32)


def _dot_nt(a, b):
    return lax.dot_general(a.astype(BF16), b.astype(BF16), (((1,), (1,)), ((), ())),
                           preferred_element_type=F32)


def _dot_tn(a, b):
    return lax.dot_general(a.astype(BF16), b.astype(BF16), (((0,), (0,)), ((), ())),
                           preferred_element_type=F32)


def _split3(x):
    hi = x.astype(BF16)
    r1 = x - hi.astype(F32)
    mid = r1.astype(BF16)
    lo = (r1 - mid.astype(F32)).astype(BF16)
    return hi, mid, lo


def _sel_dot(sel, x, nt=False):
    dims = (((1,), (1,)), ((), ())) if nt else (((1,), (0,)), ((), ()))
    hi, mid, lo = _split3(x)
    f = lambda p: lax.dot_general(sel, p, dims, preferred_element_type=F32)
    return (f(hi) + f(mid)) + f(lo)


def _dot_hp(a, b):
    ah = a.astype(BF16)
    al = (a - ah.astype(F32)).astype(BF16)
    bh = b.astype(BF16)
    bl = (b - bh.astype(F32)).astype(BF16)
    f = lambda p, q: jnp.dot(p, q, preferred_element_type=F32)
    return f(ah, bh) + (f(ah, bl) + f(al, bh))


def _sigmoid(x):
    return 1.0 / (1.0 + jnp.exp(-x))


def _silu(x):
    return x * _sigmoid(x)


def _softplus(x):
    return jnp.maximum(x, 0.0) + jnp.log(1.0 + jnp.exp(-jnp.abs(x)))


def _gelu_tanh(x):
    c = math.sqrt(2.0 / math.pi)
    return 0.5 * x * (1.0 + jnp.tanh(c * (x + 0.044715 * (x * x * x))))


def _head_rms(x, w_row):
    outs = []
    for h in range(HEADS):
        xh = x[:, h * DH:(h + 1) * DH]
        outs.append(xh * lax.rsqrt(jnp.mean(xh * xh, axis=-1, keepdims=True) + EPS))
    return jnp.concatenate(outs, axis=-1) * w_row


def _dir_chunk(c, d, n):
    return c + d * (n - 1 - 2 * c)


def _mod_kernel(c_ref, w_ref, b_ref, o_ref):
    o_ref[0] = _dot(_silu(c_ref[...]), w_ref[0]) + b_ref[0]


def _modulation(cc, ada_w, ada_b):
    tn = 1536
    nmod = N_MOD * D_MODEL
    return pl.pallas_call(
        _mod_kernel,
        grid=(DEPTH, nmod // tn),
        in_specs=[pl.BlockSpec((8, D_MODEL), lambda l, j: (0, 0)),
                  pl.BlockSpec((1, D_MODEL, tn), lambda l, j: (l, 0, j)),
                  pl.BlockSpec((1, 1, tn), lambda l, j: (l, 0, j))],
        out_specs=pl.BlockSpec((1, 8, tn), lambda l, j: (l, 0, j)),
        out_shape=jax.ShapeDtypeStruct((DEPTH, 8, nmod), F32),
        name="adaln_mod",
        compiler_params=_cparams(("arbitrary", "arbitrary")),
    )(cc, ada_w, ada_b.reshape(DEPTH, 1, nmod))


def _inproj_kernel(*refs, has_pe, gate_tiles):
    if has_pe:
        x_ref, pe_ref, mod_ref, nw_ref, w_ref, ws_ref, gates_ref, z_ref, zs_ref, xs_ref, hn_scr = refs
    else:
        x_ref, mod_ref, nw_ref, w_ref, ws_ref, gates_ref, z_ref, zs_ref, hn_scr = refs
    j = pl.program_id(1)

    @pl.when(j == 0)
    def _():
        x = x_ref[...]
        if has_pe:
            x = x + pe_ref[...]
            xs_ref[...] = x
        y = x * lax.rsqrt(jnp.mean(x * x, axis=-1, keepdims=True) + EPS)
        h = (y * nw_ref[0]) * (1.0 + mod_ref[0, 1:2, :]) + mod_ref[0, 0:1, :]
        hb = h.astype(BF16)
        hn_scr[...] = hb
        zs_ref[...] = jnp.dot(hb, ws_ref[0], preferred_element_type=F32)

    @pl.when(j < gate_tiles)
    def _():
        gates_ref[...] = jnp.dot(hn_scr[...], w_ref[0], preferred_element_type=F32).astype(BF16)

    @pl.when(j >= gate_tiles)
    def _():
        z_ref[...] = jnp.dot(hn_scr[...], w_ref[0], preferred_element_type=F32)


def _inproj(x, mod, p, l, seq, pe=None):
    m = x.shape[0]
    has_pe = pe is not None
    tm, tn = (512 if has_pe else 1024), 1536
    gate_tiles = GATES_W // tn
    rows_per_mod = seq if mod.shape[0] > 1 else m
    row = lambda i, j: (i, 0)
    in_specs = [pl.BlockSpec((tm, D_MODEL), row)]
    args = [x]
    if has_pe:
        in_specs.append(pl.BlockSpec((tm, D_MODEL), lambda i, j: (i % (seq // tm), 0)))
        args.append(pe)
    in_specs += [pl.BlockSpec((1, N_MOD, D_MODEL), lambda i, j: (i * tm // rows_per_mod, 0, 0)),
                 pl.BlockSpec((1, 1, D_MODEL), lambda i, j: (l, 0, 0)),
                 pl.BlockSpec((1, D_MODEL, tn), lambda i, j: (l, 0, j)),
                 pl.BlockSpec((1, D_MODEL, N_DIR * GATE_W), lambda i, j: (l, 0, 0))]
    args += [mod, p['norm1_w'], p['w_big'], p['w_small']]
    out_specs = [pl.BlockSpec((tm, tn), lambda i, j: (i, jnp.minimum(j, gate_tiles - 1))),
                 pl.BlockSpec((tm, tn), lambda i, j: (i, jnp.maximum(j - gate_tiles, 0))),
                 pl.BlockSpec((tm, N_DIR * GATE_W), row)]
    out_shape = [jax.ShapeDtypeStruct((m, GATES_W), BF16),
                 jax.ShapeDtypeStruct((m, Z_W), F32),
                 jax.ShapeDtypeStruct((m, N_DIR * GATE_W), F32)]
    if has_pe:
        out_specs.append(pl.BlockSpec((tm, D_MODEL), row))
        out_shape.append(jax.ShapeDtypeStruct((m, D_MODEL), F32))
    return pl.pallas_call(
        functools.partial(_inproj_kernel, has_pe=has_pe, gate_tiles=gate_tiles),
        grid=(m // tm, (GATES_W + Z_W) // tn),
        in_specs=in_specs, out_specs=out_specs, out_shape=out_shape,
        scratch_shapes=[pltpu.VMEM((tm, D_MODEL), BF16)],
        name="inproj",
        compiler_params=_cparams(("arbitrary", "arbitrary")),
    )(*args)


def _mixer_rowblk(nt):
    return lambda b, d, j: b * nt + _dir_chunk(j, d, nt)


def _gate_selector():
    r = lax.broadcasted_iota(jnp.int32, (16, GATE_W), 0)
    c = lax.broadcasted_iota(jnp.int32, (16, GATE_W), 1)
    return (r == c).astype(BF16)


def _block_gate_sums(vals, other, n_other, cls_scr, cum_scr):
    tri = (cls_scr[...] > 0).astype(BF16)
    for g in range(vals.shape[0] // HC):
        grp = slice(g * HC, (g + 1) * HC)
        cum_scr[grp, :] = _sel_dot(tri, vals[grp, :])
    lane = lax.broadcasted_iota(jnp.int32, vals.shape, 1)
    mixed = jnp.where(lane < n_other, other, cum_scr[...])
    return _sel_dot(_gate_selector(), mixed, nt=True), mixed


def _mlstm_kernel(*refs, nch, zero_init, emit_state):
    q_ref, k_ref, v_ref, g_ref, bias_ref = refs[:5]
    pos = 5
    if not zero_init:
        c0_ref, m0_ref = refs[pos:pos + 2]
        pos += 2
    h_ref = refs[pos]
    pos += 1
    if emit_state:
        co_ref, mo_ref = refs[pos:pos + 2]
        pos += 2
    c_scr, m_scr, cls_all, cum_scr, row_scr, dense_scr, av_scr, ml_scr = refs[pos:pos + 8]

    d = pl.program_id(1)
    j = pl.program_id(2)

    @pl.when(j == 0)
    def _():
        if zero_init:
            c_scr[...] = jnp.zeros_like(c_scr)
            m_scr[...] = jnp.zeros_like(m_scr)
        else:
            c_scr[...] = c0_ref[0, 0]
            m_scr[...] = m0_ref[0, 0]

    _fill_pair_classes(cls_all)
    cls_scr = cls_all.at[d]
    rowid = lax.broadcasted_iota(jnp.int32, (CHUNK, DH), 0)
    last = (CHUNK - 1) * (1 - d)
    bias = bias_ref[0, 0]
    scale = DH ** -0.5
    hrows = lambda h: slice(h * CHUNK, (h + 1) * CHUNK)
    per_head = lambda vals: jnp.concatenate([jnp.broadcast_to(a, (CHUNK, a.shape[1])) for a in vals], axis=0)
    twice = lambda a: jnp.concatenate([a, a], axis=1)
    ones = jnp.ones((HC, DH), F32)

    pre_all = g_ref[...] + bias
    xt, mixed = _block_gate_sums(-_softplus(-pre_all), pre_all, HEADS, cls_scr, cum_scr)
    for c in range(nch):
        cs = slice(c * CHUNK, (c + 1) * CHUNK)
        row_scr[c] = jnp.concatenate([xt[h:h + 1, cs] - xt[HEADS + h:HEADS + h + 1, cs] for h in range(HEADS)],
                                     axis=1)
    for h in range(HEADS):
        dense_scr[h, :, 0:DH] = jnp.broadcast_to(mixed[:, HEADS + h:HEADS + h + 1], (nch * CHUNK, DH))
        dense_scr[h, :, DH:2 * DH] = jnp.broadcast_to(mixed[:, h:h + 1], (nch * CHUNK, DH))

    stacked = lambda rows, lo: jnp.concatenate([dense_scr[h, rows, lo:lo + DH] for h in range(HEADS)], axis=0)
    npar = MLSTM_LOCKSTEP

    def local(cp, carry):
        cs = [cp * npar + i for i in range(npar)]
        rows = [pl.ds(pl.multiple_of(c * CHUNK, CHUNK), CHUNK) for c in cs]
        cls = cls_scr[...]
        log_d = [jnp.where(cls > 0, twice(stacked(r, 0)) + row_scr[c], -jnp.inf) for r, c in zip(rows, cs)]
        ml = [jnp.max(a, axis=-1, keepdims=True) for a in log_d]
        q = [_stack_heads(q_ref[r, :]) for r in rows]
        k = [_stack_heads(k_ref[r, :]) * scale for r in rows]
        v = [jnp.concatenate([_stack_heads(v_ref[r, :]), ones], axis=1) for r in rows]
        s = [_dot_nt(a, b) * jnp.exp(ld - m) for a, b, ld, m in zip(q, k, log_d, ml)]
        av = [_dot(a, b) for a, b in zip(s, v)]
        for i, c in enumerate(cs):
            av_scr[c] = av[i]
            ml_scr[c] = jnp.broadcast_to(ml[i], (HC, DH))
        return carry

    lax.fori_loop(0, nch // npar, local, 0)

    def advance(c, carry):
        ci = _dir_chunk(c, d, nch)
        r0 = pl.multiple_of(ci * CHUNK, CHUNK)
        rows = pl.ds(r0, CHUNK)
        b_tok = stacked(rows, 0)
        i_tok = stacked(rows, DH)
        at_last = lambda a: [jnp.sum(jnp.where(rowid == last, a[hrows(h), :], 0.0), axis=0, keepdims=True)
                             for h in range(HEADS)]
        b_last = at_last(b_tok)
        m_prev = [m_scr[h] for h in range(HEADS)]
        ml = ml_scr[ci]
        log_0 = b_tok + per_head(m_prev)
        m_t = jnp.maximum(log_0, ml)
        w_0 = jnp.exp(log_0 - m_t)
        f = jnp.exp(ml - m_t)
        q = _stack_heads(q_ref[rows, :])
        k = _stack_heads(k_ref[rows, :]) * scale
        v = jnp.concatenate([_stack_heads(v_ref[rows, :]), ones], axis=1)
        cst = [c_scr[h] for h in range(HEADS)]
        qc = jnp.concatenate([_dot(q[hrows(h), :], cst[h]) for h in range(HEADS)], axis=0)
        num = f * av_scr[ci, :, 0:DH] + w_0 * qc[:, 0:DH]
        den = f * av_scr[ci, :, DH:2 * DH] + w_0 * qc[:, DH:2 * DH]
        hv = num / jnp.maximum(jnp.abs(den), jnp.exp(-m_t))
        m_new = at_last(m_t)
        kw = k * jnp.exp(per_head(b_last) - b_tok + i_tok - per_head(m_new))
        for h in range(HEADS):
            h_ref[0, rows, h * DH:(h + 1) * DH] = hv[hrows(h), :].astype(BF16)
            c_0 = jnp.exp(b_last[h] + m_prev[h] - m_new[h])
            c_scr[h] = twice(c_0) * cst[h] + _dot_tn(kw[hrows(h), :], v[hrows(h), :])
            m_scr[h] = m_new[h]
        return carry

    lax.fori_loop(0, nch, advance, 0)

    if emit_state:
        @pl.when(j == pl.num_programs(2) - 1)
        def _():
            co_ref[0, 0] = c_scr[...]
            mo_ref[0, 0] = m_scr[...]


def _mlstm(z, zs, p, l, batch, seq, tb, states):
    m = batch * seq
    nt = seq // tb
    nch = tb // CHUNK
    zero_init = states is None
    emit_state = states is None
    rb = _mixer_rowblk(nt)
    zspec = lambda cb: pl.BlockSpec((tb, BRANCH_W), lambda b, d, j: (rb(b, d, j), cb))
    st5 = lambda shape: pl.BlockSpec((1, 1) + shape, lambda b, d, j: (b, d) + (0,) * len(shape))
    in_specs = [zspec(CB_Q), zspec(CB_K), zspec(CB_V),
                pl.BlockSpec((tb, GATE_W), lambda b, d, j: (rb(b, d, j), d)),
                pl.BlockSpec((1, 1, 1, GATE_W), lambda b, d, j: (l, d, 0, 0))]
    args = [z, z, z, zs, p['gate_bias']]
    if not zero_init:
        in_specs += [st5((HEADS, DH, 2 * DH)), st5((HEADS, 1, DH))]
        args += list(states)
    out_specs = [pl.BlockSpec((1, tb, BRANCH_W), lambda b, d, j: (d, rb(b, d, j), 0))]
    out_shape = [jax.ShapeDtypeStruct((N_DIR, m, BRANCH_W), BF16)]
    if emit_state:
        out_specs += [st5((HEADS, DH, 2 * DH)), st5((HEADS, 1, DH))]
        out_shape += [jax.ShapeDtypeStruct((batch, N_DIR, HEADS, DH, 2 * DH), F32),
                      jax.ShapeDtypeStruct((batch, N_DIR, HEADS, 1, DH), F32)]
    return pl.pallas_call(
        functools.partial(_mlstm_kernel, nch=nch, zero_init=zero_init, emit_state=emit_state),
        grid=(batch, N_DIR, nt),
        in_specs=in_specs, out_specs=out_specs, out_shape=out_shape,
        scratch_shapes=[pltpu.VMEM((HEADS, DH, 2 * DH), F32), pltpu.VMEM((HEADS, 1, DH), F32),
                        pltpu.VMEM((N_DIR, HC, HC), jnp.int32),
                        pltpu.VMEM((tb, GATE_W), F32), pltpu.VMEM((nch, 1, HC), F32),
                        pltpu.VMEM((HEADS, tb, 2 * DH), F32),
                        pltpu.VMEM((nch, HC, 2 * DH), F32), pltpu.VMEM((nch, HC, DH), F32)],
        name="mlstm",
        compiler_params=_cparams(("arbitrary", "arbitrary", "arbitrary")),
    )(*args)


def _gdn_prep_kernel(x_ref, prev_ref, next_ref, w_ref, o_ref, xe_scr, *, tb, seq):
    i = pl.program_id(0)
    p = pl.program_id(1)
    pad = 8
    at_start = (i * tb) % seq == 0
    at_end = ((i + 1) * tb) % seq == 0
    xe_scr[0:pad, :] = jnp.where(at_start, 0.0, prev_ref[...])
    xe_scr[pad:pad + tb, :] = x_ref[...]
    xe_scr[pad + tb:pad + tb + pad, :] = jnp.where(at_end, 0.0, next_ref[...])
    acc = None
    for t in range(CONV_K):
        term = xe_scr[pl.ds(pad + t - CONV_K // 2, tb), :] * w_ref[0, t:t + 1, :]
        acc = term if acc is None else acc + term
    y = _silu(acc)
    outs = []
    for h in range(HEADS):
        yh = y[:, h * DH:(h + 1) * DH]
        outs.append(yh * lax.rsqrt(jnp.sum(yh * yh, axis=-1, keepdims=True) + EPS))
    yn = jnp.concatenate(outs, axis=-1)
    qscale = jnp.where(p == 0, DH ** -0.5, 1.0)
    o_ref[...] = jnp.where(p == 2, y, yn * qscale)


def _gdn_prep(z, p, l, seq):
    m = z.shape[0]
    tb = 256
    nb8 = m // 8
    return pl.pallas_call(
        functools.partial(_gdn_prep_kernel, tb=tb, seq=seq),
        grid=(m // tb, 3),
        in_specs=[pl.BlockSpec((tb, BRANCH_W), lambda i, part: (i, CB_GQKV + part)),
                  pl.BlockSpec((8, BRANCH_W), lambda i, part: (jnp.maximum(i * (tb // 8) - 1, 0), CB_GQKV + part)),
                  pl.BlockSpec((8, BRANCH_W), lambda i, part: (jnp.minimum((i + 1) * (tb // 8), nb8 - 1), CB_GQKV + part)),
                  pl.BlockSpec((1, CONV_K, BRANCH_W), lambda i, part: (l, 0, part))],
        out_specs=pl.BlockSpec((tb, BRANCH_W), lambda i, part: (i, part)),
        out_shape=jax.ShapeDtypeStruct((m, 3 * BRANCH_W), F32),
        scratch_shapes=[pltpu.VMEM((tb + 16, BRANCH_W), F32)],
        name="gdn_prep",
        compiler_params=_cparams(("arbitrary", "arbitrary")),
    )(z, z, z, p['gd_conv_w'])


HC = HEADS * CHUNK


def _stack_heads(x):
    return jnp.concatenate([x[:, h * DH:(h + 1) * DH] for h in range(HEADS)], axis=0)


def _stack_gate(x, lane0):
    return jnp.concatenate([x[:, lane0 + h:lane0 + h + 1] for h in range(HEADS)], axis=0)


GDN_LOCKSTEP = 4
MLSTM_LOCKSTEP = 4
BASE_LG = 3
CHUNK_LG = int(math.log2(CHUNK))


def _pair_classes(d):
    r = lax.broadcasted_iota(jnp.int32, (HC, HC), 0)
    c = lax.broadcasted_iota(jnp.int32, (HC, HC), 1)
    cls = jnp.full((HC, HC), CHUNK_LG, jnp.int32)
    for s in range(CHUNK_LG - 1, BASE_LG - 1, -1):
        cls = jnp.where((r >> s) == (c >> s), s, cls)
    diff = (r - c) * (1 - 2 * d)
    cls = jnp.where(diff == 0, 1, cls)
    return jnp.where(jnp.logical_and((r >> CHUNK_LG) == (c >> CHUNK_LG), diff >= 0), cls, 0)


def _fill_pair_classes(cls_all):
    first = functools.reduce(jnp.logical_and, [pl.program_id(i) == 0 for i in range(3)])

    @pl.when(first)
    def _():
        for d in range(N_DIR):
            cls_all[d] = _pair_classes(d)


def _unit_tri_solve(xs, cls, rhss):
    n = xs[0].shape[0]
    nr = rhss[0].shape[1]
    x0 = [jnp.where(cls == BASE_LG, x, 0.0) for x in xs]
    x2 = [_dot(a, a) for a in x0]
    y = [_dot(b, jnp.concatenate([b, a], axis=1)) for a, b in zip(x0, x2)]
    x4 = [t[:, :n] for t in y]
    q2 = [a + b + t[:, n:] for a, b, t in zip(x0, x2, y)]
    t8m = [a + b + _dot(b, a) for a, b in zip(q2, x4)]
    levels = range(BASE_LG + 1, CHUNK_LG + 1)
    cur = [jnp.concatenate([r] + [jnp.where(cls == s, x, 0.0) for s in levels], axis=1) for x, r in zip(xs, rhss)]
    cur = [a + _dot(t, a) for a, t in zip(cur, t8m)]
    for _ in levels:
        last = cur[0].shape[1] == nr + n
        rest = [a[:, :nr] if last else jnp.concatenate([a[:, :nr], a[:, nr + n:]], axis=1) for a in cur]
        cur = [r + _dot(a[:, nr:nr + n], r) for a, r in zip(cur, rest)]
    return cur


def _gdn_kernel(*refs, nch, zero_init, emit_state):
    q_ref, k_ref, v_ref, g_ref, bias_ref, alog_ref = refs[:6]
    pos = 6
    if not zero_init:
        s0_ref = refs[pos]
        pos += 1
    o_ref = refs[pos]
    pos += 1
    if emit_state:
        so_ref = refs[pos]
        pos += 1
    s_scr, cls_all, cum_scr, row_scr, uw_scr, at_scr, qg_scr, kd_scr, gt_scr = refs[pos:pos + 9]

    d = pl.program_id(1)
    j = pl.program_id(2)

    @pl.when(j == 0)
    def _():
        if zero_init:
            s_scr[...] = jnp.zeros_like(s_scr)
        else:
            s_scr[...] = s0_ref[0, 0]

    _fill_pair_classes(cls_all)
    cls_scr = cls_all.at[d]
    last = (CHUNK - 1) * (1 - d)
    bias = bias_ref[0, 0]
    neg_a = -jnp.exp(alog_ref[0, 0])
    g0 = 2 * HEADS

    g_all = neg_a * _softplus(g_ref[...] + bias)
    xt, _ = _block_gate_sums(g_all, g_all, 0, cls_scr, cum_scr)
    for c in range(nch):
        cs = slice(c * CHUNK, (c + 1) * CHUNK)
        row_scr[c] = jnp.concatenate([xt[g0 + h:g0 + h + 1, cs] for h in range(HEADS)], axis=1)

    npar = GDN_LOCKSTEP

    def prepare(cp, carry):
        cs = [cp * npar + i for i in range(npar)]
        r0 = [pl.multiple_of(c * CHUNK, CHUNK) for c in cs]
        rows = [pl.ds(r, CHUNK) for r in r0]
        gtot = [cum_scr[pl.ds(r + last, 1), :] for r in r0]
        g_col = [_stack_gate(cum_scr[r, :], g0) for r in rows]
        beta = [_stack_gate(_sigmoid(g_ref[r, :] + bias), g0 + HEADS) for r in rows]
        g_last = [jnp.concatenate([jnp.broadcast_to(t[:, g0 + h:g0 + h + 1], (CHUNK, 1)) for h in range(HEADS)],
                                  axis=0) for t in gtot]
        cls = cls_scr[...]
        decay = [jnp.exp(jnp.where(cls > 0, gc - row_scr[c], -jnp.inf)) for gc, c in zip(g_col, cs)]
        eg = [jnp.exp(gc) for gc in g_col]
        q = [_stack_heads(q_ref[r, :]) for r in rows]
        k = [_stack_heads(k_ref[r, :]) for r in rows]
        v = [_stack_heads(v_ref[r, :]) for r in rows]
        kb = [a * b for a, b in zip(k, beta)]
        x = [-(_dot_nt(a, b) * dc) for a, b, dc in zip(kb, k, decay)]
        rhs = [jnp.concatenate([a * b, kbi * e], axis=-1) for a, b, kbi, e in zip(v, beta, kb, eg)]
        uw = _unit_tri_solve(x, cls, rhs)
        attn = [_dot_nt(a, b) * dc for a, b, dc in zip(q, k, decay)]
        for i, c in enumerate(cs):
            uw_scr[c] = uw[i]
            at_scr[c] = attn[i]
            qg_scr[c] = q[i] * eg[i]
            kd_scr[c] = k[i] * jnp.exp(g_last[i] - g_col[i])
            gt_scr[c] = gtot[i]
        return carry

    lax.fori_loop(0, nch // npar, prepare, 0)

    def advance(c, carry):
        ci = _dir_chunk(c, d, nch)
        rows = pl.ds(pl.multiple_of(ci * CHUNK, CHUNK), CHUNK)
        hrows = lambda h: slice(h * CHUNK, (h + 1) * CHUNK)
        st = [s_scr[h] for h in range(HEADS)]
        v_new = [uw_scr[ci, hrows(h), 0:DH] - _dot(uw_scr[ci, hrows(h), DH:2 * DH], st[h]) for h in range(HEADS)]
        qs = jnp.concatenate([_dot(qg_scr[ci, hrows(h), :], st[h]) for h in range(HEADS)], axis=0)
        o = qs + _dot(at_scr[ci], jnp.concatenate(v_new, axis=0))
        gtot = gt_scr[ci]
        for h in range(HEADS):
            o_ref[0, rows, h * DH:(h + 1) * DH] = o[hrows(h), :].astype(BF16)
            s_scr[h] = st[h] * jnp.exp(gtot[:, g0 + h:g0 + h + 1]) + _dot_tn(kd_scr[ci, hrows(h), :], v_new[h])
        return carry

    lax.fori_loop(0, nch, advance, 0)

    if emit_state:
        @pl.when(j == pl.num_programs(2) - 1)
        def _():
            so_ref[0, 0] = s_scr[...]


def _gdn(qkv, zs, p, l, batch, seq, tb, s0):
    m = batch * seq
    nt = seq // tb
    nch = tb // CHUNK
    zero_init = s0 is None
    emit_state = s0 is None
    rb = _mixer_rowblk(nt)
    spec = lambda cb: pl.BlockSpec((tb, BRANCH_W), lambda b, d, j: (rb(b, d, j), cb))
    st = pl.BlockSpec((1, 1, HEADS, DH, DH), lambda b, d, j: (b, d, 0, 0, 0))
    dirrow = pl.BlockSpec((1, 1, 1, GATE_W), lambda b, d, j: (l, d, 0, 0))
    in_specs = [spec(0), spec(1), spec(2),
                pl.BlockSpec((tb, GATE_W), lambda b, d, j: (rb(b, d, j), d)), dirrow, dirrow]
    args = [qkv, qkv, qkv, zs, p['gate_bias'], p['gate_alog']]
    if not zero_init:
        in_specs.append(st)
        args.append(s0)
    out_specs = [pl.BlockSpec((1, tb, BRANCH_W), lambda b, d, j: (d, rb(b, d, j), 0))]
    out_shape = [jax.ShapeDtypeStruct((N_DIR, m, BRANCH_W), BF16)]
    if emit_state:
        out_specs.append(st)
        out_shape.append(jax.ShapeDtypeStruct((batch, N_DIR, HEADS, DH, DH), F32))
    return pl.pallas_call(
        functools.partial(_gdn_kernel, nch=nch, zero_init=zero_init, emit_state=emit_state),
        grid=(batch, N_DIR, nt),
        in_specs=in_specs, out_specs=out_specs, out_shape=out_shape,
        scratch_shapes=[pltpu.VMEM((HEADS, DH, DH), F32), pltpu.VMEM((N_DIR, HC, HC), jnp.int32),
                        pltpu.VMEM((tb, GATE_W), F32), pltpu.VMEM((nch, 1, HC), F32),
                        pltpu.VMEM((nch, HC, 2 * DH), F32), pltpu.VMEM((nch, HC, HC), F32),
                        pltpu.VMEM((nch, HC, DH), F32), pltpu.VMEM((nch, HC, DH), F32),
                        pltpu.VMEM((nch, 1, GATE_W), F32)],
        name="gdn",
        compiler_params=_cparams(("arbitrary", "arbitrary", "arbitrary")),
    )(*args)


def _s5_discretise(lre, lim, ls):
    dt = jnp.exp(ls)
    mag = jnp.exp(lre * dt)
    ar = mag * jnp.cos(lim * dt)
    ai = mag * jnp.sin(lim * dt)
    den = lre * lre + lim * lim
    nr = ar - 1.0
    return ar, ai, (nr * lre + ai * lim) / den, (ai * lre - nr * lim) / den


def _s5_kernel(*refs, rb, seg, chain):
    u_ref, b_ref, c_ref, lre_ref, lim_ref, ls_ref = refs[:6]
    pos = 6
    if chain:
        h0r_ref, h0i_ref = refs[pos:pos + 2]
        pos += 2
    y_ref = refs[pos]
    pos += 1
    if not chain:
        sr_ref, si_ref = refs[pos:pos + 2]
        pos += 2
    up, yp, xr, xi, er, ei, pr, pi_ = refs[pos:pos + 8]

    w = S5_CBW
    nsub = S5_SUBLANES
    rc = 256
    ngrp = seg // nsub
    par = [_s5_discretise(lre_ref[0, d, 0], lim_ref[0, d, 0], ls_ref[0, d, 0]) for d in range(N_DIR)]

    def interleave(g, carry):
        for s in range(nsub):
            src = pl.ds(pl.multiple_of(s * seg + g * nsub, nsub), nsub)
            up[pl.ds(g * nsub * nsub + s, nsub, stride=nsub), :] = u_ref[src, :]
        return carry

    lax.fori_loop(0, ngrp, interleave, 0)

    def fill(c, carry):
        rows = pl.ds(pl.multiple_of(c * rc, rc), rc)
        bu = _dot(up[rows, :], b_ref[0, 0])
        bre = bu[:, :w]
        bim = bu[:, w:]
        for d in range(N_DIR):
            _, _, zr, zi = par[d]
            xr[d, rows, :] = zr * bre - zi * bim
            xi[d, rows, :] = zr * bim + zi * bre
        return carry

    lax.fori_loop(0, rb // rc, fill, 0, unroll=2)

    ab = [(jnp.broadcast_to(p[0], (nsub, w)), jnp.broadcast_to(p[1], (nsub, w))) for p in par]

    def step_rows(tt, d):
        t = tt if d == 0 else seg - 1 - tt
        return pl.ds(pl.multiple_of(t * nsub, nsub), nsub)

    def scan_step(tt, carry):
        out = []
        for d in range(N_DIR):
            hr, hi = carry[d]
            arb, aib = ab[d]
            idx = step_rows(tt, d)
            nhr = arb * hr - aib * hi + xr[d, idx, :]
            nhi = arb * hi + aib * hr + xi[d, idx, :]
            xr[d, idx, :] = nhr
            xi[d, idx, :] = nhi
            out.append((nhr, nhi))
        return tuple(out)

    zero = jnp.zeros((nsub, w), F32)
    ends = lax.fori_loop(0, seg, scan_step, ((zero, zero), (zero, zero)), unroll=2)

    if chain:
        for d in range(N_DIR):
            ar, ai = par[d][0], par[d][1]
            er[d] = ends[d][0]
            ei[d] = ends[d][1]
            sr, si = ar, ai
            for _ in range(int(math.log2(seg))):
                sr, si = sr * sr - si * si, 2.0 * sr * si
            fr = h0r_ref[d, 0, 0]
            fi = h0i_ref[d, 0, 0]
            for k in range(nsub):
                row = k if d == 0 else nsub - 1 - k
                pr[d, row:row + 1, :] = fr
                pi_[d, row:row + 1, :] = fi
                fr, fi = (er[d, row:row + 1, :] + (sr * fr - si * fi),
                          ei[d, row:row + 1, :] + (sr * fi + si * fr))

        def fix_step(tt, carry):
            out = []
            for d in range(N_DIR):
                cr, ci = carry[d]
                arb, aib = ab[d]
                idx = step_rows(tt, d)
                ncr = arb * cr - aib * ci
                nci = arb * ci + aib * cr
                xr[d, idx, :] = xr[d, idx, :] + ncr
                xi[d, idx, :] = xi[d, idx, :] + nci
                out.append((ncr, nci))
            return tuple(out)

        lax.fori_loop(0, seg, fix_step, tuple((pr[d], pi_[d]) for d in range(N_DIR)), unroll=2)
    else:
        for d in range(N_DIR):
            sr_ref[d, 0] = ends[d][0]
            si_ref[d, 0] = ends[d][1]

    def proj(c, carry):
        rows = pl.ds(pl.multiple_of(c * rc, rc), rc)
        yp[rows, :] = (_dot(xr[0, rows, :] + xr[1, rows, :], c_ref[0, 0, :w, :])
                       + _dot(xi[0, rows, :] + xi[1, rows, :], c_ref[0, 0, w:, :]))
        return carry

    lax.fori_loop(0, rb // rc, proj, 0, unroll=4)

    def deinterleave(g, carry):
        for s in range(nsub):
            dst = pl.ds(pl.multiple_of(s * seg + g * nsub, nsub), nsub)
            y_ref[dst, :] = yp[pl.ds(g * nsub * nsub + s, nsub, stride=nsub), :]
        return carry

    lax.fori_loop(0, ngrp, deinterleave, 0)


def _s5(z, p, l, batch, seq, h0):
    m = batch * seq
    nsub = S5_SUBLANES
    chain = h0 is not None
    if chain:
        seg = seq // nsub
        ng = batch
        st_spec = pl.BlockSpec((N_DIR, 1, 1, 1, S5_CBW), lambda g, cb: (0, cb, g, 0, 0))
    else:
        seg = seq
        ng = batch // nsub
        st_spec = pl.BlockSpec((N_DIR, 1, nsub, S5_CBW), lambda g, cb: (0, cb, g, 0))
    rb = nsub * seg
    lam_spec = pl.BlockSpec((1, N_DIR, 1, 1, S5_CBW), lambda g, cb: (l, 0, cb, 0, 0))
    in_specs = [pl.BlockSpec((rb, 128), lambda g, cb: (g, CB_SU * 4 + cb)),
                pl.BlockSpec((1, 1, 128, 2 * S5_CBW), lambda g, cb: (l, cb, 0, 0)),
                pl.BlockSpec((1, 1, 2 * S5_CBW, 128), lambda g, cb: (l, cb, 0, 0)),
                lam_spec, lam_spec, lam_spec]
    args = [z, p['bbd'], p['cbd'], p['lam_re'], p['lam_im'], p['log_step']]
    if chain:
        in_specs += [st_spec, st_spec]
        args += list(h0)
    out_specs = [pl.BlockSpec((rb, 128), lambda g, cb: (g, cb))]
    out_shape = [jax.ShapeDtypeStruct((m, BRANCH_W), F32)]
    if not chain:
        out_specs += [st_spec, st_spec]
        out_shape += [jax.ShapeDtypeStruct((N_DIR, S5_CB, batch, S5_CBW), F32)] * 2
    vdir = lambda n: pltpu.VMEM((N_DIR, n, S5_CBW), F32)
    lanes = pltpu.VMEM((rb, 128), F32)
    return pl.pallas_call(
        functools.partial(_s5_kernel, rb=rb, seg=seg, chain=chain),
        grid=(ng, S5_CB),
        in_specs=in_specs, out_specs=out_specs, out_shape=out_shape,
        scratch_shapes=[lanes, lanes, vdir(rb), vdir(rb), vdir(nsub), vdir(nsub), vdir(nsub), vdir(nsub)],
        name="s5",
        compiler_params=_cparams(("arbitrary", "arbitrary")),
    )(*args)


def _merge_kernel(x_ref, mod_ref, gates_ref, o_ref, u_ref, gz_ref, hm_ref, ys_ref, go_ref,
                  mlw_ref, s5d_ref, gluw_ref, glub_ref, gdw_ref, wbr_ref, wout_ref, out_ref):
    both = lambda ref: ref[0].astype(F32) + ref[1].astype(F32)
    ya = _sigmoid(o_ref[...]) * _head_rms(both(hm_ref), mlw_ref[0])
    y5 = _gelu_tanh(ys_ref[...] + s5d_ref[0] * u_ref[...])
    yb = y5 * _sigmoid(_dot(y5, gluw_ref[0]) + glub_ref[0])
    yc = _head_rms(both(go_ref), gdw_ref[0]) * _silu(gz_ref[...])
    merged = None
    for n, y in enumerate((ya, yb, yc)):
        gate = _sigmoid(gates_ref[:, n * D_MODEL:(n + 1) * D_MODEL].astype(F32))
        term = gate * _dot(y, wbr_ref[0, n])
        merged = term if merged is None else merged + term
    out_ref[...] = x_ref[...] + mod_ref[0, 2:3, :] * _dot(merged, wout_ref[0])


def _merge(x, mod, gates, z, hm, ys, go, p, l, seq):
    m = x.shape[0]
    tm = 256
    rows_per_mod = seq if mod.shape[0] > 1 else m
    row = lambda i: (i, 0)
    zspec = lambda cb: pl.BlockSpec((tm, BRANCH_W), lambda i: (i, cb))
    dspec = pl.BlockSpec((N_DIR, tm, BRANCH_W), lambda i: (0, i, 0))
    full = lambda a: pl.BlockSpec((1,) + a.shape[1:], lambda i: (l,) + (0,) * (a.ndim - 1))
    consts = [p['ml_norm_w'], p['s5_D'], p['s5_glu_w'], p['s5_glu_b'], p['gd_norm_w'], p['w_branch'], p['w_out']]
    return pl.pallas_call(
        _merge_kernel,
        grid=(m // tm,),
        in_specs=[pl.BlockSpec((tm, D_MODEL), row),
                  pl.BlockSpec((1, N_MOD, D_MODEL), lambda i: (i * tm // rows_per_mod, 0, 0)),
                  pl.BlockSpec((tm, GATES_W), row),
                  zspec(CB_O), zspec(CB_SU), zspec(CB_GZ), dspec, pl.BlockSpec((tm, BRANCH_W), row), dspec]
                 + [full(a) for a in consts],
        out_specs=pl.BlockSpec((tm, D_MODEL), row),
        out_shape=jax.ShapeDtypeStruct((m, D_MODEL), F32),
        name="merge",
        compiler_params=_cparams(("arbitrary",)),
    )(x, mod, gates, z, z, z, hm, ys, go, *consts)


def _ffn_kernel(x_ref, mod_ref, nw_ref, wg_ref, wu_ref, wd_ref, fw_ref, out_ref, h_scr, acc_scr, *, final):
    jf = pl.program_id(1)

    @pl.when(jf == 0)
    def _():
        x = x_ref[...]
        y = x * lax.rsqrt(jnp.mean(x * x, axis=-1, keepdims=True) + EPS)
        h_scr[...] = ((y * nw_ref[0]) * (1.0 + mod_ref[0, 4:5, :]) + mod_ref[0, 3:4, :]).astype(BF16)
        acc_scr[...] = jnp.zeros_like(acc_scr)

    hb = h_scr[...]
    a = _silu(jnp.dot(hb, wg_ref[0], preferred_element_type=F32))
    b = jnp.dot(hb, wu_ref[0], preferred_element_type=F32)
    acc_scr[...] += _dot(a * b, wd_ref[0])

    @pl.when(jf == pl.num_programs(1) - 1)
    def _():
        x = x_ref[...] + mod_ref[0, 5:6, :] * acc_scr[...]
        if final:
            x = x * lax.rsqrt(jnp.mean(x * x, axis=-1, keepdims=True) + EPS) * fw_ref[...]
        out_ref[...] = x


def _ffn(x, mod, p, l, seq, final_w, final):
    m = x.shape[0]
    tm, tf = 1024, D_FF // 2
    rows_per_mod = seq if mod.shape[0] > 1 else m
    row = lambda i, jf: (i, 0)
    vec = pl.BlockSpec((1, D_MODEL), lambda i, jf: (0, 0))
    return pl.pallas_call(
        functools.partial(_ffn_kernel, final=final),
        grid=(m // tm, D_FF // tf),
        in_specs=[pl.BlockSpec((tm, D_MODEL), row),
                  pl.BlockSpec((1, N_MOD, D_MODEL), lambda i, jf: (i * tm // rows_per_mod, 0, 0)),
                  pl.BlockSpec((1, 1, D_MODEL), lambda i, jf: (l, 0, 0)),
                  pl.BlockSpec((1, D_MODEL, tf), lambda i, jf: (l, 0, jf)),
                  pl.BlockSpec((1, D_MODEL, tf), lambda i, jf: (l, 0, jf)),
                  pl.BlockSpec((1, tf, D_MODEL), lambda i, jf: (l, jf, 0)),
                  vec],
        out_specs=pl.BlockSpec((tm, D_MODEL), row),
        out_shape=jax.ShapeDtypeStruct((m, D_MODEL), F32),
        scratch_shapes=[pltpu.VMEM((tm, D_MODEL), BF16), pltpu.VMEM((tm, D_MODEL), F32)],
        name="ffn",
        compiler_params=_cparams(("arbitrary", "arbitrary")),
    )(x, mod, p['norm2_w'], p['w_gate'], p['w_up'], p['w_down'], final_w)


def _gate_lanes(parts):
    row = jnp.concatenate([a.astype(F32) for a in parts], axis=-1)
    return jnp.pad(row, ((0, 0), (0, 0), (0, GATE_W - row.shape[-1]))).reshape(DEPTH, N_DIR, 1, GATE_W)


def _block_diag(a):
    dp, cb, g, r, c = a.shape
    eye = jnp.eye(g, dtype=a.dtype)
    return jnp.einsum('lbgrc,gh->lbgrhc', a, eye).reshape(dp, cb, g * r, g * c)


def _prep_params(w):
    gpb = S5_GROUPS // S5_CB
    idx, acc = [], 0
    for size in IN_SIZES[:-1]:
        acc += size
        idx.append(acc)
    mq, mk, mv, mo, mi, mf, su, gqkv, gz, ga, gb, gates = jnp.split(w['w_in'], idx, axis=-1)
    w_big = jnp.concatenate([gates, mq, mk, mv, mo, su, gqkv, gz], axis=-1).astype(BF16)
    smalls = []
    for d in range(N_DIR):
        sl = slice(d * HEADS, (d + 1) * HEADS)
        blk = jnp.concatenate([mi[..., sl], mf[..., sl], ga[..., sl], gb[..., sl]], axis=-1)
        smalls.append(jnp.pad(blk, ((0, 0), (0, 0), (0, GATE_W - 4 * HEADS))))
    w_small = jnp.concatenate(smalls, axis=-1).astype(BF16)
    zeros = jnp.zeros((DEPTH, N_DIR, HEADS), F32)
    gate_bias = _gate_lanes([w['ml_i_bias'], w['ml_f_bias'], w['gd_dt_bias'], zeros])
    gate_alog = _gate_lanes([zeros, zeros, w['gd_A_log'], zeros])
    b_shape = (DEPTH, S5_CB, gpb, S5_GROUP, S5_STATE)
    c_shape = (DEPTH, S5_CB, gpb, S5_STATE, S5_GROUP)
    b_re = jnp.swapaxes(w['s5_B_re'], 2, 3).reshape(b_shape)
    b_im = jnp.swapaxes(w['s5_B_im'], 2, 3).reshape(b_shape)
    bbd = jnp.concatenate([_block_diag(b_re), _block_diag(b_im)], axis=-1).astype(BF16)
    c_re = jnp.swapaxes(w['s5_C_re'], 2, 3).reshape(c_shape)
    c_im = jnp.swapaxes(w['s5_C_im'], 2, 3).reshape(c_shape)
    cbd = jnp.concatenate([_block_diag(c_re), -_block_diag(c_im)], axis=2).astype(BF16)
    lam_shape = (DEPTH, N_DIR, S5_CB, 1, S5_CBW)
    ls = jnp.broadcast_to(w['s5_log_step'][..., None], (DEPTH, N_DIR, S5_GROUPS, S5_STATE))
    vec = lambda a: a.reshape(DEPTH, 1, a.shape[-1])
    return dict(
        norm1_w=vec(w['norm1_w']), w_big=w_big, w_small=w_small, gate_bias=gate_bias, gate_alog=gate_alog,
        ml_norm_w=vec(w['ml_norm_w']), bbd=bbd, cbd=cbd,
        lam_re=w['s5_lam_re'].reshape(lam_shape), lam_im=w['s5_lam_im'].reshape(lam_shape),
        log_step=ls.reshape(lam_shape),
        s5_D=vec(w['s5_D']), s5_glu_w=w['s5_glu_w'].astype(BF16), s5_glu_b=vec(w['s5_glu_b']),
        gd_conv_w=w['gd_conv_w'], gd_norm_w=vec(jnp.tile(w['gd_norm_w'], (1, HEADS))),
        w_branch=w['w_branch'].astype(BF16), w_out=w['w_out'].astype(BF16), norm2_w=vec(w['norm2_w']),
        w_gate=w['w_gate'].astype(BF16), w_up=w['w_up'].astype(BF16), w_down=w['w_down'].astype(BF16),
    )


def _grid_pos_embed(n_tok):
    grid_w = 64
    t = np.arange(n_tok)
    quarter = D_MODEL // 4
    omega = (1.0 / (10000.0 ** (np.arange(quarter, dtype=np.float32) / quarter))).astype(np.float32)

    def enc(pos):
        ang = pos.astype(np.float32)[:, None] * omega[None, :]
        return np.concatenate([np.sin(ang), np.cos(ang)], axis=-1)

    return jnp.asarray(np.concatenate([enc(t // grid_w), enc(t % grid_w)], axis=-1).astype(np.float32))


def _trunk_layer(x, mod, p, l, batch, seq, tb, states, final_w, pe=None):
    res = _inproj(x, mod, p, l, seq, pe)
    if pe is not None:
        gates, z, zs, x = res
    else:
        gates, z, zs = res
    if states is None:
        ml_st = s5_st = gd_st = None
    else:
        ml_st, s5_st, gd_st = states
    ml = _mlstm(z, zs, p, l, batch, seq, tb, ml_st)
    s5 = _s5(z, p, l, batch, seq, s5_st)
    qkv = _gdn_prep(z, p, l, seq)
    gd = _gdn(qkv, zs, p, l, batch, seq, tb, gd_st)
    x = _merge(x, mod, gates, z, ml[0], s5[0], gd[0], p, l, seq)
    x = _ffn(x, mod, p, l, seq, final_w, l == DEPTH - 1)
    new_states = None
    if states is None:
        new_states = (ml[1][..., :DH], ml[1][..., DH], ml[2][:, :, :, 0, 0],
                      jnp.transpose(s5[1], (2, 0, 1, 3)).reshape(batch, N_DIR, S5_GROUPS, S5_STATE),
                      jnp.transpose(s5[2], (2, 0, 1, 3)).reshape(batch, N_DIR, S5_GROUPS, S5_STATE),
                      gd[1])
    return x, new_states


def kernel(x_prompt, x_sample, state_mlstm_C, state_mlstm_n, state_mlstm_m, state_s5_re, state_s5_im,
           state_gdn_S, c, c_ctx, ada_w, ada_b, norm1_w, w_in, ml_i_bias, ml_f_bias, ml_norm_w,
           s5_lam_re, s5_lam_im, s5_log_step, s5_B_re, s5_B_im, s5_C_re, s5_C_im, s5_D, s5_glu_w, s5_glu_b,
           gd_conv_w, gd_A_log, gd_dt_bias, gd_norm_w, w_branch, w_out, norm2_w, w_gate, w_up, w_down,
           final_norm_w):
    w = dict(norm1_w=norm1_w, w_in=w_in, ml_i_bias=ml_i_bias, ml_f_bias=ml_f_bias, ml_norm_w=ml_norm_w,
             s5_lam_re=s5_lam_re, s5_lam_im=s5_lam_im, s5_log_step=s5_log_step, s5_B_re=s5_B_re,
             s5_B_im=s5_B_im, s5_C_re=s5_C_re, s5_C_im=s5_C_im, s5_D=s5_D, s5_glu_w=s5_glu_w,
             s5_glu_b=s5_glu_b, gd_conv_w=gd_conv_w, gd_A_log=gd_A_log, gd_dt_bias=gd_dt_bias,
             gd_norm_w=gd_norm_w, w_branch=w_branch, w_out=w_out, norm2_w=norm2_w, w_gate=w_gate,
             w_up=w_up, w_down=w_down)
    bp, sp, _ = x_prompt.shape
    bs, ss, _ = x_sample.shape
    params = _prep_params(w)
    final_w = final_norm_w.reshape(1, D_MODEL)

    cc = jnp.concatenate([c_ctx[None, :], c, jnp.zeros((8 - 1 - bs, D_MODEL), F32)], axis=0)
    mods = _modulation(cc, ada_w, ada_b).reshape(DEPTH, 8, N_MOD, D_MODEL)

    xp = x_prompt.reshape(bp * sp, D_MODEL)
    per_layer = []
    for l in range(DEPTH):
        xp, st = _trunk_layer(xp, mods[l, 0:1], params, l, bp, sp, sp, None, final_w)
        per_layer.append(st)
    y_prompt = xp.reshape(bp, sp, D_MODEL)
    new_states = [jnp.stack([st[i] for st in per_layer], axis=1) for i in range(6)]

    xs = x_sample.reshape(bs * ss, D_MODEL)
    pe = _grid_pos_embed(ss)
    for l in range(DEPTH):
        n_cols = jnp.broadcast_to(state_mlstm_n[:, l][..., None], (bs, N_DIR, HEADS, DH, DH))
        ml_st = (jnp.concatenate([state_mlstm_C[:, l], n_cols], axis=-1),
                 jnp.broadcast_to(state_mlstm_m[:, l][:, :, :, None, None], (bs, N_DIR, HEADS, 1, DH)))
        s5_st = tuple(jnp.transpose(a[:, l].reshape(bs, N_DIR, S5_CB, 1, S5_CBW), (1, 2, 0, 3, 4))
                      for a in (state_s5_re, state_s5_im))
        xs, _ = _trunk_layer(xs, mods[l, 1:1 + bs], params, l, bs, ss, 512, (ml_st, s5_st, state_gdn_S[:, l]),
                             final_w, pe if l == 0 else None)
    y_sample = xs.reshape(bs, ss, D_MODEL)
    return (y_prompt, y_sample, *new_states)
```

```python
import functools
import math

import jax
import jax.numpy as jnp
import numpy as np
from jax import lax
from jax.experimental import pallas as pl
from jax.experimental.pallas import tpu as pltpu

F32 = jnp.float32
BF16 = jnp.bfloat16

D_MODEL = 1024
DEPTH = 2
N_DIR = 2
N_MOD = 6
EPS = 1e-6
HEADS = 4
DH = 128
BRANCH_W = HEADS * DH
CHUNK = 64
S5_GROUPS = 32
S5_GROUP = 16
S5_STATE = 64
S5_CB = 4
S5_CBW = S5_GROUPS * S5_STATE // S5_CB
S5_SUBLANES = 8
CONV_K = 5
D_FF = -(-8 * D_MODEL // (3 * 256)) * 256
IN_SIZES = (512, 512, 512, 512, 8, 8, 512, 1536, 512, 8, 8, 3072)
GATE_W = 128

GATES_W = 3 * D_MODEL
CB_Q, CB_K, CB_V, CB_O, CB_SU, CB_GQKV, CB_GZ = 0, 1, 2, 3, 4, 5, 8
Z_W = 9 * BRANCH_W

VMEM_LIMIT = 56 * 1024 * 1024


def _cparams(sem):
    return pltpu.CompilerParams(dimension_semantics=sem, vmem_limit_bytes=VMEM_LIMIT)


def _dot(a, b):
    return jnp.dot(a.astype(BF16), b.astype(BF16), preferred_element_type=F32)


def _dot_nt(a, b):
    return lax.dot_general(a.astype(BF16), b.astype(BF16), (((1,), (1,)), ((), ())),
                           preferred_element_type=F32)


def _dot_tn(a, b):
    return lax.dot_general(a.astype(BF16), b.astype(BF16), (((0,), (0,)), ((), ())),
                           preferred_element_type=F32)


def _split3(x):
    hi = x.astype(BF16)
    r1 = x - hi.astype(F32)
    mid = r1.astype(BF16)
    lo = (r1 - mid.astype(F32)).astype(BF16)
    return hi, mid, lo


def _sel_dot(sel, x, nt=False):
    dims = (((1,), (1,)), ((), ())) if nt else (((1,), (0,)), ((), ()))
    hi, mid, lo = _split3(x)
    f = lambda p: lax.dot_general(sel, p, dims, preferred_element_type=F32)
    return (f(hi) + f(mid)) + f(lo)


def _dot_hp(a, b):
    ah = a.astype(BF16)
    al = (a - ah.astype(F32)).astype(BF16)
    bh = b.astype(BF16)
    bl = (b - bh.astype(F32)).astype(BF16)
    f = lambda p, q: jnp.dot(p, q, preferred_element_type=F32)
    return f(ah, bh) + (f(ah, bl) + f(al, bh))


def _sigmoid(x):
    return 1.0 / (1.0 + jnp.exp(-x))


def _silu(x):
    return x * _sigmoid(x)


def _softplus(x):
    return jnp.maximum(x, 0.0) + jnp.log(1.0 + jnp.exp(-jnp.abs(x)))


def _gelu_tanh(x):
    c = math.sqrt(2.0 / math.pi)
    return 0.5 * x * (1.0 + jnp.tanh(c * (x + 0.044715 * (x * x * x))))


def _head_rms(x, w_row):
    outs = []
    for h in range(HEADS):
        xh = x[:, h * DH:(h + 1) * DH]
        outs.append(xh * lax.rsqrt(jnp.mean(xh * xh, axis=-1, keepdims=True) + EPS))
    return jnp.concatenate(outs, axis=-1) * w_row


def _dir_chunk(c, d, n):
    return c + d * (n - 1 - 2 * c)


def _mod_kernel(c_ref, w_ref, b_ref, o_ref):
    o_ref[0] = _dot(_silu(c_ref[...]), w_ref[0]) + b_ref[0]


def _modulation(cc, ada_w, ada_b):
    tn = 1536
    nmod = N_MOD * D_MODEL
    return pl.pallas_call(
        _mod_kernel,
        grid=(DEPTH, nmod // tn),
        in_specs=[pl.BlockSpec((8, D_MODEL), lambda l, j: (0, 0)),
                  pl.BlockSpec((1, D_MODEL, tn), lambda l, j: (l, 0, j)),
                  pl.BlockSpec((1, 1, tn), lambda l, j: (l, 0, j))],
        out_specs=pl.BlockSpec((1, 8, tn), lambda l, j: (l, 0, j)),
        out_shape=jax.ShapeDtypeStruct((DEPTH, 8, nmod), F32),
        name="adaln_mod",
        compiler_params=_cparams(("arbitrary", "arbitrary")),
    )(cc, ada_w, ada_b.reshape(DEPTH, 1, nmod))


def _inproj_kernel(*refs, has_pe, gate_tiles):
    if has_pe:
        x_ref, pe_ref, mod_ref, nw_ref, w_ref, ws_ref, gates_ref, z_ref, zs_ref, xs_ref, hn_scr = refs
    else:
        x_ref, mod_ref, nw_ref, w_ref, ws_ref, gates_ref, z_ref, zs_ref, hn_scr = refs
    j = pl.program_id(1)

    @pl.when(j == 0)
    def _():
        x = x_ref[...]
        if has_pe:
            x = x + pe_ref[...]
            xs_ref[...] = x
        y = x * lax.rsqrt(jnp.mean(x * x, axis=-1, keepdims=True) + EPS)
        h = (y * nw_ref[0]) * (1.0 + mod_ref[0, 1:2, :]) + mod_ref[0, 0:1, :]
        hb = h.astype(BF16)
        hn_scr[...] = hb
        zs_ref[...] = jnp.dot(hb, ws_ref[0], preferred_element_type=F32)

    @pl.when(j < gate_tiles)
    def _():
        gates_ref[...] = jnp.dot(hn_scr[...], w_ref[0], preferred_element_type=F32).astype(BF16)

    @pl.when(j >= gate_tiles)
    def _():
        z_ref[...] = jnp.dot(hn_scr[...], w_ref[0], preferred_element_type=F32)


def _inproj(x, mod, p, l, seq, pe=None):
    m = x.shape[0]
    has_pe = pe is not None
    tm, tn = (512 if has_pe else 1024), 1536
    gate_tiles = GATES_W // tn
    rows_per_mod = seq if mod.shape[0] > 1 else m
    row = lambda i, j: (i, 0)
    in_specs = [pl.BlockSpec((tm, D_MODEL), row)]
    args = [x]
    if has_pe:
        in_specs.append(pl.BlockSpec((tm, D_MODEL), lambda i, j: (i % (seq // tm), 0)))
        args.append(pe)
    in_specs += [pl.BlockSpec((1, N_MOD, D_MODEL), lambda i, j: (i * tm // rows_per_mod, 0, 0)),
                 pl.BlockSpec((1, 1, D_MODEL), lambda i, j: (l, 0, 0)),
                 pl.BlockSpec((1, D_MODEL, tn), lambda i, j: (l, 0, j)),
                 pl.BlockSpec((1, D_MODEL, N_DIR * GATE_W), lambda i, j: (l, 0, 0))]
    args += [mod, p['norm1_w'], p['w_big'], p['w_small']]
    out_specs = [pl.BlockSpec((tm, tn), lambda i, j: (i, jnp.minimum(j, gate_tiles - 1))),
                 pl.BlockSpec((tm, tn), lambda i, j: (i, jnp.maximum(j - gate_tiles, 0))),
                 pl.BlockSpec((tm, N_DIR * GATE_W), row)]
    out_shape = [jax.ShapeDtypeStruct((m, GATES_W), BF16),
                 jax.ShapeDtypeStruct((m, Z_W), F32),
                 jax.ShapeDtypeStruct((m, N_DIR * GATE_W), F32)]
    if has_pe:
        out_specs.append(pl.BlockSpec((tm, D_MODEL), row))
        out_shape.append(jax.ShapeDtypeStruct((m, D_MODEL), F32))
    return pl.pallas_call(
        functools.partial(_inproj_kernel, has_pe=has_pe, gate_tiles=gate_tiles),
        grid=(m // tm, (GATES_W + Z_W) // tn),
        in_specs=in_specs, out_specs=out_specs, out_shape=out_shape,
        scratch_shapes=[pltpu.VMEM((tm, D_MODEL), BF16)],
        name="inproj",
        compiler_params=_cparams(("arbitrary", "arbitrary")),
    )(*args)


def _mixer_rowblk(nt):
    return lambda b, d, j: b * nt + _dir_chunk(j, d, nt)


def _gate_selector():
    r = lax.broadcasted_iota(jnp.int32, (16, GATE_W), 0)
    c = lax.broadcasted_iota(jnp.int32, (16, GATE_W), 1)
    return (r == c).astype(BF16)


GATE_ROWS = 1024


def _gate_prep_kernel(g_ref, bias_ref, alog_ref, mix_ref, rows_ref):
    d = pl.program_id(0)
    pre = g_ref[...] + bias_ref[0, 0]
    lane = lax.broadcasted_iota(jnp.int32, pre.shape, 1)
    log_sig = -_softplus(-pre)
    vals = jnp.where(lane < 2 * HEADS, log_sig, -jnp.exp(alog_ref[0, 0]) * (pre - log_sig))
    r = lax.broadcasted_iota(jnp.int32, (HC, HC), 0)
    c = lax.broadcasted_iota(jnp.int32, (HC, HC), 1)
    ordered = jnp.logical_and((r >> CHUNK_LG) == (c >> CHUNK_LG), (r - c) * (1 - 2 * d) >= 0)
    tri = ordered.astype(BF16)
    cum = jnp.concatenate([_sel_dot(tri, vals[g * HC:(g + 1) * HC, :]) for g in range(GATE_ROWS // HC)], axis=0)
    mix = jnp.where(lane < HEADS, pre, jnp.where(lane < 3 * HEADS, cum, jnp.exp(log_sig)))
    mix_ref[0] = mix
    xt = _sel_dot(_gate_selector(), mix, nt=True)
    for c in range(GATE_ROWS // CHUNK):
        cs = slice(c * CHUNK, (c + 1) * CHUNK)
        rows_ref[0, c, 0:1, :] = jnp.concatenate(
            [xt[h:h + 1, cs] - xt[HEADS + h:HEADS + h + 1, cs] for h in range(HEADS)], axis=1)
        rows_ref[0, c, 1:2, :] = jnp.concatenate(
            [xt[2 * HEADS + h:2 * HEADS + h + 1, cs] for h in range(HEADS)], axis=1)


def _gate_prep(zs, p, l):
    m = zs.shape[0]
    dirrow = pl.BlockSpec((1, 1, 1, GATE_W), lambda d, i: (l, d, 0, 0))
    return pl.pallas_call(
        _gate_prep_kernel,
        grid=(N_DIR, m // GATE_ROWS),
        in_specs=[pl.BlockSpec((GATE_ROWS, GATE_W), lambda d, i: (i, d)), dirrow, dirrow],
        out_specs=[pl.BlockSpec((1, GATE_ROWS, GATE_W), lambda d, i: (d, i, 0)),
                   pl.BlockSpec((1, GATE_ROWS // CHUNK, 2, HC), lambda d, i: (d, i, 0, 0))],
        out_shape=[jax.ShapeDtypeStruct((N_DIR, m, GATE_W), F32),
                   jax.ShapeDtypeStruct((N_DIR, m // CHUNK, 2, HC), F32)],
        name="gate_prep",
        compiler_params=_cparams(("arbitrary", "arbitrary")),
    )(zs, p['gate_bias'], p['gate_alog'])


def _mlstm_kernel(*refs, nch, zero_init, emit_state):
    q_ref, k_ref, v_ref, mix_ref, rows_ref = refs[:5]
    pos = 5
    if not zero_init:
        c0_ref, m0_ref = refs[pos:pos + 2]
        pos += 2
    h_ref = refs[pos]
    pos += 1
    if emit_state:
        co_ref, mo_ref = refs[pos:pos + 2]
        pos += 2
    c_scr, m_scr, cls_all, dense_scr, av_scr, ml_scr = refs[pos:pos + 6]

    d = pl.program_id(1)
    j = pl.program_id(2)

    @pl.when(j == 0)
    def _():
        if zero_init:
            c_scr[...] = jnp.zeros_like(c_scr)
            m_scr[...] = jnp.zeros_like(m_scr)
        else:
            c_scr[...] = c0_ref[0, 0]
            m_scr[...] = m0_ref[0, 0]

    _fill_pair_classes(cls_all)
    cls_scr = cls_all.at[d]
    rowid = lax.broadcasted_iota(jnp.int32, (CHUNK, DH), 0)
    last = (CHUNK - 1) * (1 - d)
    scale = DH ** -0.5
    hrows = lambda h: slice(h * CHUNK, (h + 1) * CHUNK)
    per_head = lambda vals: jnp.concatenate([jnp.broadcast_to(a, (CHUNK, a.shape[1])) for a in vals], axis=0)
    twice = lambda a: jnp.concatenate([a, a], axis=1)
    ones = jnp.ones((HC, DH), F32)

    mixed = mix_ref[0]
    for h in range(HEADS):
        dense_scr[h, :, 0:DH] = jnp.broadcast_to(mixed[:, HEADS + h:HEADS + h + 1], (nch * CHUNK, DH))
        dense_scr[h, :, DH:2 * DH] = jnp.broadcast_to(mixed[:, h:h + 1], (nch * CHUNK, DH))

    stacked = lambda rows, lo: jnp.concatenate([dense_scr[h, rows, lo:lo + DH] for h in range(HEADS)], axis=0)
    npar = MLSTM_LOCKSTEP

    def local(cp, carry):
        cs = [cp * npar + i for i in range(npar)]
        rows = [pl.ds(pl.multiple_of(c * CHUNK, CHUNK), CHUNK) for c in cs]
        cls = cls_scr[...]
        log_d = [jnp.where(cls > 0, twice(stacked(r, 0)) + rows_ref[0, c, 0:1, :], -jnp.inf)
                 for r, c in zip(rows, cs)]
        ml = [jnp.max(a, axis=-1, keepdims=True) for a in log_d]
        q = [_stack_heads(q_ref[r, :]) for r in rows]
        k = [_stack_heads(k_ref[r, :]) * scale for r in rows]
        v = [jnp.concatenate([_stack_heads(v_ref[r, :]), ones], axis=1) for r in rows]
        s = [_dot_nt(a, b) * jnp.exp(ld - m) for a, b, ld, m in zip(q, k, log_d, ml)]
        av = [_dot(a, b) for a, b in zip(s, v)]
        for i, c in enumerate(cs):
            av_scr[c] = av[i]
            ml_scr[c] = jnp.broadcast_to(ml[i], (HC, DH))
        return carry

    lax.fori_loop(0, nch // npar, local, 0)

    def advance(c, carry):
        ci = _dir_chunk(c, d, nch)
        r0 = pl.multiple_of(ci * CHUNK, CHUNK)
        rows = pl.ds(r0, CHUNK)
        b_tok = stacked(rows, 0)
        i_tok = stacked(rows, DH)
        at_last = lambda a: [jnp.sum(jnp.where(rowid == last, a[hrows(h), :], 0.0), axis=0, keepdims=True)
                             for h in range(HEADS)]
        b_last = at_last(b_tok)
        m_prev = [m_scr[h] for h in range(HEADS)]
        ml = ml_scr[ci]
        log_0 = b_tok + per_head(m_prev)
        m_t = jnp.maximum(log_0, ml)
        w_0 = jnp.exp(log_0 - m_t)
        f = jnp.exp(ml - m_t)
        q = _stack_heads(q_ref[rows, :])
        k = _stack_heads(k_ref[rows, :]) * scale
        v = jnp.concatenate([_stack_heads(v_ref[rows, :]), ones], axis=1)
        cst = [c_scr[h] for h in range(HEADS)]
        qc = jnp.concatenate([_dot(q[hrows(h), :], cst[h]) for h in range(HEADS)], axis=0)
        num = f * av_scr[ci, :, 0:DH] + w_0 * qc[:, 0:DH]
        den = f * av_scr[ci, :, DH:2 * DH] + w_0 * qc[:, DH:2 * DH]
        hv = num / jnp.maximum(jnp.abs(den), jnp.exp(-m_t))
        m_new = at_last(m_t)
        kw = k * jnp.exp(per_head(b_last) - b_tok + i_tok - per_head(m_new))
        for h in range(HEADS):
            h_ref[0, rows, h * DH:(h + 1) * DH] = hv[hrows(h), :].astype(BF16)
            c_0 = jnp.exp(b_last[h] + m_prev[h] - m_new[h])
            c_scr[h] = twice(c_0) * cst[h] + _dot_tn(kw[hrows(h), :], v[hrows(h), :])
            m_scr[h] = m_new[h]
        return carry

    lax.fori_loop(0, nch, advance, 0)

    if emit_state:
        @pl.when(j == pl.num_programs(2) - 1)
        def _():
            co_ref[0, 0] = c_scr[...]
            mo_ref[0, 0] = m_scr[...]


def _gate_specs(rb, tb):
    return [pl.BlockSpec((1, tb, GATE_W), lambda b, d, j: (d, rb(b, d, j), 0)),
            pl.BlockSpec((1, tb // CHUNK, 2, HC), lambda b, d, j: (d, rb(b, d, j), 0, 0))]


def _mlstm(z, gmix, grows, batch, seq, tb, states):
    m = batch * seq
    nt = seq // tb
    nch = tb // CHUNK
    zero_init = states is None
    emit_state = states is None
    rb = _mixer_rowblk(nt)
    zspec = lambda cb: pl.BlockSpec((tb, BRANCH_W), lambda b, d, j: (rb(b, d, j), cb))
    st5 = lambda shape: pl.BlockSpec((1, 1) + shape, lambda b, d, j: (b, d) + (0,) * len(shape))
    in_specs = [zspec(CB_Q), zspec(CB_K), zspec(CB_V)] + _gate_specs(rb, tb)
    args = [z, z, z, gmix, grows]
    if not zero_init:
        in_specs += [st5((HEADS, DH, 2 * DH)), st5((HEADS, 1, DH))]
        args += list(states)
    out_specs = [pl.BlockSpec((1, tb, BRANCH_W), lambda b, d, j: (d, rb(b, d, j), 0))]
    out_shape = [jax.ShapeDtypeStruct((N_DIR, m, BRANCH_W), BF16)]
    if emit_state:
        out_specs += [st5((HEADS, DH, 2 * DH)), st5((HEADS, 1, DH))]
        out_shape += [jax.ShapeDtypeStruct((batch, N_DIR, HEADS, DH, 2 * DH), F32),
                      jax.ShapeDtypeStruct((batch, N_DIR, HEADS, 1, DH), F32)]
    return pl.pallas_call(
        functools.partial(_mlstm_kernel, nch=nch, zero_init=zero_init, emit_state=emit_state),
        grid=(batch, N_DIR, nt),
        in_specs=in_specs, out_specs=out_specs, out_shape=out_shape,
        scratch_shapes=[pltpu.VMEM((HEADS, DH, 2 * DH), F32), pltpu.VMEM((HEADS, 1, DH), F32),
                        pltpu.VMEM((N_DIR, HC, HC), jnp.int32),
                        pltpu.VMEM((HEADS, tb, 2 * DH), F32),
                        pltpu.VMEM((nch, HC, 2 * DH), F32), pltpu.VMEM((nch, HC, DH), F32)],
        name="mlstm",
        compiler_params=_cparams(("arbitrary", "arbitrary", "arbitrary")),
    )(*args)


def _gdn_prep_kernel(x_ref, prev_ref, next_ref, w_ref, o_ref, xe_scr, *, tb, seq):
    i = pl.program_id(0)
    p = pl.program_id(1)
    pad = 8
    at_start = (i * tb) % seq == 0
    at_end = ((i + 1) * tb) % seq == 0
    xe_scr[0:pad, :] = jnp.where(at_start, 0.0, prev_ref[...])
    xe_scr[pad:pad + tb, :] = x_ref[...]
    xe_scr[pad + tb:pad + tb + pad, :] = jnp.where(at_end, 0.0, next_ref[...])
    acc = None
    for t in range(CONV_K):
        term = xe_scr[pl.ds(pad + t - CONV_K // 2, tb), :] * w_ref[0, t:t + 1, :]
        acc = term if acc is None else acc + term
    y = _silu(acc)
    outs = []
    for h in range(HEADS):
        yh = y[:, h * DH:(h + 1) * DH]
        outs.append(yh * lax.rsqrt(jnp.sum(yh * yh, axis=-1, keepdims=True) + EPS))
    yn = jnp.concatenate(outs, axis=-1)
    qscale = jnp.where(p == 0, DH ** -0.5, 1.0)
    o_ref[...] = jnp.where(p == 2, y, yn * qscale)


def _gdn_prep(z, p, l, seq):
    m = z.shape[0]
    tb = 256
    nb8 = m // 8
    return pl.pallas_call(
        functools.partial(_gdn_prep_kernel, tb=tb, seq=seq),
        grid=(m // tb, 3),
        in_specs=[pl.BlockSpec((tb, BRANCH_W), lambda i, part: (i, CB_GQKV + part)),
                  pl.BlockSpec((8, BRANCH_W), lambda i, part: (jnp.maximum(i * (tb // 8) - 1, 0), CB_GQKV + part)),
                  pl.BlockSpec((8, BRANCH_W), lambda i, part: (jnp.minimum((i + 1) * (tb // 8), nb8 - 1), CB_GQKV + part)),
                  pl.BlockSpec((1, CONV_K, BRANCH_W), lambda i, part: (l, 0, part))],
        out_specs=pl.BlockSpec((tb, BRANCH_W), lambda i, part: (i, part)),
        out_shape=jax.ShapeDtypeStruct((m, 3 * BRANCH_W), F32),
        scratch_shapes=[pltpu.VMEM((tb + 16, BRANCH_W), F32)],
        name="gdn_prep",
        compiler_params=_cparams(("arbitrary", "arbitrary")),
    )(z, z, z, p['gd_conv_w'])


HC = HEADS * CHUNK


def _stack_heads(x):
    return jnp.concatenate([x[:, h * DH:(h + 1) * DH] for h in range(HEADS)], axis=0)


def _stack_gate(x, lane0):
    return jnp.concatenate([x[:, lane0 + h:lane0 + h + 1] for h in range(HEADS)], axis=0)


GDN_LOCKSTEP = 4
MLSTM_LOCKSTEP = 4
BASE_LG = 3
CHUNK_LG = int(math.log2(CHUNK))


def _pair_classes(d):
    r = lax.broadcasted_iota(jnp.int32, (HC, HC), 0)
    c = lax.broadcasted_iota(jnp.int32, (HC, HC), 1)
    cls = jnp.full((HC, HC), CHUNK_LG, jnp.int32)
    for s in range(CHUNK_LG - 1, BASE_LG - 1, -1):
        cls = jnp.where((r >> s) == (c >> s), s, cls)
    diff = (r - c) * (1 - 2 * d)
    cls = jnp.where(diff == 0, 1, cls)
    return jnp.where(jnp.logical_and((r >> CHUNK_LG) == (c >> CHUNK_LG), diff >= 0), cls, 0)


def _fill_pair_classes(cls_all):
    first = functools.reduce(jnp.logical_and, [pl.program_id(i) == 0 for i in range(3)])

    @pl.when(first)
    def _():
        for d in range(N_DIR):
            cls_all[d] = _pair_classes(d)


def _unit_tri_solve(xs, cls, rhss):
    n = xs[0].shape[0]
    nr = rhss[0].shape[1]
    x0 = [jnp.where(cls == BASE_LG, x, 0.0) for x in xs]
    x2 = [_dot(a, a) for a in x0]
    y = [_dot(b, jnp.concatenate([b, a], axis=1)) for a, b in zip(x0, x2)]
    x4 = [t[:, :n] for t in y]
    q2 = [a + b + t[:, n:] for a, b, t in zip(x0, x2, y)]
    t8m = [a + b + _dot(b, a) for a, b in zip(q2, x4)]
    levels = range(BASE_LG + 1, CHUNK_LG + 1)
    cur = [jnp.concatenate([r] + [jnp.where(cls == s, x, 0.0) for s in levels], axis=1) for x, r in zip(xs, rhss)]
    cur = [a + _dot(t, a) for a, t in zip(cur, t8m)]
    for _ in levels:
        last = cur[0].shape[1] == nr + n
        rest = [a[:, :nr] if last else jnp.concatenate([a[:, :nr], a[:, nr + n:]], axis=1) for a in cur]
        cur = [r + _dot(a[:, nr:nr + n], r) for a, r in zip(cur, rest)]
    return cur


def _gdn_kernel(*refs, nch, zero_init, emit_state):
    q_ref, k_ref, v_ref, mix_ref, rows_ref = refs[:5]
    pos = 5
    if not zero_init:
        s0_ref = refs[pos]
        pos += 1
    o_ref = refs[pos]
    pos += 1
    if emit_state:
        so_ref = refs[pos]
        pos += 1
    s_scr, cls_all, uw_scr, at_scr, qg_scr, kd_scr, gt_scr = refs[pos:pos + 7]

    d = pl.program_id(1)
    j = pl.program_id(2)

    @pl.when(j == 0)
    def _():
        if zero_init:
            s_scr[...] = jnp.zeros_like(s_scr)
        else:
            s_scr[...] = s0_ref[0, 0]

    _fill_pair_classes(cls_all)
    cls_scr = cls_all.at[d]
    last = (CHUNK - 1) * (1 - d)
    g0 = 2 * HEADS
    npar = GDN_LOCKSTEP

    def prepare(cp, carry):
        cs = [cp * npar + i for i in range(npar)]
        r0 = [pl.multiple_of(c * CHUNK, CHUNK) for c in cs]
        rows = [pl.ds(r, CHUNK) for r in r0]
        mix = mix_ref.at[0]
        gtot = [mix[pl.ds(r + last, 1), :] for r in r0]
        g_col = [_stack_gate(mix[r, :], g0) for r in rows]
        beta = [_stack_gate(mix[r, :], g0 + HEADS) for r in rows]
        g_last = [jnp.concatenate([jnp.broadcast_to(t[:, g0 + h:g0 + h + 1], (CHUNK, 1)) for h in range(HEADS)],
                                  axis=0) for t in gtot]
        cls = cls_scr[...]
        decay = [jnp.exp(jnp.where(cls > 0, gc - rows_ref[0, c, 1:2, :], -jnp.inf)) for gc, c in zip(g_col, cs)]
        eg = [jnp.exp(gc) for gc in g_col]
        q = [_stack_heads(q_ref[r, :]) for r in rows]
        k = [_stack_heads(k_ref[r, :]) for r in rows]
        v = [_stack_heads(v_ref[r, :]) for r in rows]
        kb = [a * b for a, b in zip(k, beta)]
        x = [-(_dot_nt(a, b) * dc) for a, b, dc in zip(kb, k, decay)]
        rhs = [jnp.concatenate([a * b, kbi * e], axis=-1) for a, b, kbi, e in zip(v, beta, kb, eg)]
        uw = _unit_tri_solve(x, cls, rhs)
        attn = [_dot_nt(a, b) * dc for a, b, dc in zip(q, k, decay)]
        for i, c in enumerate(cs):
            uw_scr[c] = uw[i]
            at_scr[c] = attn[i]
            qg_scr[c] = q[i] * eg[i]
            kd_scr[c] = k[i] * jnp.exp(g_last[i] - g_col[i])
            gt_scr[c] = gtot[i]
        return carry

    lax.fori_loop(0, nch // npar, prepare, 0)

    def advance(c, carry):
        ci = _dir_chunk(c, d, nch)
        rows = pl.ds(pl.multiple_of(ci * CHUNK, CHUNK), CHUNK)
        hrows = lambda h: slice(h * CHUNK, (h + 1) * CHUNK)
        st = [s_scr[h] for h in range(HEADS)]
        v_new = [uw_scr[ci, hrows(h), 0:DH] - _dot(uw_scr[ci, hrows(h), DH:2 * DH], st[h]) for h in range(HEADS)]
        qs = jnp.concatenate([_dot(qg_scr[ci, hrows(h), :], st[h]) for h in range(HEADS)], axis=0)
        o = qs + _dot(at_scr[ci], jnp.concatenate(v_new, axis=0))
        gtot = gt_scr[ci]
        for h in range(HEADS):
            o_ref[0, rows, h * DH:(h + 1) * DH] = o[hrows(h), :].astype(BF16)
            s_scr[h] = st[h] * jnp.exp(gtot[:, g0 + h:g0 + h + 1]) + _dot_tn(kd_scr[ci, hrows(h), :], v_new[h])
        return carry

    lax.fori_loop(0, nch, advance, 0)

    if emit_state:
        @pl.when(j == pl.num_programs(2) - 1)
        def _():
            so_ref[0, 0] = s_scr[...]


def _gdn(qkv, gmix, grows, batch, seq, tb, s0):
    m = batch * seq
    nt = seq // tb
    nch = tb // CHUNK
    zero_init = s0 is None
    emit_state = s0 is None
    rb = _mixer_rowblk(nt)
    spec = lambda cb: pl.BlockSpec((tb, BRANCH_W), lambda b, d, j: (rb(b, d, j), cb))
    st = pl.BlockSpec((1, 1, HEADS, DH, DH), lambda b, d, j: (b, d, 0, 0, 0))
    in_specs = [spec(0), spec(1), spec(2)] + _gate_specs(rb, tb)
    args = [qkv, qkv, qkv, gmix, grows]
    if not zero_init:
        in_specs.append(st)
        args.append(s0)
    out_specs = [pl.BlockSpec((1, tb, BRANCH_W), lambda b, d, j: (d, rb(b, d, j), 0))]
    out_shape = [jax.ShapeDtypeStruct((N_DIR, m, BRANCH_W), BF16)]
    if emit_state:
        out_specs.append(st)
        out_shape.append(jax.ShapeDtypeStruct((batch, N_DIR, HEADS, DH, DH), F32))
    return pl.pallas_call(
        functools.partial(_gdn_kernel, nch=nch, zero_init=zero_init, emit_state=emit_state),
        grid=(batch, N_DIR, nt),
        in_specs=in_specs, out_specs=out_specs, out_shape=out_shape,
        scratch_shapes=[pltpu.VMEM((HEADS, DH, DH), F32), pltpu.VMEM((N_DIR, HC, HC), jnp.int32),
                        pltpu.VMEM((nch, HC, 2 * DH), F32), pltpu.VMEM((nch, HC, HC), F32),
                        pltpu.VMEM((nch, HC, DH), F32), pltpu.VMEM((nch, HC, DH), F32),
                        pltpu.VMEM((nch, 1, GATE_W), F32)],
        name="gdn",
        compiler_params=_cparams(("arbitrary", "arbitrary", "arbitrary")),
    )(*args)


def _s5_discretise(lre, lim, ls):
    dt = jnp.exp(ls)
    mag = jnp.exp(lre * dt)
    ar = mag * jnp.cos(lim * dt)
    ai = mag * jnp.sin(lim * dt)
    den = lre * lre + lim * lim
    nr = ar - 1.0
    return ar, ai, (nr * lre + ai * lim) / den, (ai * lre - nr * lim) / den


def _s5_kernel(*refs, rb, seg, chain):
    u_ref, b_ref, c_ref, lre_ref, lim_ref, ls_ref = refs[:6]
    pos = 6
    if chain:
        h0r_ref, h0i_ref = refs[pos:pos + 2]
        pos += 2
    y_ref = refs[pos]
    pos += 1
    if not chain:
        sr_ref, si_ref = refs[pos:pos + 2]
        pos += 2
    up, yp, xr, xi, er, ei, pr, pi_ = refs[pos:pos + 8]

    w = S5_CBW
    nsub = S5_SUBLANES
    rc = 256
    ngrp = seg // nsub
    par = [_s5_discretise(lre_ref[0, d, 0], lim_ref[0, d, 0], ls_ref[0, d, 0]) for d in range(N_DIR)]

    def interleave(g, carry):
        for s in range(nsub):
            src = pl.ds(pl.multiple_of(s * seg + g * nsub, nsub), nsub)
            up[pl.ds(g * nsub * nsub + s, nsub, stride=nsub), :] = u_ref[src, :]
        return carry

    lax.fori_loop(0, ngrp, interleave, 0)

    def fill(c, carry):
        rows = pl.ds(pl.multiple_of(c * rc, rc), rc)
        bu = _dot(up[rows, :], b_ref[0, 0])
        bre = bu[:, :w]
        bim = bu[:, w:]
        for d in range(N_DIR):
            _, _, zr, zi = par[d]
            xr[d, rows, :] = zr * bre - zi * bim
            xi[d, rows, :] = zr * bim + zi * bre
        return carry

    lax.fori_loop(0, rb // rc, fill, 0, unroll=2)

    ab = [(jnp.broadcast_to(p[0], (nsub, w)), jnp.broadcast_to(p[1], (nsub, w))) for p in par]

    def step_rows(tt, d):
        t = tt if d == 0 else seg - 1 - tt
        return pl.ds(pl.multiple_of(t * nsub, nsub), nsub)

    def scan_step(tt, carry):
        out = []
        for d in range(N_DIR):
            hr, hi = carry[d]
            arb, aib = ab[d]
            idx = step_rows(tt, d)
            nhr = arb * hr - aib * hi + xr[d, idx, :]
            nhi = arb * hi + aib * hr + xi[d, idx, :]
            xr[d, idx, :] = nhr
            xi[d, idx, :] = nhi
            out.append((nhr, nhi))
        return tuple(out)

    zero = jnp.zeros((nsub, w), F32)
    ends = lax.fori_loop(0, seg, scan_step, ((zero, zero), (zero, zero)), unroll=2)

    if chain:
        for d in range(N_DIR):
            ar, ai = par[d][0], par[d][1]
            er[d] = ends[d][0]
            ei[d] = ends[d][1]
            sr, si = ar, ai
            for _ in range(int(math.log2(seg))):
                sr, si = sr * sr - si * si, 2.0 * sr * si
            fr = h0r_ref[d, 0, 0]
            fi = h0i_ref[d, 0, 0]
            for k in range(nsub):
                row = k if d == 0 else nsub - 1 - k
                pr[d, row:row + 1, :] = fr
                pi_[d, row:row + 1, :] = fi
                fr, fi = (er[d, row:row + 1, :] + (sr * fr - si * fi),
                          ei[d, row:row + 1, :] + (sr * fi + si * fr))

        def fix_step(tt, carry):
            out = []
            for d in range(N_DIR):
                cr, ci = carry[d]
                arb, aib = ab[d]
                idx = step_rows(tt, d)
                ncr = arb * cr - aib * ci
                nci = arb * ci + aib * cr
                xr[d, idx, :] = xr[d, idx, :] + ncr
                xi[d, idx, :] = xi[d, idx, :] + nci
                out.append((ncr, nci))
            return tuple(out)

        lax.fori_loop(0, seg, fix_step, tuple((pr[d], pi_[d]) for d in range(N_DIR)), unroll=2)
    else:
        for d in range(N_DIR):
            sr_ref[d, 0] = ends[d][0]
            si_ref[d, 0] = ends[d][1]

    def proj(c, carry):
        rows = pl.ds(pl.multiple_of(c * rc, rc), rc)
        yp[rows, :] = (_dot(xr[0, rows, :] + xr[1, rows, :], c_ref[0, 0, :w, :])
                       + _dot(xi[0, rows, :] + xi[1, rows, :], c_ref[0, 0, w:, :]))
        return carry

    lax.fori_loop(0, rb // rc, proj, 0, unroll=4)

    def deinterleave(g, carry):
        for s in range(nsub):
            dst = pl.ds(pl.multiple_of(s * seg + g * nsub, nsub), nsub)
            y_ref[dst, :] = yp[pl.ds(g * nsub * nsub + s, nsub, stride=nsub), :]
        return carry

    lax.fori_loop(0, ngrp, deinterleave, 0)


def _s5(z, p, l, batch, seq, h0):
    m = batch * seq
    nsub = S5_SUBLANES
    chain = h0 is not None
    if chain:
        seg = seq // nsub
        ng = batch
        st_spec = pl.BlockSpec((N_DIR, 1, 1, 1, S5_CBW), lambda g, cb: (0, cb, g, 0, 0))
    else:
        seg = seq
        ng = batch // nsub
        st_spec = pl.BlockSpec((N_DIR, 1, nsub, S5_CBW), lambda g, cb: (0, cb, g, 0))
    rb = nsub * seg
    lam_spec = pl.BlockSpec((1, N_DIR, 1, 1, S5_CBW), lambda g, cb: (l, 0, cb, 0, 0))
    in_specs = [pl.BlockSpec((rb, 128), lambda g, cb: (g, CB_SU * 4 + cb)),
                pl.BlockSpec((1, 1, 128, 2 * S5_CBW), lambda g, cb: (l, cb, 0, 0)),
                pl.BlockSpec((1, 1, 2 * S5_CBW, 128), lambda g, cb: (l, cb, 0, 0)),
                lam_spec, lam_spec, lam_spec]
    args = [z, p['bbd'], p['cbd'], p['lam_re'], p['lam_im'], p['log_step']]
    if chain:
        in_specs += [st_spec, st_spec]
        args += list(h0)
    out_specs = [pl.BlockSpec((rb, 128), lambda g, cb: (g, cb))]
    out_shape = [jax.ShapeDtypeStruct((m, BRANCH_W), F32)]
    if not chain:
        out_specs += [st_spec, st_spec]
        out_shape += [jax.ShapeDtypeStruct((N_DIR, S5_CB, batch, S5_CBW), F32)] * 2
    vdir = lambda n: pltpu.VMEM((N_DIR, n, S5_CBW), F32)
    lanes = pltpu.VMEM((rb, 128), F32)
    return pl.pallas_call(
        functools.partial(_s5_kernel, rb=rb, seg=seg, chain=chain),
        grid=(ng, S5_CB),
        in_specs=in_specs, out_specs=out_specs, out_shape=out_shape,
        scratch_shapes=[lanes, lanes, vdir(rb), vdir(rb), vdir(nsub), vdir(nsub), vdir(nsub), vdir(nsub)],
        name="s5",
        compiler_params=_cparams(("arbitrary", "arbitrary")),
    )(*args)


def _merge_kernel(x_ref, mod_ref, gates_ref, o_ref, u_ref, gz_ref, hm_ref, ys_ref, go_ref,
                  mlw_ref, s5d_ref, gluw_ref, glub_ref, gdw_ref, wbr_ref, wout_ref, out_ref):
    both = lambda ref: ref[0].astype(F32) + ref[1].astype(F32)
    ya = _sigmoid(o_ref[...]) * _head_rms(both(hm_ref), mlw_ref[0])
    y5 = _gelu_tanh(ys_ref[...] + s5d_ref[0] * u_ref[...])
    yb = y5 * _sigmoid(_dot(y5, gluw_ref[0]) + glub_ref[0])
    yc = _head_rms(both(go_ref), gdw_ref[0]) * _silu(gz_ref[...])
    merged = None
    for n, y in enumerate((ya, yb, yc)):
        gate = _sigmoid(gates_ref[:, n * D_MODEL:(n + 1) * D_MODEL].astype(F32))
        term = gate * _dot(y, wbr_ref[0, n])
        merged = term if merged is None else merged + term
    out_ref[...] = x_ref[...] + mod_ref[0, 2:3, :] * _dot(merged, wout_ref[0])


def _merge(x, mod, gates, z, hm, ys, go, p, l, seq):
    m = x.shape[0]
    tm = 256
    rows_per_mod = seq if mod.shape[0] > 1 else m
    row = lambda i: (i, 0)
    zspec = lambda cb: pl.BlockSpec((tm, BRANCH_W), lambda i: (i, cb))
    dspec = pl.BlockSpec((N_DIR, tm, BRANCH_W), lambda i: (0, i, 0))
    full = lambda a: pl.BlockSpec((1,) + a.shape[1:], lambda i: (l,) + (0,) * (a.ndim - 1))
    consts = [p['ml_norm_w'], p['s5_D'], p['s5_glu_w'], p['s5_glu_b'], p['gd_norm_w'], p['w_branch'], p['w_out']]
    return pl.pallas_call(
        _merge_kernel,
        grid=(m // tm,),
        in_specs=[pl.BlockSpec((tm, D_MODEL), row),
                  pl.BlockSpec((1, N_MOD, D_MODEL), lambda i: (i * tm // rows_per_mod, 0, 0)),
                  pl.BlockSpec((tm, GATES_W), row),
                  zspec(CB_O), zspec(CB_SU), zspec(CB_GZ), dspec, pl.BlockSpec((tm, BRANCH_W), row), dspec]
                 + [full(a) for a in consts],
        out_specs=pl.BlockSpec((tm, D_MODEL), row),
        out_shape=jax.ShapeDtypeStruct((m, D_MODEL), F32),
        name="merge",
        compiler_params=_cparams(("arbitrary",)),
    )(x, mod, gates, z, z, z, hm, ys, go, *consts)


def _ffn_kernel(x_ref, mod_ref, nw_ref, wg_ref, wu_ref, wd_ref, fw_ref, out_ref, h_scr, acc_scr, *, final):
    jf = pl.program_id(1)

    @pl.when(jf == 0)
    def _():
        x = x_ref[...]
        y = x * lax.rsqrt(jnp.mean(x * x, axis=-1, keepdims=True) + EPS)
        h_scr[...] = ((y * nw_ref[0]) * (1.0 + mod_ref[0, 4:5, :]) + mod_ref[0, 3:4, :]).astype(BF16)
        acc_scr[...] = jnp.zeros_like(acc_scr)

    hb = h_scr[...]
    a = _silu(jnp.dot(hb, wg_ref[0], preferred_element_type=F32))
    b = jnp.dot(hb, wu_ref[0], preferred_element_type=F32)
    acc_scr[...] += _dot(a * b, wd_ref[0])

    @pl.when(jf == pl.num_programs(1) - 1)
    def _():
        x = x_ref[...] + mod_ref[0, 5:6, :] * acc_scr[...]
        if final:
            x = x * lax.rsqrt(jnp.mean(x * x, axis=-1, keepdims=True) + EPS) * fw_ref[...]
        out_ref[...] = x


def _ffn(x, mod, p, l, seq, final_w, final):
    m = x.shape[0]
    tm, tf = 1024, D_FF // 2
    rows_per_mod = seq if mod.shape[0] > 1 else m
    row = lambda i, jf: (i, 0)
    vec = pl.BlockSpec((1, D_MODEL), lambda i, jf: (0, 0))
    return pl.pallas_call(
        functools.partial(_ffn_kernel, final=final),
        grid=(m // tm, D_FF // tf),
        in_specs=[pl.BlockSpec((tm, D_MODEL), row),
                  pl.BlockSpec((1, N_MOD, D_MODEL), lambda i, jf: (i * tm // rows_per_mod, 0, 0)),
                  pl.BlockSpec((1, 1, D_MODEL), lambda i, jf: (l, 0, 0)),
                  pl.BlockSpec((1, D_MODEL, tf), lambda i, jf: (l, 0, jf)),
                  pl.BlockSpec((1, D_MODEL, tf), lambda i, jf: (l, 0, jf)),
                  pl.BlockSpec((1, tf, D_MODEL), lambda i, jf: (l, jf, 0)),
                  vec],
        out_specs=pl.BlockSpec((tm, D_MODEL), row),
        out_shape=jax.ShapeDtypeStruct((m, D_MODEL), F32),
        scratch_shapes=[pltpu.VMEM((tm, D_MODEL), BF16), pltpu.VMEM((tm, D_MODEL), F32)],
        name="ffn",
        compiler_params=_cparams(("arbitrary", "arbitrary")),
    )(x, mod, p['norm2_w'], p['w_gate'], p['w_up'], p['w_down'], final_w)


def _gate_lanes(parts):
    row = jnp.concatenate([a.astype(F32) for a in parts], axis=-1)
    return jnp.pad(row, ((0, 0), (0, 0), (0, GATE_W - row.shape[-1]))).reshape(DEPTH, N_DIR, 1, GATE_W)


def _block_diag(a):
    dp, cb, g, r, c = a.shape
    eye = jnp.eye(g, dtype=a.dtype)
    return jnp.einsum('lbgrc,gh->lbgrhc', a, eye).reshape(dp, cb, g * r, g * c)


def _prep_params(w):
    gpb = S5_GROUPS // S5_CB
    idx, acc = [], 0
    for size in IN_SIZES[:-1]:
        acc += size
        idx.append(acc)
    mq, mk, mv, mo, mi, mf, su, gqkv, gz, ga, gb, gates = jnp.split(w['w_in'], idx, axis=-1)
    w_big = jnp.concatenate([gates, mq, mk, mv, mo, su, gqkv, gz], axis=-1).astype(BF16)
    smalls = []
    for d in range(N_DIR):
        sl = slice(d * HEADS, (d + 1) * HEADS)
        blk = jnp.concatenate([mi[..., sl], mf[..., sl], ga[..., sl], gb[..., sl]], axis=-1)
        smalls.append(jnp.pad(blk, ((0, 0), (0, 0), (0, GATE_W - 4 * HEADS))))
    w_small = jnp.concatenate(smalls, axis=-1).astype(BF16)
    zeros = jnp.zeros((DEPTH, N_DIR, HEADS), F32)
    gate_bias = _gate_lanes([w['ml_i_bias'], w['ml_f_bias'], w['gd_dt_bias'], zeros])
    gate_alog = _gate_lanes([zeros, zeros, w['gd_A_log'], zeros])
    b_shape = (DEPTH, S5_CB, gpb, S5_GROUP, S5_STATE)
    c_shape = (DEPTH, S5_CB, gpb, S5_STATE, S5_GROUP)
    b_re = jnp.swapaxes(w['s5_B_re'], 2, 3).reshape(b_shape)
    b_im = jnp.swapaxes(w['s5_B_im'], 2, 3).reshape(b_shape)
    bbd = jnp.concatenate([_block_diag(b_re), _block_diag(b_im)], axis=-1).astype(BF16)
    c_re = jnp.swapaxes(w['s5_C_re'], 2, 3).reshape(c_shape)
    c_im = jnp.swapaxes(w['s5_C_im'], 2, 3).reshape(c_shape)
    cbd = jnp.concatenate([_block_diag(c_re), -_block_diag(c_im)], axis=2).astype(BF16)
    lam_shape = (DEPTH, N_DIR, S5_CB, 1, S5_CBW)
    ls = jnp.broadcast_to(w['s5_log_step'][..., None], (DEPTH, N_DIR, S5_GROUPS, S5_STATE))
    vec = lambda a: a.reshape(DEPTH, 1, a.shape[-1])
    return dict(
        norm1_w=vec(w['norm1_w']), w_big=w_big, w_small=w_small, gate_bias=gate_bias, gate_alog=gate_alog,
        ml_norm_w=vec(w['ml_norm_w']), bbd=bbd, cbd=cbd,
        lam_re=w['s5_lam_re'].reshape(lam_shape), lam_im=w['s5_lam_im'].reshape(lam_shape),
        log_step=ls.reshape(lam_shape),
        s5_D=vec(w['s5_D']), s5_glu_w=w['s5_glu_w'].astype(BF16), s5_glu_b=vec(w['s5_glu_b']),
        gd_conv_w=w['gd_conv_w'], gd_norm_w=vec(jnp.tile(w['gd_norm_w'], (1, HEADS))),
        w_branch=w['w_branch'].astype(BF16), w_out=w['w_out'].astype(BF16), norm2_w=vec(w['norm2_w']),
        w_gate=w['w_gate'].astype(BF16), w_up=w['w_up'].astype(BF16), w_down=w['w_down'].astype(BF16),
    )


def _grid_pos_embed(n_tok):
    grid_w = 64
    t = np.arange(n_tok)
    quarter = D_MODEL // 4
    omega = (1.0 / (10000.0 ** (np.arange(quarter, dtype=np.float32) / quarter))).astype(np.float32)

    def enc(pos):
        ang = pos.astype(np.float32)[:, None] * omega[None, :]
        return np.concatenate([np.sin(ang), np.cos(ang)], axis=-1)

    return jnp.asarray(np.concatenate([enc(t // grid_w), enc(t % grid_w)], axis=-1).astype(np.float32))


def _trunk_layer(x, mod, p, l, batch, seq, tb, states, final_w, pe=None):
    res = _inproj(x, mod, p, l, seq, pe)
    if pe is not None:
        gates, z, zs, x = res
    else:
        gates, z, zs = res
    if states is None:
        ml_st = s5_st = gd_st = None
    else:
        ml_st, s5_st, gd_st = states
    gmix, grows = _gate_prep(zs, p, l)
    ml = _mlstm(z, gmix, grows, batch, seq, tb, ml_st)
    s5 = _s5(z, p, l, batch, seq, s5_st)
    qkv = _gdn_prep(z, p, l, seq)
    gd = _gdn(qkv, gmix, grows, batch, seq, tb, gd_st)
    x = _merge(x, mod, gates, z, ml[0], s5[0], gd[0], p, l, seq)
    x = _ffn(x, mod, p, l, seq, final_w, l == DEPTH - 1)
    new_states = None
    if states is None:
        new_states = (ml[1][..., :DH], ml[1][..., DH], ml[2][:, :, :, 0, 0],
                      jnp.transpose(s5[1], (2, 0, 1, 3)).reshape(batch, N_DIR, S5_GROUPS, S5_STATE),
                      jnp.transpose(s5[2], (2, 0, 1, 3)).reshape(batch, N_DIR, S5_GROUPS, S5_STATE),
                      gd[1])
    return x, new_states


def kernel(x_prompt, x_sample, state_mlstm_C, state_mlstm_n, state_mlstm_m, state_s5_re, state_s5_im,
           state_gdn_S, c, c_ctx, ada_w, ada_b, norm1_w, w_in, ml_i_bias, ml_f_bias, ml_norm_w,
           s5_lam_re, s5_lam_im, s5_log_step, s5_B_re, s5_B_im, s5_C_re, s5_C_im, s5_D, s5_glu_w, s5_glu_b,
           gd_conv_w, gd_A_log, gd_dt_bias, gd_norm_w, w_branch, w_out, norm2_w, w_gate, w_up, w_down,
           final_norm_w):
    w = dict(norm1_w=norm1_w, w_in=w_in, ml_i_bias=ml_i_bias, ml_f_bias=ml_f_bias, ml_norm_w=ml_norm_w,
             s5_lam_re=s5_lam_re, s5_lam_im=s5_lam_im, s5_log_step=s5_log_step, s5_B_re=s5_B_re,
             s5_B_im=s5_B_im, s5_C_re=s5_C_re, s5_C_im=s5_C_im, s5_D=s5_D, s5_glu_w=s5_glu_w,
             s5_glu_b=s5_glu_b, gd_conv_w=gd_conv_w, gd_A_log=gd_A_log, gd_dt_bias=gd_dt_bias,
             gd_norm_w=gd_norm_w, w_branch=w_branch, w_out=w_out, norm2_w=norm2_w, w_gate=w_gate,
             w_up=w_up, w_down=w_down)
    bp, sp, _ = x_prompt.shape
    bs, ss, _ = x_sample.shape
    params = _prep_params(w)
    final_w = final_norm_w.reshape(1, D_MODEL)

    cc = jnp.concatenate([c_ctx[None, :], c, jnp.zeros((8 - 1 - bs, D_MODEL), F32)], axis=0)
    mods = _modulation(cc, ada_w, ada_b).reshape(DEPTH, 8, N_MOD, D_MODEL)

    xp = x_prompt.reshape(bp * sp, D_MODEL)
    per_layer = []
    for l in range(DEPTH):
        xp, st = _trunk_layer(xp, mods[l, 0:1], params, l, bp, sp, sp, None, final_w)
        per_layer.append(st)
    y_prompt = xp.reshape(bp, sp, D_MODEL)
    new_states = [jnp.stack([st[i] for st in per_layer], axis=1) for i in range(6)]

    xs = x_sample.reshape(bs * ss, D_MODEL)
    pe = _grid_pos_embed(ss)
    for l in range(DEPTH):
        n_cols = jnp.broadcast_to(state_mlstm_n[:, l][..., None], (bs, N_DIR, HEADS, DH, DH))
        ml_st = (jnp.concatenate([state_mlstm_C[:, l], n_cols], axis=-1),
                 jnp.broadcast_to(state_mlstm_m[:, l][:, :, :, None, None], (bs, N_DIR, HEADS, 1, DH)))
        s5_st = tuple(jnp.transpose(a[:, l].reshape(bs, N_DIR, S5_CB, 1, S5_CBW), (1, 2, 0, 3, 4))
                      for a in (state_s5_re, state_s5_im))
        xs, _ = _trunk_layer(xs, mods[l, 1:1 + bs], params, l, bs, ss, 512, (ml_st, s5_st, state_gdn_S[:, l]),
                             final_w, pe if l == 0 else None)
    y_sample = xs.reshape(bs, ss, D_MODEL)
    return (y_prompt, y_sample, *new_states)
```

```python
import functools
import math

import jax
import jax.numpy as jnp
import numpy as np
from jax import lax
from jax.experimental import pallas as pl
from jax.experimental.pallas import tpu as pltpu

F32 = jnp.float32
BF16 = jnp.bfloat16

D_MODEL = 1024
DEPTH = 2
N_DIR = 2
N_MOD = 6
EPS = 1e-6
HEADS = 4
DH = 128
BRANCH_W = HEADS * DH
CHUNK = 64
S5_GROUPS = 32
S5_GROUP = 16
S5_STATE = 64
S5_CB = 4
S5_CBW = S5_GROUPS * S5_STATE // S5_CB
S5_SUBLANES = 8
CONV_K = 5
D_FF = -(-8 * D_MODEL // (3 * 256)) * 256
IN_SIZES = (512, 512, 512, 512, 8, 8, 512, 1536, 512, 8, 8, 3072)
GATE_W = 128

GATES_W = 3 * D_MODEL
CB_Q, CB_K, CB_V, CB_O, CB_SU, CB_GQKV, CB_GZ = 0, 1, 2, 3, 4, 5, 8
Z_W = 9 * BRANCH_W

VMEM_LIMIT = 56 * 1024 * 1024


def _cparams(sem):
    return pltpu.CompilerParams(dimension_semantics=sem, vmem_limit_bytes=VMEM_LIMIT)


def _dot(a, b):
    return jnp.dot(a.astype(BF16), b.astype(BF16), preferred_element_type=F32)


def _dot_nt(a, b):
    return lax.dot_general(a.astype(BF16), b.astype(BF16), (((1,), (1,)), ((), ())),
                           preferred_element_type=F32)


def _dot_tn(a, b):
    return lax.dot_general(a.astype(BF16), b.astype(BF16), (((0,), (0,)), ((), ())),
                           preferred_element_type=F32)


def _split3(x):
    hi = x.astype(BF16)
    r1 = x - hi.astype(F32)
    mid = r1.astype(BF16)
    lo = (r1 - mid.astype(F32)).astype(BF16)
    return hi, mid, lo


def _sel_dot(sel, x, nt=False):
    dims = (((1,), (1,)), ((), ())) if nt else (((1,), (0,)), ((), ()))
    hi, mid, lo = _split3(x)
    f = lambda p: lax.dot_general(sel, p, dims, preferred_element_type=F32)
    return (f(hi) + f(mid)) + f(lo)


def _sigmoid(x):
    return 1.0 / (1.0 + jnp.exp(-x))


def _silu(x):
    return x * _sigmoid(x)


def _softplus(x):
    return jnp.maximum(x, 0.0) + jnp.log(1.0 + jnp.exp(-jnp.abs(x)))


def _gelu_tanh(x):
    c = math.sqrt(2.0 / math.pi)
    return 0.5 * x * (1.0 + jnp.tanh(c * (x + 0.044715 * (x * x * x))))


def _head_rms(x, w_row):
    outs = []
    for h in range(HEADS):
        xh = x[:, h * DH:(h + 1) * DH]
        outs.append(xh * lax.rsqrt(jnp.mean(xh * xh, axis=-1, keepdims=True) + EPS))
    return jnp.concatenate(outs, axis=-1) * w_row


def _dir_chunk(c, d, n):
    return c + d * (n - 1 - 2 * c)


def _mod_kernel(c_ref, w_ref, b_ref, o_ref):
    o_ref[0] = _dot(_silu(c_ref[...]), w_ref[0]) + b_ref[0]


def _modulation(cc, ada_w, ada_b):
    tn = 1536
    nmod = N_MOD * D_MODEL
    return pl.pallas_call(
        _mod_kernel,
        grid=(DEPTH, nmod // tn),
        in_specs=[pl.BlockSpec((8, D_MODEL), lambda l, j: (0, 0)),
                  pl.BlockSpec((1, D_MODEL, tn), lambda l, j: (l, 0, j)),
                  pl.BlockSpec((1, 1, tn), lambda l, j: (l, 0, j))],
        out_specs=pl.BlockSpec((1, 8, tn), lambda l, j: (l, 0, j)),
        out_shape=jax.ShapeDtypeStruct((DEPTH, 8, nmod), F32),
        name="adaln_mod",
        compiler_params=_cparams(("arbitrary", "arbitrary")),
    )(cc, ada_w, ada_b.reshape(DEPTH, 1, nmod))


def _inproj_kernel(*refs, has_pe, gate_tiles):
    if has_pe:
        x_ref, pe_ref, mod_ref, nw_ref, w_ref, ws_ref, gates_ref, z_ref, zs_ref, xs_ref, hn_scr = refs
    else:
        x_ref, mod_ref, nw_ref, w_ref, ws_ref, gates_ref, z_ref, zs_ref, hn_scr = refs
    j = pl.program_id(1)

    @pl.when(j == 0)
    def _():
        x = x_ref[...]
        if has_pe:
            x = x + pe_ref[...]
            xs_ref[...] = x
        y = x * lax.rsqrt(jnp.mean(x * x, axis=-1, keepdims=True) + EPS)
        h = (y * nw_ref[0]) * (1.0 + mod_ref[0, 1:2, :]) + mod_ref[0, 0:1, :]
        hb = h.astype(BF16)
        hn_scr[...] = hb
        zs_ref[...] = jnp.dot(hb, ws_ref[0], preferred_element_type=F32)

    @pl.when(j < gate_tiles)
    def _():
        gates_ref[...] = jnp.dot(hn_scr[...], w_ref[0], preferred_element_type=F32).astype(BF16)

    @pl.when(j >= gate_tiles)
    def _():
        z_ref[...] = jnp.dot(hn_scr[...], w_ref[0], preferred_element_type=F32)


def _inproj(x, mod, p, l, seq, pe=None):
    m = x.shape[0]
    has_pe = pe is not None
    tm, tn = (512 if has_pe else 1024), 1536
    gate_tiles = GATES_W // tn
    rows_per_mod = seq if mod.shape[0] > 1 else m
    row = lambda i, j: (i, 0)
    in_specs = [pl.BlockSpec((tm, D_MODEL), row)]
    args = [x]
    if has_pe:
        in_specs.append(pl.BlockSpec((tm, D_MODEL), lambda i, j: (i % (seq // tm), 0)))
        args.append(pe)
    in_specs += [pl.BlockSpec((1, N_MOD, D_MODEL), lambda i, j: (i * tm // rows_per_mod, 0, 0)),
                 pl.BlockSpec((1, 1, D_MODEL), lambda i, j: (l, 0, 0)),
                 pl.BlockSpec((1, D_MODEL, tn), lambda i, j: (l, 0, j)),
                 pl.BlockSpec((1, D_MODEL, N_DIR * GATE_W), lambda i, j: (l, 0, 0))]
    args += [mod, p['norm1_w'], p['w_big'], p['w_small']]
    out_specs = [pl.BlockSpec((tm, tn), lambda i, j: (i, jnp.minimum(j, gate_tiles - 1))),
                 pl.BlockSpec((tm, tn), lambda i, j: (i, jnp.maximum(j - gate_tiles, 0))),
                 pl.BlockSpec((tm, N_DIR * GATE_W), row)]
    out_shape = [jax.ShapeDtypeStruct((m, GATES_W), BF16),
                 jax.ShapeDtypeStruct((m, Z_W), F32),
                 jax.ShapeDtypeStruct((m, N_DIR * GATE_W), F32)]
    if has_pe:
        out_specs.append(pl.BlockSpec((tm, D_MODEL), row))
        out_shape.append(jax.ShapeDtypeStruct((m, D_MODEL), F32))
    return pl.pallas_call(
        functools.partial(_inproj_kernel, has_pe=has_pe, gate_tiles=gate_tiles),
        grid=(m // tm, (GATES_W + Z_W) // tn),
        in_specs=in_specs, out_specs=out_specs, out_shape=out_shape,
        scratch_shapes=[pltpu.VMEM((tm, D_MODEL), BF16)],
        name="inproj",
        compiler_params=_cparams(("arbitrary", "arbitrary")),
    )(*args)


def _mixer_rowblk(nt):
    return lambda b, d, j: b * nt + _dir_chunk(j, d, nt)


def _gate_selector():
    r = lax.broadcasted_iota(jnp.int32, (16, GATE_W), 0)
    c = lax.broadcasted_iota(jnp.int32, (16, GATE_W), 1)
    return (r == c).astype(BF16)


GATE_ROWS = 1024


def _gate_prep_kernel(g_ref, bias_ref, alog_ref, mix_ref, rows_ref):
    d = pl.program_id(0)
    pre = g_ref[...] + bias_ref[0, 0]
    lane = lax.broadcasted_iota(jnp.int32, pre.shape, 1)
    log_sig = -_softplus(-pre)
    vals = jnp.where(lane < 2 * HEADS, log_sig, -jnp.exp(alog_ref[0, 0]) * (pre - log_sig))
    r = lax.broadcasted_iota(jnp.int32, (HC, HC), 0)
    c = lax.broadcasted_iota(jnp.int32, (HC, HC), 1)
    ordered = jnp.logical_and((r >> CHUNK_LG) == (c >> CHUNK_LG), (r - c) * (1 - 2 * d) >= 0)
    tri = ordered.astype(BF16)
    cum = jnp.concatenate([_sel_dot(tri, vals[g * HC:(g + 1) * HC, :]) for g in range(GATE_ROWS // HC)], axis=0)
    mix = jnp.where(lane < HEADS, pre, jnp.where(lane < 3 * HEADS, cum, jnp.exp(log_sig)))
    mix_ref[0] = mix
    xt = _sel_dot(_gate_selector(), mix, nt=True)
    for c in range(GATE_ROWS // CHUNK):
        cs = slice(c * CHUNK, (c + 1) * CHUNK)
        rows_ref[0, c, 0:1, :] = jnp.concatenate(
            [xt[h:h + 1, cs] - xt[HEADS + h:HEADS + h + 1, cs] for h in range(HEADS)], axis=1)
        rows_ref[0, c, 1:2, :] = jnp.concatenate(
            [xt[2 * HEADS + h:2 * HEADS + h + 1, cs] for h in range(HEADS)], axis=1)


def _gate_prep(zs, p, l):
    m = zs.shape[0]
    dirrow = pl.BlockSpec((1, 1, 1, GATE_W), lambda d, i: (l, d, 0, 0))
    return pl.pallas_call(
        _gate_prep_kernel,
        grid=(N_DIR, m // GATE_ROWS),
        in_specs=[pl.BlockSpec((GATE_ROWS, GATE_W), lambda d, i: (i, d)), dirrow, dirrow],
        out_specs=[pl.BlockSpec((1, GATE_ROWS, GATE_W), lambda d, i: (d, i, 0)),
                   pl.BlockSpec((1, GATE_ROWS // CHUNK, 2, HC), lambda d, i: (d, i, 0, 0))],
        out_shape=[jax.ShapeDtypeStruct((N_DIR, m, GATE_W), F32),
                   jax.ShapeDtypeStruct((N_DIR, m // CHUNK, 2, HC), F32)],
        name="gate_prep",
        compiler_params=_cparams(("arbitrary", "arbitrary")),
    )(zs, p['gate_bias'], p['gate_alog'])


def _mlstm_kernel(*refs, nch, zero_init, emit_state):
    q_ref, k_ref, v_ref, mix_ref, rows_ref = refs[:5]
    pos = 5
    if not zero_init:
        c0_ref, m0_ref = refs[pos:pos + 2]
        pos += 2
    h_ref = refs[pos]
    pos += 1
    if emit_state:
        co_ref, mo_ref = refs[pos:pos + 2]
        pos += 2
    c_scr, m_scr, cls_all, dense_scr, av_scr, ml_scr = refs[pos:pos + 6]

    d = pl.program_id(1)
    j = pl.program_id(2)

    @pl.when(j == 0)
    def _():
        if zero_init:
            c_scr[...] = jnp.zeros_like(c_scr)
            m_scr[...] = jnp.zeros_like(m_scr)
        else:
            c_scr[...] = c0_ref[0, 0]
            m_scr[...] = m0_ref[0, 0]

    _fill_pair_classes(cls_all)
    cls_scr = cls_all.at[d]
    rowid = lax.broadcasted_iota(jnp.int32, (CHUNK, DH), 0)
    last = (CHUNK - 1) * (1 - d)
    scale = DH ** -0.5
    hrows = lambda h: slice(h * CHUNK, (h + 1) * CHUNK)
    per_head = lambda vals: jnp.concatenate([jnp.broadcast_to(a, (CHUNK, a.shape[1])) for a in vals], axis=0)
    twice = lambda a: jnp.concatenate([a, a], axis=1)
    ones = jnp.ones((HC, DH), F32)

    mixed = mix_ref[0]
    for h in range(HEADS):
        dense_scr[h, :, 0:DH] = jnp.broadcast_to(mixed[:, HEADS + h:HEADS + h + 1], (nch * CHUNK, DH))
        dense_scr[h, :, DH:2 * DH] = jnp.broadcast_to(mixed[:, h:h + 1], (nch * CHUNK, DH))

    stacked = lambda rows, lo: jnp.concatenate([dense_scr[h, rows, lo:lo + DH] for h in range(HEADS)], axis=0)
    npar = MLSTM_LOCKSTEP

    def local(cp, carry):
        cs = [cp * npar + i for i in range(npar)]
        rows = [pl.ds(pl.multiple_of(c * CHUNK, CHUNK), CHUNK) for c in cs]
        cls = cls_scr[...]
        log_d = [jnp.where(cls > 0, twice(stacked(r, 0)) + rows_ref[0, c, 0:1, :], -jnp.inf)
                 for r, c in zip(rows, cs)]
        ml = [jnp.max(a, axis=-1, keepdims=True) for a in log_d]
        q = [_stack_heads(q_ref[r, :]) for r in rows]
        k = [_stack_heads(k_ref[r, :]) * scale for r in rows]
        v = [jnp.concatenate([_stack_heads(v_ref[r, :]), ones], axis=1) for r in rows]
        s = [_dot_nt(a, b) * jnp.exp(ld - m) for a, b, ld, m in zip(q, k, log_d, ml)]
        av = [_dot(a, b) for a, b in zip(s, v)]
        for i, c in enumerate(cs):
            av_scr[c] = av[i]
            ml_scr[c] = jnp.broadcast_to(ml[i], (HC, DH))
        return carry

    lax.fori_loop(0, nch // npar, local, 0)

    def advance(c, carry):
        ci = _dir_chunk(c, d, nch)
        r0 = pl.multiple_of(ci * CHUNK, CHUNK)
        rows = pl.ds(r0, CHUNK)
        b_tok = stacked(rows, 0)
        i_tok = stacked(rows, DH)
        at_last = lambda a: [jnp.sum(jnp.where(rowid == last, a[hrows(h), :], 0.0), axis=0, keepdims=True)
                             for h in range(HEADS)]
        b_last = at_last(b_tok)
        m_prev = [m_scr[h] for h in range(HEADS)]
        ml = ml_scr[ci]
        log_0 = b_tok + per_head(m_prev)
        m_t = jnp.maximum(log_0, ml)
        w_0 = jnp.exp(log_0 - m_t)
        f = jnp.exp(ml - m_t)
        q = _stack_heads(q_ref[rows, :])
        k = _stack_heads(k_ref[rows, :]) * scale
        v = jnp.concatenate([_stack_heads(v_ref[rows, :]), ones], axis=1)
        cst = [c_scr[h] for h in range(HEADS)]
        qc = jnp.concatenate([_dot(q[hrows(h), :], cst[h]) for h in range(HEADS)], axis=0)
        num = f * av_scr[ci, :, 0:DH] + w_0 * qc[:, 0:DH]
        den = f * av_scr[ci, :, DH:2 * DH] + w_0 * qc[:, DH:2 * DH]
        hv = num / jnp.maximum(jnp.abs(den), jnp.exp(-m_t))
        m_new = at_last(m_t)
        kw = k * jnp.exp(per_head(b_last) - b_tok + i_tok - per_head(m_new))
        for h in range(HEADS):
            h_ref[0, rows, h * DH:(h + 1) * DH] = hv[hrows(h), :].astype(BF16)
            c_0 = jnp.exp(b_last[h] + m_prev[h] - m_new[h])
            c_scr[h] = twice(c_0) * cst[h] + _dot_tn(kw[hrows(h), :], v[hrows(h), :])
            m_scr[h] = m_new[h]
        return carry

    lax.fori_loop(0, nch, advance, 0)

    if emit_state:
        @pl.when(j == pl.num_programs(2) - 1)
        def _():
            co_ref[0, 0] = c_scr[...]
            mo_ref[0, 0] = m_scr[...]


def _gate_specs(rb, tb):
    return [pl.BlockSpec((1, tb, GATE_W), lambda b, d, j: (d, rb(b, d, j), 0)),
            pl.BlockSpec((1, tb // CHUNK, 2, HC), lambda b, d, j: (d, rb(b, d, j), 0, 0))]


def _mlstm(z, gmix, grows, batch, seq, tb, states):
    m = batch * seq
    nt = seq // tb
    nch = tb // CHUNK
    zero_init = states is None
    emit_state = states is None
    rb = _mixer_rowblk(nt)
    zspec = lambda cb: pl.BlockSpec((tb, BRANCH_W), lambda b, d, j: (rb(b, d, j), cb))
    st5 = lambda shape: pl.BlockSpec((1, 1) + shape, lambda b, d, j: (b, d) + (0,) * len(shape))
    in_specs = [zspec(CB_Q), zspec(CB_K), zspec(CB_V)] + _gate_specs(rb, tb)
    args = [z, z, z, gmix, grows]
    if not zero_init:
        in_specs += [st5((HEADS, DH, 2 * DH)), st5((HEADS, 1, DH))]
        args += list(states)
    out_specs = [pl.BlockSpec((1, tb, BRANCH_W), lambda b, d, j: (d, rb(b, d, j), 0))]
    out_shape = [jax.ShapeDtypeStruct((N_DIR, m, BRANCH_W), BF16)]
    if emit_state:
        out_specs += [st5((HEADS, DH, 2 * DH)), st5((HEADS, 1, DH))]
        out_shape += [jax.ShapeDtypeStruct((batch, N_DIR, HEADS, DH, 2 * DH), F32),
                      jax.ShapeDtypeStruct((batch, N_DIR, HEADS, 1, DH), F32)]
    return pl.pallas_call(
        functools.partial(_mlstm_kernel, nch=nch, zero_init=zero_init, emit_state=emit_state),
        grid=(batch, N_DIR, nt),
        in_specs=in_specs, out_specs=out_specs, out_shape=out_shape,
        scratch_shapes=[pltpu.VMEM((HEADS, DH, 2 * DH), F32), pltpu.VMEM((HEADS, 1, DH), F32),
                        pltpu.VMEM((N_DIR, HC, HC), jnp.int32),
                        pltpu.VMEM((HEADS, tb, 2 * DH), F32),
                        pltpu.VMEM((nch, HC, 2 * DH), F32), pltpu.VMEM((nch, HC, DH), F32)],
        name="mlstm",
        compiler_params=_cparams(("arbitrary", "arbitrary", "arbitrary")),
    )(*args)


def _gdn_prep_kernel(x_ref, prev_ref, next_ref, w_ref, o_ref, xe_scr, *, tb, seq):
    i = pl.program_id(0)
    p = pl.program_id(1)
    pad = 8
    at_start = (i * tb) % seq == 0
    at_end = ((i + 1) * tb) % seq == 0
    xe_scr[0:pad, :] = jnp.where(at_start, 0.0, prev_ref[...])
    xe_scr[pad:pad + tb, :] = x_ref[...]
    xe_scr[pad + tb:pad + tb + pad, :] = jnp.where(at_end, 0.0, next_ref[...])
    acc = None
    for t in range(CONV_K):
        term = xe_scr[pl.ds(pad + t - CONV_K // 2, tb), :] * w_ref[0, t:t + 1, :]
        acc = term if acc is None else acc + term
    y = _silu(acc)
    outs = []
    for h in range(HEADS):
        yh = y[:, h * DH:(h + 1) * DH]
        outs.append(yh * lax.rsqrt(jnp.sum(yh * yh, axis=-1, keepdims=True) + EPS))
    yn = jnp.concatenate(outs, axis=-1)
    qscale = jnp.where(p == 0, DH ** -0.5, 1.0)
    o_ref[...] = jnp.where(p == 2, y, yn * qscale)


def _gdn_prep(z, p, l, seq):
    m = z.shape[0]
    tb = 256
    nb8 = m // 8
    return pl.pallas_call(
        functools.partial(_gdn_prep_kernel, tb=tb, seq=seq),
        grid=(m // tb, 3),
        in_specs=[pl.BlockSpec((tb, BRANCH_W), lambda i, part: (i, CB_GQKV + part)),
                  pl.BlockSpec((8, BRANCH_W), lambda i, part: (jnp.maximum(i * (tb // 8) - 1, 0), CB_GQKV + part)),
                  pl.BlockSpec((8, BRANCH_W), lambda i, part: (jnp.minimum((i + 1) * (tb // 8), nb8 - 1), CB_GQKV + part)),
                  pl.BlockSpec((1, CONV_K, BRANCH_W), lambda i, part: (l, 0, part))],
        out_specs=pl.BlockSpec((tb, BRANCH_W), lambda i, part: (i, part)),
        out_shape=jax.ShapeDtypeStruct((m, 3 * BRANCH_W), F32),
        scratch_shapes=[pltpu.VMEM((tb + 16, BRANCH_W), F32)],
        name="gdn_prep",
        compiler_params=_cparams(("arbitrary", "arbitrary")),
    )(z, z, z, p['gd_conv_w'])


HC = HEADS * CHUNK


def _stack_heads(x):
    return jnp.concatenate([x[:, h * DH:(h + 1) * DH] for h in range(HEADS)], axis=0)


def _stack_gate(x, lane0):
    return jnp.concatenate([x[:, lane0 + h:lane0 + h + 1] for h in range(HEADS)], axis=0)


GDN_LOCKSTEP = 4
MLSTM_LOCKSTEP = 4
BASE_LG = 3
CHUNK_LG = int(math.log2(CHUNK))


def _pair_classes(d):
    r = lax.broadcasted_iota(jnp.int32, (HC, HC), 0)
    c = lax.broadcasted_iota(jnp.int32, (HC, HC), 1)
    cls = jnp.full((HC, HC), CHUNK_LG, jnp.int32)
    for s in range(CHUNK_LG - 1, BASE_LG - 1, -1):
        cls = jnp.where((r >> s) == (c >> s), s, cls)
    diff = (r - c) * (1 - 2 * d)
    cls = jnp.where(diff == 0, 1, cls)
    return jnp.where(jnp.logical_and((r >> CHUNK_LG) == (c >> CHUNK_LG), diff >= 0), cls, 0)


def _fill_pair_classes(cls_all):
    first = functools.reduce(jnp.logical_and, [pl.program_id(i) == 0 for i in range(3)])

    @pl.when(first)
    def _():
        for d in range(N_DIR):
            cls_all[d] = _pair_classes(d)


def _unit_tri_solve(xs, cls, rhss):
    n = xs[0].shape[0]
    nr = rhss[0].shape[1]
    x0 = [jnp.where(cls == BASE_LG, x, 0.0) for x in xs]
    x2 = [_dot(a, a) for a in x0]
    y = [_dot(b, jnp.concatenate([b, a], axis=1)) for a, b in zip(x0, x2)]
    x4 = [t[:, :n] for t in y]
    q2 = [a + b + t[:, n:] for a, b, t in zip(x0, x2, y)]
    t8m = [a + b + _dot(b, a) for a, b in zip(q2, x4)]
    levels = range(BASE_LG + 1, CHUNK_LG + 1)
    cur = [jnp.concatenate([r] + [jnp.where(cls == s, x, 0.0) for s in levels], axis=1) for x, r in zip(xs, rhss)]
    cur = [a + _dot(t, a) for a, t in zip(cur, t8m)]
    for _ in levels:
        last = cur[0].shape[1] == nr + n
        rest = [a[:, :nr] if last else jnp.concatenate([a[:, :nr], a[:, nr + n:]], axis=1) for a in cur]
        cur = [r + _dot(a[:, nr:nr + n], r) for a, r in zip(cur, rest)]
    return cur


def _gdn_kernel(*refs, nch, zero_init, emit_state):
    q_ref, k_ref, v_ref, mix_ref, rows_ref = refs[:5]
    pos = 5
    if not zero_init:
        s0_ref = refs[pos]
        pos += 1
    o_ref = refs[pos]
    pos += 1
    if emit_state:
        so_ref = refs[pos]
        pos += 1
    s_scr, cls_all, uw_scr, at_scr, qg_scr, kd_scr, gt_scr = refs[pos:pos + 7]

    d = pl.program_id(1)
    j = pl.program_id(2)

    @pl.when(j == 0)
    def _():
        if zero_init:
            s_scr[...] = jnp.zeros_like(s_scr)
        else:
            s_scr[...] = s0_ref[0, 0]

    _fill_pair_classes(cls_all)
    cls_scr = cls_all.at[d]
    last = (CHUNK - 1) * (1 - d)
    g0 = 2 * HEADS
    npar = GDN_LOCKSTEP

    def prepare(cp, carry):
        cs = [cp * npar + i for i in range(npar)]
        r0 = [pl.multiple_of(c * CHUNK, CHUNK) for c in cs]
        rows = [pl.ds(r, CHUNK) for r in r0]
        mix = mix_ref.at[0]
        gtot = [mix[pl.ds(r + last, 1), :] for r in r0]
        g_col = [_stack_gate(mix[r, :], g0) for r in rows]
        beta = [_stack_gate(mix[r, :], g0 + HEADS) for r in rows]
        g_last = [jnp.concatenate([jnp.broadcast_to(t[:, g0 + h:g0 + h + 1], (CHUNK, 1)) for h in range(HEADS)],
                                  axis=0) for t in gtot]
        cls = cls_scr[...]
        decay = [jnp.exp(jnp.where(cls > 0, gc - rows_ref[0, c, 1:2, :], -jnp.inf)) for gc, c in zip(g_col, cs)]
        eg = [jnp.exp(gc) for gc in g_col]
        q = [_stack_heads(q_ref[r, :]) for r in rows]
        k = [_stack_heads(k_ref[r, :]) for r in rows]
        v = [_stack_heads(v_ref[r, :]) for r in rows]
        kb = [a * b for a, b in zip(k, beta)]
        x = [-(_dot_nt(a, b) * dc) for a, b, dc in zip(kb, k, decay)]
        rhs = [jnp.concatenate([a * b, kbi * e], axis=-1) for a, b, kbi, e in zip(v, beta, kb, eg)]
        uw = _unit_tri_solve(x, cls, rhs)
        attn = [_dot_nt(a, b) * dc for a, b, dc in zip(q, k, decay)]
        for i, c in enumerate(cs):
            uw_scr[c] = uw[i]
            at_scr[c] = attn[i]
            qg_scr[c] = q[i] * eg[i]
            kd_scr[c] = k[i] * jnp.exp(g_last[i] - g_col[i])
            gt_scr[c] = gtot[i]
        return carry

    lax.fori_loop(0, nch // npar, prepare, 0)

    def advance(c, carry):
        ci = _dir_chunk(c, d, nch)
        rows = pl.ds(pl.multiple_of(ci * CHUNK, CHUNK), CHUNK)
        hrows = lambda h: slice(h * CHUNK, (h + 1) * CHUNK)
        st = [s_scr[h] for h in range(HEADS)]
        v_new = [uw_scr[ci, hrows(h), 0:DH] - _dot(uw_scr[ci, hrows(h), DH:2 * DH], st[h]) for h in range(HEADS)]
        qs = jnp.concatenate([_dot(qg_scr[ci, hrows(h), :], st[h]) for h in range(HEADS)], axis=0)
        o = qs + _dot(at_scr[ci], jnp.concatenate(v_new, axis=0))
        gtot = gt_scr[ci]
        for h in range(HEADS):
            o_ref[0, rows, h * DH:(h + 1) * DH] = o[hrows(h), :].astype(BF16)
            s_scr[h] = st[h] * jnp.exp(gtot[:, g0 + h:g0 + h + 1]) + _dot_tn(kd_scr[ci, hrows(h), :], v_new[h])
        return carry

    lax.fori_loop(0, nch, advance, 0)

    if emit_state:
        @pl.when(j == pl.num_programs(2) - 1)
        def _():
            so_ref[0, 0] = s_scr[...]


def _gdn(qkv, gmix, grows, batch, seq, tb, s0):
    m = batch * seq
    nt = seq // tb
    nch = tb // CHUNK
    zero_init = s0 is None
    emit_state = s0 is None
    rb = _mixer_rowblk(nt)
    spec = lambda cb: pl.BlockSpec((tb, BRANCH_W), lambda b, d, j: (rb(b, d, j), cb))
    st = pl.BlockSpec((1, 1, HEADS, DH, DH), lambda b, d, j: (b, d, 0, 0, 0))
    in_specs = [spec(0), spec(1), spec(2)] + _gate_specs(rb, tb)
    args = [qkv, qkv, qkv, gmix, grows]
    if not zero_init:
        in_specs.append(st)
        args.append(s0)
    out_specs = [pl.BlockSpec((1, tb, BRANCH_W), lambda b, d, j: (d, rb(b, d, j), 0))]
    out_shape = [jax.ShapeDtypeStruct((N_DIR, m, BRANCH_W), BF16)]
    if emit_state:
        out_specs.append(st)
        out_shape.append(jax.ShapeDtypeStruct((batch, N_DIR, HEADS, DH, DH), F32))
    return pl.pallas_call(
        functools.partial(_gdn_kernel, nch=nch, zero_init=zero_init, emit_state=emit_state),
        grid=(batch, N_DIR, nt),
        in_specs=in_specs, out_specs=out_specs, out_shape=out_shape,
        scratch_shapes=[pltpu.VMEM((HEADS, DH, DH), F32), pltpu.VMEM((N_DIR, HC, HC), jnp.int32),
                        pltpu.VMEM((nch, HC, 2 * DH), F32), pltpu.VMEM((nch, HC, HC), F32),
                        pltpu.VMEM((nch, HC, DH), F32), pltpu.VMEM((nch, HC, DH), F32),
                        pltpu.VMEM((nch, 1, GATE_W), F32)],
        name="gdn",
        compiler_params=_cparams(("arbitrary", "arbitrary", "arbitrary")),
    )(*args)


def _s5_discretise(lre, lim, ls):
    dt = jnp.exp(ls)
    mag = jnp.exp(lre * dt)
    ar = mag * jnp.cos(lim * dt)
    ai = mag * jnp.sin(lim * dt)
    den = lre * lre + lim * lim
    nr = ar - 1.0
    return ar, ai, (nr * lre + ai * lim) / den, (ai * lre - nr * lim) / den


def _s5_kernel(*refs, rb, seg, chain):
    u_ref, b_ref, c_ref, lre_ref, lim_ref, ls_ref = refs[:6]
    pos = 6
    if chain:
        h0r_ref, h0i_ref = refs[pos:pos + 2]
        pos += 2
    y_ref = refs[pos]
    pos += 1
    if not chain:
        sr_ref, si_ref = refs[pos:pos + 2]
        pos += 2
    up, yp, xr, xi, er, ei, pr, pi_ = refs[pos:pos + 8]

    w = S5_CBW
    nsub = S5_SUBLANES
    rc = 256
    ngrp = seg // nsub
    par = [_s5_discretise(lre_ref[0, d, 0], lim_ref[0, d, 0], ls_ref[0, d, 0]) for d in range(N_DIR)]

    def interleave(g, carry):
        for s in range(nsub):
            src = pl.ds(pl.multiple_of(s * seg + g * nsub, nsub), nsub)
            up[pl.ds(g * nsub * nsub + s, nsub, stride=nsub), :] = u_ref[src, :]
        return carry

    lax.fori_loop(0, ngrp, interleave, 0)

    def fill(c, carry):
        rows = pl.ds(pl.multiple_of(c * rc, rc), rc)
        bu = _dot(up[rows, :], b_ref[0, 0])
        bre = bu[:, :w]
        bim = bu[:, w:]
        for d in range(N_DIR):
            _, _, zr, zi = par[d]
            xr[d, rows, :] = zr * bre - zi * bim
            xi[d, rows, :] = zr * bim + zi * bre
        return carry

    lax.fori_loop(0, rb // rc, fill, 0, unroll=2)

    ab = [(jnp.broadcast_to(p[0], (nsub, w)), jnp.broadcast_to(p[1], (nsub, w))) for p in par]

    def step_rows(tt, d):
        t = tt if d == 0 else seg - 1 - tt
        return pl.ds(pl.multiple_of(t * nsub, nsub), nsub)

    def scan_step(tt, carry):
        out = []
        for d in range(N_DIR):
            hr, hi = carry[d]
            arb, aib = ab[d]
            idx = step_rows(tt, d)
            nhr = arb * hr - aib * hi + xr[d, idx, :]
            nhi = arb * hi + aib * hr + xi[d, idx, :]
            xr[d, idx, :] = nhr
            xi[d, idx, :] = nhi
            out.append((nhr, nhi))
        return tuple(out)

    zero = jnp.zeros((nsub, w), F32)
    ends = lax.fori_loop(0, seg, scan_step, ((zero, zero), (zero, zero)), unroll=2)

    if chain:
        for d in range(N_DIR):
            ar, ai = par[d][0], par[d][1]
            er[d] = ends[d][0]
            ei[d] = ends[d][1]
            sr, si = ar, ai
            for _ in range(int(math.log2(seg))):
                sr, si = sr * sr - si * si, 2.0 * sr * si
            fr = h0r_ref[d, 0, 0]
            fi = h0i_ref[d, 0, 0]
            for k in range(nsub):
                row = k if d == 0 else nsub - 1 - k
                pr[d, row:row + 1, :] = fr
                pi_[d, row:row + 1, :] = fi
                fr, fi = (er[d, row:row + 1, :] + (sr * fr - si * fi),
                          ei[d, row:row + 1, :] + (sr * fi + si * fr))

        def fix_step(tt, carry):
            out = []
            for d in range(N_DIR):
                cr, ci = carry[d]
                arb, aib = ab[d]
                idx = step_rows(tt, d)
                ncr = arb * cr - aib * ci
                nci = arb * ci + aib * cr
                xr[d, idx, :] = xr[d, idx, :] + ncr
                xi[d, idx, :] = xi[d, idx, :] + nci
                out.append((ncr, nci))
            return tuple(out)

        lax.fori_loop(0, seg, fix_step, tuple((pr[d], pi_[d]) for d in range(N_DIR)), unroll=2)
    else:
        for d in range(N_DIR):
            sr_ref[d, 0] = ends[d][0]
            si_ref[d, 0] = ends[d][1]

    def proj(c, carry):
        rows = pl.ds(pl.multiple_of(c * rc, rc), rc)
        yp[rows, :] = (_dot(xr[0, rows, :] + xr[1, rows, :], c_ref[0, 0, :w, :])
                       + _dot(xi[0, rows, :] + xi[1, rows, :], c_ref[0, 0, w:, :]))
        return carry

    lax.fori_loop(0, rb // rc, proj, 0, unroll=4)

    def deinterleave(g, carry):
        for s in range(nsub):
            dst = pl.ds(pl.multiple_of(s * seg + g * nsub, nsub), nsub)
            y_ref[dst, :] = yp[pl.ds(g * nsub * nsub + s, nsub, stride=nsub), :]
        return carry

    lax.fori_loop(0, ngrp, deinterleave, 0)


def _s5(z, p, l, batch, seq, h0):
    m = batch * seq
    nsub = S5_SUBLANES
    chain = h0 is not None
    if chain:
        seg = seq // nsub
        ng = batch
        st_spec = pl.BlockSpec((N_DIR, 1, 1, 1, S5_CBW), lambda g, cb: (0, cb, g, 0, 0))
    else:
        seg = seq
        ng = batch // nsub
        st_spec = pl.BlockSpec((N_DIR, 1, nsub, S5_CBW), lambda g, cb: (0, cb, g, 0))
    rb = nsub * seg
    lam_spec = pl.BlockSpec((1, N_DIR, 1, 1, S5_CBW), lambda g, cb: (l, 0, cb, 0, 0))
    in_specs = [pl.BlockSpec((rb, 128), lambda g, cb: (g, CB_SU * 4 + cb)),
                pl.BlockSpec((1, 1, 128, 2 * S5_CBW), lambda g, cb: (l, cb, 0, 0)),
                pl.BlockSpec((1, 1, 2 * S5_CBW, 128), lambda g, cb: (l, cb, 0, 0)),
                lam_spec, lam_spec, lam_spec]
    args = [z, p['bbd'], p['cbd'], p['lam_re'], p['lam_im'], p['log_step']]
    if chain:
        in_specs += [st_spec, st_spec]
        args += list(h0)
    out_specs = [pl.BlockSpec((rb, 128), lambda g, cb: (g, cb))]
    out_shape = [jax.ShapeDtypeStruct((m, BRANCH_W), F32)]
    if not chain:
        out_specs += [st_spec, st_spec]
        out_shape += [jax.ShapeDtypeStruct((N_DIR, S5_CB, batch, S5_CBW), F32)] * 2
    vdir = lambda n: pltpu.VMEM((N_DIR, n, S5_CBW), F32)
    lanes = pltpu.VMEM((rb, 128), F32)
    return pl.pallas_call(
        functools.partial(_s5_kernel, rb=rb, seg=seg, chain=chain),
        grid=(ng, S5_CB),
        in_specs=in_specs, out_specs=out_specs, out_shape=out_shape,
        scratch_shapes=[lanes, lanes, vdir(rb), vdir(rb), vdir(nsub), vdir(nsub), vdir(nsub), vdir(nsub)],
        name="s5",
        compiler_params=_cparams(("arbitrary", "arbitrary")),
    )(*args)


def _merge_kernel(x_ref, mod_ref, gates_ref, o_ref, u_ref, gz_ref, hm_ref, ys_ref, go_ref,
                  mlw_ref, s5d_ref, gluw_ref, glub_ref, gdw_ref, wbr_ref, wout_ref, out_ref):
    both = lambda ref: ref[0].astype(F32) + ref[1].astype(F32)
    ya = _sigmoid(o_ref[...]) * _head_rms(both(hm_ref), mlw_ref[0])
    y5 = _gelu_tanh(ys_ref[...] + s5d_ref[0] * u_ref[...])
    yb = y5 * _sigmoid(_dot(y5, gluw_ref[0]) + glub_ref[0])
    yc = _head_rms(both(go_ref), gdw_ref[0]) * _silu(gz_ref[...])
    merged = None
    for n, y in enumerate((ya, yb, yc)):
        gate = _sigmoid(gates_ref[:, n * D_MODEL:(n + 1) * D_MODEL].astype(F32))
        term = gate * _dot(y, wbr_ref[0, n])
        merged = term if merged is None else merged + term
    out_ref[...] = x_ref[...] + mod_ref[0, 2:3, :] * _dot(merged, wout_ref[0])


def _merge(x, mod, gates, z, hm, ys, go, p, l, seq):
    m = x.shape[0]
    tm = 256
    rows_per_mod = seq if mod.shape[0] > 1 else m
    row = lambda i: (i, 0)
    zspec = lambda cb: pl.BlockSpec((tm, BRANCH_W), lambda i: (i, cb))
    dspec = pl.BlockSpec((N_DIR, tm, BRANCH_W), lambda i: (0, i, 0))
    full = lambda a: pl.BlockSpec((1,) + a.shape[1:], lambda i: (l,) + (0,) * (a.ndim - 1))
    consts = [p['ml_norm_w'], p['s5_D'], p['s5_glu_w'], p['s5_glu_b'], p['gd_norm_w'], p['w_branch'], p['w_out']]
    return pl.pallas_call(
        _merge_kernel,
        grid=(m // tm,),
        in_specs=[pl.BlockSpec((tm, D_MODEL), row),
                  pl.BlockSpec((1, N_MOD, D_MODEL), lambda i: (i * tm // rows_per_mod, 0, 0)),
                  pl.BlockSpec((tm, GATES_W), row),
                  zspec(CB_O), zspec(CB_SU), zspec(CB_GZ), dspec, pl.BlockSpec((tm, BRANCH_W), row), dspec]
                 + [full(a) for a in consts],
        out_specs=pl.BlockSpec((tm, D_MODEL), row),
        out_shape=jax.ShapeDtypeStruct((m, D_MODEL), F32),
        name="merge",
        compiler_params=_cparams(("arbitrary",)),
    )(x, mod, gates, z, z, z, hm, ys, go, *consts)


def _ffn_kernel(x_ref, mod_ref, nw_ref, wg_ref, wu_ref, wd_ref, fw_ref, out_ref, h_scr, acc_scr, *, final):
    jf = pl.program_id(1)

    @pl.when(jf == 0)
    def _():
        x = x_ref[...]
        y = x * lax.rsqrt(jnp.mean(x * x, axis=-1, keepdims=True) + EPS)
        h_scr[...] = ((y * nw_ref[0]) * (1.0 + mod_ref[0, 4:5, :]) + mod_ref[0, 3:4, :]).astype(BF16)
        acc_scr[...] = jnp.zeros_like(acc_scr)

    hb = h_scr[...]
    a = _silu(jnp.dot(hb, wg_ref[0], preferred_element_type=F32))
    b = jnp.dot(hb, wu_ref[0], preferred_element_type=F32)
    acc_scr[...] += _dot(a * b, wd_ref[0])

    @pl.when(jf == pl.num_programs(1) - 1)
    def _():
        x = x_ref[...] + mod_ref[0, 5:6, :] * acc_scr[...]
        if final:
            x = x * lax.rsqrt(jnp.mean(x * x, axis=-1, keepdims=True) + EPS) * fw_ref[...]
        out_ref[...] = x


def _ffn(x, mod, p, l, seq, final_w, final):
    m = x.shape[0]
    tm, tf = 1024, D_FF // 2
    rows_per_mod = seq if mod.shape[0] > 1 else m
    row = lambda i, jf: (i, 0)
    vec = pl.BlockSpec((1, D_MODEL), lambda i, jf: (0, 0))
    return pl.pallas_call(
        functools.partial(_ffn_kernel, final=final),
        grid=(m // tm, D_FF // tf),
        in_specs=[pl.BlockSpec((tm, D_MODEL), row),
                  pl.BlockSpec((1, N_MOD, D_MODEL), lambda i, jf: (i * tm // rows_per_mod, 0, 0)),
                  pl.BlockSpec((1, 1, D_MODEL), lambda i, jf: (l, 0, 0)),
                  pl.BlockSpec((1, D_MODEL, tf), lambda i, jf: (l, 0, jf)),
                  pl.BlockSpec((1, D_MODEL, tf), lambda i, jf: (l, 0, jf)),
                  pl.BlockSpec((1, tf, D_MODEL), lambda i, jf: (l, jf, 0)),
                  vec],
        out_specs=pl.BlockSpec((tm, D_MODEL), row),
        out_shape=jax.ShapeDtypeStruct((m, D_MODEL), F32),
        scratch_shapes=[pltpu.VMEM((tm, D_MODEL), BF16), pltpu.VMEM((tm, D_MODEL), F32)],
        name="ffn",
        compiler_params=_cparams(("arbitrary", "arbitrary")),
    )(x, mod, p['norm2_w'], p['w_gate'], p['w_up'], p['w_down'], final_w)


def _gate_lanes(parts):
    row = jnp.concatenate([a.astype(F32) for a in parts], axis=-1)
    return jnp.pad(row, ((0, 0), (0, 0), (0, GATE_W - row.shape[-1]))).reshape(DEPTH, N_DIR, 1, GATE_W)


def _block_diag(a):
    dp, cb, g, r, c = a.shape
    eye = jnp.eye(g, dtype=a.dtype)
    return jnp.einsum('lbgrc,gh->lbgrhc', a, eye).reshape(dp, cb, g * r, g * c)


def _prep_params(w):
    gpb = S5_GROUPS // S5_CB
    idx, acc = [], 0
    for size in IN_SIZES[:-1]:
        acc += size
        idx.append(acc)
    mq, mk, mv, mo, mi, mf, su, gqkv, gz, ga, gb, gates = jnp.split(w['w_in'], idx, axis=-1)
    w_big = jnp.concatenate([gates, mq, mk, mv, mo, su, gqkv, gz], axis=-1).astype(BF16)
    smalls = []
    for d in range(N_DIR):
        sl = slice(d * HEADS, (d + 1) * HEADS)
        blk = jnp.concatenate([mi[..., sl], mf[..., sl], ga[..., sl], gb[..., sl]], axis=-1)
        smalls.append(jnp.pad(blk, ((0, 0), (0, 0), (0, GATE_W - 4 * HEADS))))
    w_small = jnp.concatenate(smalls, axis=-1).astype(BF16)
    zeros = jnp.zeros((DEPTH, N_DIR, HEADS), F32)
    gate_bias = _gate_lanes([w['ml_i_bias'], w['ml_f_bias'], w['gd_dt_bias'], zeros])
    gate_alog = _gate_lanes([zeros, zeros, w['gd_A_log'], zeros])
    b_shape = (DEPTH, S5_CB, gpb, S5_GROUP, S5_STATE)
    c_shape = (DEPTH, S5_CB, gpb, S5_STATE, S5_GROUP)
    b_re = jnp.swapaxes(w['s5_B_re'], 2, 3).reshape(b_shape)
    b_im = jnp.swapaxes(w['s5_B_im'], 2, 3).reshape(b_shape)
    bbd = jnp.concatenate([_block_diag(b_re), _block_diag(b_im)], axis=-1).astype(BF16)
    c_re = jnp.swapaxes(w['s5_C_re'], 2, 3).reshape(c_shape)
    c_im = jnp.swapaxes(w['s5_C_im'], 2, 3).reshape(c_shape)
    cbd = jnp.concatenate([_block_diag(c_re), -_block_diag(c_im)], axis=2).astype(BF16)
    lam_shape = (DEPTH, N_DIR, S5_CB, 1, S5_CBW)
    ls = jnp.broadcast_to(w['s5_log_step'][..., None], (DEPTH, N_DIR, S5_GROUPS, S5_STATE))
    vec = lambda a: a.reshape(DEPTH, 1, a.shape[-1])
    return dict(
        norm1_w=vec(w['norm1_w']), w_big=w_big, w_small=w_small, gate_bias=gate_bias, gate_alog=gate_alog,
        ml_norm_w=vec(w['ml_norm_w']), bbd=bbd, cbd=cbd,
        lam_re=w['s5_lam_re'].reshape(lam_shape), lam_im=w['s5_lam_im'].reshape(lam_shape),
        log_step=ls.reshape(lam_shape),
        s5_D=vec(w['s5_D']), s5_glu_w=w['s5_glu_w'].astype(BF16), s5_glu_b=vec(w['s5_glu_b']),
        gd_conv_w=w['gd_conv_w'], gd_norm_w=vec(jnp.tile(w['gd_norm_w'], (1, HEADS))),
        w_branch=w['w_branch'].astype(BF16), w_out=w['w_out'].astype(BF16), norm2_w=vec(w['norm2_w']),
        w_gate=w['w_gate'].astype(BF16), w_up=w['w_up'].astype(BF16), w_down=w['w_down'].astype(BF16),
    )


def _grid_pos_embed(n_tok):
    grid_w = 64
    t = np.arange(n_tok)
    quarter = D_MODEL // 4
    omega = (1.0 / (10000.0 ** (np.arange(quarter, dtype=np.float32) / quarter))).astype(np.float32)

    def enc(pos):
        ang = pos.astype(np.float32)[:, None] * omega[None, :]
        return np.concatenate([np.sin(ang), np.cos(ang)], axis=-1)

    return jnp.asarray(np.concatenate([enc(t // grid_w), enc(t % grid_w)], axis=-1).astype(np.float32))


def _trunk_layer(x, mod, p, l, batch, seq, tb, states, final_w, pe=None):
    res = _inproj(x, mod, p, l, seq, pe)
    if pe is not None:
        gates, z, zs, x = res
    else:
        gates, z, zs = res
    if states is None:
        ml_st = s5_st = gd_st = None
    else:
        ml_st, s5_st, gd_st = states
    gmix, grows = _gate_prep(zs, p, l)
    ml = _mlstm(z, gmix, grows, batch, seq, tb, ml_st)
    s5 = _s5(z, p, l, batch, seq, s5_st)
    qkv = _gdn_prep(z, p, l, seq)
    gd = _gdn(qkv, gmix, grows, batch, seq, tb, gd_st)
    x = _merge(x, mod, gates, z, ml[0], s5[0], gd[0], p, l, seq)
    x = _ffn(x, mod, p, l, seq, final_w, l == DEPTH - 1)
    new_states = None
    if states is None:
        new_states = (ml[1][..., :DH], ml[1][..., DH], ml[2][:, :, :, 0, 0],
                      jnp.transpose(s5[1], (2, 0, 1, 3)).reshape(batch, N_DIR, S5_GROUPS, S5_STATE),
                      jnp.transpose(s5[2], (2, 0, 1, 3)).reshape(batch, N_DIR, S5_GROUPS, S5_STATE),
                      gd[1])
    return x, new_states


def kernel(x_prompt, x_sample, state_mlstm_C, state_mlstm_n, state_mlstm_m, state_s5_re, state_s5_im,
           state_gdn_S, c, c_ctx, ada_w, ada_b, norm1_w, w_in, ml_i_bias, ml_f_bias, ml_norm_w,
           s5_lam_re, s5_lam_im, s5_log_step, s5_B_re, s5_B_im, s5_C_re, s5_C_im, s5_D, s5_glu_w, s5_glu_b,
           gd_conv_w, gd_A_log, gd_dt_bias, gd_norm_w, w_branch, w_out, norm2_w, w_gate, w_up, w_down,
           final_norm_w):
    w = dict(norm1_w=norm1_w, w_in=w_in, ml_i_bias=ml_i_bias, ml_f_bias=ml_f_bias, ml_norm_w=ml_norm_w,
             s5_lam_re=s5_lam_re, s5_lam_im=s5_lam_im, s5_log_step=s5_log_step, s5_B_re=s5_B_re,
             s5_B_im=s5_B_im, s5_C_re=s5_C_re, s5_C_im=s5_C_im, s5_D=s5_D, s5_glu_w=s5_glu_w,
             s5_glu_b=s5_glu_b, gd_conv_w=gd_conv_w, gd_A_log=gd_A_log, gd_dt_bias=gd_dt_bias,
             gd_norm_w=gd_norm_w, w_branch=w_branch, w_out=w_out, norm2_w=norm2_w, w_gate=w_gate,
             w_up=w_up, w_down=w_down)
    bp, sp, _ = x_prompt.shape
    bs, ss, _ = x_sample.shape
    params = _prep_params(w)
    final_w = final_norm_w.reshape(1, D_MODEL)

    cc = jnp.concatenate([c_ctx[None, :], c, jnp.zeros((8 - 1 - bs, D_MODEL), F32)], axis=0)
    mods = _modulation(cc, ada_w, ada_b).reshape(DEPTH, 8, N_MOD, D_MODEL)

    xp = x_prompt.reshape(bp * sp, D_MODEL)
    per_layer = []
    for l in range(DEPTH):
        xp, st = _trunk_layer(xp, mods[l, 0:1], params, l, bp, sp, sp, None, final_w)
        per_layer.append(st)
    y_prompt = xp.reshape(bp, sp, D_MODEL)
    new_states = [jnp.stack([st[i] for st in per_layer], axis=1) for i in range(6)]

    xs = x_sample.reshape(bs * ss, D_MODEL)
    pe = _grid_pos_embed(ss)
    for l in range(DEPTH):
        n_cols = jnp.broadcast_to(state_mlstm_n[:, l][..., None], (bs, N_DIR, HEADS, DH, DH))
        ml_st = (jnp.concatenate([state_mlstm_C[:, l], n_cols], axis=-1),
                 jnp.broadcast_to(state_mlstm_m[:, l][:, :, :, None, None], (bs, N_DIR, HEADS, 1, DH)))
        s5_st = tuple(jnp.transpose(a[:, l].reshape(bs, N_DIR, S5_CB, 1, S5_CBW), (1, 2, 0, 3, 4))
                      for a in (state_s5_re, state_s5_im))
        xs, _ = _trunk_layer(xs, mods[l, 1:1 + bs], params, l, bs, ss, 512, (ml_st, s5_st, state_gdn_S[:, l]),
                             final_w, pe if l == 0 else None)
    y_sample = xs.reshape(bs, ss, D_MODEL)
    return (y_prompt, y_sample, *new_states)
```

```python
import functools
import math

import jax
import jax.numpy as jnp
import numpy as np
from jax import lax
from jax.experimental import pallas as pl
from jax.experimental.pallas import tpu as pltpu

F32 = jnp.float32
BF16 = jnp.bfloat16

D_MODEL = 1024
DEPTH = 2
N_DIR = 2
N_MOD = 6
EPS = 1e-6
HEADS = 4
DH = 128
BRANCH_W = HEADS * DH
CHUNK = 64
S5_GROUPS = 32
S5_GROUP = 16
S5_STATE = 64
S5_CB = 4
S5_CBW = S5_GROUPS * S5_STATE // S5_CB
S5_SUBLANES = 8
CONV_K = 5
D_FF = -(-8 * D_MODEL // (3 * 256)) * 256
IN_SIZES = (512, 512, 512, 512, 8, 8, 512, 1536, 512, 8, 8, 3072)
GATE_W = 128

GATES_W = 3 * D_MODEL
CB_Q, CB_K, CB_V, CB_O, CB_SU, CB_GQKV, CB_GZ = 0, 1, 2, 3, 4, 5, 8
Z_W = 9 * BRANCH_W

VMEM_LIMIT = 56 * 1024 * 1024


def _cparams(sem):
    return pltpu.CompilerParams(dimension_semantics=sem, vmem_limit_bytes=VMEM_LIMIT)


def _dot(a, b):
    return jnp.dot(a.astype(BF16), b.astype(BF16), preferred_element_type=F32)


def _dot_nt(a, b):
    return lax.dot_general(a.astype(BF16), b.astype(BF16), (((1,), (1,)), ((), ())),
                           preferred_element_type=F32)


def _dot_tn(a, b):
    return lax.dot_general(a.astype(BF16), b.astype(BF16), (((0,), (0,)), ((), ())),
                           preferred_element_type=F32)


def _split3(x):
    hi = x.astype(BF16)
    r1 = x - hi.astype(F32)
    mid = r1.astype(BF16)
    lo = (r1 - mid.astype(F32)).astype(BF16)
    return hi, mid, lo


def _sel_dot(sel, x, nt=False):
    dims = (((1,), (1,)), ((), ())) if nt else (((1,), (0,)), ((), ()))
    hi, mid, lo = _split3(x)
    f = lambda p: lax.dot_general(sel, p, dims, preferred_element_type=F32)
    return (f(hi) + f(mid)) + f(lo)


def _sigmoid(x):
    return 1.0 / (1.0 + jnp.exp(-x))


def _silu(x):
    return x * _sigmoid(x)


def _softplus(x):
    return jnp.maximum(x, 0.0) + jnp.log(1.0 + jnp.exp(-jnp.abs(x)))


def _gelu_tanh(x):
    c = math.sqrt(2.0 / math.pi)
    return 0.5 * x * (1.0 + jnp.tanh(c * (x + 0.044715 * (x * x * x))))


def _head_rms(x, w_row):
    outs = []
    for h in range(HEADS):
        xh = x[:, h * DH:(h + 1) * DH]
        outs.append(xh * lax.rsqrt(jnp.mean(xh * xh, axis=-1, keepdims=True) + EPS))
    return jnp.concatenate(outs, axis=-1) * w_row


def _dir_chunk(c, d, n):
    return c + d * (n - 1 - 2 * c)


def _mod_kernel(c_ref, w_ref, b_ref, o_ref):
    o_ref[0] = _dot(_silu(c_ref[...]), w_ref[0]) + b_ref[0]


def _modulation(cc, ada_w, ada_b):
    tn = 1536
    nmod = N_MOD * D_MODEL
    return pl.pallas_call(
        _mod_kernel,
        grid=(DEPTH, nmod // tn),
        in_specs=[pl.BlockSpec((8, D_MODEL), lambda l, j: (0, 0)),
                  pl.BlockSpec((1, D_MODEL, tn), lambda l, j: (l, 0, j)),
                  pl.BlockSpec((1, 1, tn), lambda l, j: (l, 0, j))],
        out_specs=pl.BlockSpec((1, 8, tn), lambda l, j: (l, 0, j)),
        out_shape=jax.ShapeDtypeStruct((DEPTH, 8, nmod), F32),
        name="adaln_mod",
        compiler_params=_cparams(("arbitrary", "arbitrary")),
    )(cc, ada_w, ada_b.reshape(DEPTH, 1, nmod))


def _inproj_kernel(*refs, has_pe, gate_tiles):
    if has_pe:
        x_ref, pe_ref, mod_ref, nw_ref, w_ref, ws_ref, gates_ref, z_ref, zs_ref, xs_ref, hn_scr = refs
    else:
        x_ref, mod_ref, nw_ref, w_ref, ws_ref, gates_ref, z_ref, zs_ref, hn_scr = refs
    j = pl.program_id(1)

    @pl.when(j == 0)
    def _():
        x = x_ref[...]
        if has_pe:
            x = x + pe_ref[...]
            xs_ref[...] = x
        y = x * lax.rsqrt(jnp.mean(x * x, axis=-1, keepdims=True) + EPS)
        h = (y * nw_ref[0]) * (1.0 + mod_ref[0, 1:2, :]) + mod_ref[0, 0:1, :]
        hb = h.astype(BF16)
        hn_scr[...] = hb
        zs_ref[...] = jnp.dot(hb, ws_ref[0], preferred_element_type=F32)

    @pl.when(j < gate_tiles)
    def _():
        gates_ref[...] = jnp.dot(hn_scr[...], w_ref[0], preferred_element_type=F32).astype(BF16)

    @pl.when(j >= gate_tiles)
    def _():
        z_ref[...] = jnp.dot(hn_scr[...], w_ref[0], preferred_element_type=F32)


def _inproj(x, mod, p, l, seq, pe=None):
    m = x.shape[0]
    has_pe = pe is not None
    tm, tn = (512 if has_pe else 1024), 1536
    gate_tiles = GATES_W // tn
    rows_per_mod = seq if mod.shape[0] > 1 else m
    row = lambda i, j: (i, 0)
    in_specs = [pl.BlockSpec((tm, D_MODEL), row)]
    args = [x]
    if has_pe:
        in_specs.append(pl.BlockSpec((tm, D_MODEL), lambda i, j: (i % (seq // tm), 0)))
        args.append(pe)
    in_specs += [pl.BlockSpec((1, N_MOD, D_MODEL), lambda i, j: (i * tm // rows_per_mod, 0, 0)),
                 pl.BlockSpec((1, 1, D_MODEL), lambda i, j: (l, 0, 0)),
                 pl.BlockSpec((1, D_MODEL, tn), lambda i, j: (l, 0, j)),
                 pl.BlockSpec((1, D_MODEL, N_DIR * GATE_W), lambda i, j: (l, 0, 0))]
    args += [mod, p['norm1_w'], p['w_big'], p['w_small']]
    out_specs = [pl.BlockSpec((tm, tn), lambda i, j: (i, jnp.minimum(j, gate_tiles - 1))),
                 pl.BlockSpec((tm, tn), lambda i, j: (i, jnp.maximum(j - gate_tiles, 0))),
                 pl.BlockSpec((tm, N_DIR * GATE_W), row)]
    out_shape = [jax.ShapeDtypeStruct((m, GATES_W), BF16),
                 jax.ShapeDtypeStruct((m, Z_W), F32),
                 jax.ShapeDtypeStruct((m, N_DIR * GATE_W), F32)]
    if has_pe:
        out_specs.append(pl.BlockSpec((tm, D_MODEL), row))
        out_shape.append(jax.ShapeDtypeStruct((m, D_MODEL), F32))
    return pl.pallas_call(
        functools.partial(_inproj_kernel, has_pe=has_pe, gate_tiles=gate_tiles),
        grid=(m // tm, (GATES_W + Z_W) // tn),
        in_specs=in_specs, out_specs=out_specs, out_shape=out_shape,
        scratch_shapes=[pltpu.VMEM((tm, D_MODEL), BF16)],
        name="inproj",
        compiler_params=_cparams(("arbitrary", "arbitrary")),
    )(*args)


def _mixer_rowblk(nt):
    return lambda b, d, j: b * nt + _dir_chunk(j, d, nt)


def _gate_selector():
    r = lax.broadcasted_iota(jnp.int32, (16, GATE_W), 0)
    c = lax.broadcasted_iota(jnp.int32, (16, GATE_W), 1)
    return (r == c).astype(BF16)


GATE_ROWS = 2048


def _gate_prep_kernel(g_ref, bias_ref, alog_ref, mix_ref, rows_ref):
    d = pl.program_id(0)
    pre = g_ref[...] + bias_ref[0, 0]
    lane = lax.broadcasted_iota(jnp.int32, pre.shape, 1)
    log_sig = -_softplus(-pre)
    vals = jnp.where(lane < 2 * HEADS, log_sig, -jnp.exp(alog_ref[0, 0]) * (pre - log_sig))
    r = lax.broadcasted_iota(jnp.int32, (HC, HC), 0)
    c = lax.broadcasted_iota(jnp.int32, (HC, HC), 1)
    ordered = jnp.logical_and((r >> CHUNK_LG) == (c >> CHUNK_LG), (r - c) * (1 - 2 * d) >= 0)
    tri = ordered.astype(BF16)
    cum = jnp.concatenate([_sel_dot(tri, vals[g * HC:(g + 1) * HC, :]) for g in range(GATE_ROWS // HC)], axis=0)
    mix = jnp.where(lane < HEADS, pre, jnp.where(lane < 3 * HEADS, cum, jnp.exp(log_sig)))
    mix_ref[0] = mix
    xt = _sel_dot(_gate_selector(), mix, nt=True)
    for c in range(GATE_ROWS // CHUNK):
        cs = slice(c * CHUNK, (c + 1) * CHUNK)
        rows_ref[0, c, 0:1, :] = jnp.concatenate(
            [xt[h:h + 1, cs] - xt[HEADS + h:HEADS + h + 1, cs] for h in range(HEADS)], axis=1)
        rows_ref[0, c, 1:2, :] = jnp.concatenate(
            [xt[2 * HEADS + h:2 * HEADS + h + 1, cs] for h in range(HEADS)], axis=1)


def _gate_prep(zs, p, l):
    m = zs.shape[0]
    dirrow = pl.BlockSpec((1, 1, 1, GATE_W), lambda d, i: (l, d, 0, 0))
    return pl.pallas_call(
        _gate_prep_kernel,
        grid=(N_DIR, m // GATE_ROWS),
        in_specs=[pl.BlockSpec((GATE_ROWS, GATE_W), lambda d, i: (i, d)), dirrow, dirrow],
        out_specs=[pl.BlockSpec((1, GATE_ROWS, GATE_W), lambda d, i: (d, i, 0)),
                   pl.BlockSpec((1, GATE_ROWS // CHUNK, 2, HC), lambda d, i: (d, i, 0, 0))],
        out_shape=[jax.ShapeDtypeStruct((N_DIR, m, GATE_W), F32),
                   jax.ShapeDtypeStruct((N_DIR, m // CHUNK, 2, HC), F32)],
        name="gate_prep",
        compiler_params=_cparams(("arbitrary", "arbitrary")),
    )(zs, p['gate_bias'], p['gate_alog'])


def _mlstm_kernel(*refs, nch, zero_init, emit_state):
    q_ref, k_ref, v_ref, mix_ref, rows_ref = refs[:5]
    pos = 5
    if not zero_init:
        c0_ref, m0_ref = refs[pos:pos + 2]
        pos += 2
    h_ref = refs[pos]
    pos += 1
    if emit_state:
        co_ref, mo_ref = refs[pos:pos + 2]
        pos += 2
    c_scr, m_scr, cls_all, dense_scr, av_scr, ml_scr = refs[pos:pos + 6]

    d = pl.program_id(1)
    j = pl.program_id(2)

    @pl.when(j == 0)
    def _():
        if zero_init:
            c_scr[...] = jnp.zeros_like(c_scr)
            m_scr[...] = jnp.zeros_like(m_scr)
        else:
            c_scr[...] = c0_ref[0, 0]
            m_scr[...] = m0_ref[0, 0]

    _fill_pair_classes(cls_all)
    cls_scr = cls_all.at[d]
    rowid = lax.broadcasted_iota(jnp.int32, (CHUNK, DH), 0)
    last = (CHUNK - 1) * (1 - d)
    scale = DH ** -0.5
    hrows = lambda h: slice(h * CHUNK, (h + 1) * CHUNK)
    per_head = lambda vals: jnp.concatenate([jnp.broadcast_to(a, (CHUNK, a.shape[1])) for a in vals], axis=0)
    twice = lambda a: jnp.concatenate([a, a], axis=1)
    ones = jnp.ones((HC, DH), F32)

    mixed = mix_ref[0]
    for h in range(HEADS):
        dense_scr[h, :, 0:DH] = jnp.broadcast_to(mixed[:, HEADS + h:HEADS + h + 1], (nch * CHUNK, DH))
        dense_scr[h, :, DH:2 * DH] = jnp.broadcast_to(mixed[:, h:h + 1], (nch * CHUNK, DH))

    stacked = lambda rows, lo: jnp.concatenate([dense_scr[h, rows, lo:lo + DH] for h in range(HEADS)], axis=0)
    npar = MLSTM_LOCKSTEP

    def local(cp, carry):
        cs = [cp * npar + i for i in range(npar)]
        rows = [pl.ds(pl.multiple_of(c * CHUNK, CHUNK), CHUNK) for c in cs]
        cls = cls_scr[...]
        log_d = [jnp.where(cls > 0, twice(stacked(r, 0)) + rows_ref[0, c, 0:1, :], -jnp.inf)
                 for r, c in zip(rows, cs)]
        ml = [jnp.max(a, axis=-1, keepdims=True) for a in log_d]
        q = [_stack_heads(q_ref[r, :]) for r in rows]
        k = [_stack_heads(k_ref[r, :]) * scale for r in rows]
        v = [jnp.concatenate([_stack_heads(v_ref[r, :]), ones], axis=1) for r in rows]
        s = [_dot_nt(a, b) * jnp.exp(ld - m) for a, b, ld, m in zip(q, k, log_d, ml)]
        av = [_dot(a, b) for a, b in zip(s, v)]
        for i, c in enumerate(cs):
            av_scr[c] = av[i]
            ml_scr[c] = jnp.broadcast_to(ml[i], (HC, DH))
        return carry

    lax.fori_loop(0, nch // npar, local, 0)

    def advance(c, carry):
        ci = _dir_chunk(c, d, nch)
        r0 = pl.multiple_of(ci * CHUNK, CHUNK)
        rows = pl.ds(r0, CHUNK)
        b_tok = stacked(rows, 0)
        i_tok = stacked(rows, DH)
        at_last = lambda a: [jnp.sum(jnp.where(rowid == last, a[hrows(h), :], 0.0), axis=0, keepdims=True)
                             for h in range(HEADS)]
        b_last = at_last(b_tok)
        m_prev = [m_scr[h] for h in range(HEADS)]
        ml = ml_scr[ci]
        log_0 = b_tok + per_head(m_prev)
        m_t = jnp.maximum(log_0, ml)
        w_0 = jnp.exp(log_0 - m_t)
        f = jnp.exp(ml - m_t)
        q = _stack_heads(q_ref[rows, :])
        k = _stack_heads(k_ref[rows, :]) * scale
        v = jnp.concatenate([_stack_heads(v_ref[rows, :]), ones], axis=1)
        cst = [c_scr[h] for h in range(HEADS)]
        qc = jnp.concatenate([_dot(q[hrows(h), :], cst[h]) for h in range(HEADS)], axis=0)
        num = f * av_scr[ci, :, 0:DH] + w_0 * qc[:, 0:DH]
        den = f * av_scr[ci, :, DH:2 * DH] + w_0 * qc[:, DH:2 * DH]
        hv = num / jnp.maximum(jnp.abs(den), jnp.exp(-m_t))
        m_new = at_last(m_t)
        kw = k * jnp.exp(per_head(b_last) - b_tok + i_tok - per_head(m_new))
        for h in range(HEADS):
            h_ref[0, rows, h * DH:(h + 1) * DH] = hv[hrows(h), :].astype(BF16)
            c_0 = jnp.exp(b_last[h] + m_prev[h] - m_new[h])
            c_scr[h] = twice(c_0) * cst[h] + _dot_tn(kw[hrows(h), :], v[hrows(h), :])
            m_scr[h] = m_new[h]
        return carry

    lax.fori_loop(0, nch, advance, 0)

    if emit_state:
        @pl.when(j == pl.num_programs(2) - 1)
        def _():
            co_ref[0, 0] = c_scr[...]
            mo_ref[0, 0] = m_scr[...]


def _gate_specs(rb, tb):
    return [pl.BlockSpec((1, tb, GATE_W), lambda b, d, j: (d, rb(b, d, j), 0)),
            pl.BlockSpec((1, tb // CHUNK, 2, HC), lambda b, d, j: (d, rb(b, d, j), 0, 0))]


def _mlstm(z, gmix, grows, batch, seq, tb, states):
    m = batch * seq
    nt = seq // tb
    nch = tb // CHUNK
    zero_init = states is None
    emit_state = states is None
    rb = _mixer_rowblk(nt)
    zspec = lambda cb: pl.BlockSpec((tb, BRANCH_W), lambda b, d, j: (rb(b, d, j), cb))
    st5 = lambda shape: pl.BlockSpec((1, 1) + shape, lambda b, d, j: (b, d) + (0,) * len(shape))
    in_specs = [zspec(CB_Q), zspec(CB_K), zspec(CB_V)] + _gate_specs(rb, tb)
    args = [z, z, z, gmix, grows]
    if not zero_init:
        in_specs += [st5((HEADS, DH, 2 * DH)), st5((HEADS, 1, DH))]
        args += list(states)
    out_specs = [pl.BlockSpec((1, tb, BRANCH_W), lambda b, d, j: (d, rb(b, d, j), 0))]
    out_shape = [jax.ShapeDtypeStruct((N_DIR, m, BRANCH_W), BF16)]
    if emit_state:
        out_specs += [st5((HEADS, DH, 2 * DH)), st5((HEADS, 1, DH))]
        out_shape += [jax.ShapeDtypeStruct((batch, N_DIR, HEADS, DH, 2 * DH), F32),
                      jax.ShapeDtypeStruct((batch, N_DIR, HEADS, 1, DH), F32)]
    return pl.pallas_call(
        functools.partial(_mlstm_kernel, nch=nch, zero_init=zero_init, emit_state=emit_state),
        grid=(batch, N_DIR, nt),
        in_specs=in_specs, out_specs=out_specs, out_shape=out_shape,
        scratch_shapes=[pltpu.VMEM((HEADS, DH, 2 * DH), F32), pltpu.VMEM((HEADS, 1, DH), F32),
                        pltpu.VMEM((N_DIR, HC, HC), jnp.int32),
                        pltpu.VMEM((HEADS, tb, 2 * DH), F32),
                        pltpu.VMEM((nch, HC, 2 * DH), F32), pltpu.VMEM((nch, HC, DH), F32)],
        name="mlstm",
        compiler_params=_cparams(("arbitrary", "arbitrary", "arbitrary")),
    )(*args)


def _gdn_prep_kernel(x_ref, prev_ref, next_ref, w_ref, o_ref, xe_scr, *, tb, seq):
    i = pl.program_id(0)
    p = pl.program_id(1)
    pad = 8
    at_start = (i * tb) % seq == 0
    at_end = ((i + 1) * tb) % seq == 0
    xe_scr[0:pad, :] = jnp.where(at_start, 0.0, prev_ref[...])
    xe_scr[pad:pad + tb, :] = x_ref[...]
    xe_scr[pad + tb:pad + tb + pad, :] = jnp.where(at_end, 0.0, next_ref[...])
    acc = None
    for t in range(CONV_K):
        term = xe_scr[pl.ds(pad + t - CONV_K // 2, tb), :] * w_ref[0, t:t + 1, :]
        acc = term if acc is None else acc + term
    y = _silu(acc)
    outs = []
    for h in range(HEADS):
        yh = y[:, h * DH:(h + 1) * DH]
        outs.append(yh * lax.rsqrt(jnp.sum(yh * yh, axis=-1, keepdims=True) + EPS))
    yn = jnp.concatenate(outs, axis=-1)
    qscale = jnp.where(p == 0, DH ** -0.5, 1.0)
    o_ref[...] = jnp.where(p == 2, y, yn * qscale)


def _gdn_prep(z, p, l, seq):
    m = z.shape[0]
    tb = 256
    nb8 = m // 8
    return pl.pallas_call(
        functools.partial(_gdn_prep_kernel, tb=tb, seq=seq),
        grid=(m // tb, 3),
        in_specs=[pl.BlockSpec((tb, BRANCH_W), lambda i, part: (i, CB_GQKV + part)),
                  pl.BlockSpec((8, BRANCH_W), lambda i, part: (jnp.maximum(i * (tb // 8) - 1, 0), CB_GQKV + part)),
                  pl.BlockSpec((8, BRANCH_W), lambda i, part: (jnp.minimum((i + 1) * (tb // 8), nb8 - 1), CB_GQKV + part)),
                  pl.BlockSpec((1, CONV_K, BRANCH_W), lambda i, part: (l, 0, part))],
        out_specs=pl.BlockSpec((tb, BRANCH_W), lambda i, part: (i, part)),
        out_shape=jax.ShapeDtypeStruct((m, 3 * BRANCH_W), F32),
        scratch_shapes=[pltpu.VMEM((tb + 16, BRANCH_W), F32)],
        name="gdn_prep",
        compiler_params=_cparams(("arbitrary", "arbitrary")),
    )(z, z, z, p['gd_conv_w'])


HC = HEADS * CHUNK


def _stack_heads(x):
    return jnp.concatenate([x[:, h * DH:(h + 1) * DH] for h in range(HEADS)], axis=0)


def _stack_gate(x, lane0):
    return jnp.concatenate([x[:, lane0 + h:lane0 + h + 1] for h in range(HEADS)], axis=0)


GDN_LOCKSTEP = 4
MLSTM_LOCKSTEP = 4
BASE_LG = 3
CHUNK_LG = int(math.log2(CHUNK))


def _pair_classes(d):
    r = lax.broadcasted_iota(jnp.int32, (HC, HC), 0)
    c = lax.broadcasted_iota(jnp.int32, (HC, HC), 1)
    cls = jnp.full((HC, HC), CHUNK_LG, jnp.int32)
    for s in range(CHUNK_LG - 1, BASE_LG - 1, -1):
        cls = jnp.where((r >> s) == (c >> s), s, cls)
    diff = (r - c) * (1 - 2 * d)
    cls = jnp.where(diff == 0, 1, cls)
    return jnp.where(jnp.logical_and((r >> CHUNK_LG) == (c >> CHUNK_LG), diff >= 0), cls, 0)


def _fill_pair_classes(cls_all):
    first = functools.reduce(jnp.logical_and, [pl.program_id(i) == 0 for i in range(3)])

    @pl.when(first)
    def _():
        for d in range(N_DIR):
            cls_all[d] = _pair_classes(d)


def _unit_tri_solve(xs, cls, rhss):
    n = xs[0].shape[0]
    nr = rhss[0].shape[1]
    x0 = [jnp.where(cls == BASE_LG, x, 0.0) for x in xs]
    x2 = [_dot(a, a) for a in x0]
    y = [_dot(b, jnp.concatenate([b, a], axis=1)) for a, b in zip(x0, x2)]
    x4 = [t[:, :n] for t in y]
    q2 = [a + b + t[:, n:] for a, b, t in zip(x0, x2, y)]
    t8m = [a + b + _dot(b, a) for a, b in zip(q2, x4)]
    levels = range(BASE_LG + 1, CHUNK_LG + 1)
    cur = [jnp.concatenate([r] + [jnp.where(cls == s, x, 0.0) for s in levels], axis=1) for x, r in zip(xs, rhss)]
    cur = [a + _dot(t, a) for a, t in zip(cur, t8m)]
    for _ in levels:
        last = cur[0].shape[1] == nr + n
        rest = [a[:, :nr] if last else jnp.concatenate([a[:, :nr], a[:, nr + n:]], axis=1) for a in cur]
        cur = [r + _dot(a[:, nr:nr + n], r) for a, r in zip(cur, rest)]
    return cur


def _gdn_kernel(*refs, nch, zero_init, emit_state):
    q_ref, k_ref, v_ref, mix_ref, rows_ref = refs[:5]
    pos = 5
    if not zero_init:
        s0_ref = refs[pos]
        pos += 1
    o_ref = refs[pos]
    pos += 1
    if emit_state:
        so_ref = refs[pos]
        pos += 1
    s_scr, cls_all, uw_scr, at_scr, qg_scr, kd_scr, gt_scr = refs[pos:pos + 7]

    d = pl.program_id(1)
    j = pl.program_id(2)

    @pl.when(j == 0)
    def _():
        if zero_init:
            s_scr[...] = jnp.zeros_like(s_scr)
        else:
            s_scr[...] = s0_ref[0, 0]

    _fill_pair_classes(cls_all)
    cls_scr = cls_all.at[d]
    last = (CHUNK - 1) * (1 - d)
    g0 = 2 * HEADS
    npar = GDN_LOCKSTEP

    def prepare(cp, carry):
        cs = [cp * npar + i for i in range(npar)]
        r0 = [pl.multiple_of(c * CHUNK, CHUNK) for c in cs]
        rows = [pl.ds(r, CHUNK) for r in r0]
        mix = mix_ref.at[0]
        gtot = [mix[pl.ds(r + last, 1), :] for r in r0]
        g_col = [_stack_gate(mix[r, :], g0) for r in rows]
        beta = [_stack_gate(mix[r, :], g0 + HEADS) for r in rows]
        g_last = [jnp.concatenate([jnp.broadcast_to(t[:, g0 + h:g0 + h + 1], (CHUNK, 1)) for h in range(HEADS)],
                                  axis=0) for t in gtot]
        cls = cls_scr[...]
        decay = [jnp.exp(jnp.where(cls > 0, gc - rows_ref[0, c, 1:2, :], -jnp.inf)) for gc, c in zip(g_col, cs)]
        eg = [jnp.exp(gc) for gc in g_col]
        q = [_stack_heads(q_ref[r, :]) for r in rows]
        k = [_stack_heads(k_ref[r, :]) for r in rows]
        v = [_stack_heads(v_ref[r, :]) for r in rows]
        kb = [a * b for a, b in zip(k, beta)]
        x = [-(_dot_nt(a, b) * dc) for a, b, dc in zip(kb, k, decay)]
        rhs = [jnp.concatenate([a * b, kbi * e], axis=-1) for a, b, kbi, e in zip(v, beta, kb, eg)]
        uw = _unit_tri_solve(x, cls, rhs)
        attn = [_dot_nt(a, b) * dc for a, b, dc in zip(q, k, decay)]
        for i, c in enumerate(cs):
            uw_scr[c] = uw[i]
            at_scr[c] = attn[i]
            qg_scr[c] = q[i] * eg[i]
            kd_scr[c] = k[i] * jnp.exp(g_last[i] - g_col[i])
            gt_scr[c] = gtot[i]
        return carry

    lax.fori_loop(0, nch // npar, prepare, 0)

    def advance(c, carry):
        ci = _dir_chunk(c, d, nch)
        rows = pl.ds(pl.multiple_of(ci * CHUNK, CHUNK), CHUNK)
        hrows = lambda h: slice(h * CHUNK, (h + 1) * CHUNK)
        st = [s_scr[h] for h in range(HEADS)]
        v_new = [uw_scr[ci, hrows(h), 0:DH] - _dot(uw_scr[ci, hrows(h), DH:2 * DH], st[h]) for h in range(HEADS)]
        qs = jnp.concatenate([_dot(qg_scr[ci, hrows(h), :], st[h]) for h in range(HEADS)], axis=0)
        o = qs + _dot(at_scr[ci], jnp.concatenate(v_new, axis=0))
        gtot = gt_scr[ci]
        for h in range(HEADS):
            o_ref[0, rows, h * DH:(h + 1) * DH] = o[hrows(h), :].astype(BF16)
            s_scr[h] = st[h] * jnp.exp(gtot[:, g0 + h:g0 + h + 1]) + _dot_tn(kd_scr[ci, hrows(h), :], v_new[h])
        return carry

    lax.fori_loop(0, nch, advance, 0)

    if emit_state:
        @pl.when(j == pl.num_programs(2) - 1)
        def _():
            so_ref[0, 0] = s_scr[...]


def _gdn(qkv, gmix, grows, batch, seq, tb, s0):
    m = batch * seq
    nt = seq // tb
    nch = tb // CHUNK
    zero_init = s0 is None
    emit_state = s0 is None
    rb = _mixer_rowblk(nt)
    spec = lambda cb: pl.BlockSpec((tb, BRANCH_W), lambda b, d, j: (rb(b, d, j), cb))
    st = pl.BlockSpec((1, 1, HEADS, DH, DH), lambda b, d, j: (b, d, 0, 0, 0))
    in_specs = [spec(0), spec(1), spec(2)] + _gate_specs(rb, tb)
    args = [qkv, qkv, qkv, gmix, grows]
    if not zero_init:
        in_specs.append(st)
        args.append(s0)
    out_specs = [pl.BlockSpec((1, tb, BRANCH_W), lambda b, d, j: (d, rb(b, d, j), 0))]
    out_shape = [jax.ShapeDtypeStruct((N_DIR, m, BRANCH_W), BF16)]
    if emit_state:
        out_specs.append(st)
        out_shape.append(jax.ShapeDtypeStruct((batch, N_DIR, HEADS, DH, DH), F32))
    return pl.pallas_call(
        functools.partial(_gdn_kernel, nch=nch, zero_init=zero_init, emit_state=emit_state),
        grid=(batch, N_DIR, nt),
        in_specs=in_specs, out_specs=out_specs, out_shape=out_shape,
        scratch_shapes=[pltpu.VMEM((HEADS, DH, DH), F32), pltpu.VMEM((N_DIR, HC, HC), jnp.int32),
                        pltpu.VMEM((nch, HC, 2 * DH), F32), pltpu.VMEM((nch, HC, HC), F32),
                        pltpu.VMEM((nch, HC, DH), F32), pltpu.VMEM((nch, HC, DH), F32),
                        pltpu.VMEM((nch, 1, GATE_W), F32)],
        name="gdn",
        compiler_params=_cparams(("arbitrary", "arbitrary", "arbitrary")),
    )(*args)


def _s5_discretise(lre, lim, ls):
    dt = jnp.exp(ls)
    mag = jnp.exp(lre * dt)
    ar = mag * jnp.cos(lim * dt)
    ai = mag * jnp.sin(lim * dt)
    den = lre * lre + lim * lim
    nr = ar - 1.0
    return ar, ai, (nr * lre + ai * lim) / den, (ai * lre - nr * lim) / den


def _s5_kernel(*refs, rb, seg, chain):
    u_ref, b_ref, c_ref, lre_ref, lim_ref, ls_ref = refs[:6]
    pos = 6
    if chain:
        h0r_ref, h0i_ref = refs[pos:pos + 2]
        pos += 2
    y_ref = refs[pos]
    pos += 1
    if not chain:
        sr_ref, si_ref = refs[pos:pos + 2]
        pos += 2
    up, yp, xr, xi, er, ei, pr, pi_ = refs[pos:pos + 8]

    w = S5_CBW
    nsub = S5_SUBLANES
    rc = 256
    ngrp = seg // nsub
    par = [_s5_discretise(lre_ref[0, d, 0], lim_ref[0, d, 0], ls_ref[0, d, 0]) for d in range(N_DIR)]

    def interleave(g, carry):
        for s in range(nsub):
            src = pl.ds(pl.multiple_of(s * seg + g * nsub, nsub), nsub)
            up[pl.ds(g * nsub * nsub + s, nsub, stride=nsub), :] = u_ref[src, :]
        return carry

    lax.fori_loop(0, ngrp, interleave, 0)

    def fill(c, carry):
        rows = pl.ds(pl.multiple_of(c * rc, rc), rc)
        bu = _dot(up[rows, :], b_ref[0, 0])
        bre = bu[:, :w]
        bim = bu[:, w:]
        for d in range(N_DIR):
            _, _, zr, zi = par[d]
            xr[d, rows, :] = zr * bre - zi * bim
            xi[d, rows, :] = zr * bim + zi * bre
        return carry

    lax.fori_loop(0, rb // rc, fill, 0, unroll=2)

    ab = [(jnp.broadcast_to(p[0], (nsub, w)), jnp.broadcast_to(p[1], (nsub, w))) for p in par]

    def step_rows(tt, d):
        t = tt if d == 0 else seg - 1 - tt
        return pl.ds(pl.multiple_of(t * nsub, nsub), nsub)

    def scan_step(tt, carry):
        out = []
        for d in range(N_DIR):
            hr, hi = carry[d]
            arb, aib = ab[d]
            idx = step_rows(tt, d)
            nhr = arb * hr - aib * hi + xr[d, idx, :]
            nhi = arb * hi + aib * hr + xi[d, idx, :]
            xr[d, idx, :] = nhr
            xi[d, idx, :] = nhi
            out.append((nhr, nhi))
        return tuple(out)

    zero = jnp.zeros((nsub, w), F32)
    ends = lax.fori_loop(0, seg, scan_step, ((zero, zero), (zero, zero)), unroll=2)

    if chain:
        for d in range(N_DIR):
            ar, ai = par[d][0], par[d][1]
            er[d] = ends[d][0]
            ei[d] = ends[d][1]
            sr, si = ar, ai
            for _ in range(int(math.log2(seg))):
                sr, si = sr * sr - si * si, 2.0 * sr * si
            fr = h0r_ref[d, 0, 0]
            fi = h0i_ref[d, 0, 0]
            for k in range(nsub):
                row = k if d == 0 else nsub - 1 - k
                pr[d, row:row + 1, :] = fr
                pi_[d, row:row + 1, :] = fi
                fr, fi = (er[d, row:row + 1, :] + (sr * fr - si * fi),
                          ei[d, row:row + 1, :] + (sr * fi + si * fr))

        def fix_step(tt, carry):
            out = []
            for d in range(N_DIR):
                cr, ci = carry[d]
                arb, aib = ab[d]
                idx = step_rows(tt, d)
                ncr = arb * cr - aib * ci
                nci = arb * ci + aib * cr
                xr[d, idx, :] = xr[d, idx, :] + ncr
                xi[d, idx, :] = xi[d, idx, :] + nci
                out.append((ncr, nci))
            return tuple(out)

        lax.fori_loop(0, seg, fix_step, tuple((pr[d], pi_[d]) for d in range(N_DIR)), unroll=2)
    else:
        for d in range(N_DIR):
            sr_ref[d, 0] = ends[d][0]
            si_ref[d, 0] = ends[d][1]

    def proj(c, carry):
        rows = pl.ds(pl.multiple_of(c * rc, rc), rc)
        yp[rows, :] = (_dot(xr[0, rows, :] + xr[1, rows, :], c_ref[0, 0, :w, :])
                       + _dot(xi[0, rows, :] + xi[1, rows, :], c_ref[0, 0, w:, :]))
        return carry

    lax.fori_loop(0, rb // rc, proj, 0, unroll=4)

    def deinterleave(g, carry):
        for s in range(nsub):
            dst = pl.ds(pl.multiple_of(s * seg + g * nsub, nsub), nsub)
            y_ref[dst, :] = yp[pl.ds(g * nsub * nsub + s, nsub, stride=nsub), :]
        return carry

    lax.fori_loop(0, ngrp, deinterleave, 0)


def _s5(z, p, l, batch, seq, h0):
    m = batch * seq
    nsub = S5_SUBLANES
    chain = h0 is not None
    if chain:
        seg = seq // nsub
        ng = batch
        st_spec = pl.BlockSpec((N_DIR, 1, 1, 1, S5_CBW), lambda g, cb: (0, cb, g, 0, 0))
    else:
        seg = seq
        ng = batch // nsub
        st_spec = pl.BlockSpec((N_DIR, 1, nsub, S5_CBW), lambda g, cb: (0, cb, g, 0))
    rb = nsub * seg
    lam_spec = pl.BlockSpec((1, N_DIR, 1, 1, S5_CBW), lambda g, cb: (l, 0, cb, 0, 0))
    in_specs = [pl.BlockSpec((rb, 128), lambda g, cb: (g, CB_SU * 4 + cb)),
                pl.BlockSpec((1, 1, 128, 2 * S5_CBW), lambda g, cb: (l, cb, 0, 0)),
                pl.BlockSpec((1, 1, 2 * S5_CBW, 128), lambda g, cb: (l, cb, 0, 0)),
                lam_spec, lam_spec, lam_spec]
    args = [z, p['bbd'], p['cbd'], p['lam_re'], p['lam_im'], p['log_step']]
    if chain:
        in_specs += [st_spec, st_spec]
        args += list(h0)
    out_specs = [pl.BlockSpec((rb, 128), lambda g, cb: (g, cb))]
    out_shape = [jax.ShapeDtypeStruct((m, BRANCH_W), F32)]
    if not chain:
        out_specs += [st_spec, st_spec]
        out_shape += [jax.ShapeDtypeStruct((N_DIR, S5_CB, batch, S5_CBW), F32)] * 2
    vdir = lambda n: pltpu.VMEM((N_DIR, n, S5_CBW), F32)
    lanes = pltpu.VMEM((rb, 128), F32)
    return pl.pallas_call(
        functools.partial(_s5_kernel, rb=rb, seg=seg, chain=chain),
        grid=(ng, S5_CB),
        in_specs=in_specs, out_specs=out_specs, out_shape=out_shape,
        scratch_shapes=[lanes, lanes, vdir(rb), vdir(rb), vdir(nsub), vdir(nsub), vdir(nsub), vdir(nsub)],
        name="s5",
        compiler_params=_cparams(("arbitrary", "arbitrary")),
    )(*args)


def _merge_kernel(x_ref, mod_ref, gates_ref, o_ref, u_ref, gz_ref, hm_ref, ys_ref, go_ref,
                  mlw_ref, s5d_ref, gluw_ref, glub_ref, gdw_ref, wbr_ref, wout_ref, out_ref):
    both = lambda ref: ref[0].astype(F32) + ref[1].astype(F32)
    ya = _sigmoid(o_ref[...]) * _head_rms(both(hm_ref), mlw_ref[0])
    y5 = _gelu_tanh(ys_ref[...] + s5d_ref[0] * u_ref[...])
    yb = y5 * _sigmoid(_dot(y5, gluw_ref[0]) + glub_ref[0])
    yc = _head_rms(both(go_ref), gdw_ref[0]) * _silu(gz_ref[...])
    merged = None
    for n, y in enumerate((ya, yb, yc)):
        gate = _sigmoid(gates_ref[:, n * D_MODEL:(n + 1) * D_MODEL].astype(F32))
        term = gate * _dot(y, wbr_ref[0, n])
        merged = term if merged is None else merged + term
    out_ref[...] = x_ref[...] + mod_ref[0, 2:3, :] * _dot(merged, wout_ref[0])


def _merge(x, mod, gates, z, hm, ys, go, p, l, seq):
    m = x.shape[0]
    tm = 512
    rows_per_mod = seq if mod.shape[0] > 1 else m
    row = lambda i: (i, 0)
    zspec = lambda cb: pl.BlockSpec((tm, BRANCH_W), lambda i: (i, cb))
    dspec = pl.BlockSpec((N_DIR, tm, BRANCH_W), lambda i: (0, i, 0))
    full = lambda a: pl.BlockSpec((1,) + a.shape[1:], lambda i: (l,) + (0,) * (a.ndim - 1))
    consts = [p['ml_norm_w'], p['s5_D'], p['s5_glu_w'], p['s5_glu_b'], p['gd_norm_w'], p['w_branch'], p['w_out']]
    return pl.pallas_call(
        _merge_kernel,
        grid=(m // tm,),
        in_specs=[pl.BlockSpec((tm, D_MODEL), row),
                  pl.BlockSpec((1, N_MOD, D_MODEL), lambda i: (i * tm // rows_per_mod, 0, 0)),
                  pl.BlockSpec((tm, GATES_W), row),
                  zspec(CB_O), zspec(CB_SU), zspec(CB_GZ), dspec, pl.BlockSpec((tm, BRANCH_W), row), dspec]
                 + [full(a) for a in consts],
        out_specs=pl.BlockSpec((tm, D_MODEL), row),
        out_shape=jax.ShapeDtypeStruct((m, D_MODEL), F32),
        name="merge",
        compiler_params=_cparams(("arbitrary",)),
    )(x, mod, gates, z, z, z, hm, ys, go, *consts)


def _ffn_kernel(x_ref, mod_ref, nw_ref, wg_ref, wu_ref, wd_ref, fw_ref, out_ref, h_scr, acc_scr, *, final):
    jf = pl.program_id(1)

    @pl.when(jf == 0)
    def _():
        x = x_ref[...]
        y = x * lax.rsqrt(jnp.mean(x * x, axis=-1, keepdims=True) + EPS)
        h_scr[...] = ((y * nw_ref[0]) * (1.0 + mod_ref[0, 4:5, :]) + mod_ref[0, 3:4, :]).astype(BF16)
        acc_scr[...] = jnp.zeros_like(acc_scr)

    hb = h_scr[...]
    a = _silu(jnp.dot(hb, wg_ref[0], preferred_element_type=F32))
    b = jnp.dot(hb, wu_ref[0], preferred_element_type=F32)
    acc_scr[...] += _dot(a * b, wd_ref[0])

    @pl.when(jf == pl.num_programs(1) - 1)
    def _():
        x = x_ref[...] + mod_ref[0, 5:6, :] * acc_scr[...]
        if final:
            x = x * lax.rsqrt(jnp.mean(x * x, axis=-1, keepdims=True) + EPS) * fw_ref[...]
        out_ref[...] = x


def _ffn(x, mod, p, l, seq, final_w, final):
    m = x.shape[0]
    tm, tf = 1024, D_FF // 2
    rows_per_mod = seq if mod.shape[0] > 1 else m
    row = lambda i, jf: (i, 0)
    vec = pl.BlockSpec((1, D_MODEL), lambda i, jf: (0, 0))
    return pl.pallas_call(
        functools.partial(_ffn_kernel, final=final),
        grid=(m // tm, D_FF // tf),
        in_specs=[pl.BlockSpec((tm, D_MODEL), row),
                  pl.BlockSpec((1, N_MOD, D_MODEL), lambda i, jf: (i * tm // rows_per_mod, 0, 0)),
                  pl.BlockSpec((1, 1, D_MODEL), lambda i, jf: (l, 0, 0)),
                  pl.BlockSpec((1, D_MODEL, tf), lambda i, jf: (l, 0, jf)),
                  pl.BlockSpec((1, D_MODEL, tf), lambda i, jf: (l, 0, jf)),
                  pl.BlockSpec((1, tf, D_MODEL), lambda i, jf: (l, jf, 0)),
                  vec],
        out_specs=pl.BlockSpec((tm, D_MODEL), row),
        out_shape=jax.ShapeDtypeStruct((m, D_MODEL), F32),
        scratch_shapes=[pltpu.VMEM((tm, D_MODEL), BF16), pltpu.VMEM((tm, D_MODEL), F32)],
        name="ffn",
        compiler_params=_cparams(("arbitrary", "arbitrary")),
    )(x, mod, p['norm2_w'], p['w_gate'], p['w_up'], p['w_down'], final_w)


def _gate_lanes(parts):
    row = jnp.concatenate([a.astype(F32) for a in parts], axis=-1)
    return jnp.pad(row, ((0, 0), (0, 0), (0, GATE_W - row.shape[-1]))).reshape(DEPTH, N_DIR, 1, GATE_W)


def _block_diag(a):
    dp, cb, g, r, c = a.shape
    eye = jnp.eye(g, dtype=a.dtype)
    return jnp.einsum('lbgrc,gh->lbgrhc', a, eye).reshape(dp, cb, g * r, g * c)


def _prep_params(w):
    gpb = S5_GROUPS // S5_CB
    idx, acc = [], 0
    for size in IN_SIZES[:-1]:
        acc += size
        idx.append(acc)
    mq, mk, mv, mo, mi, mf, su, gqkv, gz, ga, gb, gates = jnp.split(w['w_in'], idx, axis=-1)
    w_big = jnp.concatenate([gates, mq, mk, mv, mo, su, gqkv, gz], axis=-1).astype(BF16)
    smalls = []
    for d in range(N_DIR):
        sl = slice(d * HEADS, (d + 1) * HEADS)
        blk = jnp.concatenate([mi[..., sl], mf[..., sl], ga[..., sl], gb[..., sl]], axis=-1)
        smalls.append(jnp.pad(blk, ((0, 0), (0, 0), (0, GATE_W - 4 * HEADS))))
    w_small = jnp.concatenate(smalls, axis=-1).astype(BF16)
    zeros = jnp.zeros((DEPTH, N_DIR, HEADS), F32)
    gate_bias = _gate_lanes([w['ml_i_bias'], w['ml_f_bias'], w['gd_dt_bias'], zeros])
    gate_alog = _gate_lanes([zeros, zeros, w['gd_A_log'], zeros])
    b_shape = (DEPTH, S5_CB, gpb, S5_GROUP, S5_STATE)
    c_shape = (DEPTH, S5_CB, gpb, S5_STATE, S5_GROUP)
    b_re = jnp.swapaxes(w['s5_B_re'], 2, 3).reshape(b_shape)
    b_im = jnp.swapaxes(w['s5_B_im'], 2, 3).reshape(b_shape)
    bbd = jnp.concatenate([_block_diag(b_re), _block_diag(b_im)], axis=-1).astype(BF16)
    c_re = jnp.swapaxes(w['s5_C_re'], 2, 3).reshape(c_shape)
    c_im = jnp.swapaxes(w['s5_C_im'], 2, 3).reshape(c_shape)
    cbd = jnp.concatenate([_block_diag(c_re), -_block_diag(c_im)], axis=2).astype(BF16)
    lam_shape = (DEPTH, N_DIR, S5_CB, 1, S5_CBW)
    ls = jnp.broadcast_to(w['s5_log_step'][..., None], (DEPTH, N_DIR, S5_GROUPS, S5_STATE))
    vec = lambda a: a.reshape(DEPTH, 1, a.shape[-1])
    return dict(
        norm1_w=vec(w['norm1_w']), w_big=w_big, w_small=w_small, gate_bias=gate_bias, gate_alog=gate_alog,
        ml_norm_w=vec(w['ml_norm_w']), bbd=bbd, cbd=cbd,
        lam_re=w['s5_lam_re'].reshape(lam_shape), lam_im=w['s5_lam_im'].reshape(lam_shape),
        log_step=ls.reshape(lam_shape),
        s5_D=vec(w['s5_D']), s5_glu_w=w['s5_glu_w'].astype(BF16), s5_glu_b=vec(w['s5_glu_b']),
        gd_conv_w=w['gd_conv_w'], gd_norm_w=vec(jnp.tile(w['gd_norm_w'], (1, HEADS))),
        w_branch=w['w_branch'].astype(BF16), w_out=w['w_out'].astype(BF16), norm2_w=vec(w['norm2_w']),
        w_gate=w['w_gate'].astype(BF16), w_up=w['w_up'].astype(BF16), w_down=w['w_down'].astype(BF16),
    )


def _grid_pos_embed(n_tok):
    grid_w = 64
    t = np.arange(n_tok)
    quarter = D_MODEL // 4
    omega = (1.0 / (10000.0 ** (np.arange(quarter, dtype=np.float32) / quarter))).astype(np.float32)

    def enc(pos):
        ang = pos.astype(np.float32)[:, None] * omega[None, :]
        return np.concatenate([np.sin(ang), np.cos(ang)], axis=-1)

    return jnp.asarray(np.concatenate([enc(t // grid_w), enc(t % grid_w)], axis=-1).astype(np.float32))


def _trunk_layer(x, mod, p, l, batch, seq, tb, states, final_w, pe=None):
    res = _inproj(x, mod, p, l, seq, pe)
    if pe is not None:
        gates, z, zs, x = res
    else:
        gates, z, zs = res
    if states is None:
        ml_st = s5_st = gd_st = None
    else:
        ml_st, s5_st, gd_st = states
    gmix, grows = _gate_prep(zs, p, l)
    ml = _mlstm(z, gmix, grows, batch, seq, tb, ml_st)
    s5 = _s5(z, p, l, batch, seq, s5_st)
    qkv = _gdn_prep(z, p, l, seq)
    gd = _gdn(qkv, gmix, grows, batch, seq, tb, gd_st)
    x = _merge(x, mod, gates, z, ml[0], s5[0], gd[0], p, l, seq)
    x = _ffn(x, mod, p, l, seq, final_w, l == DEPTH - 1)
    new_states = None
    if states is None:
        new_states = (ml[1][..., :DH], ml[1][..., DH], ml[2][:, :, :, 0, 0],
                      jnp.transpose(s5[1], (2, 0, 1, 3)).reshape(batch, N_DIR, S5_GROUPS, S5_STATE),
                      jnp.transpose(s5[2], (2, 0, 1, 3)).reshape(batch, N_DIR, S5_GROUPS, S5_STATE),
                      gd[1])
    return x, new_states


def kernel(x_prompt, x_sample, state_mlstm_C, state_mlstm_n, state_mlstm_m, state_s5_re, state_s5_im,
           state_gdn_S, c, c_ctx, ada_w, ada_b, norm1_w, w_in, ml_i_bias, ml_f_bias, ml_norm_w,
           s5_lam_re, s5_lam_im, s5_log_step, s5_B_re, s5_B_im, s5_C_re, s5_C_im, s5_D, s5_glu_w, s5_glu_b,
           gd_conv_w, gd_A_log, gd_dt_bias, gd_norm_w, w_branch, w_out, norm2_w, w_gate, w_up, w_down,
           final_norm_w):
    w = dict(norm1_w=norm1_w, w_in=w_in, ml_i_bias=ml_i_bias, ml_f_bias=ml_f_bias, ml_norm_w=ml_norm_w,
             s5_lam_re=s5_lam_re, s5_lam_im=s5_lam_im, s5_log_step=s5_log_step, s5_B_re=s5_B_re,
             s5_B_im=s5_B_im, s5_C_re=s5_C_re, s5_C_im=s5_C_im, s5_D=s5_D, s5_glu_w=s5_glu_w,
             s5_glu_b=s5_glu_b, gd_conv_w=gd_conv_w, gd_A_log=gd_A_log, gd_dt_bias=gd_dt_bias,
             gd_norm_w=gd_norm_w, w_branch=w_branch, w_out=w_out, norm2_w=norm2_w, w_gate=w_gate,
             w_up=w_up, w_down=w_down)
    bp, sp, _ = x_prompt.shape
    bs, ss, _ = x_sample.shape
    params = _prep_params(w)
    final_w = final_norm_w.reshape(1, D_MODEL)

    cc = jnp.concatenate([c_ctx[None, :], c, jnp.zeros((8 - 1 - bs, D_MODEL), F32)], axis=0)
    mods = _modulation(cc, ada_w, ada_b).reshape(DEPTH, 8, N_MOD, D_MODEL)

    xp = x_prompt.reshape(bp * sp, D_MODEL)
    per_layer = []
    for l in range(DEPTH):
        xp, st = _trunk_layer(xp, mods[l, 0:1], params, l, bp, sp, sp, None, final_w)
        per_layer.append(st)
    y_prompt = xp.reshape(bp, sp, D_MODEL)
    new_states = [jnp.stack([st[i] for st in per_layer], axis=1) for i in range(6)]

    xs = x_sample.reshape(bs * ss, D_MODEL)
    pe = _grid_pos_embed(ss)
    for l in range(DEPTH):
        n_cols = jnp.broadcast_to(state_mlstm_n[:, l][..., None], (bs, N_DIR, HEADS, DH, DH))
        ml_st = (jnp.concatenate([state_mlstm_C[:, l], n_cols], axis=-1),
                 jnp.broadcast_to(state_mlstm_m[:, l][:, :, :, None, None], (bs, N_DIR, HEADS, 1, DH)))
        s5_st = tuple(jnp.transpose(a[:, l].reshape(bs, N_DIR, S5_CB, 1, S5_CBW), (1, 2, 0, 3, 4))
                      for a in (state_s5_re, state_s5_im))
        xs, _ = _trunk_layer(xs, mods[l, 1:1 + bs], params, l, bs, ss, 512, (ml_st, s5_st, state_gdn_S[:, l]),
                             final_w, pe if l == 0 else None)
    y_sample = xs.reshape(bs, ss, D_MODEL)
    return (y_prompt, y_sample, *new_states)
```

```python
import functools
import math

import jax
import jax.numpy as jnp
import numpy as np
from jax import lax
from jax.experimental import pallas as pl
from jax.experimental.pallas import tpu as pltpu

F32 = jnp.float32
BF16 = jnp.bfloat16

D_MODEL = 1024
DEPTH = 2
N_DIR = 2
N_MOD = 6
EPS = 1e-6
HEADS = 4
DH = 128
BRANCH_W = HEADS * DH
CHUNK = 64
S5_GROUPS = 32
S5_GROUP = 16
S5_STATE = 64
S5_CB = 4
S5_CBW = S5_GROUPS * S5_STATE // S5_CB
S5_SUBLANES = 8
CONV_K = 5
D_FF = -(-8 * D_MODEL // (3 * 256)) * 256
IN_SIZES = (512, 512, 512, 512, 8, 8, 512, 1536, 512, 8, 8, 3072)
GATE_W = 128

GATES_W = 3 * D_MODEL
CB_Q, CB_K, CB_V, CB_O, CB_SU, CB_GQKV, CB_GZ = 0, 1, 2, 3, 4, 5, 8
Z_W = 9 * BRANCH_W

VMEM_LIMIT = 56 * 1024 * 1024


def _cparams(sem):
    return pltpu.CompilerParams(dimension_semantics=sem, vmem_limit_bytes=VMEM_LIMIT)


def _dot(a, b):
    return jnp.dot(a.astype(BF16), b.astype(BF16), preferred_element_type=F32)


def _dot_nt(a, b):
    return lax.dot_general(a.astype(BF16), b.astype(BF16), (((1,), (1,)), ((), ())),
                           preferred_element_type=F32)


def _dot_tn(a, b):
    return lax.dot_general(a.astype(BF16), b.astype(BF16), (((0,), (0,)), ((), ())),
                           preferred_element_type=F32)


def _split3(x):
    hi = x.astype(BF16)
    r1 = x - hi.astype(F32)
    mid = r1.astype(BF16)
    lo = (r1 - mid.astype(F32)).astype(BF16)
    return hi, mid, lo


def _sel_dot(sel, x, nt=False):
    dims = (((1,), (1,)), ((), ())) if nt else (((1,), (0,)), ((), ()))
    hi, mid, lo = _split3(x)
    f = lambda p: lax.dot_general(sel, p, dims, preferred_element_type=F32)
    return (f(hi) + f(mid)) + f(lo)


def _sigmoid(x):
    return 1.0 / (1.0 + jnp.exp(-x))


def _silu(x):
    return x * _sigmoid(x)


def _softplus(x):
    return jnp.maximum(x, 0.0) + jnp.log(1.0 + jnp.exp(-jnp.abs(x)))


def _gelu_tanh(x):
    c = math.sqrt(2.0 / math.pi)
    return 0.5 * x * (1.0 + jnp.tanh(c * (x + 0.044715 * (x * x * x))))


def _head_rms(x, w_row):
    outs = []
    for h in range(HEADS):
        xh = x[:, h * DH:(h + 1) * DH]
        outs.append(xh * lax.rsqrt(jnp.mean(xh * xh, axis=-1, keepdims=True) + EPS))
    return jnp.concatenate(outs, axis=-1) * w_row


def _dir_chunk(c, d, n):
    return c + d * (n - 1 - 2 * c)


def _mod_kernel(c_ref, w_ref, b_ref, o_ref):
    o_ref[0] = _dot(_silu(c_ref[...]), w_ref[0]) + b_ref[0]


def _modulation(cc, ada_w, ada_b):
    tn = 1536
    nmod = N_MOD * D_MODEL
    return pl.pallas_call(
        _mod_kernel,
        grid=(DEPTH, nmod // tn),
        in_specs=[pl.BlockSpec((8, D_MODEL), lambda l, j: (0, 0)),
                  pl.BlockSpec((1, D_MODEL, tn), lambda l, j: (l, 0, j)),
                  pl.BlockSpec((1, 1, tn), lambda l, j: (l, 0, j))],
        out_specs=pl.BlockSpec((1, 8, tn), lambda l, j: (l, 0, j)),
        out_shape=jax.ShapeDtypeStruct((DEPTH, 8, nmod), F32),
        name="adaln_mod",
        compiler_params=_cparams(("arbitrary", "arbitrary")),
    )(cc, ada_w, ada_b.reshape(DEPTH, 1, nmod))


def _inproj_kernel(*refs, has_pe, gate_tiles):
    if has_pe:
        x_ref, pe_ref, mod_ref, nw_ref, w_ref, ws_ref, gates_ref, z_ref, zs_ref, xs_ref, hn_scr = refs
    else:
        x_ref, mod_ref, nw_ref, w_ref, ws_ref, gates_ref, z_ref, zs_ref, hn_scr = refs
    j = pl.program_id(1)

    @pl.when(j == 0)
    def _():
        x = x_ref[...]
        if has_pe:
            x = x + pe_ref[...]
            xs_ref[...] = x
        y = x * lax.rsqrt(jnp.mean(x * x, axis=-1, keepdims=True) + EPS)
        h = (y * nw_ref[0]) * (1.0 + mod_ref[0, 1:2, :]) + mod_ref[0, 0:1, :]
        hb = h.astype(BF16)
        hn_scr[...] = hb
        zs_ref[...] = jnp.dot(hb, ws_ref[0], preferred_element_type=F32)

    @pl.when(j < gate_tiles)
    def _():
        gates_ref[...] = jnp.dot(hn_scr[...], w_ref[0], preferred_element_type=F32).astype(BF16)

    @pl.when(j >= gate_tiles)
    def _():
        z_ref[...] = jnp.dot(hn_scr[...], w_ref[0], preferred_element_type=F32)


def _inproj(x, mod, p, l, seq, pe=None):
    m = x.shape[0]
    has_pe = pe is not None
    tm, tn = 1024, 1536
    gate_tiles = GATES_W // tn
    rows_per_mod = seq if mod.shape[0] > 1 else m
    row = lambda i, j: (i, 0)
    once = dict(pipeline_mode=pl.Buffered(1)) if has_pe else {}
    in_specs = [pl.BlockSpec((tm, D_MODEL), row, **once)]
    args = [x]
    if has_pe:
        in_specs.append(pl.BlockSpec((tm, D_MODEL), lambda i, j: (i % (seq // tm), 0), **once))
        args.append(pe)
    in_specs += [pl.BlockSpec((1, N_MOD, D_MODEL), lambda i, j: (i * tm // rows_per_mod, 0, 0)),
                 pl.BlockSpec((1, 1, D_MODEL), lambda i, j: (l, 0, 0)),
                 pl.BlockSpec((1, D_MODEL, tn), lambda i, j: (l, 0, j)),
                 pl.BlockSpec((1, D_MODEL, N_DIR * GATE_W), lambda i, j: (l, 0, 0))]
    args += [mod, p['norm1_w'], p['w_big'], p['w_small']]
    out_specs = [pl.BlockSpec((tm, tn), lambda i, j: (i, jnp.minimum(j, gate_tiles - 1))),
                 pl.BlockSpec((tm, tn), lambda i, j: (i, jnp.maximum(j - gate_tiles, 0))),
                 pl.BlockSpec((tm, N_DIR * GATE_W), row)]
    out_shape = [jax.ShapeDtypeStruct((m, GATES_W), BF16),
                 jax.ShapeDtypeStruct((m, Z_W), F32),
                 jax.ShapeDtypeStruct((m, N_DIR * GATE_W), F32)]
    if has_pe:
        out_specs.append(pl.BlockSpec((tm, D_MODEL), row))
        out_shape.append(jax.ShapeDtypeStruct((m, D_MODEL), F32))
    return pl.pallas_call(
        functools.partial(_inproj_kernel, has_pe=has_pe, gate_tiles=gate_tiles),
        grid=(m // tm, (GATES_W + Z_W) // tn),
        in_specs=in_specs, out_specs=out_specs, out_shape=out_shape,
        scratch_shapes=[pltpu.VMEM((tm, D_MODEL), BF16)],
        name="inproj",
        compiler_params=_cparams(("arbitrary", "arbitrary")),
    )(*args)


def _mixer_rowblk(nt):
    return lambda b, d, j: b * nt + _dir_chunk(j, d, nt)


def _gate_selector():
    r = lax.broadcasted_iota(jnp.int32, (16, GATE_W), 0)
    c = lax.broadcasted_iota(jnp.int32, (16, GATE_W), 1)
    return (r == c).astype(BF16)


GATE_ROWS = 2048


def _gate_prep_kernel(g_ref, bias_ref, alog_ref, mix_ref, rows_ref):
    d = pl.program_id(0)
    pre = g_ref[...] + bias_ref[0, 0]
    lane = lax.broadcasted_iota(jnp.int32, pre.shape, 1)
    log_sig = -_softplus(-pre)
    vals = jnp.where(lane < 2 * HEADS, log_sig, -jnp.exp(alog_ref[0, 0]) * (pre - log_sig))
    r = lax.broadcasted_iota(jnp.int32, (HC, HC), 0)
    c = lax.broadcasted_iota(jnp.int32, (HC, HC), 1)
    ordered = jnp.logical_and((r >> CHUNK_LG) == (c >> CHUNK_LG), (r - c) * (1 - 2 * d) >= 0)
    tri = ordered.astype(BF16)
    cum = jnp.concatenate([_sel_dot(tri, vals[g * HC:(g + 1) * HC, :]) for g in range(GATE_ROWS // HC)], axis=0)
    mix = jnp.where(lane < HEADS, pre, jnp.where(lane < 3 * HEADS, cum, jnp.exp(log_sig)))
    mix_ref[0] = mix
    xt = _sel_dot(_gate_selector(), mix, nt=True)
    for c in range(GATE_ROWS // CHUNK):
        cs = slice(c * CHUNK, (c + 1) * CHUNK)
        rows_ref[0, c, 0:1, :] = jnp.concatenate(
            [xt[h:h + 1, cs] - xt[HEADS + h:HEADS + h + 1, cs] for h in range(HEADS)], axis=1)
        rows_ref[0, c, 1:2, :] = jnp.concatenate(
            [xt[2 * HEADS + h:2 * HEADS + h + 1, cs] for h in range(HEADS)], axis=1)


def _gate_prep(zs, p, l):
    m = zs.shape[0]
    dirrow = pl.BlockSpec((1, 1, 1, GATE_W), lambda d, i: (l, d, 0, 0))
    return pl.pallas_call(
        _gate_prep_kernel,
        grid=(N_DIR, m // GATE_ROWS),
        in_specs=[pl.BlockSpec((GATE_ROWS, GATE_W), lambda d, i: (i, d)), dirrow, dirrow],
        out_specs=[pl.BlockSpec((1, GATE_ROWS, GATE_W), lambda d, i: (d, i, 0)),
                   pl.BlockSpec((1, GATE_ROWS // CHUNK, 2, HC), lambda d, i: (d, i, 0, 0))],
        out_shape=[jax.ShapeDtypeStruct((N_DIR, m, GATE_W), F32),
                   jax.ShapeDtypeStruct((N_DIR, m // CHUNK, 2, HC), F32)],
        name="gate_prep",
        compiler_params=_cparams(("arbitrary", "arbitrary")),
    )(zs, p['gate_bias'], p['gate_alog'])


def _mlstm_kernel(*refs, nch, zero_init, emit_state):
    q_ref, k_ref, v_ref, mix_ref, rows_ref = refs[:5]
    pos = 5
    if not zero_init:
        c0_ref, m0_ref = refs[pos:pos + 2]
        pos += 2
    h_ref = refs[pos]
    pos += 1
    if emit_state:
        co_ref, mo_ref = refs[pos:pos + 2]
        pos += 2
    c_scr, m_scr, cls_all, dense_scr, av_scr, ml_scr = refs[pos:pos + 6]

    d = pl.program_id(1)
    j = pl.program_id(2)

    @pl.when(j == 0)
    def _():
        if zero_init:
            c_scr[...] = jnp.zeros_like(c_scr)
            m_scr[...] = jnp.zeros_like(m_scr)
        else:
            c_scr[...] = c0_ref[0, 0]
            m_scr[...] = m0_ref[0, 0]

    _fill_pair_classes(cls_all)
    cls_scr = cls_all.at[d]
    rowid = lax.broadcasted_iota(jnp.int32, (CHUNK, DH), 0)
    last = (CHUNK - 1) * (1 - d)
    scale = DH ** -0.5
    hrows = lambda h: slice(h * CHUNK, (h + 1) * CHUNK)
    per_head = lambda vals: jnp.concatenate([jnp.broadcast_to(a, (CHUNK, a.shape[1])) for a in vals], axis=0)
    twice = lambda a: jnp.concatenate([a, a], axis=1)
    ones = jnp.ones((HC, DH), F32)

    mixed = mix_ref[0]
    for h in range(HEADS):
        dense_scr[h, :, 0:DH] = jnp.broadcast_to(mixed[:, HEADS + h:HEADS + h + 1], (nch * CHUNK, DH))
        dense_scr[h, :, DH:2 * DH] = jnp.broadcast_to(mixed[:, h:h + 1], (nch * CHUNK, DH))

    stacked = lambda rows, lo: jnp.concatenate([dense_scr[h, rows, lo:lo + DH] for h in range(HEADS)], axis=0)
    npar = MLSTM_LOCKSTEP

    def local(cp, carry):
        cs = [cp * npar + i for i in range(npar)]
        rows = [pl.ds(pl.multiple_of(c * CHUNK, CHUNK), CHUNK) for c in cs]
        cls = cls_scr[...]
        log_d = [jnp.where(cls > 0, twice(stacked(r, 0)) + rows_ref[0, c, 0:1, :], -jnp.inf)
                 for r, c in zip(rows, cs)]
        ml = [jnp.max(a, axis=-1, keepdims=True) for a in log_d]
        q = [_stack_heads(q_ref[r, :]) for r in rows]
        k = [_stack_heads(k_ref[r, :]) * scale for r in rows]
        v = [jnp.concatenate([_stack_heads(v_ref[r, :]), ones], axis=1) for r in rows]
        s = [_dot_nt(a, b) * jnp.exp(ld - m) for a, b, ld, m in zip(q, k, log_d, ml)]
        av = [_dot(a, b) for a, b in zip(s, v)]
        for i, c in enumerate(cs):
            av_scr[c] = av[i]
            ml_scr[c] = jnp.broadcast_to(ml[i], (HC, DH))
        return carry

    lax.fori_loop(0, nch // npar, local, 0)

    def advance(c, carry):
        ci = _dir_chunk(c, d, nch)
        r0 = pl.multiple_of(ci * CHUNK, CHUNK)
        rows = pl.ds(r0, CHUNK)
        b_tok = stacked(rows, 0)
        i_tok = stacked(rows, DH)
        at_last = lambda a: [jnp.sum(jnp.where(rowid == last, a[hrows(h), :], 0.0), axis=0, keepdims=True)
                             for h in range(HEADS)]
        b_last = at_last(b_tok)
        m_prev = [m_scr[h] for h in range(HEADS)]
        ml = ml_scr[ci]
        log_0 = b_tok + per_head(m_prev)
        m_t = jnp.maximum(log_0, ml)
        w_0 = jnp.exp(log_0 - m_t)
        f = jnp.exp(ml - m_t)
        q = _stack_heads(q_ref[rows, :])
        k = _stack_heads(k_ref[rows, :]) * scale
        v = jnp.concatenate([_stack_heads(v_ref[rows, :]), ones], axis=1)
        cst = [c_scr[h] for h in range(HEADS)]
        qc = jnp.concatenate([_dot(q[hrows(h), :], cst[h]) for h in range(HEADS)], axis=0)
        num = f * av_scr[ci, :, 0:DH] + w_0 * qc[:, 0:DH]
        den = f * av_scr[ci, :, DH:2 * DH] + w_0 * qc[:, DH:2 * DH]
        hv = num / jnp.maximum(jnp.abs(den), jnp.exp(-m_t))
        m_new = at_last(m_t)
        kw = k * jnp.exp(per_head(b_last) - b_tok + i_tok - per_head(m_new))
        for h in range(HEADS):
            h_ref[0, rows, h * DH:(h + 1) * DH] = hv[hrows(h), :].astype(BF16)
            c_0 = jnp.exp(b_last[h] + m_prev[h] - m_new[h])
            c_scr[h] = twice(c_0) * cst[h] + _dot_tn(kw[hrows(h), :], v[hrows(h), :])
            m_scr[h] = m_new[h]
        return carry

    lax.fori_loop(0, nch, advance, 0)

    if emit_state:
        @pl.when(j == pl.num_programs(2) - 1)
        def _():
            co_ref[0, 0] = c_scr[...]
            mo_ref[0, 0] = m_scr[...]


def _gate_specs(rb, tb):
    return [pl.BlockSpec((1, tb, GATE_W), lambda b, d, j: (d, rb(b, d, j), 0)),
            pl.BlockSpec((1, tb // CHUNK, 2, HC), lambda b, d, j: (d, rb(b, d, j), 0, 0))]


def _mlstm(z, gmix, grows, batch, seq, tb, states):
    m = batch * seq
    nt = seq // tb
    nch = tb // CHUNK
    zero_init = states is None
    emit_state = states is None
    rb = _mixer_rowblk(nt)
    zspec = lambda cb: pl.BlockSpec((tb, BRANCH_W), lambda b, d, j: (rb(b, d, j), cb))
    st5 = lambda shape: pl.BlockSpec((1, 1) + shape, lambda b, d, j: (b, d) + (0,) * len(shape))
    in_specs = [zspec(CB_Q), zspec(CB_K), zspec(CB_V)] + _gate_specs(rb, tb)
    args = [z, z, z, gmix, grows]
    if not zero_init:
        in_specs += [st5((HEADS, DH, 2 * DH)), st5((HEADS, 1, DH))]
        args += list(states)
    out_specs = [pl.BlockSpec((1, tb, BRANCH_W), lambda b, d, j: (d, rb(b, d, j), 0))]
    out_shape = [jax.ShapeDtypeStruct((N_DIR, m, BRANCH_W), BF16)]
    if emit_state:
        out_specs += [st5((HEADS, DH, 2 * DH)), st5((HEADS, 1, DH))]
        out_shape += [jax.ShapeDtypeStruct((batch, N_DIR, HEADS, DH, 2 * DH), F32),
                      jax.ShapeDtypeStruct((batch, N_DIR, HEADS, 1, DH), F32)]
    return pl.pallas_call(
        functools.partial(_mlstm_kernel, nch=nch, zero_init=zero_init, emit_state=emit_state),
        grid=(batch, N_DIR, nt),
        in_specs=in_specs, out_specs=out_specs, out_shape=out_shape,
        scratch_shapes=[pltpu.VMEM((HEADS, DH, 2 * DH), F32), pltpu.VMEM((HEADS, 1, DH), F32),
                        pltpu.VMEM((N_DIR, HC, HC), jnp.int32),
                        pltpu.VMEM((HEADS, tb, 2 * DH), F32),
                        pltpu.VMEM((nch, HC, 2 * DH), F32), pltpu.VMEM((nch, HC, DH), F32)],
        name="mlstm",
        compiler_params=_cparams(("arbitrary", "arbitrary", "arbitrary")),
    )(*args)


def _gdn_prep_kernel(x_ref, prev_ref, next_ref, w_ref, o_ref, xe_scr, *, tb, seq):
    i = pl.program_id(0)
    p = pl.program_id(1)
    pad = 8
    at_start = (i * tb) % seq == 0
    at_end = ((i + 1) * tb) % seq == 0
    xe_scr[0:pad, :] = jnp.where(at_start, 0.0, prev_ref[...])
    xe_scr[pad:pad + tb, :] = x_ref[...]
    xe_scr[pad + tb:pad + tb + pad, :] = jnp.where(at_end, 0.0, next_ref[...])
    acc = None
    for t in range(CONV_K):
        term = xe_scr[pl.ds(pad + t - CONV_K // 2, tb), :] * w_ref[0, t:t + 1, :]
        acc = term if acc is None else acc + term
    y = _silu(acc)
    outs = []
    for h in range(HEADS):
        yh = y[:, h * DH:(h + 1) * DH]
        outs.append(yh * lax.rsqrt(jnp.sum(yh * yh, axis=-1, keepdims=True) + EPS))
    yn = jnp.concatenate(outs, axis=-1)
    qscale = jnp.where(p == 0, DH ** -0.5, 1.0)
    o_ref[...] = jnp.where(p == 2, y, yn * qscale)


def _gdn_prep(z, p, l, seq):
    m = z.shape[0]
    tb = 256
    nb8 = m // 8
    return pl.pallas_call(
        functools.partial(_gdn_prep_kernel, tb=tb, seq=seq),
        grid=(m // tb, 3),
        in_specs=[pl.BlockSpec((tb, BRANCH_W), lambda i, part: (i, CB_GQKV + part)),
                  pl.BlockSpec((8, BRANCH_W), lambda i, part: (jnp.maximum(i * (tb // 8) - 1, 0), CB_GQKV + part)),
                  pl.BlockSpec((8, BRANCH_W), lambda i, part: (jnp.minimum((i + 1) * (tb // 8), nb8 - 1), CB_GQKV + part)),
                  pl.BlockSpec((1, CONV_K, BRANCH_W), lambda i, part: (l, 0, part))],
        out_specs=pl.BlockSpec((tb, BRANCH_W), lambda i, part: (i, part)),
        out_shape=jax.ShapeDtypeStruct((m, 3 * BRANCH_W), F32),
        scratch_shapes=[pltpu.VMEM((tb + 16, BRANCH_W), F32)],
        name="gdn_prep",
        compiler_params=_cparams(("arbitrary", "arbitrary")),
    )(z, z, z, p['gd_conv_w'])


HC = HEADS * CHUNK


def _stack_heads(x):
    return jnp.concatenate([x[:, h * DH:(h + 1) * DH] for h in range(HEADS)], axis=0)


def _stack_gate(x, lane0):
    return jnp.concatenate([x[:, lane0 + h:lane0 + h + 1] for h in range(HEADS)], axis=0)


GDN_LOCKSTEP = 4
MLSTM_LOCKSTEP = 4
BASE_LG = 3
CHUNK_LG = int(math.log2(CHUNK))


def _pair_classes(d):
    r = lax.broadcasted_iota(jnp.int32, (HC, HC), 0)
    c = lax.broadcasted_iota(jnp.int32, (HC, HC), 1)
    cls = jnp.full((HC, HC), CHUNK_LG, jnp.int32)
    for s in range(CHUNK_LG - 1, BASE_LG - 1, -1):
        cls = jnp.where((r >> s) == (c >> s), s, cls)
    diff = (r - c) * (1 - 2 * d)
    cls = jnp.where(diff == 0, 1, cls)
    return jnp.where(jnp.logical_and((r >> CHUNK_LG) == (c >> CHUNK_LG), diff >= 0), cls, 0)


def _fill_pair_classes(cls_all):
    first = functools.reduce(jnp.logical_and, [pl.program_id(i) == 0 for i in range(3)])

    @pl.when(first)
    def _():
        for d in range(N_DIR):
            cls_all[d] = _pair_classes(d)


def _unit_tri_solve(xs, cls, rhss):
    n = xs[0].shape[0]
    nr = rhss[0].shape[1]
    x0 = [jnp.where(cls == BASE_LG, x, 0.0) for x in xs]
    x2 = [_dot(a, a) for a in x0]
    y = [_dot(b, jnp.concatenate([b, a], axis=1)) for a, b in zip(x0, x2)]
    x4 = [t[:, :n] for t in y]
    q2 = [a + b + t[:, n:] for a, b, t in zip(x0, x2, y)]
    t8m = [a + b + _dot(b, a) for a, b in zip(q2, x4)]
    levels = range(BASE_LG + 1, CHUNK_LG + 1)
    cur = [jnp.concatenate([r] + [jnp.where(cls == s, x, 0.0) for s in levels], axis=1) for x, r in zip(xs, rhss)]
    cur = [a + _dot(t, a) for a, t in zip(cur, t8m)]
    for _ in levels:
        last = cur[0].shape[1] == nr + n
        rest = [a[:, :nr] if last else jnp.concatenate([a[:, :nr], a[:, nr + n:]], axis=1) for a in cur]
        cur = [r + _dot(a[:, nr:nr + n], r) for a, r in zip(cur, rest)]
    return cur


def _gdn_kernel(*refs, nch, zero_init, emit_state):
    q_ref, k_ref, v_ref, mix_ref, rows_ref = refs[:5]
    pos = 5
    if not zero_init:
        s0_ref = refs[pos]
        pos += 1
    o_ref = refs[pos]
    pos += 1
    if emit_state:
        so_ref = refs[pos]
        pos += 1
    s_scr, cls_all, uw_scr, at_scr, qg_scr, kd_scr, gt_scr = refs[pos:pos + 7]

    d = pl.program_id(1)
    j = pl.program_id(2)

    @pl.when(j == 0)
    def _():
        if zero_init:
            s_scr[...] = jnp.zeros_like(s_scr)
        else:
            s_scr[...] = s0_ref[0, 0]

    _fill_pair_classes(cls_all)
    cls_scr = cls_all.at[d]
    last = (CHUNK - 1) * (1 - d)
    g0 = 2 * HEADS
    npar = GDN_LOCKSTEP

    def prepare(cp, carry):
        cs = [cp * npar + i for i in range(npar)]
        r0 = [pl.multiple_of(c * CHUNK, CHUNK) for c in cs]
        rows = [pl.ds(r, CHUNK) for r in r0]
        mix = mix_ref.at[0]
        gtot = [mix[pl.ds(r + last, 1), :] for r in r0]
        g_col = [_stack_gate(mix[r, :], g0) for r in rows]
        beta = [_stack_gate(mix[r, :], g0 + HEADS) for r in rows]
        g_last = [jnp.concatenate([jnp.broadcast_to(t[:, g0 + h:g0 + h + 1], (CHUNK, 1)) for h in range(HEADS)],
                                  axis=0) for t in gtot]
        cls = cls_scr[...]
        decay = [jnp.exp(jnp.where(cls > 0, gc - rows_ref[0, c, 1:2, :], -jnp.inf)) for gc, c in zip(g_col, cs)]
        eg = [jnp.exp(gc) for gc in g_col]
        q = [_stack_heads(q_ref[r, :]) for r in rows]
        k = [_stack_heads(k_ref[r, :]) for r in rows]
        v = [_stack_heads(v_ref[r, :]) for r in rows]
        kb = [a * b for a, b in zip(k, beta)]
        x = [-(_dot_nt(a, b) * dc) for a, b, dc in zip(kb, k, decay)]
        rhs = [jnp.concatenate([a * b, kbi * e], axis=-1) for a, b, kbi, e in zip(v, beta, kb, eg)]
        uw = _unit_tri_solve(x, cls, rhs)
        attn = [_dot_nt(a, b) * dc for a, b, dc in zip(q, k, decay)]
        for i, c in enumerate(cs):
            uw_scr[c] = uw[i]
            at_scr[c] = attn[i]
            qg_scr[c] = q[i] * eg[i]
            kd_scr[c] = k[i] * jnp.exp(g_last[i] - g_col[i])
            gt_scr[c] = gtot[i]
        return carry

    lax.fori_loop(0, nch // npar, prepare, 0)

    def advance(c, carry):
        ci = _dir_chunk(c, d, nch)
        rows = pl.ds(pl.multiple_of(ci * CHUNK, CHUNK), CHUNK)
        hrows = lambda h: slice(h * CHUNK, (h + 1) * CHUNK)
        st = [s_scr[h] for h in range(HEADS)]
        v_new = [uw_scr[ci, hrows(h), 0:DH] - _dot(uw_scr[ci, hrows(h), DH:2 * DH], st[h]) for h in range(HEADS)]
        qs = jnp.concatenate([_dot(qg_scr[ci, hrows(h), :], st[h]) for h in range(HEADS)], axis=0)
        o = qs + _dot(at_scr[ci], jnp.concatenate(v_new, axis=0))
        gtot = gt_scr[ci]
        for h in range(HEADS):
            o_ref[0, rows, h * DH:(h + 1) * DH] = o[hrows(h), :].astype(BF16)
            s_scr[h] = st[h] * jnp.exp(gtot[:, g0 + h:g0 + h + 1]) + _dot_tn(kd_scr[ci, hrows(h), :], v_new[h])
        return carry

    lax.fori_loop(0, nch, advance, 0)

    if emit_state:
        @pl.when(j == pl.num_programs(2) - 1)
        def _():
            so_ref[0, 0] = s_scr[...]


def _gdn(qkv, gmix, grows, batch, seq, tb, s0):
    m = batch * seq
    nt = seq // tb
    nch = tb // CHUNK
    zero_init = s0 is None
    emit_state = s0 is None
    rb = _mixer_rowblk(nt)
    spec = lambda cb: pl.BlockSpec((tb, BRANCH_W), lambda b, d, j: (rb(b, d, j), cb))
    st = pl.BlockSpec((1, 1, HEADS, DH, DH), lambda b, d, j: (b, d, 0, 0, 0))
    in_specs = [spec(0), spec(1), spec(2)] + _gate_specs(rb, tb)
    args = [qkv, qkv, qkv, gmix, grows]
    if not zero_init:
        in_specs.append(st)
        args.append(s0)
    out_specs = [pl.BlockSpec((1, tb, BRANCH_W), lambda b, d, j: (d, rb(b, d, j), 0))]
    out_shape = [jax.ShapeDtypeStruct((N_DIR, m, BRANCH_W), BF16)]
    if emit_state:
        out_specs.append(st)
        out_shape.append(jax.ShapeDtypeStruct((batch, N_DIR, HEADS, DH, DH), F32))
    return pl.pallas_call(
        functools.partial(_gdn_kernel, nch=nch, zero_init=zero_init, emit_state=emit_state),
        grid=(batch, N_DIR, nt),
        in_specs=in_specs, out_specs=out_specs, out_shape=out_shape,
        scratch_shapes=[pltpu.VMEM((HEADS, DH, DH), F32), pltpu.VMEM((N_DIR, HC, HC), jnp.int32),
                        pltpu.VMEM((nch, HC, 2 * DH), F32), pltpu.VMEM((nch, HC, HC), F32),
                        pltpu.VMEM((nch, HC, DH), F32), pltpu.VMEM((nch, HC, DH), F32),
                        pltpu.VMEM((nch, 1, GATE_W), F32)],
        name="gdn",
        compiler_params=_cparams(("arbitrary", "arbitrary", "arbitrary")),
    )(*args)


def _s5_discretise(lre, lim, ls):
    dt = jnp.exp(ls)
    mag = jnp.exp(lre * dt)
    ar = mag * jnp.cos(lim * dt)
    ai = mag * jnp.sin(lim * dt)
    den = lre * lre + lim * lim
    nr = ar - 1.0
    return ar, ai, (nr * lre + ai * lim) / den, (ai * lre - nr * lim) / den


def _s5_kernel(*refs, rb, seg, chain):
    u_ref, b_ref, c_ref, lre_ref, lim_ref, ls_ref = refs[:6]
    pos = 6
    if chain:
        h0r_ref, h0i_ref = refs[pos:pos + 2]
        pos += 2
    y_ref = refs[pos]
    pos += 1
    if not chain:
        sr_ref, si_ref = refs[pos:pos + 2]
        pos += 2
    up, yp, xr, xi, er, ei, pr, pi_ = refs[pos:pos + 8]

    w = S5_CBW
    nsub = S5_SUBLANES
    rc = 256
    ngrp = seg // nsub
    par = [_s5_discretise(lre_ref[0, d, 0], lim_ref[0, d, 0], ls_ref[0, d, 0]) for d in range(N_DIR)]

    def interleave(g, carry):
        for s in range(nsub):
            src = pl.ds(pl.multiple_of(s * seg + g * nsub, nsub), nsub)
            up[pl.ds(g * nsub * nsub + s, nsub, stride=nsub), :] = u_ref[src, :]
        return carry

    lax.fori_loop(0, ngrp, interleave, 0)

    def fill(c, carry):
        rows = pl.ds(pl.multiple_of(c * rc, rc), rc)
        bu = _dot(up[rows, :], b_ref[0, 0])
        bre = bu[:, :w]
        bim = bu[:, w:]
        for d in range(N_DIR):
            _, _, zr, zi = par[d]
            xr[d, rows, :] = zr * bre - zi * bim
            xi[d, rows, :] = zr * bim + zi * bre
        return carry

    lax.fori_loop(0, rb // rc, fill, 0, unroll=2)

    ab = [(jnp.broadcast_to(p[0], (nsub, w)), jnp.broadcast_to(p[1], (nsub, w))) for p in par]

    def step_rows(tt, d):
        t = tt if d == 0 else seg - 1 - tt
        return pl.ds(pl.multiple_of(t * nsub, nsub), nsub)

    def scan_step(tt, carry):
        out = []
        for d in range(N_DIR):
            hr, hi = carry[d]
            arb, aib = ab[d]
            idx = step_rows(tt, d)
            nhr = arb * hr - aib * hi + xr[d, idx, :]
            nhi = arb * hi + aib * hr + xi[d, idx, :]
            xr[d, idx, :] = nhr
            xi[d, idx, :] = nhi
            out.append((nhr, nhi))
        return tuple(out)

    zero = jnp.zeros((nsub, w), F32)
    ends = lax.fori_loop(0, seg, scan_step, ((zero, zero), (zero, zero)), unroll=2)

    if chain:
        for d in range(N_DIR):
            ar, ai = par[d][0], par[d][1]
            er[d] = ends[d][0]
            ei[d] = ends[d][1]
            sr, si = ar, ai
            for _ in range(int(math.log2(seg))):
                sr, si = sr * sr - si * si, 2.0 * sr * si
            fr = h0r_ref[d, 0, 0]
            fi = h0i_ref[d, 0, 0]
            for k in range(nsub):
                row = k if d == 0 else nsub - 1 - k
                pr[d, row:row + 1, :] = fr
                pi_[d, row:row + 1, :] = fi
                fr, fi = (er[d, row:row + 1, :] + (sr * fr - si * fi),
                          ei[d, row:row + 1, :] + (sr * fi + si * fr))

        def fix_step(tt, carry):
            out = []
            for d in range(N_DIR):
                cr, ci = carry[d]
                arb, aib = ab[d]
                idx = step_rows(tt, d)
                ncr = arb * cr - aib * ci
                nci = arb * ci + aib * cr
                xr[d, idx, :] = xr[d, idx, :] + ncr
                xi[d, idx, :] = xi[d, idx, :] + nci
                out.append((ncr, nci))
            return tuple(out)

        lax.fori_loop(0, seg, fix_step, tuple((pr[d], pi_[d]) for d in range(N_DIR)), unroll=2)
    else:
        for d in range(N_DIR):
            sr_ref[d, 0] = ends[d][0]
            si_ref[d, 0] = ends[d][1]

    def proj(c, carry):
        rows = pl.ds(pl.multiple_of(c * rc, rc), rc)
        yp[rows, :] = (_dot(xr[0, rows, :] + xr[1, rows, :], c_ref[0, 0, :w, :])
                       + _dot(xi[0, rows, :] + xi[1, rows, :], c_ref[0, 0, w:, :]))
        return carry

    lax.fori_loop(0, rb // rc, proj, 0, unroll=4)

    def deinterleave(g, carry):
        for s in range(nsub):
            dst = pl.ds(pl.multiple_of(s * seg + g * nsub, nsub), nsub)
            y_ref[dst, :] = yp[pl.ds(g * nsub * nsub + s, nsub, stride=nsub), :]
        return carry

    lax.fori_loop(0, ngrp, deinterleave, 0)


def _s5(z, p, l, batch, seq, h0):
    m = batch * seq
    nsub = S5_SUBLANES
    chain = h0 is not None
    if chain:
        seg = seq // nsub
        ng = batch
        st_spec = pl.BlockSpec((N_DIR, 1, 1, 1, S5_CBW), lambda g, cb: (0, cb, g, 0, 0))
    else:
        seg = seq
        ng = batch // nsub
        st_spec = pl.BlockSpec((N_DIR, 1, nsub, S5_CBW), lambda g, cb: (0, cb, g, 0))
    rb = nsub * seg
    lam_spec = pl.BlockSpec((1, N_DIR, 1, 1, S5_CBW), lambda g, cb: (l, 0, cb, 0, 0))
    in_specs = [pl.BlockSpec((rb, 128), lambda g, cb: (g, CB_SU * 4 + cb)),
                pl.BlockSpec((1, 1, 128, 2 * S5_CBW), lambda g, cb: (l, cb, 0, 0)),
                pl.BlockSpec((1, 1, 2 * S5_CBW, 128), lambda g, cb: (l, cb, 0, 0)),
                lam_spec, lam_spec, lam_spec]
    args = [z, p['bbd'], p['cbd'], p['lam_re'], p['lam_im'], p['log_step']]
    if chain:
        in_specs += [st_spec, st_spec]
        args += list(h0)
    out_specs = [pl.BlockSpec((rb, 128), lambda g, cb: (g, cb))]
    out_shape = [jax.ShapeDtypeStruct((m, BRANCH_W), F32)]
    if not chain:
        out_specs += [st_spec, st_spec]
        out_shape += [jax.ShapeDtypeStruct((N_DIR, S5_CB, batch, S5_CBW), F32)] * 2
    vdir = lambda n: pltpu.VMEM((N_DIR, n, S5_CBW), F32)
    lanes = pltpu.VMEM((rb, 128), F32)
    return pl.pallas_call(
        functools.partial(_s5_kernel, rb=rb, seg=seg, chain=chain),
        grid=(ng, S5_CB),
        in_specs=in_specs, out_specs=out_specs, out_shape=out_shape,
        scratch_shapes=[lanes, lanes, vdir(rb), vdir(rb), vdir(nsub), vdir(nsub), vdir(nsub), vdir(nsub)],
        name="s5",
        compiler_params=_cparams(("arbitrary", "arbitrary")),
    )(*args)


def _merge_kernel(x_ref, mod_ref, gates_ref, o_ref, u_ref, gz_ref, hm_ref, ys_ref, go_ref,
                  mlw_ref, s5d_ref, gluw_ref, glub_ref, gdw_ref, wbr_ref, wout_ref, out_ref):
    both = lambda ref: ref[0].astype(F32) + ref[1].astype(F32)
    ya = _sigmoid(o_ref[...]) * _head_rms(both(hm_ref), mlw_ref[0])
    y5 = _gelu_tanh(ys_ref[...] + s5d_ref[0] * u_ref[...])
    yb = y5 * _sigmoid(_dot(y5, gluw_ref[0]) + glub_ref[0])
    yc = _head_rms(both(go_ref), gdw_ref[0]) * _silu(gz_ref[...])
    merged = None
    for n, y in enumerate((ya, yb, yc)):
        gate = _sigmoid(gates_ref[:, n * D_MODEL:(n + 1) * D_MODEL].astype(F32))
        term = gate * _dot(y, wbr_ref[0, n])
        merged = term if merged is None else merged + term
    out_ref[...] = x_ref[...] + mod_ref[0, 2:3, :] * _dot(merged, wout_ref[0])


def _merge(x, mod, gates, z, hm, ys, go, p, l, seq):
    m = x.shape[0]
    tm = 512
    rows_per_mod = seq if mod.shape[0] > 1 else m
    row = lambda i: (i, 0)
    deep = {}
    zspec = lambda cb: pl.BlockSpec((tm, BRANCH_W), lambda i: (i, cb), **deep)
    dspec = pl.BlockSpec((N_DIR, tm, BRANCH_W), lambda i: (0, i, 0), **deep)
    full = lambda a: pl.BlockSpec((1,) + a.shape[1:], lambda i: (l,) + (0,) * (a.ndim - 1),
                                  pipeline_mode=pl.Buffered(1))
    consts = [p['ml_norm_w'], p['s5_D'], p['s5_glu_w'], p['s5_glu_b'], p['gd_norm_w'], p['w_branch'], p['w_out']]
    return pl.pallas_call(
        _merge_kernel,
        grid=(m // tm,),
        in_specs=[pl.BlockSpec((tm, D_MODEL), row, **deep),
                  pl.BlockSpec((1, N_MOD, D_MODEL), lambda i: (i * tm // rows_per_mod, 0, 0)),
                  pl.BlockSpec((tm, GATES_W), row, **deep),
                  zspec(CB_O), zspec(CB_SU), zspec(CB_GZ), dspec, pl.BlockSpec((tm, BRANCH_W), row, **deep), dspec]
                 + [full(a) for a in consts],
        out_specs=pl.BlockSpec((tm, D_MODEL), row),
        out_shape=jax.ShapeDtypeStruct((m, D_MODEL), F32),
        name="merge",
        compiler_params=_cparams(("arbitrary",)),
    )(x, mod, gates, z, z, z, hm, ys, go, *consts)


def _ffn_kernel(x_ref, mod_ref, nw_ref, wg_ref, wu_ref, wd_ref, fw_ref, out_ref, h_scr, acc_scr, *, final):
    jf = pl.program_id(1)

    @pl.when(jf == 0)
    def _():
        x = x_ref[...]
        y = x * lax.rsqrt(jnp.mean(x * x, axis=-1, keepdims=True) + EPS)
        h_scr[...] = ((y * nw_ref[0]) * (1.0 + mod_ref[0, 4:5, :]) + mod_ref[0, 3:4, :]).astype(BF16)
        acc_scr[...] = jnp.zeros_like(acc_scr)

    hb = h_scr[...]
    a = _silu(jnp.dot(hb, wg_ref[0], preferred_element_type=F32))
    b = jnp.dot(hb, wu_ref[0], preferred_element_type=F32)
    acc_scr[...] += _dot(a * b, wd_ref[0])

    @pl.when(jf == pl.num_programs(1) - 1)
    def _():
        x = x_ref[...] + mod_ref[0, 5:6, :] * acc_scr[...]
        if final:
            x = x * lax.rsqrt(jnp.mean(x * x, axis=-1, keepdims=True) + EPS) * fw_ref[...]
        out_ref[...] = x


def _ffn(x, mod, p, l, seq, final_w, final):
    m = x.shape[0]
    tm, tf = 1024, D_FF // 2
    rows_per_mod = seq if mod.shape[0] > 1 else m
    row = lambda i, jf: (i, 0)
    vec = pl.BlockSpec((1, D_MODEL), lambda i, jf: (0, 0))
    return pl.pallas_call(
        functools.partial(_ffn_kernel, final=final),
        grid=(m // tm, D_FF // tf),
        in_specs=[pl.BlockSpec((tm, D_MODEL), row),
                  pl.BlockSpec((1, N_MOD, D_MODEL), lambda i, jf: (i * tm // rows_per_mod, 0, 0)),
                  pl.BlockSpec((1, 1, D_MODEL), lambda i, jf: (l, 0, 0)),
                  pl.BlockSpec((1, D_MODEL, tf), lambda i, jf: (l, 0, jf)),
                  pl.BlockSpec((1, D_MODEL, tf), lambda i, jf: (l, 0, jf)),
                  pl.BlockSpec((1, tf, D_MODEL), lambda i, jf: (l, jf, 0)),
                  vec],
        out_specs=pl.BlockSpec((tm, D_MODEL), row),
        out_shape=jax.ShapeDtypeStruct((m, D_MODEL), F32),
        scratch_shapes=[pltpu.VMEM((tm, D_MODEL), BF16), pltpu.VMEM((tm, D_MODEL), F32)],
        name="ffn",
        compiler_params=_cparams(("arbitrary", "arbitrary")),
    )(x, mod, p['norm2_w'], p['w_gate'], p['w_up'], p['w_down'], final_w)


def _gate_lanes(parts):
    row = jnp.concatenate([a.astype(F32) for a in parts], axis=-1)
    return jnp.pad(row, ((0, 0), (0, 0), (0, GATE_W - row.shape[-1]))).reshape(DEPTH, N_DIR, 1, GATE_W)


def _block_diag(a):
    dp, cb, g, r, c = a.shape
    eye = jnp.eye(g, dtype=a.dtype)
    return jnp.einsum('lbgrc,gh->lbgrhc', a, eye).reshape(dp, cb, g * r, g * c)


def _prep_params(w):
    gpb = S5_GROUPS // S5_CB
    idx, acc = [], 0
    for size in IN_SIZES[:-1]:
        acc += size
        idx.append(acc)
    mq, mk, mv, mo, mi, mf, su, gqkv, gz, ga, gb, gates = jnp.split(w['w_in'], idx, axis=-1)
    w_big = jnp.concatenate([gates, mq, mk, mv, mo, su, gqkv, gz], axis=-1).astype(BF16)
    smalls = []
    for d in range(N_DIR):
        sl = slice(d * HEADS, (d + 1) * HEADS)
        blk = jnp.concatenate([mi[..., sl], mf[..., sl], ga[..., sl], gb[..., sl]], axis=-1)
        smalls.append(jnp.pad(blk, ((0, 0), (0, 0), (0, GATE_W - 4 * HEADS))))
    w_small = jnp.concatenate(smalls, axis=-1).astype(BF16)
    zeros = jnp.zeros((DEPTH, N_DIR, HEADS), F32)
    gate_bias = _gate_lanes([w['ml_i_bias'], w['ml_f_bias'], w['gd_dt_bias'], zeros])
    gate_alog = _gate_lanes([zeros, zeros, w['gd_A_log'], zeros])
    b_shape = (DEPTH, S5_CB, gpb, S5_GROUP, S5_STATE)
    c_shape = (DEPTH, S5_CB, gpb, S5_STATE, S5_GROUP)
    b_re = jnp.swapaxes(w['s5_B_re'], 2, 3).reshape(b_shape)
    b_im = jnp.swapaxes(w['s5_B_im'], 2, 3).reshape(b_shape)
    bbd = jnp.concatenate([_block_diag(b_re), _block_diag(b_im)], axis=-1).astype(BF16)
    c_re = jnp.swapaxes(w['s5_C_re'], 2, 3).reshape(c_shape)
    c_im = jnp.swapaxes(w['s5_C_im'], 2, 3).reshape(c_shape)
    cbd = jnp.concatenate([_block_diag(c_re), -_block_diag(c_im)], axis=2).astype(BF16)
    lam_shape = (DEPTH, N_DIR, S5_CB, 1, S5_CBW)
    ls = jnp.broadcast_to(w['s5_log_step'][..., None], (DEPTH, N_DIR, S5_GROUPS, S5_STATE))
    vec = lambda a: a.reshape(DEPTH, 1, a.shape[-1])
    return dict(
        norm1_w=vec(w['norm1_w']), w_big=w_big, w_small=w_small, gate_bias=gate_bias, gate_alog=gate_alog,
        ml_norm_w=vec(w['ml_norm_w']), bbd=bbd, cbd=cbd,
        lam_re=w['s5_lam_re'].reshape(lam_shape), lam_im=w['s5_lam_im'].reshape(lam_shape),
        log_step=ls.reshape(lam_shape),
        s5_D=vec(w['s5_D']), s5_glu_w=w['s5_glu_w'].astype(BF16), s5_glu_b=vec(w['s5_glu_b']),
        gd_conv_w=w['gd_conv_w'], gd_norm_w=vec(jnp.tile(w['gd_norm_w'], (1, HEADS))),
        w_branch=w['w_branch'].astype(BF16), w_out=w['w_out'].astype(BF16), norm2_w=vec(w['norm2_w']),
        w_gate=w['w_gate'].astype(BF16), w_up=w['w_up'].astype(BF16), w_down=w['w_down'].astype(BF16),
    )


def _grid_pos_embed(n_tok):
    grid_w = 64
    t = np.arange(n_tok)
    quarter = D_MODEL // 4
    omega = (1.0 / (10000.0 ** (np.arange(quarter, dtype=np.float32) / quarter))).astype(np.float32)

    def enc(pos):
        ang = pos.astype(np.float32)[:, None] * omega[None, :]
        return np.concatenate([np.sin(ang), np.cos(ang)], axis=-1)

    return jnp.asarray(np.concatenate([enc(t // grid_w), enc(t % grid_w)], axis=-1).astype(np.float32))


def _trunk_layer(x, mod, p, l, batch, seq, tb, states, final_w, pe=None):
    res = _inproj(x, mod, p, l, seq, pe)
    if pe is not None:
        gates, z, zs, x = res
    else:
        gates, z, zs = res
    if states is None:
        ml_st = s5_st = gd_st = None
    else:
        ml_st, s5_st, gd_st = states
    gmix, grows = _gate_prep(zs, p, l)
    ml = _mlstm(z, gmix, grows, batch, seq, tb, ml_st)
    s5 = _s5(z, p, l, batch, seq, s5_st)
    qkv = _gdn_prep(z, p, l, seq)
    gd = _gdn(qkv, gmix, grows, batch, seq, tb, gd_st)
    x = _merge(x, mod, gates, z, ml[0], s5[0], gd[0], p, l, seq)
    x = _ffn(x, mod, p, l, seq, final_w, l == DEPTH - 1)
    new_states = None
    if states is None:
        new_states = (ml[1][..., :DH], ml[1][..., DH], ml[2][:, :, :, 0, 0],
                      jnp.transpose(s5[1], (2, 0, 1, 3)).reshape(batch, N_DIR, S5_GROUPS, S5_STATE),
                      jnp.transpose(s5[2], (2, 0, 1, 3)).reshape(batch, N_DIR, S5_GROUPS, S5_STATE),
                      gd[1])
    return x, new_states


def kernel(x_prompt, x_sample, state_mlstm_C, state_mlstm_n, state_mlstm_m, state_s5_re, state_s5_im,
           state_gdn_S, c, c_ctx, ada_w, ada_b, norm1_w, w_in, ml_i_bias, ml_f_bias, ml_norm_w,
           s5_lam_re, s5_lam_im, s5_log_step, s5_B_re, s5_B_im, s5_C_re, s5_C_im, s5_D, s5_glu_w, s5_glu_b,
           gd_conv_w, gd_A_log, gd_dt_bias, gd_norm_w, w_branch, w_out, norm2_w, w_gate, w_up, w_down,
           final_norm_w):
    w = dict(norm1_w=norm1_w, w_in=w_in, ml_i_bias=ml_i_bias, ml_f_bias=ml_f_bias, ml_norm_w=ml_norm_w,
             s5_lam_re=s5_lam_re, s5_lam_im=s5_lam_im, s5_log_step=s5_log_step, s5_B_re=s5_B_re,
             s5_B_im=s5_B_im, s5_C_re=s5_C_re, s5_C_im=s5_C_im, s5_D=s5_D, s5_glu_w=s5_glu_w,
             s5_glu_b=s5_glu_b, gd_conv_w=gd_conv_w, gd_A_log=gd_A_log, gd_dt_bias=gd_dt_bias,
             gd_norm_w=gd_norm_w, w_branch=w_branch, w_out=w_out, norm2_w=norm2_w, w_gate=w_gate,
             w_up=w_up, w_down=w_down)
    bp, sp, _ = x_prompt.shape
    bs, ss, _ = x_sample.shape
    params = _prep_params(w)
    final_w = final_norm_w.reshape(1, D_MODEL)

    cc = jnp.concatenate([c_ctx[None, :], c, jnp.zeros((8 - 1 - bs, D_MODEL), F32)], axis=0)
    mods = _modulation(cc, ada_w, ada_b).reshape(DEPTH, 8, N_MOD, D_MODEL)

    xp = x_prompt.reshape(bp * sp, D_MODEL)
    per_layer = []
    for l in range(DEPTH):
        xp, st = _trunk_layer(xp, mods[l, 0:1], params, l, bp, sp, sp, None, final_w)
        per_layer.append(st)
    y_prompt = xp.reshape(bp, sp, D_MODEL)
    new_states = [jnp.stack([st[i] for st in per_layer], axis=1) for i in range(6)]

    xs = x_sample.reshape(bs * ss, D_MODEL)
    pe = _grid_pos_embed(ss)
    for l in range(DEPTH):
        n_cols = jnp.broadcast_to(state_mlstm_n[:, l][..., None], (bs, N_DIR, HEADS, DH, DH))
        ml_st = (jnp.concatenate([state_mlstm_C[:, l], n_cols], axis=-1),
                 jnp.broadcast_to(state_mlstm_m[:, l][:, :, :, None, None], (bs, N_DIR, HEADS, 1, DH)))
        s5_st = tuple(jnp.transpose(a[:, l].reshape(bs, N_DIR, S5_CB, 1, S5_CBW), (1, 2, 0, 3, 4))
                      for a in (state_s5_re, state_s5_im))
        xs, _ = _trunk_layer(xs, mods[l, 1:1 + bs], params, l, bs, ss, 512, (ml_st, s5_st, state_gdn_S[:, l]),
                             final_w, pe if l == 0 else None)
    y_sample = xs.reshape(bs, ss, D_MODEL)
    return (y_prompt, y_sample, *new_states)
```
